```python
import math
import jax
import jax.numpy as jnp
from jax import lax
import numpy as np

D_MODEL = 2048
BATCH = 2
SEQ = 4096
DEPTH = 1
DEC_BATCH = 8
DEC_SEQ = 4096
PAST_LEN = 128

RWKV_HEAD = 64
RWKV_W = D_MODEL
RWKV_H = RWKV_W // RWKV_HEAD
DECAY_LORA = max(32, int(round(1.8 * D_MODEL ** 0.5 / 32)) * 32)
AAA_LORA = max(32, int(round(1.8 * D_MODEL ** 0.5 / 32)) * 32)
GATE_LORA = max(32, int(round(0.6 * D_MODEL ** 0.8 / 32)) * 32)
DECAY_SCALE = math.exp(-0.5)
LNX_EPS = RWKV_HEAD * 1e-5
CONV_W = D_MODEL // 2
CONV_K = 31
N_MEM = 256
XATTN_HEADS = 4
XATTN_HEAD_DIM = D_MODEL // XATTN_HEADS
N_EXPERTS = 32
TOP_K = 4
D_EXPERT = D_MODEL
SWIGLU_ALPHA = 1.702
SWIGLU_LIMIT = 7.0
MOE_BLOCK = 512
RWKV_COLS = 3 * RWKV_W + 2 * DECAY_LORA + 2 * AAA_LORA + GATE_LORA
IN_COLS = RWKV_COLS + 2 * CONV_W + 2 * D_MODEL
RMS_EPS = 1e-5
LN_EPS = 1e-5

kernel_name = 'hybrid_rwkv7_conformer_moe_encoder'


def rms_norm(x, g):
    x32 = x.astype(jnp.float32)
    y = x32 * lax.rsqrt(jnp.mean(x32 * x32, axis=-1, keepdims=True) + RMS_EPS)
    return (y * g.astype(jnp.float32)).astype(x.dtype)


def layer_norm(x, g, b):
    x32 = x.astype(jnp.float32)
    xc = x32 - jnp.mean(x32, axis=-1, keepdims=True)
    y = xc * lax.rsqrt(jnp.mean(xc * xc, axis=-1, keepdims=True) + LN_EPS)
    return (y * g.astype(jnp.float32) + b.astype(jnp.float32)).astype(x.dtype)


def centred_shift(z):
    zp = jnp.pad(z, ((0, 0), (1, 1), (0, 0)))
    return 0.5 * (zp[:, :-2] + zp[:, 2:]) - z


def to_heads(t):
    return t.reshape(t.shape[:-1] + (RWKV_H, RWKV_HEAD))


def wkv_scan(r, w, k, v, a_vec, b_vec, reverse):
    B, S, H, N = r.shape
    xs = tuple(jnp.moveaxis(t, 1, 0) for t in (r, w, k, v, a_vec, b_vec))

    def step(state, inp):
        r_t, w_t, k_t, v_t, a_t, b_t = inp
        sa = jnp.einsum('bhvk,bhk->bhv', state, a_t)
        state = (state * w_t[:, :, None, :] + sa[..., None] * b_t[:, :, None, :]
                 + v_t[..., None] * k_t[:, :, None, :])
        return state, jnp.einsum('bhvk,bhk->bhv', state, r_t)

    s0 = jnp.zeros((B, H, N, N), jnp.float32)
    _, o = lax.scan(step, s0, xs, reverse=reverse)
    return jnp.moveaxis(o, 0, 1)


def rwkv7_bidir(z, decay_w0, decay_w2, iclr_a0, iclr_a2, gate_g2, k_k, k_a, r_k, lnx_g, lnx_b):
    B, S, _ = z.shape
    f32 = jnp.float32
    o3 = 3 * RWKV_W
    o4 = o3 + 2 * DECAY_LORA
    o5 = o4 + 2 * AAA_LORA
    r = z[..., :RWKV_W]
    k = z[..., RWKV_W:2 * RWKV_W]
    v = z[..., 2 * RWKV_W:o3]
    dd = z[..., o3:o4].reshape(B, S, 2, DECAY_LORA)
    ad = z[..., o4:o5].reshape(B, S, 2, AAA_LORA)
    gd = z[..., o5:]
    d = decay_w0 + jnp.einsum('bsjr,jrc->bsjc', jnp.tanh(dd), decay_w2)
    w = to_heads(jnp.exp(-DECAY_SCALE * jax.nn.sigmoid(d.astype(f32))))
    a = to_heads(jax.nn.sigmoid((iclr_a0 + jnp.einsum('bsjr,jrc->bsjc', ad, iclr_a2)).astype(f32)))
    g = (jax.nn.sigmoid(gd) @ gate_g2).astype(f32)
    r32 = to_heads(r.astype(f32))
    k32 = to_heads(k.astype(f32))
    v32 = to_heads(v.astype(f32))
    kk = k32 * to_heads(k_k.astype(f32))
    kk = kk / jnp.maximum(jnp.sqrt(jnp.sum(kk * kk, axis=-1, keepdims=True)), 1e-12)
    k_mod = k32[:, :, None] * (1.0 + (a - 1.0) * to_heads(k_a.astype(f32)))
    o = (wkv_scan(r32, w[:, :, 0], k_mod[:, :, 0], v32, -kk, kk * a[:, :, 0], False)
         + wkv_scan(r32, w[:, :, 1], k_mod[:, :, 1], v32, -kk, kk * a[:, :, 1], True))
    oc = o - jnp.mean(o, axis=-1, keepdims=True)
    o = oc * lax.rsqrt(jnp.mean(oc * oc, axis=-1, keepdims=True) + LNX_EPS)
    o = o.reshape(B, S, RWKV_W) * lnx_g.astype(f32) + lnx_b.astype(f32)
    bonus = jnp.sum(jnp.sum(r32[:, :, None] * k_mod * r_k.astype(f32), axis=-1, keepdims=True)
                    * v32[:, :, None], axis=2)
    return ((o + bonus.reshape(B, S, RWKV_W)) * g).astype(z.dtype)


def conformer_conv(z, conv_w, conv_b, ln_g, ln_b):
    a, b = jnp.split(z, 2, axis=-1)
    u = a * jax.nn.sigmoid(b)
    u = lax.conv_general_dilated(
        u, conv_w[:, None, :].astype(u.dtype), window_strides=(1,),
        padding=[(CONV_K // 2, CONV_K // 2)],
        dimension_numbers=('NWC', 'WIO', 'NWC'),
        feature_group_count=CONV_W) + conv_b
    return jax.nn.silu(layer_norm(u, ln_g, ln_b))


def memory_cross_attention(h, mem_h, w_xq, w_xkv, w_xo):
    B, S, _ = h.shape
    M = mem_h.shape[1]
    q = (h @ w_xq).reshape(B, S, XATTN_HEADS, XATTN_HEAD_DIM)
    kv = mem_h @ w_xkv
    k = kv[..., :D_MODEL].reshape(B, M, XATTN_HEADS, XATTN_HEAD_DIM)
    v = kv[..., D_MODEL:].reshape(B, M, XATTN_HEADS, XATTN_HEAD_DIM)
    s = jnp.einsum('bshd,bmhd->bhsm', q, k).astype(jnp.float32) * (XATTN_HEAD_DIM ** -0.5)
    p = jax.nn.softmax(s, axis=-1).astype(v.dtype)
    o = jnp.einsum('bhsm,bmhd->bshd', p, v).reshape(B, S, D_MODEL)
    return o @ w_xo


def moe_ffn(h, w_router, b_router, w_gu, b_gu, w_dn, b_dn):
    T, D = h.shape
    logits = (h @ w_router).astype(jnp.float32) + b_router.astype(jnp.float32)
    top_logit, top_e = lax.top_k(logits, TOP_K)
    gate = jax.nn.softmax(top_logit, axis=-1)
    A = T * TOP_K
    n_blocks = (A + N_EXPERTS * (MOE_BLOCK - 1) + MOE_BLOCK - 1) // MOE_BLOCK
    n_rows = n_blocks * MOE_BLOCK
    flat_e = top_e.reshape(A).astype(jnp.int32)
    order = jnp.argsort(flat_e).astype(jnp.int32)
    sorted_e = flat_e[order]
    sorted_tok = order // TOP_K
    sorted_gate = gate.reshape(A)[order]
    counts = jnp.bincount(flat_e, length=N_EXPERTS).astype(jnp.int32)
    padded = (counts + MOE_BLOCK - 1) // MOE_BLOCK * MOE_BLOCK
    start = jnp.cumsum(counts) - counts
    pad_end = jnp.cumsum(padded)
    pad_start = pad_end - padded
    dest = pad_start[sorted_e] + jnp.arange(A, dtype=jnp.int32) - start[sorted_e]
    row_tok = jnp.full((n_rows,), T, dtype=jnp.int32).at[dest].set(sorted_tok)
    row_gate = jnp.zeros((n_rows,), jnp.float32).at[dest].set(sorted_gate)
    block_e = jnp.minimum(
        jnp.searchsorted(pad_end, jnp.arange(n_blocks, dtype=jnp.int32) * MOE_BLOCK, side='right'),
        N_EXPERTS - 1)
    h_pad = jnp.concatenate([h, jnp.zeros((1, D), h.dtype)], axis=0)
    xb = h_pad[row_tok].reshape(n_blocks, MOE_BLOCK, D)

    def expert_block(args):
        xblk, e = args
        gu = xblk @ w_gu[e] + b_gu[e]
        g, u = jnp.split(gu, 2, axis=-1)
        g = jnp.minimum(g, SWIGLU_LIMIT)
        u = jnp.clip(u, -SWIGLU_LIMIT, SWIGLU_LIMIT)
        act = (u + 1.0) * (g * jax.nn.sigmoid(SWIGLU_ALPHA * g))
        return act @ w_dn[e] + b_dn[e]

    yb = lax.map(expert_block, (xb, block_e)).reshape(n_rows, D)
    out = jnp.zeros((T + 1, D), yb.dtype).at[row_tok].add(yb * row_gate[:, None].astype(yb.dtype))
    return out[:T]


def encoder_layer(x, mem, norm_mix_g, w_in, shift_mu, decay_w0, decay_w2, iclr_a0, iclr_a2,
                  gate_g2, k_k, k_a, r_k, lnx_g, lnx_b, conv_w, conv_b, conv_ln_g, conv_ln_b,
                  w_branch, w_o, norm_x_g, norm_mem_g, w_xq, w_xkv, w_xo, norm_ffn_g,
                  w_router, b_router, w_gu, b_gu, w_dn, b_dn):
    B, S, D = x.shape
    h = rms_norm(x, norm_mix_g)
    proj = h @ w_in
    z_rwkv = proj[..., :RWKV_COLS]
    z_rwkv = z_rwkv + shift_mu * centred_shift(z_rwkv)
    z_conv = proj[..., RWKV_COLS:RWKV_COLS + 2 * CONV_W]
    gates = jax.nn.sigmoid(proj[..., RWKV_COLS + 2 * CONV_W:])
    o_rwkv = rwkv7_bidir(z_rwkv, decay_w0, decay_w2, iclr_a0, iclr_a2, gate_g2,
                         k_k, k_a, r_k, lnx_g, lnx_b)
    o_conv = conformer_conv(z_conv, conv_w, conv_b, conv_ln_g, conv_ln_b)
    y_rwkv = o_rwkv @ w_branch[:RWKV_W]
    y_conv = o_conv @ w_branch[RWKV_W:]
    merged = gates[..., :D_MODEL] * y_rwkv + gates[..., D_MODEL:] * y_conv
    x = x + merged @ w_o
    x = x + memory_cross_attention(rms_norm(x, norm_x_g), rms_norm(mem, norm_mem_g), w_xq, w_xkv, w_xo)
    hf = rms_norm(x, norm_ffn_g).reshape(B * S, D)
    x = x + moe_ffn(hf, w_router, b_router, w_gu, b_gu, w_dn, b_dn).reshape(B, S, D)
    return x


def setup_inputs(seed: int = 0) -> dict:
    key = jax.random.key(seed)
    ks = iter(jax.random.split(key, 48))
    f32 = jnp.float32
    L, D, E, F = DEPTH, D_MODEL, N_EXPERTS, D_EXPERT

    def nrm(shape, scale):
        return jax.random.normal(next(ks), shape, f32) * scale

    def gain(shape):
        return 1.0 + nrm(shape, 0.02)

    return {
        'x_prompt': nrm((BATCH, SEQ, D), 1.0),
        'x_sample': nrm((DEC_BATCH, DEC_SEQ, D), 1.0),
        'mem_prompt': nrm((BATCH, N_MEM, D), 1.0),
        'mem_sample': nrm((DEC_BATCH, N_MEM, D), 1.0),
        'norm_mix_g': gain((L, D)),
        'w_in': nrm((L, D, IN_COLS), D ** -0.5),
        'shift_mu': jax.random.uniform(next(ks), (L, RWKV_COLS), f32),
        'decay_w0': nrm((L, 2, RWKV_W), 0.5),
        'decay_w2': nrm((L, 2, DECAY_LORA, RWKV_W), 0.1 * DECAY_LORA ** -0.5),
        'iclr_a0': nrm((L, 2, RWKV_W), 0.5),
        'iclr_a2': nrm((L, 2, AAA_LORA, RWKV_W), 0.3 * AAA_LORA ** -0.5),
        'gate_g2': nrm((L, GATE_LORA, RWKV_W), GATE_LORA ** -0.5),
        'k_k': 0.85 + nrm((L, RWKV_W), 0.02),
        'k_a': gain((L, RWKV_W)),
        'r_k': nrm((L, RWKV_H, RWKV_HEAD), 0.1),
        'lnx_g': gain((L, RWKV_W)),
        'lnx_b': nrm((L, RWKV_W), 0.02),
        'conv_w': nrm((L, CONV_K, CONV_W), CONV_K ** -0.5),
        'conv_b': nrm((L, CONV_W), 0.02),
        'conv_ln_g': gain((L, CONV_W)),
        'conv_ln_b': nrm((L, CONV_W), 0.02),
        'w_branch': jnp.concatenate([nrm((L, RWKV_W, D), RWKV_W ** -0.5),
                                     nrm((L, CONV_W, D), CONV_W ** -0.5)], axis=1),
        'w_o': nrm((L, D, D), D ** -0.5),
        'norm_x_g': gain((L, D)),
        'norm_mem_g': gain((L, D)),
        'w_xq': nrm((L, D, D), D ** -0.5),
        'w_xkv': nrm((L, D, 2 * D), D ** -0.5),
        'w_xo': nrm((L, D, D), D ** -0.5),
        'norm_ffn_g': gain((L, D)),
        'w_router': nrm((L, D, E), D ** -0.5),
        'b_router': nrm((L, E), 0.01),
        'w_gu': nrm((L, E, D, 2 * F), D ** -0.5),
        'b_gu': nrm((L, E, 2 * F), 0.01),
        'w_dn': nrm((L, E, F, D), F ** -0.5),
        'b_dn': nrm((L, E, D), 0.01),
        'final_g': gain((D,)),
    }


def reference(x_prompt, x_sample, mem_prompt, mem_sample, norm_mix_g, w_in, shift_mu,
              decay_w0, decay_w2, iclr_a0, iclr_a2, gate_g2, k_k, k_a, r_k, lnx_g, lnx_b,
              conv_w, conv_b, conv_ln_g, conv_ln_b, w_branch, w_o, norm_x_g, norm_mem_g,
              w_xq, w_xkv, w_xo, norm_ffn_g, w_router, b_router, w_gu, b_gu, w_dn, b_dn,
              final_g):
    layer_params = (norm_mix_g, w_in, shift_mu, decay_w0, decay_w2, iclr_a0, iclr_a2,
                    gate_g2, k_k, k_a, r_k, lnx_g, lnx_b, conv_w, conv_b, conv_ln_g, conv_ln_b,
                    w_branch, w_o, norm_x_g, norm_mem_g, w_xq, w_xkv, w_xo, norm_ffn_g,
                    w_router, b_router, w_gu, b_gu, w_dn, b_dn)

    def trunk(x, mem):
        for layer in range(DEPTH):
            x = encoder_layer(x, mem, *[p[layer] for p in layer_params])
        return rms_norm(x, final_g)

    y_prompt = trunk(x_prompt, mem_prompt)
    y_sample = trunk(x_sample, mem_sample)
    return (y_prompt, y_sample)
```

```python
import functools
import math

import jax
import jax.numpy as jnp
from jax import lax
from jax.experimental import pallas as pl
from jax.experimental.pallas import tpu as pltpu

F32 = jnp.float32
BF16 = jnp.bfloat16

RWKV_HEAD = 64
DECAY_SCALE = math.exp(-0.5)
LNX_EPS = RWKV_HEAD * 1e-5
RMS_EPS = 1e-5
LN_EPS = 1e-5
XATTN_HEADS = 4
TOP_K = 4
SWIGLU_ALPHA = 1.702
SWIGLU_LIMIT = 7.0

LANES = 128
SUBLANES = 8
WKV_CHUNK = 64
LORA_PAD = 128
VMEM_LIMIT = 56 * 1024 * 1024


def _cparams(sem):
    return pltpu.CompilerParams(dimension_semantics=sem, vmem_limit_bytes=VMEM_LIMIT)


def _tile(n, prefs):
    for p in prefs:
        if n % p == 0:
            return p
    return n


def _dot(a, b):
    return jnp.dot(a, b, preferred_element_type=F32)


def _dot_nt(a, b):
    return lax.dot_general(a, b, (((1,), (1,)), ((), ())), preferred_element_type=F32)


def _dot_tn(a, b):
    return lax.dot_general(a, b, (((0,), (0,)), ((), ())), preferred_element_type=F32)


def _split2(x):
    hi = x.astype(BF16)
    lo = (x - hi.astype(F32)).astype(BF16)
    return hi, lo


def _split3(x):
    hi = x.astype(BF16)
    r1 = x - hi.astype(F32)
    mid = r1.astype(BF16)
    lo = (r1 - mid.astype(F32)).astype(BF16)
    return hi, mid, lo


def _dot_exact_rhs(x, w_bf16):
    h, m, l = _split3(x)
    return _dot(h, w_bf16) + _dot(m, w_bf16) + _dot(l, w_bf16)


def _dot_exact_lhs(w_bf16, x):
    h, m, l = _split3(x)
    return _dot(w_bf16, h) + _dot(w_bf16, m) + _dot(w_bf16, l)


def _dot3(x, w_hi, w_lo):
    xh, xl = _split2(x)
    return _dot(xh, w_hi) + _dot(xl, w_hi) + _dot(xh, w_lo)


def _rms(x, g):
    return x * lax.rsqrt(jnp.mean(x * x, axis=-1, keepdims=True) + RMS_EPS) * g


def _sigmoid(x):
    return 1.0 / (1.0 + jnp.exp(-x))


def _inproj_shift_kernel(x_ref, xp_ref, xn_ref, g_ref, w_ref, mu_ref, o_ref, h_scr, hp_scr, hn_scr,
                         *, seq_len):
    i = pl.program_id(0)
    j = pl.program_id(1)
    tm = x_ref.shape[0]

    @pl.when(j == 0)
    def _():
        g = g_ref[...]
        h_scr[...] = _rms(x_ref[...], g).astype(BF16)
        hp_scr[...] = _rms(xp_ref[...], g).astype(BF16)
        hn_scr[...] = _rms(xn_ref[...], g).astype(BF16)

    w = w_ref[...]
    p = _dot(h_scr[...], w)
    pp = _dot(hp_scr[...], w)[SUBLANES - 1:SUBLANES, :]
    pn = _dot(hn_scr[...], w)[0:1, :]
    first = (i * tm) % seq_len == 0
    last = ((i + 1) * tm) % seq_len == 0
    pp = jnp.where(first, 0.0, pp)
    pn = jnp.where(last, 0.0, pn)
    rid = lax.broadcasted_iota(jnp.int32, p.shape, 0)
    up = jnp.where(rid == 0, pp, pltpu.roll(p, 1, 0))
    dn = jnp.where(rid == tm - 1, pn, pltpu.roll(p, tm - 1, 0))
    o_ref[...] = p + mu_ref[...] * (0.5 * (up + dn) - p)


def _inproj_shift(x, g, w_bf16, mu, seq_len, tn):
    T, D = x.shape
    N = w_bf16.shape[1]
    tm = _tile(seq_len, (512, 256, 128, 64, 32, 16, 8))
    nb8 = T // SUBLANES
    r8 = tm // SUBLANES
    return pl.pallas_call(
        functools.partial(_inproj_shift_kernel, seq_len=seq_len),
        grid=(T // tm, N // tn),
        in_specs=[
            pl.BlockSpec((tm, D), lambda i, j: (i, 0)),
            pl.BlockSpec((SUBLANES, D), lambda i, j: (jnp.maximum(i * r8 - 1, 0), 0)),
            pl.BlockSpec((SUBLANES, D), lambda i, j: (jnp.minimum((i + 1) * r8, nb8 - 1), 0)),
            pl.BlockSpec((1, D), lambda i, j: (0, 0)),
            pl.BlockSpec((D, tn), lambda i, j: (0, j)),
            pl.BlockSpec((1, tn), lambda i, j: (0, j)),
        ],
        out_specs=pl.BlockSpec((tm, tn), lambda i, j: (i, j)),
        out_shape=jax.ShapeDtypeStruct((T, N), F32),
        scratch_shapes=[pltpu.VMEM((tm, D), BF16), pltpu.VMEM((SUBLANES, D), BF16),
                        pltpu.VMEM((SUBLANES, D), BF16)],
        compiler_params=_cparams(("parallel", "arbitrary")),
        name="inproj_shift",
    )(x, x, x, g, w_bf16, mu)


def _inproj_glu_kernel(x_ref, g_ref, wa_ref, wb_ref, o_ref, h_scr):
    @pl.when(pl.program_id(1) == 0)
    def _():
        h_scr[...] = _rms(x_ref[...], g_ref[...]).astype(BF16)

    h = h_scr[...]
    a = _dot(h, wa_ref[...])
    b = _dot(h, wb_ref[...])
    o_ref[...] = a * _sigmoid(b)


def _inproj_glu(x, g, wa, wb):
    T, D = x.shape
    N = wa.shape[1]
    tm = _tile(T, (512, 256, 128, 64, 32, 16, 8))
    tn = _tile(N, (512, 256, 128))
    return pl.pallas_call(
        _inproj_glu_kernel,
        grid=(T // tm, N // tn),
        in_specs=[
            pl.BlockSpec((tm, D), lambda i, j: (i, 0)),
            pl.BlockSpec((1, D), lambda i, j: (0, 0)),
            pl.BlockSpec((D, tn), lambda i, j: (0, j)),
            pl.BlockSpec((D, tn), lambda i, j: (0, j)),
        ],
        out_specs=pl.BlockSpec((tm, tn), lambda i, j: (i, j)),
        out_shape=jax.ShapeDtypeStruct((T, N), F32),
        scratch_shapes=[pltpu.VMEM((tm, D), BF16)],
        compiler_params=_cparams(("parallel", "arbitrary")),
        name="inproj_glu",
    )(x, g, wa, wb)


def _wkv_kernel(zr_ref, zk_ref, zv_ref, dd_ref, ad_ref, w0_ref, w2h_ref, w2l_ref, a0_ref, a2h_ref,
                a2l_ref, kk_ref, ka_ref, rk_ref, o_ref, bonus_ref, state_ref, *, n_pairs):
    L = WKV_CHUNK
    PW = 2 * RWKV_HEAD
    d = pl.program_id(0)
    c = pl.program_id(3)
    n_chunks = zr_ref.shape[0] // L
    bwd = d == 1

    @pl.when(c == 0)
    def _():
        state_ref[...] = jnp.zeros_like(state_ref)

    sgn = jnp.where(bwd, -1, 1)
    row = lax.broadcasted_iota(jnp.int32, (L, 2 * L), 0)
    col = lax.broadcasted_iota(jnp.int32, (L, 2 * L), 1) % L
    strict = (row - col) * sgn > 0
    incl = (row - col) * sgn >= 0
    trow = lax.broadcasted_iota(jnp.int32, (L, L), 0)
    tcol = lax.broadcasted_iota(jnp.int32, (L, L), 1)
    tri = jnp.where((trow - tcol) * sgn >= 0, 1.0, 0.0).astype(BF16)
    lane = lax.broadcasted_iota(jnp.int32, (1, PW), 1)
    m0 = lane < RWKV_HEAD
    lane2 = lax.broadcasted_iota(jnp.int32, (1, 2 * PW), 1) % PW
    m0w = lane2 < RWKV_HEAD
    srow = lax.broadcasted_iota(jnp.int32, (PW, PW), 0)
    scol = lax.broadcasted_iota(jnp.int32, (PW, PW), 1)
    same_head = (srow // RWKV_HEAD) == (scol // RWKV_HEAD)
    eye = srow == scol
    head_ones = jnp.where(same_head, 1.0, 0.0).astype(BF16)

    def bd(x):
        return jnp.concatenate([jnp.where(m0, x, 0.0), jnp.where(m0, 0.0, x)], axis=0).astype(BF16)

    def bd2(x):
        return jnp.concatenate([jnp.where(m0w, x, 0.0), jnp.where(m0w, 0.0, x)], axis=0).astype(BF16)

    w0 = w0_ref[...]
    a0 = a0_ref[...]
    k_k = kk_ref[...]
    k_a = ka_ref[...]
    r_k = rk_ref[...]

    def chunk(ci, carry):
        cidx = jnp.where(bwd, n_chunks - 1 - ci, ci)
        rows = pl.ds(pl.multiple_of(cidx * L, L), L)
        r = zr_ref[rows, :]
        k = zk_ref[rows, :]
        v = zv_ref[rows, :]
        dlin = w0 + _dot3(jnp.tanh(dd_ref[rows, :]), w2h_ref[...], w2l_ref[...])
        lw = -DECAY_SCALE * _sigmoid(dlin)
        iclr = _sigmoid(a0 + _dot3(ad_ref[rows, :], a2h_ref[...], a2l_ref[...]))
        kkr = k * k_k
        kmod = k * (1.0 + (iclr - 1.0) * k_a)
        cum = _dot_exact_lhs(tri, lw)
        tot = jnp.sum(lw, axis=0, keepdims=True)
        e_in = jnp.exp(cum)
        e_ex = jnp.exp(cum - lw)
        e_ng = jnp.exp(-cum)
        e_rm = jnp.exp(tot - cum)
        e_tot = jnp.exp(tot)
        rkr = r * kmod * r_k
        outs = []
        bons = []
        for p in range(n_pairs):
            sl = slice(p * PW, (p + 1) * PW)
            kkr_p = kkr[:, sl]
            ss = _dot_exact_rhs(kkr_p * kkr_p, head_ones)
            kk = kkr_p / jnp.maximum(jnp.sqrt(ss), 1e-12)
            bons.append(_dot_exact_rhs(rkr[:, sl], head_ones) * v[:, sl])
            b_p = kk * iclr[:, sl]
            ah = -kk * e_ex[:, sl]
            rh = r[:, sl] * e_in[:, sl]
            bh = b_p * e_ng[:, sl]
            kh = kmod[:, sl] * e_ng[:, sl]
            bt = b_p * e_rm[:, sl]
            kt = kmod[:, sl] * e_rm[:, sl]
            v_p = v[:, sl]

            lhs = jnp.concatenate([ah, rh], axis=0).astype(BF16)
            gb = _dot_nt(lhs, bd(bh))
            gk = _dot_nt(lhs, bd(kh))
            a_m = jnp.where(strict, gb[:L], 0.0)
            ak_m = jnp.where(strict, gk[:L], 0.0)
            rb_m = jnp.where(incl, gb[L:], 0.0).astype(BF16)
            rk_m = jnp.where(incl, gk[L:], 0.0).astype(BF16)
            bdv = bd(v_p)
            y = jnp.concatenate([ah, _dot(ak_m.astype(BF16), bdv)], axis=1)
            a_i = a_m
            for lvl in range(6):
                a_b = a_i.astype(BF16)
                y = y + _dot(a_b, bd2(y))
                if lvl < 5:
                    a_i = _dot(a_b, bd(a_i))
            ry = _dot(rb_m, bd2(y))
            rt = rh + ry[:, :PW]
            ob = ry[:, PW:] + _dot(rk_m, bdv)
            lhs_t = jnp.concatenate([bt, kt], axis=0).astype(BF16)
            rhs_t = jnp.concatenate(
                [y, jnp.concatenate([jnp.zeros((L, PW), F32), v_p], axis=1)], axis=0).astype(BF16)
            mn = _dot_tn(lhs_t, rhs_t)
            mm_t = jnp.where(eye, e_tot[:, sl], 0.0) + jnp.where(same_head, mn[:, :PW], 0.0)
            nn_t = jnp.where(same_head, mn[:, PW:], 0.0)
            s_b = state_ref[p].astype(BF16)
            outs.append(_dot(rt.astype(BF16), s_b) + ob)
            state_ref[p] = _dot(mm_t.astype(BF16), s_b) + nn_t
        o_ref[rows, :] = jnp.concatenate(outs, axis=1) if n_pairs > 1 else outs[0]
        bonus_ref[rows, :] = jnp.concatenate(bons, axis=1) if n_pairs > 1 else bons[0]
        return carry

    lax.fori_loop(0, n_chunks, chunk, 0)


def _wkv(z_rkv, z_lora, decay_w0, w2h, w2l, iclr_a0, a2h, a2l, k_k, k_a, r_k, n_seq, seq_len):
    T = z_rkv.shape[0]
    W = z_rkv.shape[1] // 3
    PW = 2 * RWKV_HEAD
    n_pairs = _tile(W // PW, (4, 2, 1))
    GW = n_pairs * PW
    n_groups = W // GW
    TT = _tile(seq_len, (256, 128, 64))
    nT = seq_len // TT

    def tb(d, b, c):
        return b * nT + c + d * (nT - 1 - 2 * c)

    zspec = lambda off: pl.BlockSpec((TT, GW), lambda d, b, g, c: (tb(d, b, c), off * n_groups + g))
    pspec = pl.BlockSpec((None, 1, GW), lambda d, b, g, c: (d, 0, g))
    lspec = pl.BlockSpec((None, LORA_PAD, GW), lambda d, b, g, c: (d, 0, g))
    cspec = pl.BlockSpec((1, GW), lambda d, b, g, c: (0, g))
    ospec = pl.BlockSpec((None, TT, GW), lambda d, b, g, c: (d, tb(d, b, c), g))
    return pl.pallas_call(
        functools.partial(_wkv_kernel, n_pairs=n_pairs),
        grid=(2, n_seq, n_groups, nT),
        in_specs=[
            zspec(0), zspec(1), zspec(2),
            pl.BlockSpec((TT, LORA_PAD), lambda d, b, g, c: (tb(d, b, c), d)),
            pl.BlockSpec((TT, LORA_PAD), lambda d, b, g, c: (tb(d, b, c), 2 + d)),
            pspec, lspec, lspec, pspec, lspec, lspec, cspec, cspec, cspec,
        ],
        out_specs=[ospec, ospec],
        out_shape=[jax.ShapeDtypeStruct((2, T, W), F32), jax.ShapeDtypeStruct((2, T, W), F32)],
        scratch_shapes=[pltpu.VMEM((n_pairs, PW, PW), F32)],
        compiler_params=_cparams(("parallel", "parallel", "parallel", "arbitrary")),
        name="wkv_scan",
    )(z_rkv, z_rkv, z_rkv, z_lora, z_lora, decay_w0, w2h, w2l, iclr_a0, a2h, a2l, k_k, k_a, r_k)


def _rwkv_post_kernel(o_ref, bonus_ref, gd_ref, g2_ref, lg_ref, lb_ref, out_ref):
    W = out_ref.shape[1]
    PW = 2 * RWKV_HEAD
    srow = lax.broadcasted_iota(jnp.int32, (PW, PW), 0)
    scol = lax.broadcasted_iota(jnp.int32, (PW, PW), 1)
    head_mean = jnp.where((srow // RWKV_HEAD) == (scol // RWKV_HEAD), 1.0, 0.0).astype(BF16)
    inv = 1.0 / RWKV_HEAD
    gate = _dot(_sigmoid(gd_ref[...]).astype(BF16), g2_ref[...])
    for p in range(W // PW):
        sl = slice(p * PW, (p + 1) * PW)
        o = o_ref[0, :, sl] + o_ref[1, :, sl]
        mean = _dot_exact_rhs(o, head_mean) * inv
        oc = o - mean
        var = _dot_exact_rhs(oc * oc, head_mean) * inv
        y = oc * lax.rsqrt(var + LNX_EPS) * lg_ref[:, sl] + lb_ref[:, sl]
        y = y + bonus_ref[0, :, sl] + bonus_ref[1, :, sl]
        out_ref[:, sl] = (y * gate[:, sl]).astype(out_ref.dtype)


def _rwkv_post(o, bonus, z_lora, g2, lnx_g, lnx_b):
    _, T, W = o.shape
    tm = _tile(T, (256, 128, 64, 32, 16, 8))
    GL = g2.shape[0]
    gd_blk = (4 * LORA_PAD) // GL
    return pl.pallas_call(
        _rwkv_post_kernel,
        grid=(T // tm,),
        in_specs=[
            pl.BlockSpec((2, tm, W), lambda i: (0, i, 0)),
            pl.BlockSpec((2, tm, W), lambda i: (0, i, 0)),
            pl.BlockSpec((tm, GL), lambda i: (i, gd_blk)),
            pl.BlockSpec((GL, W), lambda i: (0, 0)),
            pl.BlockSpec((1, W), lambda i: (0, 0)),
            pl.BlockSpec((1, W), lambda i: (0, 0)),
        ],
        out_specs=pl.BlockSpec((tm, W), lambda i: (i, 0)),
        out_shape=jax.ShapeDtypeStruct((T, W), BF16),
        compiler_params=_cparams(("parallel",)),
        name="rwkv_post",
    )(o, bonus, z_lora, g2, lnx_g, lnx_b)


def _conv_kernel(u_ref, up_ref, un_ref, w_ref, b_ref, lg_ref, lb_ref, o_ref, ext_scr, *, seq_len, halo):
    i = pl.program_id(0)
    tm = u_ref.shape[0]
    K = w_ref.shape[0]
    first = (i * tm) % seq_len == 0
    last = ((i + 1) * tm) % seq_len == 0
    ext_scr[0:halo, :] = jnp.where(first, 0.0, up_ref[...])
    ext_scr[halo:halo + tm, :] = u_ref[...]
    ext_scr[halo + tm:halo + tm + halo, :] = jnp.where(last, 0.0, un_ref[...])
    sub = min(tm, 32)
    base = halo - K // 2
    for s in range(tm // sub):
        acc = jnp.zeros((sub, u_ref.shape[1]), F32)
        for j in range(K):
            acc = acc + w_ref[j:j + 1, :] * ext_scr[base + s * sub + j:base + s * sub + j + sub, :]
        acc = acc + b_ref[...]
        mean = jnp.mean(acc, axis=-1, keepdims=True)
        xc = acc - mean
        var = jnp.mean(xc * xc, axis=-1, keepdims=True)
        y = xc * lax.rsqrt(var + LN_EPS) * lg_ref[...] + lb_ref[...]
        o_ref[s * sub:(s + 1) * sub, :] = (y * _sigmoid(y)).astype(o_ref.dtype)


def _conv(u, conv_w, conv_b, ln_g, ln_b, seq_len):
    T, C = u.shape
    K = conv_w.shape[0]
    halo = 16
    assert K // 2 <= halo
    tm = _tile(seq_len, (128, 64, 32, 16))
    rh = tm // halo
    nbh = T // halo
    return pl.pallas_call(
        functools.partial(_conv_kernel, seq_len=seq_len, halo=halo),
        grid=(T // tm,),
        in_specs=[
            pl.BlockSpec((tm, C), lambda i: (i, 0)),
            pl.BlockSpec((halo, C), lambda i: (jnp.maximum(i * rh - 1, 0), 0)),
            pl.BlockSpec((halo, C), lambda i: (jnp.minimum((i + 1) * rh, nbh - 1), 0)),
            pl.BlockSpec((K, C), lambda i: (0, 0)),
            pl.BlockSpec((1, C), lambda i: (0, 0)),
            pl.BlockSpec((1, C), lambda i: (0, 0)),
            pl.BlockSpec((1, C), lambda i: (0, 0)),
        ],
        out_specs=pl.BlockSpec((tm, C), lambda i: (i, 0)),
        out_shape=jax.ShapeDtypeStruct((T, C), BF16),
        scratch_shapes=[pltpu.VMEM((tm + 2 * halo, C), F32)],
        compiler_params=_cparams(("parallel",)),
        name="conformer_conv",
    )(u, u, u, conv_w, conv_b, ln_g, ln_b)


def _merge_kernel(x_ref, g_ref, orw_ref, ocv_ref, wg1_ref, wg2_ref, wb1_ref, wb2_ref, o_ref, h_scr):
    @pl.when(pl.program_id(1) == 0)
    def _():
        h_scr[...] = _rms(x_ref[...], g_ref[...]).astype(BF16)

    h = h_scr[...]
    g1 = _sigmoid(_dot(h, wg1_ref[...]))
    g2 = _sigmoid(_dot(h, wg2_ref[...]))
    y1 = _dot(orw_ref[...], wb1_ref[...])
    y2 = _dot(ocv_ref[...], wb2_ref[...])
    o_ref[...] = (g1 * y1 + g2 * y2).astype(o_ref.dtype)


def _merge(x, g, o_rwkv, o_conv, wg1, wg2, wb1, wb2):
    T, D = x.shape
    W = o_rwkv.shape[1]
    C = o_conv.shape[1]
    tm = _tile(T, (512, 256, 128, 64, 32, 16, 8))
    tn = _tile(D, (512, 256, 128))
    return pl.pallas_call(
        _merge_kernel,
        grid=(T // tm, D // tn),
        in_specs=[
            pl.BlockSpec((tm, D), lambda i, j: (i, 0)),
            pl.BlockSpec((1, D), lambda i, j: (0, 0)),
            pl.BlockSpec((tm, W), lambda i, j: (i, 0)),
            pl.BlockSpec((tm, C), lambda i, j: (i, 0)),
            pl.BlockSpec((D, tn), lambda i, j: (0, j)),
            pl.BlockSpec((D, tn), lambda i, j: (0, j)),
            pl.BlockSpec((W, tn), lambda i, j: (0, j)),
            pl.BlockSpec((C, tn), lambda i, j: (0, j)),
        ],
        out_specs=pl.BlockSpec((tm, tn), lambda i, j: (i, j)),
        out_shape=jax.ShapeDtypeStruct((T, D), BF16),
        scratch_shapes=[pltpu.VMEM((tm, D), BF16)],
        compiler_params=_cparams(("parallel", "arbitrary")),
        name="merge_gates",
    )(x, g, o_rwkv, o_conv, wg1, wg2, wb1, wb2)


def _mm_res_kernel(a_ref, w_ref, res_ref, o_ref):
    o_ref[...] = res_ref[...] + _dot(a_ref[...], w_ref[...])


def _mm_res(a, w, res):
    T, K = a.shape
    N = w.shape[1]
    tm = _tile(T, (1024, 512, 256, 128, 64, 32, 16, 8))
    tn = _tile(N, (1024, 512, 256, 128))
    return pl.pallas_call(
        _mm_res_kernel,
        grid=(T // tm, N // tn),
        in_specs=[
            pl.BlockSpec((tm, K), lambda i, j: (i, 0)),
            pl.BlockSpec((K, tn), lambda i, j: (0, j)),
            pl.BlockSpec((tm, tn), lambda i, j: (i, j)),
        ],
        out_specs=pl.BlockSpec((tm, tn), lambda i, j: (i, j)),
        out_shape=jax.ShapeDtypeStruct((T, N), F32),
        compiler_params=_cparams(("parallel", "parallel")),
        name="proj_residual",
    )(a, w, res)


def _mm_norm_kernel(x_ref, g_ref, w_ref, o_ref, h_scr):
    @pl.when(pl.program_id(1) == 0)
    def _():
        h_scr[...] = _rms(x_ref[...], g_ref[...]).astype(BF16)

    o_ref[...] = _dot(h_scr[...], w_ref[...]).astype(o_ref.dtype)


def _mm_norm(x, g, w):
    T, D = x.shape
    N = w.shape[1]
    tm = _tile(T, (512, 256, 128, 64, 32, 16, 8))
    tn = _tile(N, (1024, 512, 256, 128))
    return pl.pallas_call(
        _mm_norm_kernel,
        grid=(T // tm, N // tn),
        in_specs=[
            pl.BlockSpec((tm, D), lambda i, j: (i, 0)),
            pl.BlockSpec((1, D), lambda i, j: (0, 0)),
            pl.BlockSpec((D, tn), lambda i, j: (0, j)),
        ],
        out_specs=pl.BlockSpec((tm, tn), lambda i, j: (i, j)),
        out_shape=jax.ShapeDtypeStruct((T, N), BF16),
        scratch_shapes=[pltpu.VMEM((tm, D), BF16)],
        compiler_params=_cparams(("parallel", "arbitrary")),
        name="norm_proj",
    )(x, g, w)


def _xattn_kernel(q_ref, k_ref, v_ref, o_ref):
    D = q_ref.shape[1]
    hd = D // XATTN_HEADS
    scale = hd ** -0.5
    for h in range(XATTN_HEADS):
        sl = slice(h * hd, (h + 1) * hd)
        s = _dot_nt(q_ref[:, sl], k_ref[:, sl]) * scale
        s = s - jnp.max(s, axis=-1, keepdims=True)
        e = jnp.exp(s)
        p = e / jnp.sum(e, axis=-1, keepdims=True)
        o_ref[:, sl] = _dot(p.astype(BF16), v_ref[:, sl]).astype(o_ref.dtype)


def _xattn(q, kv, n_seq, seq_len, n_mem):
    T, D = q.shape
    tm = _tile(seq_len, (512, 256, 128, 64, 32, 16, 8))
    nT = seq_len // tm
    return pl.pallas_call(
        _xattn_kernel,
        grid=(n_seq, nT),
        in_specs=[
            pl.BlockSpec((tm, D), lambda b, i: (b * nT + i, 0)),
            pl.BlockSpec((n_mem, D), lambda b, i: (b, 0)),
            pl.BlockSpec((n_mem, D), lambda b, i: (b, 1)),
        ],
        out_specs=pl.BlockSpec((tm, D), lambda b, i: (b * nT + i, 0)),
        out_shape=jax.ShapeDtypeStruct((T, D), BF16),
        compiler_params=_cparams(("parallel", "parallel")),
        name="cross_attention",
    )(q, kv, kv)


def _router_kernel(x_ref, g_ref, wh_ref, wl_ref, b_ref, hf_ref, e_ref, gate_ref):
    hf = _rms(x_ref[...], g_ref[...])
    hf_ref[...] = hf.astype(hf_ref.dtype)
    logits = _dot3(hf, wh_ref[...], wl_ref[...]) + b_ref[...]
    E = logits.shape[1]
    eid = lax.broadcasted_iota(jnp.int32, logits.shape, 1).astype(F32)
    work = logits
    vals = []
    idxs = []
    for _ in range(TOP_K):
        m = jnp.max(work, axis=-1, keepdims=True)
        idx = jnp.min(jnp.where(work == m, eid, float(E)), axis=-1, keepdims=True)
        vals.append(m)
        idxs.append(idx.astype(jnp.int32))
        work = jnp.where(eid == idx, -jnp.inf, work)
    ex = [jnp.exp(vv - vals[0]) for vv in vals]
    den = ex[0]
    for t in ex[1:]:
        den = den + t
    kid = lax.broadcasted_iota(jnp.int32, (logits.shape[0], TOP_K), 1)
    e_out = jnp.zeros((logits.shape[0], TOP_K), jnp.int32)
    g_out = jnp.zeros((logits.shape[0], TOP_K), F32)
    for t in range(TOP_K):
        e_out = jnp.where(kid == t, idxs[t], e_out)
        g_out = jnp.where(kid == t, ex[t] / den, g_out)
    e_ref[...] = e_out
    gate_ref[...] = g_out


def _router(x, g, w_router, b_router):
    T, D = x.shape
    E = w_router.shape[1]
    wh, wl = _split2(w_router)
    tm = _tile(T, (256, 128, 64, 32, 16, 8))
    return pl.pallas_call(
        _router_kernel,
        grid=(T // tm,),
        in_specs=[
            pl.BlockSpec((tm, D), lambda i: (i, 0)),
            pl.BlockSpec((1, D), lambda i: (0, 0)),
            pl.BlockSpec((D, E), lambda i: (0, 0)),
            pl.BlockSpec((D, E), lambda i: (0, 0)),
            pl.BlockSpec((1, E), lambda i: (0, 0)),
        ],
        out_specs=[
            pl.BlockSpec((tm, D), lambda i: (i, 0)),
            pl.BlockSpec((tm, TOP_K), lambda i: (i, 0)),
            pl.BlockSpec((tm, TOP_K), lambda i: (i, 0)),
        ],
        out_shape=[jax.ShapeDtypeStruct((T, D), BF16),
                   jax.ShapeDtypeStruct((T, TOP_K), jnp.int32),
                   jax.ShapeDtypeStruct((T, TOP_K), F32)],
        compiler_params=_cparams(("parallel",)),
        name="router",
    )(x, g, wh, wl, b_router)


def _expert_kernel(te_ref, tv_ref, x_ref, wg_ref, wu_ref, bg_ref, bu_ref, wd_ref, bd_ref, o_ref, acc_ref):
    i = pl.program_id(0)
    f = pl.program_id(1)
    nf = pl.num_programs(1)

    @pl.when(tv_ref[i] > 0)
    def _():
        x = x_ref[...]
        g = _dot(x, wg_ref[...].astype(BF16)) + bg_ref[...]
        u = _dot(x, wu_ref[...].astype(BF16)) + bu_ref[...]
        g = jnp.minimum(g, SWIGLU_LIMIT)
        u = jnp.clip(u, -SWIGLU_LIMIT, SWIGLU_LIMIT)
        act = (u + 1.0) * (g * _sigmoid(SWIGLU_ALPHA * g))
        part = _dot(act.astype(BF16), wd_ref[...].astype(BF16))

        @pl.when(f == 0)
        def _():
            acc_ref[...] = part + bd_ref[...]

        @pl.when(f > 0)
        def _():
            acc_ref[...] += part

        @pl.when(f == nf - 1)
        def _():
            o_ref[...] = acc_ref[...].astype(o_ref.dtype)

    @pl.when((tv_ref[i] == 0) & (f == nf - 1))
    def _():
        o_ref[...] = jnp.zeros_like(o_ref)


def _experts(xb, tile_e, tile_valid, w_gu, b_gu, w_dn, b_dn, tm):
    R, D = xb.shape
    E, _, F2 = w_gu.shape
    F = F2 // 2
    tf = _tile(F, (256, 128))
    nf = F // tf
    n_tiles = R // tm
    b_gu3 = b_gu.reshape(E, 1, F2)
    b_dn3 = b_dn.reshape(E, 1, D)

    def ff(i, f, tv):
        return jnp.where(tv[i] > 0, f, nf - 1)

    grid_spec = pltpu.PrefetchScalarGridSpec(
        num_scalar_prefetch=2,
        grid=(n_tiles, nf),
        in_specs=[
            pl.BlockSpec((tm, D), lambda i, f, te, tv: (i, 0)),
            pl.BlockSpec((None, D, tf), lambda i, f, te, tv: (te[i], 0, ff(i, f, tv))),
            pl.BlockSpec((None, D, tf), lambda i, f, te, tv: (te[i], 0, nf + ff(i, f, tv))),
            pl.BlockSpec((None, 1, tf), lambda i, f, te, tv: (te[i], 0, ff(i, f, tv))),
            pl.BlockSpec((None, 1, tf), lambda i, f, te, tv: (te[i], 0, nf + ff(i, f, tv))),
            pl.BlockSpec((None, tf, D), lambda i, f, te, tv: (te[i], ff(i, f, tv), 0)),
            pl.BlockSpec((None, 1, D), lambda i, f, te, tv: (te[i], 0, 0)),
        ],
        out_specs=pl.BlockSpec((tm, D), lambda i, f, te, tv: (i, 0)),
        scratch_shapes=[pltpu.VMEM((tm, D), F32)],
    )
    return pl.pallas_call(
        _expert_kernel,
        grid_spec=grid_spec,
        out_shape=jax.ShapeDtypeStruct((R, D), BF16),
        compiler_params=_cparams(("arbitrary", "arbitrary")),
        name="moe_experts",
    )(tile_e, tile_valid, xb, w_gu, w_gu, b_gu3, b_gu3, w_dn, b_dn3)


def _combine_kernel(x_ref, y_ref, gate_ref, g_ref, o_ref):
    D = x_ref.shape[1]
    acc = x_ref[...]
    gate = gate_ref[...]
    for t in range(TOP_K):
        acc = acc + gate[:, t:t + 1] * y_ref[:, t * D:(t + 1) * D].astype(F32)
    o_ref[...] = _rms(acc, g_ref[...])


def _combine(x, y4, gate, final_g):
    T, D = x.shape
    tm = _tile(T, (256, 128, 64, 32, 16, 8))
    return pl.pallas_call(
        _combine_kernel,
        grid=(T // tm,),
        in_specs=[
            pl.BlockSpec((tm, D), lambda i: (i, 0)),
            pl.BlockSpec((tm, TOP_K * D), lambda i: (i, 0)),
            pl.BlockSpec((tm, TOP_K), lambda i: (i, 0)),
            pl.BlockSpec((1, D), lambda i: (0, 0)),
        ],
        out_specs=pl.BlockSpec((tm, D), lambda i: (i, 0)),
        out_shape=jax.ShapeDtypeStruct((T, D), F32),
        compiler_params=_cparams(("parallel",)),
        name="moe_combine_norm",
    )(x, y4, gate, final_g)


def _pad_rows(w, n):
    return jnp.pad(w, ((0, 0),) * (w.ndim - 2) + ((0, n - w.shape[-2]), (0, 0)))


def _pad_cols(w, n):
    return jnp.pad(w, ((0, 0),) * (w.ndim - 1) + ((0, n - w.shape[-1]),))


def _moe_tile_rows(n_assign, n_experts):
    for tm in (1024, 512, 256, 128, 64, 32, 16, 8):
        if n_assign >= 4 * n_experts * tm or tm == 8:
            return tm


def _layer(x, mem, n_seq, seq_len, norm_mix_g, w_in, shift_mu, decay_w0, decay_w2, iclr_a0, iclr_a2,
           gate_g2, k_k, k_a, r_k, lnx_g, lnx_b, conv_w, conv_b, conv_ln_g, conv_ln_b, w_branch, w_o,
           norm_x_g, norm_mem_g, w_xq, w_xkv, w_xo, norm_ffn_g, w_router, b_router, w_gu, b_gu,
           w_dn, b_dn, final_g):
    T, D = x.shape
    W = k_k.shape[0]
    DL = decay_w2.shape[1]
    AL = iclr_a2.shape[1]
    GL = gate_g2.shape[0]
    C = conv_w.shape[1]
    E = w_router.shape[1]
    n_mem = mem.shape[0] // n_seq
    row = lambda v: v.reshape(1, -1)

    o3 = 3 * W
    o4 = o3 + 2 * DL
    o5 = o4 + 2 * AL
    o6 = o5 + GL
    seg = lambda m, a, b, n: _pad_cols(m[..., a:b], n)
    lora_cols = lambda m: jnp.concatenate(
        [seg(m, o3, o3 + DL, LORA_PAD), seg(m, o3 + DL, o4, LORA_PAD), seg(m, o4, o4 + AL, LORA_PAD),
         seg(m, o4 + AL, o5, LORA_PAD), m[..., o5:o6]], axis=-1)
    w_rkv = w_in[:, :o3].astype(BF16)
    w_lora = lora_cols(w_in).astype(BF16)
    mu_rkv = row(shift_mu[:o3])
    mu_lora = row(lora_cols(shift_mu))
    w_ca = w_in[:, o6:o6 + C].astype(BF16)
    w_cb = w_in[:, o6 + C:o6 + 2 * C].astype(BF16)
    w_g1 = w_in[:, o6 + 2 * C:o6 + 2 * C + D].astype(BF16)
    w_g2 = w_in[:, o6 + 2 * C + D:].astype(BF16)
    g_mix = row(norm_mix_g)

    z_rkv = _inproj_shift(x, g_mix, w_rkv, mu_rkv, seq_len, _tile(o3, (1024, 512, 256, 128)))
    z_lora = _inproj_shift(x, g_mix, w_lora, mu_lora, seq_len, w_lora.shape[1])
    u = _inproj_glu(x, g_mix, w_ca, w_cb)

    w2h, w2l = _split2(_pad_rows(decay_w2, LORA_PAD))
    a2h, a2l = _split2(_pad_rows(iclr_a2, LORA_PAD))
    o_dir, bonus = _wkv(z_rkv, z_lora, decay_w0.reshape(2, 1, W), w2h, w2l, iclr_a0.reshape(2, 1, W),
                        a2h, a2l, row(k_k), row(k_a), row(r_k), n_seq, seq_len)
    o_rwkv = _rwkv_post(o_dir, bonus, z_lora, gate_g2.astype(BF16), row(lnx_g), row(lnx_b))
    o_conv = _conv(u, conv_w, row(conv_b), row(conv_ln_g), row(conv_ln_b), seq_len)

    merged = _merge(x, g_mix, o_rwkv, o_conv, w_g1, w_g2, w_branch[:W].astype(BF16),
                    w_branch[W:].astype(BF16))
    x1 = _mm_res(merged, w_o.astype(BF16), x)

    q = _mm_norm(x1, row(norm_x_g), w_xq.astype(BF16))
    kv = _mm_norm(mem, row(norm_mem_g), w_xkv.astype(BF16))
    att = _xattn(q, kv, n_seq, seq_len, n_mem)
    x2 = _mm_res(att, w_xo.astype(BF16), x1)

    hf, top_e, gate = _router(x2, row(norm_ffn_g), w_router, row(b_router))

    A = T * TOP_K
    tm_e = _moe_tile_rows(A, E)
    n_tiles = (A + E * (tm_e - 1) + tm_e - 1) // tm_e
    flat_e = top_e.reshape(A)
    order = jnp.argsort(flat_e).astype(jnp.int32)
    sorted_e = flat_e[order]
    counts = jnp.bincount(flat_e, length=E).astype(jnp.int32)
    padded = (counts + tm_e - 1) // tm_e * tm_e
    start = jnp.cumsum(counts) - counts
    pad_end = jnp.cumsum(padded)
    pad_start = pad_end - padded
    dest = pad_start[sorted_e] + jnp.arange(A, dtype=jnp.int32) - start[sorted_e]
    row_tok = jnp.zeros((n_tiles * tm_e,), jnp.int32).at[dest].set(order // TOP_K)
    dest_of = jnp.zeros((A,), jnp.int32).at[order].set(dest)
    tile_start = jnp.arange(n_tiles, dtype=jnp.int32) * tm_e
    tile_e = jnp.minimum(jnp.searchsorted(pad_end, tile_start, side='right'), E - 1).astype(jnp.int32)
    tile_valid = jnp.clip(pad_start[tile_e] + counts[tile_e] - tile_start, 0, tm_e).astype(jnp.int32)

    xb = jnp.take(hf, row_tok, axis=0)
    yb = _experts(xb, tile_e, tile_valid, w_gu, b_gu, w_dn, b_dn, tm_e)
    y4 = jnp.take(yb, dest_of, axis=0).reshape(T, TOP_K * D)
    return _combine(x2, y4, gate, row(final_g))


def kernel(x_prompt, x_sample, mem_prompt, mem_sample, norm_mix_g, w_in, shift_mu, decay_w0, decay_w2,
           iclr_a0, iclr_a2, gate_g2, k_k, k_a, r_k, lnx_g, lnx_b, conv_w, conv_b, conv_ln_g, conv_ln_b,
           w_branch, w_o, norm_x_g, norm_mem_g, w_xq, w_xkv, w_xo, norm_ffn_g, w_router, b_router,
           w_gu, b_gu, w_dn, b_dn, final_g):
    layer_params = (norm_mix_g, w_in, shift_mu, decay_w0, decay_w2, iclr_a0, iclr_a2, gate_g2, k_k, k_a,
                    r_k, lnx_g, lnx_b, conv_w, conv_b, conv_ln_g, conv_ln_b, w_branch, w_o, norm_x_g,
                    norm_mem_g, w_xq, w_xkv, w_xo, norm_ffn_g, w_router, b_router, w_gu, b_gu, w_dn, b_dn)
    assert all(p.shape[0] == 1 for p in layer_params), "single-layer stack expected"
    bp, seq_len, D = x_prompt.shape
    bs = x_sample.shape[0]
    assert x_sample.shape[1] == seq_len
    n_seq = bp + bs
    x = jnp.concatenate([x_prompt, x_sample], axis=0).reshape(n_seq * seq_len, D)
    mem = jnp.concatenate([mem_prompt, mem_sample], axis=0).reshape(-1, D)
    y = _layer(x, mem, n_seq, seq_len, *[p[0] for p in layer_params], final_g)
    y = y.reshape(n_seq, seq_len, D)
    return y[:bp], y[bp:]
```

```python
import functools
import math

import jax
import jax.numpy as jnp
from jax import lax
from jax.experimental import pallas as pl
from jax.experimental.pallas import tpu as pltpu

F32 = jnp.float32
BF16 = jnp.bfloat16

RWKV_HEAD = 64
DECAY_SCALE = math.exp(-0.5)
LNX_EPS = RWKV_HEAD * 1e-5
RMS_EPS = 1e-5
LN_EPS = 1e-5
XATTN_HEADS = 4
TOP_K = 4
SWIGLU_ALPHA = 1.702
SWIGLU_LIMIT = 7.0

LANES = 128
SUBLANES = 8
WKV_CHUNK = 64
LORA_PAD = 128
VMEM_LIMIT = 56 * 1024 * 1024


def _cparams(sem):
    return pltpu.CompilerParams(dimension_semantics=sem, vmem_limit_bytes=VMEM_LIMIT)


def _tile(n, prefs):
    for p in prefs:
        if n % p == 0:
            return p
    return n


def _dot(a, b):
    return jnp.dot(a, b, preferred_element_type=F32)


def _dot_nt(a, b):
    return lax.dot_general(a, b, (((1,), (1,)), ((), ())), preferred_element_type=F32)


def _dot_tn(a, b):
    return lax.dot_general(a, b, (((0,), (0,)), ((), ())), preferred_element_type=F32)


def _split2(x):
    hi = x.astype(BF16)
    lo = (x - hi.astype(F32)).astype(BF16)
    return hi, lo


def _split3(x):
    hi = x.astype(BF16)
    r1 = x - hi.astype(F32)
    mid = r1.astype(BF16)
    lo = (r1 - mid.astype(F32)).astype(BF16)
    return hi, mid, lo


def _dot_exact_rhs(x, w_bf16):
    h, m, l = _split3(x)
    return _dot(h, w_bf16) + _dot(m, w_bf16) + _dot(l, w_bf16)


def _dot_exact_lhs(w_bf16, x):
    h, m, l = _split3(x)
    return _dot(w_bf16, h) + _dot(w_bf16, m) + _dot(w_bf16, l)


def _dot3(x, w_hi, w_lo):
    xh, xl = _split2(x)
    return _dot(xh, w_hi) + _dot(xl, w_hi) + _dot(xh, w_lo)


def _rms(x, g):
    return x * lax.rsqrt(jnp.mean(x * x, axis=-1, keepdims=True) + RMS_EPS) * g


def _sigmoid(x):
    return 1.0 / (1.0 + jnp.exp(-x))


def _inproj_shift_kernel(x_ref, xp_ref, xn_ref, g_ref, w_ref, mu_ref, o_ref, h_scr, hp_scr, hn_scr,
                         *, seq_len):
    i = pl.program_id(0)
    j = pl.program_id(1)
    tm = x_ref.shape[0]

    @pl.when(j == 0)
    def _():
        g = g_ref[...]
        h_scr[...] = _rms(x_ref[...], g).astype(BF16)
        hp_scr[...] = _rms(xp_ref[...], g).astype(BF16)
        hn_scr[...] = _rms(xn_ref[...], g).astype(BF16)

    w = w_ref[...]
    p = _dot(h_scr[...], w)
    pp = _dot(hp_scr[...], w)[SUBLANES - 1:SUBLANES, :]
    pn = _dot(hn_scr[...], w)[0:1, :]
    first = (i * tm) % seq_len == 0
    last = ((i + 1) * tm) % seq_len == 0
    pp = jnp.where(first, 0.0, pp)
    pn = jnp.where(last, 0.0, pn)
    rid = lax.broadcasted_iota(jnp.int32, p.shape, 0)
    up = jnp.where(rid == 0, pp, pltpu.roll(p, 1, 0))
    dn = jnp.where(rid == tm - 1, pn, pltpu.roll(p, tm - 1, 0))
    o_ref[...] = p + mu_ref[...] * (0.5 * (up + dn) - p)


def _inproj_shift(x, g, w_bf16, mu, seq_len, tn):
    T, D = x.shape
    N = w_bf16.shape[1]
    tm = _tile(seq_len, (512, 256, 128, 64, 32, 16, 8))
    nb8 = T // SUBLANES
    r8 = tm // SUBLANES
    return pl.pallas_call(
        functools.partial(_inproj_shift_kernel, seq_len=seq_len),
        grid=(T // tm, N // tn),
        in_specs=[
            pl.BlockSpec((tm, D), lambda i, j: (i, 0)),
            pl.BlockSpec((SUBLANES, D), lambda i, j: (jnp.maximum(i * r8 - 1, 0), 0)),
            pl.BlockSpec((SUBLANES, D), lambda i, j: (jnp.minimum((i + 1) * r8, nb8 - 1), 0)),
            pl.BlockSpec((1, D), lambda i, j: (0, 0)),
            pl.BlockSpec((D, tn), lambda i, j: (0, j)),
            pl.BlockSpec((1, tn), lambda i, j: (0, j)),
        ],
        out_specs=pl.BlockSpec((tm, tn), lambda i, j: (i, j)),
        out_shape=jax.ShapeDtypeStruct((T, N), F32),
        scratch_shapes=[pltpu.VMEM((tm, D), BF16), pltpu.VMEM((SUBLANES, D), BF16),
                        pltpu.VMEM((SUBLANES, D), BF16)],
        compiler_params=_cparams(("parallel", "arbitrary")),
        name="inproj_shift",
    )(x, x, x, g, w_bf16, mu)


def _inproj_glu_kernel(x_ref, g_ref, wa_ref, wb_ref, o_ref, h_scr):
    @pl.when(pl.program_id(1) == 0)
    def _():
        h_scr[...] = _rms(x_ref[...], g_ref[...]).astype(BF16)

    h = h_scr[...]
    a = _dot(h, wa_ref[...])
    b = _dot(h, wb_ref[...])
    o_ref[...] = a * _sigmoid(b)


def _inproj_glu(x, g, wa, wb):
    T, D = x.shape
    N = wa.shape[1]
    tm = _tile(T, (512, 256, 128, 64, 32, 16, 8))
    tn = _tile(N, (512, 256, 128))
    return pl.pallas_call(
        _inproj_glu_kernel,
        grid=(T // tm, N // tn),
        in_specs=[
            pl.BlockSpec((tm, D), lambda i, j: (i, 0)),
            pl.BlockSpec((1, D), lambda i, j: (0, 0)),
            pl.BlockSpec((D, tn), lambda i, j: (0, j)),
            pl.BlockSpec((D, tn), lambda i, j: (0, j)),
        ],
        out_specs=pl.BlockSpec((tm, tn), lambda i, j: (i, j)),
        out_shape=jax.ShapeDtypeStruct((T, N), F32),
        scratch_shapes=[pltpu.VMEM((tm, D), BF16)],
        compiler_params=_cparams(("parallel", "arbitrary")),
        name="inproj_glu",
    )(x, g, wa, wb)


def _wkv_kernel(zr_ref, zk_ref, zv_ref, dd_ref, ad_ref, w0_ref, w2h_ref, w2l_ref, a0_ref, a2h_ref,
                a2l_ref, kk_ref, ka_ref, rk_ref, o_ref, bonus_ref, state_ref, *, n_pairs, bwd):
    L = WKV_CHUNK
    PW = 2 * RWKV_HEAD
    TT = zr_ref.shape[0]
    n_chunks = TT // L

    @pl.when(pl.program_id(2) == 0)
    def _():
        state_ref[...] = jnp.zeros_like(state_ref)

    row = lax.broadcasted_iota(jnp.int32, (L, 2 * L), 0)
    col = lax.broadcasted_iota(jnp.int32, (L, 2 * L), 1) % L
    strict = (col > row) if bwd else (col < row)
    incl = (col >= row) if bwd else (col <= row)
    ipk = jnp.where(col == row, 1.0, 0.0)
    lane = lax.broadcasted_iota(jnp.int32, (1, PW), 1)
    m0 = lane < RWKV_HEAD
    lane2 = lax.broadcasted_iota(jnp.int32, (1, 2 * PW), 1) % PW
    m0w = lane2 < RWKV_HEAD
    srow = lax.broadcasted_iota(jnp.int32, (PW, PW), 0)
    scol = lax.broadcasted_iota(jnp.int32, (PW, PW), 1)
    same_head = (srow // RWKV_HEAD) == (scol // RWKV_HEAD)
    eye = srow == scol
    head_ones = jnp.where(same_head, 1.0, 0.0).astype(BF16)

    def bd(x):
        return jnp.concatenate([jnp.where(m0, x, 0.0), jnp.where(m0, 0.0, x)], axis=0).astype(BF16)

    def bd2(x):
        return jnp.concatenate([jnp.where(m0w, x, 0.0), jnp.where(m0w, 0.0, x)], axis=0).astype(BF16)

    w0 = w0_ref[...]
    a0 = a0_ref[...]
    k_k = kk_ref[...]
    k_a = ka_ref[...]
    r_k = rk_ref[...]

    def head_sum(x):
        n = x.shape[0]
        hi, lo = _split2(x)
        res = _dot(jnp.concatenate([hi, lo], axis=0), head_ones)
        return res[:n] + res[n:]

    r = zr_ref[...]
    k = zk_ref[...]
    v = zv_ref[...]
    dlin = w0 + _dot3(jnp.tanh(dd_ref[...]), w2h_ref[...], w2l_ref[...])
    lw = -DECAY_SCALE * _sigmoid(dlin)
    iclr = _sigmoid(a0 + _dot3(ad_ref[...], a2h_ref[...], a2l_ref[...]))
    kkr = k * k_k
    kmod = k * (1.0 + (iclr - 1.0) * k_a)
    cum = _chunk_cumsum(lw, bwd)
    e_in = jnp.exp(cum)
    e_ex = jnp.exp(cum - lw)
    e_ng = jnp.exp(-cum)
    rkr = r * kmod * r_k

    chunk_order = list(range(n_chunks - 1, -1, -1) if bwd else range(n_chunks))
    pair_vals = []
    for p in range(n_pairs):
        sl = slice(p * PW, (p + 1) * PW)
        kkr_p = kkr[:, sl]
        kk = kkr_p / jnp.maximum(jnp.sqrt(head_sum(kkr_p * kkr_p)), 1e-12)
        bonus_ref[:, sl] = head_sum(rkr[:, sl]) * v[:, sl]
        b_t = kk * iclr[:, sl]
        pair_vals.append(dict(b=b_t, ah=-kk * e_ex[:, sl], rh=r[:, sl] * e_in[:, sl],
                              bh=b_t * e_ng[:, sl], kh=kmod[:, sl] * e_ng[:, sl]))

    probs = []
    for ci in chunk_order:
        for p in range(n_pairs):
            sl = slice(p * PW, (p + 1) * PW)
            rs = slice(ci * L, (ci + 1) * L)
            pv = pair_vals[p]
            q = dict(p=p, ci=ci, sl=sl, rs=rs, ah=pv["ah"][rs], rh=pv["rh"][rs], v=v[rs, sl])
            lhs = jnp.concatenate([q["ah"], q["rh"]], axis=0).astype(BF16)
            g = _dot_nt(lhs, jnp.concatenate([bd(pv["bh"][rs]), bd(pv["kh"][rs])], axis=0))
            q["a"] = jnp.where(strict, g[:L, :2 * L], 0.0)
            q["ak"] = jnp.where(strict, g[:L, 2 * L:], 0.0)
            q["rb"] = jnp.where(incl, g[L:, :2 * L], 0.0)
            q["rk"] = jnp.where(incl, g[L:, 2 * L:], 0.0)
            probs.append(q)
    for q in probs:
        q["kv"] = _dot(jnp.concatenate([q["ak"], q["rk"]], axis=0).astype(BF16), bd(q["v"]))
    for q in probs:
        sq = _dot(jnp.concatenate([q["a"], q["rb"]], axis=0).astype(BF16), bd(q["a"]))
        q["t"] = ipk + q["a"]
        q["rbt"] = q["rb"] + sq[L:]
        q["ai"] = sq[:L]
    for lvl in range(1, 6):
        more = lvl < 5
        for q in probs:
            parts = [q["t"], q["rbt"]] + ([q["ai"]] if more else [])
            res = _dot(jnp.concatenate(parts, axis=0).astype(BF16), bd(q["ai"]))
            q["t"] = q["t"] + res[:L]
            q["rbt"] = q["rbt"] + res[L:2 * L]
            if more:
                q["ai"] = res[2 * L:]
    for q in probs:
        y0 = jnp.concatenate([q["ah"], q["kv"][:L]], axis=1)
        q["ry"] = _dot(jnp.concatenate([q["t"], q["rbt"]], axis=0).astype(BF16), bd2(y0))
    for q in probs:
        p, rs, sl, ry = q["p"], q["rs"], q["sl"], q["ry"]
        end = q["ci"] * L if bwd else q["ci"] * L + L - 1
        tot = cum[end:end + 1, sl]
        e_rm = jnp.exp(tot - cum[rs, sl])
        q["rt"] = q["rh"] + ry[L:, :PW]
        q["ob"] = ry[L:, PW:] + q["kv"][L:]
        lhs_t = jnp.concatenate([pair_vals[p]["b"][rs] * e_rm, kmod[rs, sl] * e_rm], axis=0).astype(BF16)
        rhs_t = jnp.concatenate(
            [ry[:L], jnp.concatenate([jnp.zeros((L, PW), F32), q["v"]], axis=1)], axis=0).astype(BF16)
        mn = _dot_tn(lhs_t, rhs_t)
        q["mm"] = jnp.where(eye, jnp.exp(tot), 0.0) + jnp.where(same_head, mn[:, :PW], 0.0)
        q["nn"] = jnp.where(same_head, mn[:, PW:], 0.0)
    states = [state_ref[p] for p in range(n_pairs)]
    for q in probs:
        p = q["p"]
        res = _dot(jnp.concatenate([q["rt"], q["mm"]], axis=0).astype(BF16), states[p].astype(BF16))
        o_ref[q["rs"], q["sl"]] = res[:L] + q["ob"]
        states[p] = res[L:] + q["nn"]
    for p in range(n_pairs):
        state_ref[p] = states[p]


def _chunk_cumsum(x, bwd):
    L = WKV_CHUNK
    n = x.shape[0]
    rin = lax.broadcasted_iota(jnp.int32, (n, 1), 0) % L
    s = 1
    while s < L:
        if bwd:
            x = x + jnp.where(rin < L - s, pltpu.roll(x, n - s, 0), 0.0)
        else:
            x = x + jnp.where(rin >= s, pltpu.roll(x, s, 0), 0.0)
        s *= 2
    return x


def _wkv(z_rkv, z_lora, decay_w0, w2h, w2l, iclr_a0, a2h, a2l, k_k, k_a, r_k, n_seq, seq_len, bwd):
    T = z_rkv.shape[0]
    W = z_rkv.shape[1] // 3
    PW = 2 * RWKV_HEAD
    n_pairs = _tile(W // PW, (4, 2, 1))
    GW = n_pairs * PW
    n_groups = W // GW
    TT = _tile(seq_len, (256, 128, 64))
    nT = seq_len // TT
    d = int(bwd)

    def tb(b, c):
        return b * nT + (nT - 1 - c if bwd else c)

    zspec = lambda off: pl.BlockSpec((TT, GW), lambda b, g, c: (tb(b, c), off * n_groups + g))
    pspec = pl.BlockSpec((None, 1, GW), lambda b, g, c: (d, 0, g))
    lspec = pl.BlockSpec((None, LORA_PAD, GW), lambda b, g, c: (d, 0, g))
    cspec = pl.BlockSpec((1, GW), lambda b, g, c: (0, g))
    ospec = pl.BlockSpec((TT, GW), lambda b, g, c: (tb(b, c), g))
    return pl.pallas_call(
        functools.partial(_wkv_kernel, n_pairs=n_pairs, bwd=bwd),
        grid=(n_seq, n_groups, nT),
        in_specs=[
            zspec(0), zspec(1), zspec(2),
            pl.BlockSpec((TT, LORA_PAD), lambda b, g, c: (tb(b, c), d)),
            pl.BlockSpec((TT, LORA_PAD), lambda b, g, c: (tb(b, c), 2 + d)),
            pspec, lspec, lspec, pspec, lspec, lspec, cspec, cspec, cspec,
        ],
        out_specs=[ospec, ospec],
        out_shape=[jax.ShapeDtypeStruct((T, W), F32), jax.ShapeDtypeStruct((T, W), F32)],
        scratch_shapes=[pltpu.VMEM((n_pairs, PW, PW), F32)],
        compiler_params=_cparams(("parallel", "parallel", "arbitrary")),
        name="wkv_scan_bwd" if bwd else "wkv_scan_fwd",
    )(z_rkv, z_rkv, z_rkv, z_lora, z_lora, decay_w0, w2h, w2l, iclr_a0, a2h, a2l, k_k, k_a, r_k)


def _rwkv_post_kernel(of_ref, ob_ref, bf_ref, bb_ref, gd_ref, g2_ref, lg_ref, lb_ref, out_ref):
    W = out_ref.shape[1]
    PW = 2 * RWKV_HEAD
    srow = lax.broadcasted_iota(jnp.int32, (PW, PW), 0)
    scol = lax.broadcasted_iota(jnp.int32, (PW, PW), 1)
    head_mean = jnp.where((srow // RWKV_HEAD) == (scol // RWKV_HEAD), 1.0, 0.0).astype(BF16)
    inv = 1.0 / RWKV_HEAD
    gate = _dot(_sigmoid(gd_ref[...]).astype(BF16), g2_ref[...])
    for p in range(W // PW):
        sl = slice(p * PW, (p + 1) * PW)
        o = of_ref[:, sl] + ob_ref[:, sl]
        mean = _dot_exact_rhs(o, head_mean) * inv
        oc = o - mean
        var = _dot_exact_rhs(oc * oc, head_mean) * inv
        y = oc * lax.rsqrt(var + LNX_EPS) * lg_ref[:, sl] + lb_ref[:, sl]
        y = y + bf_ref[:, sl] + bb_ref[:, sl]
        out_ref[:, sl] = (y * gate[:, sl]).astype(out_ref.dtype)


def _rwkv_post(o_f, o_b, bonus_f, bonus_b, z_lora, g2, lnx_g, lnx_b):
    T, W = o_f.shape
    tm = _tile(T, (256, 128, 64, 32, 16, 8))
    GL = g2.shape[0]
    gd_blk = (4 * LORA_PAD) // GL
    tspec = pl.BlockSpec((tm, W), lambda i: (i, 0))
    return pl.pallas_call(
        _rwkv_post_kernel,
        grid=(T // tm,),
        in_specs=[
            tspec, tspec, tspec, tspec,
            pl.BlockSpec((tm, GL), lambda i: (i, gd_blk)),
            pl.BlockSpec((GL, W), lambda i: (0, 0)),
            pl.BlockSpec((1, W), lambda i: (0, 0)),
            pl.BlockSpec((1, W), lambda i: (0, 0)),
        ],
        out_specs=pl.BlockSpec((tm, W), lambda i: (i, 0)),
        out_shape=jax.ShapeDtypeStruct((T, W), BF16),
        compiler_params=_cparams(("parallel",)),
        name="rwkv_post",
    )(o_f, o_b, bonus_f, bonus_b, z_lora, g2, lnx_g, lnx_b)


def _conv_kernel(u_ref, up_ref, un_ref, w_ref, b_ref, lg_ref, lb_ref, o_ref, ext_scr, *, seq_len, halo):
    i = pl.program_id(0)
    tm = u_ref.shape[0]
    K = w_ref.shape[0]
    first = (i * tm) % seq_len == 0
    last = ((i + 1) * tm) % seq_len == 0
    ext_scr[0:halo, :] = jnp.where(first, 0.0, up_ref[...])
    ext_scr[halo:halo + tm, :] = u_ref[...]
    ext_scr[halo + tm:halo + tm + halo, :] = jnp.where(last, 0.0, un_ref[...])
    sub = min(tm, 32)
    base = halo - K // 2
    for s in range(tm // sub):
        acc = jnp.zeros((sub, u_ref.shape[1]), F32)
        for j in range(K):
            acc = acc + w_ref[j:j + 1, :] * ext_scr[base + s * sub + j:base + s * sub + j + sub, :]
        acc = acc + b_ref[...]
        mean = jnp.mean(acc, axis=-1, keepdims=True)
        xc = acc - mean
        var = jnp.mean(xc * xc, axis=-1, keepdims=True)
        y = xc * lax.rsqrt(var + LN_EPS) * lg_ref[...] + lb_ref[...]
        o_ref[s * sub:(s + 1) * sub, :] = (y * _sigmoid(y)).astype(o_ref.dtype)


def _conv(u, conv_w, conv_b, ln_g, ln_b, seq_len):
    T, C = u.shape
    K = conv_w.shape[0]
    halo = 16
    assert K // 2 <= halo
    tm = _tile(seq_len, (128, 64, 32, 16))
    rh = tm // halo
    nbh = T // halo
    return pl.pallas_call(
        functools.partial(_conv_kernel, seq_len=seq_len, halo=halo),
        grid=(T // tm,),
        in_specs=[
            pl.BlockSpec((tm, C), lambda i: (i, 0)),
            pl.BlockSpec((halo, C), lambda i: (jnp.maximum(i * rh - 1, 0), 0)),
            pl.BlockSpec((halo, C), lambda i: (jnp.minimum((i + 1) * rh, nbh - 1), 0)),
            pl.BlockSpec((K, C), lambda i: (0, 0)),
            pl.BlockSpec((1, C), lambda i: (0, 0)),
            pl.BlockSpec((1, C), lambda i: (0, 0)),
            pl.BlockSpec((1, C), lambda i: (0, 0)),
        ],
        out_specs=pl.BlockSpec((tm, C), lambda i: (i, 0)),
        out_shape=jax.ShapeDtypeStruct((T, C), BF16),
        scratch_shapes=[pltpu.VMEM((tm + 2 * halo, C), F32)],
        compiler_params=_cparams(("parallel",)),
        name="conformer_conv",
    )(u, u, u, conv_w, conv_b, ln_g, ln_b)


def _merge_kernel(x_ref, g_ref, orw_ref, ocv_ref, wg1_ref, wg2_ref, wb1_ref, wb2_ref, o_ref, h_scr):
    @pl.when(pl.program_id(1) == 0)
    def _():
        h_scr[...] = _rms(x_ref[...], g_ref[...]).astype(BF16)

    h = h_scr[...]
    g1 = _sigmoid(_dot(h, wg1_ref[...]))
    g2 = _sigmoid(_dot(h, wg2_ref[...]))
    y1 = _dot(orw_ref[...], wb1_ref[...])
    y2 = _dot(ocv_ref[...], wb2_ref[...])
    o_ref[...] = (g1 * y1 + g2 * y2).astype(o_ref.dtype)


def _merge(x, g, o_rwkv, o_conv, wg1, wg2, wb1, wb2):
    T, D = x.shape
    W = o_rwkv.shape[1]
    C = o_conv.shape[1]
    tm = _tile(T, (512, 256, 128, 64, 32, 16, 8))
    tn = _tile(D, (512, 256, 128))
    return pl.pallas_call(
        _merge_kernel,
        grid=(T // tm, D // tn),
        in_specs=[
            pl.BlockSpec((tm, D), lambda i, j: (i, 0)),
            pl.BlockSpec((1, D), lambda i, j: (0, 0)),
            pl.BlockSpec((tm, W), lambda i, j: (i, 0)),
            pl.BlockSpec((tm, C), lambda i, j: (i, 0)),
            pl.BlockSpec((D, tn), lambda i, j: (0, j)),
            pl.BlockSpec((D, tn), lambda i, j: (0, j)),
            pl.BlockSpec((W, tn), lambda i, j: (0, j)),
            pl.BlockSpec((C, tn), lambda i, j: (0, j)),
        ],
        out_specs=pl.BlockSpec((tm, tn), lambda i, j: (i, j)),
        out_shape=jax.ShapeDtypeStruct((T, D), BF16),
        scratch_shapes=[pltpu.VMEM((tm, D), BF16)],
        compiler_params=_cparams(("parallel", "arbitrary")),
        name="merge_gates",
    )(x, g, o_rwkv, o_conv, wg1, wg2, wb1, wb2)


def _mm_res_kernel(a_ref, w_ref, res_ref, o_ref):
    o_ref[...] = res_ref[...] + _dot(a_ref[...], w_ref[...])


def _mm_res(a, w, res):
    T, K = a.shape
    N = w.shape[1]
    tm = _tile(T, (1024, 512, 256, 128, 64, 32, 16, 8))
    tn = _tile(N, (1024, 512, 256, 128))
    return pl.pallas_call(
        _mm_res_kernel,
        grid=(T // tm, N // tn),
        in_specs=[
            pl.BlockSpec((tm, K), lambda i, j: (i, 0)),
            pl.BlockSpec((K, tn), lambda i, j: (0, j)),
            pl.BlockSpec((tm, tn), lambda i, j: (i, j)),
        ],
        out_specs=pl.BlockSpec((tm, tn), lambda i, j: (i, j)),
        out_shape=jax.ShapeDtypeStruct((T, N), F32),
        compiler_params=_cparams(("parallel", "parallel")),
        name="proj_residual",
    )(a, w, res)


def _mm_norm_kernel(x_ref, g_ref, w_ref, o_ref, h_scr):
    @pl.when(pl.program_id(1) == 0)
    def _():
        h_scr[...] = _rms(x_ref[...], g_ref[...]).astype(BF16)

    o_ref[...] = _dot(h_scr[...], w_ref[...]).astype(o_ref.dtype)


def _mm_norm(x, g, w):
    T, D = x.shape
    N = w.shape[1]
    tm = _tile(T, (512, 256, 128, 64, 32, 16, 8))
    tn = _tile(N, (1024, 512, 256, 128))
    return pl.pallas_call(
        _mm_norm_kernel,
        grid=(T // tm, N // tn),
        in_specs=[
            pl.BlockSpec((tm, D), lambda i, j: (i, 0)),
            pl.BlockSpec((1, D), lambda i, j: (0, 0)),
            pl.BlockSpec((D, tn), lambda i, j: (0, j)),
        ],
        out_specs=pl.BlockSpec((tm, tn), lambda i, j: (i, j)),
        out_shape=jax.ShapeDtypeStruct((T, N), BF16),
        scratch_shapes=[pltpu.VMEM((tm, D), BF16)],
        compiler_params=_cparams(("parallel", "arbitrary")),
        name="norm_proj",
    )(x, g, w)


def _xattn_kernel(q_ref, k_ref, v_ref, o_ref):
    D = q_ref.shape[1]
    hd = D // XATTN_HEADS
    scale = hd ** -0.5
    for h in range(XATTN_HEADS):
        sl = slice(h * hd, (h + 1) * hd)
        s = _dot_nt(q_ref[:, sl], k_ref[:, sl]) * scale
        s = s - jnp.max(s, axis=-1, keepdims=True)
        e = jnp.exp(s)
        p = e / jnp.sum(e, axis=-1, keepdims=True)
        o_ref[:, sl] = _dot(p.astype(BF16), v_ref[:, sl]).astype(o_ref.dtype)


def _xattn(q, kv, n_seq, seq_len, n_mem):
    T, D = q.shape
    tm = _tile(seq_len, (512, 256, 128, 64, 32, 16, 8))
    nT = seq_len // tm
    return pl.pallas_call(
        _xattn_kernel,
        grid=(n_seq, nT),
        in_specs=[
            pl.BlockSpec((tm, D), lambda b, i: (b * nT + i, 0)),
            pl.BlockSpec((n_mem, D), lambda b, i: (b, 0)),
            pl.BlockSpec((n_mem, D), lambda b, i: (b, 1)),
        ],
        out_specs=pl.BlockSpec((tm, D), lambda b, i: (b * nT + i, 0)),
        out_shape=jax.ShapeDtypeStruct((T, D), BF16),
        compiler_params=_cparams(("parallel", "parallel")),
        name="cross_attention",
    )(q, kv, kv)


def _router_kernel(x_ref, g_ref, wh_ref, wl_ref, b_ref, hf_ref, e_ref, gate_ref):
    hf = _rms(x_ref[...], g_ref[...])
    hf_ref[...] = hf.astype(hf_ref.dtype)
    logits = _dot3(hf, wh_ref[...], wl_ref[...]) + b_ref[...]
    E = logits.shape[1]
    eid = lax.broadcasted_iota(jnp.int32, logits.shape, 1).astype(F32)
    work = logits
    vals = []
    idxs = []
    for _ in range(TOP_K):
        m = jnp.max(work, axis=-1, keepdims=True)
        idx = jnp.min(jnp.where(work == m, eid, float(E)), axis=-1, keepdims=True)
        vals.append(m)
        idxs.append(idx.astype(jnp.int32))
        work = jnp.where(eid == idx, -jnp.inf, work)
    ex = [jnp.exp(vv - vals[0]) for vv in vals]
    den = ex[0]
    for t in ex[1:]:
        den = den + t
    kid = lax.broadcasted_iota(jnp.int32, (logits.shape[0], TOP_K), 1)
    e_out = jnp.zeros((logits.shape[0], TOP_K), jnp.int32)
    g_out = jnp.zeros((logits.shape[0], TOP_K), F32)
    for t in range(TOP_K):
        e_out = jnp.where(kid == t, idxs[t], e_out)
        g_out = jnp.where(kid == t, ex[t] / den, g_out)
    e_ref[...] = e_out
    gate_ref[...] = g_out


def _router(x, g, w_router, b_router):
    T, D = x.shape
    E = w_router.shape[1]
    wh, wl = _split2(w_router)
    tm = _tile(T, (256, 128, 64, 32, 16, 8))
    return pl.pallas_call(
        _router_kernel,
        grid=(T // tm,),
        in_specs=[
            pl.BlockSpec((tm, D), lambda i: (i, 0)),
            pl.BlockSpec((1, D), lambda i: (0, 0)),
            pl.BlockSpec((D, E), lambda i: (0, 0)),
            pl.BlockSpec((D, E), lambda i: (0, 0)),
            pl.BlockSpec((1, E), lambda i: (0, 0)),
        ],
        out_specs=[
            pl.BlockSpec((tm, D), lambda i: (i, 0)),
            pl.BlockSpec((tm, TOP_K), lambda i: (i, 0)),
            pl.BlockSpec((tm, TOP_K), lambda i: (i, 0)),
        ],
        out_shape=[jax.ShapeDtypeStruct((T, D), BF16),
                   jax.ShapeDtypeStruct((T, TOP_K), jnp.int32),
                   jax.ShapeDtypeStruct((T, TOP_K), F32)],
        compiler_params=_cparams(("parallel",)),
        name="router",
    )(x, g, wh, wl, b_router)


def _expert_kernel(te_ref, tv_ref, x_ref, wg_ref, wu_ref, bg_ref, bu_ref, wd_ref, bd_ref, o_ref, acc_ref):
    i = pl.program_id(0)
    f = pl.program_id(1)
    nf = pl.num_programs(1)

    @pl.when(tv_ref[i] > 0)
    def _():
        x = x_ref[...]
        g = _dot(x, wg_ref[...].astype(BF16)) + bg_ref[...]
        u = _dot(x, wu_ref[...].astype(BF16)) + bu_ref[...]
        g = jnp.minimum(g, SWIGLU_LIMIT)
        u = jnp.clip(u, -SWIGLU_LIMIT, SWIGLU_LIMIT)
        act = (u + 1.0) * (g * _sigmoid(SWIGLU_ALPHA * g))
        part = _dot(act.astype(BF16), wd_ref[...].astype(BF16))

        @pl.when(f == 0)
        def _():
            acc_ref[...] = part + bd_ref[...]

        @pl.when(f > 0)
        def _():
            acc_ref[...] += part

        @pl.when(f == nf - 1)
        def _():
            o_ref[...] = acc_ref[...].astype(o_ref.dtype)

    @pl.when((tv_ref[i] == 0) & (f == nf - 1))
    def _():
        o_ref[...] = jnp.zeros_like(o_ref)


def _experts(xb, tile_e, tile_valid, w_gu, b_gu, w_dn, b_dn, tm):
    R, D = xb.shape
    E, _, F2 = w_gu.shape
    F = F2 // 2
    tf = _tile(F, (256, 128))
    nf = F // tf
    n_tiles = R // tm
    b_gu3 = b_gu.reshape(E, 1, F2)
    b_dn3 = b_dn.reshape(E, 1, D)

    def ff(i, f, tv):
        return jnp.where(tv[i] > 0, f, nf - 1)

    grid_spec = pltpu.PrefetchScalarGridSpec(
        num_scalar_prefetch=2,
        grid=(n_tiles, nf),
        in_specs=[
            pl.BlockSpec((tm, D), lambda i, f, te, tv: (i, 0)),
            pl.BlockSpec((None, D, tf), lambda i, f, te, tv: (te[i], 0, ff(i, f, tv))),
            pl.BlockSpec((None, D, tf), lambda i, f, te, tv: (te[i], 0, nf + ff(i, f, tv))),
            pl.BlockSpec((None, 1, tf), lambda i, f, te, tv: (te[i], 0, ff(i, f, tv))),
            pl.BlockSpec((None, 1, tf), lambda i, f, te, tv: (te[i], 0, nf + ff(i, f, tv))),
            pl.BlockSpec((None, tf, D), lambda i, f, te, tv: (te[i], ff(i, f, tv), 0)),
            pl.BlockSpec((None, 1, D), lambda i, f, te, tv: (te[i], 0, 0)),
        ],
        out_specs=pl.BlockSpec((tm, D), lambda i, f, te, tv: (i, 0)),
        scratch_shapes=[pltpu.VMEM((tm, D), F32)],
    )
    return pl.pallas_call(
        _expert_kernel,
        grid_spec=grid_spec,
        out_shape=jax.ShapeDtypeStruct((R, D), BF16),
        compiler_params=_cparams(("arbitrary", "arbitrary")),
        name="moe_experts",
    )(tile_e, tile_valid, xb, w_gu, w_gu, b_gu3, b_gu3, w_dn, b_dn3)


def _combine_kernel(x_ref, y_ref, gate_ref, g_ref, o_ref):
    D = x_ref.shape[1]
    acc = x_ref[...]
    gate = gate_ref[...]
    for t in range(TOP_K):
        acc = acc + gate[:, t:t + 1] * y_ref[:, t * D:(t + 1) * D].astype(F32)
    o_ref[...] = _rms(acc, g_ref[...])


def _combine(x, y4, gate, final_g):
    T, D = x.shape
    tm = _tile(T, (256, 128, 64, 32, 16, 8))
    return pl.pallas_call(
        _combine_kernel,
        grid=(T // tm,),
        in_specs=[
            pl.BlockSpec((tm, D), lambda i: (i, 0)),
            pl.BlockSpec((tm, TOP_K * D), lambda i: (i, 0)),
            pl.BlockSpec((tm, TOP_K), lambda i: (i, 0)),
            pl.BlockSpec((1, D), lambda i: (0, 0)),
        ],
        out_specs=pl.BlockSpec((tm, D), lambda i: (i, 0)),
        out_shape=jax.ShapeDtypeStruct((T, D), F32),
        compiler_params=_cparams(("parallel",)),
        name="moe_combine_norm",
    )(x, y4, gate, final_g)


def _pad_rows(w, n):
    return jnp.pad(w, ((0, 0),) * (w.ndim - 2) + ((0, n - w.shape[-2]), (0, 0)))


def _pad_cols(w, n):
    return jnp.pad(w, ((0, 0),) * (w.ndim - 1) + ((0, n - w.shape[-1]),))


def _moe_tile_rows(n_assign, n_experts):
    for tm in (1024, 512, 256, 128, 64, 32, 16, 8):
        if n_assign >= 4 * n_experts * tm or tm == 8:
            return tm


def _layer(x, mem, n_seq, seq_len, norm_mix_g, w_in, shift_mu, decay_w0, decay_w2, iclr_a0, iclr_a2,
           gate_g2, k_k, k_a, r_k, lnx_g, lnx_b, conv_w, conv_b, conv_ln_g, conv_ln_b, w_branch, w_o,
           norm_x_g, norm_mem_g, w_xq, w_xkv, w_xo, norm_ffn_g, w_router, b_router, w_gu, b_gu,
           w_dn, b_dn, final_g):
    T, D = x.shape
    W = k_k.shape[0]
    DL = decay_w2.shape[1]
    AL = iclr_a2.shape[1]
    GL = gate_g2.shape[0]
    C = conv_w.shape[1]
    E = w_router.shape[1]
    n_mem = mem.shape[0] // n_seq
    row = lambda v: v.reshape(1, -1)

    o3 = 3 * W
    o4 = o3 + 2 * DL
    o5 = o4 + 2 * AL
    o6 = o5 + GL
    seg = lambda m, a, b, n: _pad_cols(m[..., a:b], n)
    lora_cols = lambda m: jnp.concatenate(
        [seg(m, o3, o3 + DL, LORA_PAD), seg(m, o3 + DL, o4, LORA_PAD), seg(m, o4, o4 + AL, LORA_PAD),
         seg(m, o4 + AL, o5, LORA_PAD), m[..., o5:o6]], axis=-1)
    w_rkv = w_in[:, :o3].astype(BF16)
    w_lora = lora_cols(w_in).astype(BF16)
    mu_rkv = row(shift_mu[:o3])
    mu_lora = row(lora_cols(shift_mu))
    w_ca = w_in[:, o6:o6 + C].astype(BF16)
    w_cb = w_in[:, o6 + C:o6 + 2 * C].astype(BF16)
    w_g1 = w_in[:, o6 + 2 * C:o6 + 2 * C + D].astype(BF16)
    w_g2 = w_in[:, o6 + 2 * C + D:].astype(BF16)
    g_mix = row(norm_mix_g)

    z_rkv = _inproj_shift(x, g_mix, w_rkv, mu_rkv, seq_len, _tile(o3, (1024, 512, 256, 128)))
    z_lora = _inproj_shift(x, g_mix, w_lora, mu_lora, seq_len, w_lora.shape[1])
    u = _inproj_glu(x, g_mix, w_ca, w_cb)

    w2h, w2l = _split2(_pad_rows(decay_w2, LORA_PAD))
    a2h, a2l = _split2(_pad_rows(iclr_a2, LORA_PAD))
    wkv_args = (z_rkv, z_lora, decay_w0.reshape(2, 1, W), w2h, w2l, iclr_a0.reshape(2, 1, W), a2h, a2l,
                row(k_k), row(k_a), row(r_k), n_seq, seq_len)
    o_f, bonus_f = _wkv(*wkv_args, bwd=False)
    o_b, bonus_b = _wkv(*wkv_args, bwd=True)
    o_rwkv = _rwkv_post(o_f, o_b, bonus_f, bonus_b, z_lora, gate_g2.astype(BF16), row(lnx_g), row(lnx_b))
    o_conv = _conv(u, conv_w, row(conv_b), row(conv_ln_g), row(conv_ln_b), seq_len)

    merged = _merge(x, g_mix, o_rwkv, o_conv, w_g1, w_g2, w_branch[:W].astype(BF16),
                    w_branch[W:].astype(BF16))
    x1 = _mm_res(merged, w_o.astype(BF16), x)

    q = _mm_norm(x1, row(norm_x_g), w_xq.astype(BF16))
    kv = _mm_norm(mem, row(norm_mem_g), w_xkv.astype(BF16))
    att = _xattn(q, kv, n_seq, seq_len, n_mem)
    x2 = _mm_res(att, w_xo.astype(BF16), x1)

    hf, top_e, gate = _router(x2, row(norm_ffn_g), w_router, row(b_router))

    A = T * TOP_K
    tm_e = _moe_tile_rows(A, E)
    n_tiles = (A + E * (tm_e - 1) + tm_e - 1) // tm_e
    flat_e = top_e.reshape(A)
    order = jnp.argsort(flat_e).astype(jnp.int32)
    sorted_e = flat_e[order]
    counts = jnp.bincount(flat_e, length=E).astype(jnp.int32)
    padded = (counts + tm_e - 1) // tm_e * tm_e
    start = jnp.cumsum(counts) - counts
    pad_end = jnp.cumsum(padded)
    pad_start = pad_end - padded
    dest = pad_start[sorted_e] + jnp.arange(A, dtype=jnp.int32) - start[sorted_e]
    row_tok = jnp.zeros((n_tiles * tm_e,), jnp.int32).at[dest].set(order // TOP_K)
    dest_of = jnp.zeros((A,), jnp.int32).at[order].set(dest)
    tile_start = jnp.arange(n_tiles, dtype=jnp.int32) * tm_e
    tile_e = jnp.minimum(jnp.searchsorted(pad_end, tile_start, side='right'), E - 1).astype(jnp.int32)
    tile_valid = jnp.clip(pad_start[tile_e] + counts[tile_e] - tile_start, 0, tm_e).astype(jnp.int32)

    xb = jnp.take(hf, row_tok, axis=0)
    yb = _experts(xb, tile_e, tile_valid, w_gu, b_gu, w_dn, b_dn, tm_e)
    y4 = jnp.take(yb, dest_of, axis=0).reshape(T, TOP_K * D)
    return _combine(x2, y4, gate, row(final_g))


def kernel(x_prompt, x_sample, mem_prompt, mem_sample, norm_mix_g, w_in, shift_mu, decay_w0, decay_w2,
           iclr_a0, iclr_a2, gate_g2, k_k, k_a, r_k, lnx_g, lnx_b, conv_w, conv_b, conv_ln_g, conv_ln_b,
           w_branch, w_o, norm_x_g, norm_mem_g, w_xq, w_xkv, w_xo, norm_ffn_g, w_router, b_router,
           w_gu, b_gu, w_dn, b_dn, final_g):
    layer_params = (norm_mix_g, w_in, shift_mu, decay_w0, decay_w2, iclr_a0, iclr_a2, gate_g2, k_k, k_a,
                    r_k, lnx_g, lnx_b, conv_w, conv_b, conv_ln_g, conv_ln_b, w_branch, w_o, norm_x_g,
                    norm_mem_g, w_xq, w_xkv, w_xo, norm_ffn_g, w_router, b_router, w_gu, b_gu, w_dn, b_dn)
    assert all(p.shape[0] == 1 for p in layer_params), "single-layer stack expected"
    bp, seq_len, D = x_prompt.shape
    bs = x_sample.shape[0]
    assert x_sample.shape[1] == seq_len
    n_seq = bp + bs
    x = jnp.concatenate([x_prompt, x_sample], axis=0).reshape(n_seq * seq_len, D)
    mem = jnp.concatenate([mem_prompt, mem_sample], axis=0).reshape(-1, D)
    y = _layer(x, mem, n_seq, seq_len, *[p[0] for p in layer_params], final_g)
    y = y.reshape(n_seq, seq_len, D)
    return y[:bp], y[bp:]
```

```python
import functools
import math

import jax
import jax.numpy as jnp
from jax import lax
from jax.experimental import pallas as pl
from jax.experimental.pallas import tpu as pltpu

F32 = jnp.float32
BF16 = jnp.bfloat16

RWKV_HEAD = 64
DECAY_SCALE = math.exp(-0.5)
LNX_EPS = RWKV_HEAD * 1e-5
RMS_EPS = 1e-5
LN_EPS = 1e-5
XATTN_HEADS = 4
TOP_K = 4
SWIGLU_ALPHA = 1.702
SWIGLU_LIMIT = 7.0

LANES = 128
SUBLANES = 8
WKV_CHUNK = 64
LORA_PAD = 128
VMEM_LIMIT = 56 * 1024 * 1024


def _cparams(sem):
    return pltpu.CompilerParams(dimension_semantics=sem, vmem_limit_bytes=VMEM_LIMIT)


def _tile(n, prefs):
    for p in prefs:
        if n % p == 0:
            return p
    return n


def _dot(a, b):
    return jnp.dot(a, b, preferred_element_type=F32)


def _dot_nt(a, b):
    return lax.dot_general(a, b, (((1,), (1,)), ((), ())), preferred_element_type=F32)


def _dot_tn(a, b):
    return lax.dot_general(a, b, (((0,), (0,)), ((), ())), preferred_element_type=F32)


def _split2(x):
    hi = x.astype(BF16)
    lo = (x - hi.astype(F32)).astype(BF16)
    return hi, lo


def _split3(x):
    hi = x.astype(BF16)
    r1 = x - hi.astype(F32)
    mid = r1.astype(BF16)
    lo = (r1 - mid.astype(F32)).astype(BF16)
    return hi, mid, lo


def _dot_exact_rhs(x, w_bf16):
    h, m, l = _split3(x)
    return _dot(h, w_bf16) + _dot(m, w_bf16) + _dot(l, w_bf16)


def _dot_exact_lhs(w_bf16, x):
    h, m, l = _split3(x)
    return _dot(w_bf16, h) + _dot(w_bf16, m) + _dot(w_bf16, l)


def _dot3(x, w_hi, w_lo):
    xh, xl = _split2(x)
    return _dot(xh, w_hi) + _dot(xl, w_hi) + _dot(xh, w_lo)


def _rms(x, g):
    return x * lax.rsqrt(jnp.mean(x * x, axis=-1, keepdims=True) + RMS_EPS) * g


def _sigmoid(x):
    return 1.0 / (1.0 + jnp.exp(-x))


def _inproj_shift_kernel(x_ref, xp_ref, xn_ref, g_ref, w_ref, mu_ref, o_ref, h_scr, hp_scr, hn_scr,
                         *, seq_len):
    i = pl.program_id(0)
    j = pl.program_id(1)
    tm = x_ref.shape[0]

    @pl.when(j == 0)
    def _():
        g = g_ref[...]
        h_scr[...] = _rms(x_ref[...], g).astype(BF16)
        hp_scr[...] = _rms(xp_ref[...], g).astype(BF16)
        hn_scr[...] = _rms(xn_ref[...], g).astype(BF16)

    w = w_ref[...]
    p = _dot(h_scr[...], w)
    pp = _dot(hp_scr[...], w)[SUBLANES - 1:SUBLANES, :]
    pn = _dot(hn_scr[...], w)[0:1, :]
    first = (i * tm) % seq_len == 0
    last = ((i + 1) * tm) % seq_len == 0
    pp = jnp.where(first, 0.0, pp)
    pn = jnp.where(last, 0.0, pn)
    rid = lax.broadcasted_iota(jnp.int32, p.shape, 0)
    up = jnp.where(rid == 0, pp, pltpu.roll(p, 1, 0))
    dn = jnp.where(rid == tm - 1, pn, pltpu.roll(p, tm - 1, 0))
    o_ref[...] = p + mu_ref[...] * (0.5 * (up + dn) - p)


def _inproj_shift(x, g, w_bf16, mu, seq_len, tn):
    T, D = x.shape
    N = w_bf16.shape[1]
    tm = _tile(seq_len, (1024, 512, 256, 128, 64, 32, 16, 8))
    nb8 = T // SUBLANES
    r8 = tm // SUBLANES
    return pl.pallas_call(
        functools.partial(_inproj_shift_kernel, seq_len=seq_len),
        grid=(T // tm, N // tn),
        in_specs=[
            pl.BlockSpec((tm, D), lambda i, j: (i, 0)),
            pl.BlockSpec((SUBLANES, D), lambda i, j: (jnp.maximum(i * r8 - 1, 0), 0)),
            pl.BlockSpec((SUBLANES, D), lambda i, j: (jnp.minimum((i + 1) * r8, nb8 - 1), 0)),
            pl.BlockSpec((1, D), lambda i, j: (0, 0)),
            pl.BlockSpec((D, tn), lambda i, j: (0, j)),
            pl.BlockSpec((1, tn), lambda i, j: (0, j)),
        ],
        out_specs=pl.BlockSpec((tm, tn), lambda i, j: (i, j)),
        out_shape=jax.ShapeDtypeStruct((T, N), F32),
        scratch_shapes=[pltpu.VMEM((tm, D), BF16), pltpu.VMEM((SUBLANES, D), BF16),
                        pltpu.VMEM((SUBLANES, D), BF16)],
        compiler_params=_cparams(("parallel", "arbitrary")),
        name="inproj_shift",
    )(x, x, x, g, w_bf16, mu)


def _inproj_glu_kernel(x_ref, g_ref, wa_ref, wb_ref, o_ref, h_scr):
    @pl.when(pl.program_id(1) == 0)
    def _():
        h_scr[...] = _rms(x_ref[...], g_ref[...]).astype(BF16)

    h = h_scr[...]
    a = _dot(h, wa_ref[...])
    b = _dot(h, wb_ref[...])
    o_ref[...] = a * _sigmoid(b)


def _inproj_glu(x, g, wa, wb):
    T, D = x.shape
    N = wa.shape[1]
    tm = _tile(T, (512, 256, 128, 64, 32, 16, 8))
    tn = _tile(N, (512, 256, 128))
    return pl.pallas_call(
        _inproj_glu_kernel,
        grid=(T // tm, N // tn),
        in_specs=[
            pl.BlockSpec((tm, D), lambda i, j: (i, 0)),
            pl.BlockSpec((1, D), lambda i, j: (0, 0)),
            pl.BlockSpec((D, tn), lambda i, j: (0, j)),
            pl.BlockSpec((D, tn), lambda i, j: (0, j)),
        ],
        out_specs=pl.BlockSpec((tm, tn), lambda i, j: (i, j)),
        out_shape=jax.ShapeDtypeStruct((T, N), F32),
        scratch_shapes=[pltpu.VMEM((tm, D), BF16)],
        compiler_params=_cparams(("parallel", "arbitrary")),
        name="inproj_glu",
    )(x, g, wa, wb)


def _wkv_kernel(zr_ref, zk_ref, zv_ref, dd_ref, ad_ref, w0_ref, w2h_ref, w2l_ref, a0_ref, a2h_ref,
                a2l_ref, kk_ref, ka_ref, rk_ref, o_ref, bonus_ref, state_ref, *, n_pairs, bwd):
    L = WKV_CHUNK
    PW = 2 * RWKV_HEAD
    TT = zr_ref.shape[0]
    n_chunks = TT // L

    @pl.when(pl.program_id(2) == 0)
    def _():
        state_ref[...] = jnp.zeros_like(state_ref)

    row = lax.broadcasted_iota(jnp.int32, (L, 2 * L), 0)
    col = lax.broadcasted_iota(jnp.int32, (L, 2 * L), 1) % L
    strict = (col > row) if bwd else (col < row)
    incl = (col >= row) if bwd else (col <= row)
    ipk = jnp.where(col == row, 1.0, 0.0)
    lane = lax.broadcasted_iota(jnp.int32, (1, PW), 1)
    m0 = lane < RWKV_HEAD
    lane2 = lax.broadcasted_iota(jnp.int32, (1, 2 * PW), 1) % PW
    m0w = lane2 < RWKV_HEAD
    srow = lax.broadcasted_iota(jnp.int32, (PW, PW), 0)
    scol = lax.broadcasted_iota(jnp.int32, (PW, PW), 1)
    same_head = (srow // RWKV_HEAD) == (scol // RWKV_HEAD)
    eye = srow == scol
    head_ones = jnp.where(same_head, 1.0, 0.0).astype(BF16)

    def bd(x):
        return jnp.concatenate([jnp.where(m0, x, 0.0), jnp.where(m0, 0.0, x)], axis=0).astype(BF16)

    def bd2(x):
        return jnp.concatenate([jnp.where(m0w, x, 0.0), jnp.where(m0w, 0.0, x)], axis=0).astype(BF16)

    w0 = w0_ref[...]
    a0 = a0_ref[...]
    k_k = kk_ref[...]
    k_a = ka_ref[...]
    r_k = rk_ref[...]

    def head_sum(x):
        n = x.shape[0]
        hi, lo = _split2(x)
        res = _dot(jnp.concatenate([hi, lo], axis=0), head_ones)
        return res[:n] + res[n:]

    r = zr_ref[...]
    k = zk_ref[...]
    v = zv_ref[...]
    dlin = w0 + _dot3(jnp.tanh(dd_ref[...]), w2h_ref[...], w2l_ref[...])
    lw = -DECAY_SCALE * _sigmoid(dlin)
    iclr = _sigmoid(a0 + _dot3(ad_ref[...], a2h_ref[...], a2l_ref[...]))
    kkr = k * k_k
    kmod = k * (1.0 + (iclr - 1.0) * k_a)
    cum = _chunk_cumsum(lw, bwd)
    e_in = jnp.exp(cum)
    e_ex = jnp.exp(cum - lw)
    e_ng = jnp.exp(-cum)
    rkr = r * kmod * r_k

    chunk_order = list(range(n_chunks - 1, -1, -1) if bwd else range(n_chunks))
    pair_vals = []
    for p in range(n_pairs):
        sl = slice(p * PW, (p + 1) * PW)
        kkr_p = kkr[:, sl]
        kk = kkr_p / jnp.maximum(jnp.sqrt(head_sum(kkr_p * kkr_p)), 1e-12)
        bonus_ref[:, sl] = head_sum(rkr[:, sl]) * v[:, sl]
        b_t = kk * iclr[:, sl]
        pair_vals.append(dict(b=b_t, ah=-kk * e_ex[:, sl], rh=r[:, sl] * e_in[:, sl],
                              bh=b_t * e_ng[:, sl], kh=kmod[:, sl] * e_ng[:, sl]))

    probs = []
    for ci in chunk_order:
        for p in range(n_pairs):
            sl = slice(p * PW, (p + 1) * PW)
            rs = slice(ci * L, (ci + 1) * L)
            pv = pair_vals[p]
            q = dict(p=p, ci=ci, sl=sl, rs=rs, ah=pv["ah"][rs], rh=pv["rh"][rs], v=v[rs, sl])
            lhs = jnp.concatenate([q["ah"], q["rh"]], axis=0).astype(BF16)
            g = _dot_nt(lhs, jnp.concatenate([bd(pv["bh"][rs]), bd(pv["kh"][rs])], axis=0))
            q["a"] = jnp.where(strict, g[:L, :2 * L], 0.0)
            q["ak"] = jnp.where(strict, g[:L, 2 * L:], 0.0)
            q["rb"] = jnp.where(incl, g[L:, :2 * L], 0.0)
            q["rk"] = jnp.where(incl, g[L:, 2 * L:], 0.0)
            probs.append(q)
    for q in probs:
        q["kv"] = _dot(jnp.concatenate([q["ak"], q["rk"]], axis=0).astype(BF16), bd(q["v"]))
    for q in probs:
        sq = _dot(jnp.concatenate([q["a"], q["rb"]], axis=0).astype(BF16), bd(q["a"]))
        q["t"] = ipk + q["a"]
        q["rbt"] = q["rb"] + sq[L:]
        q["ai"] = sq[:L]
    for lvl in range(1, 6):
        more = lvl < 5
        for q in probs:
            parts = [q["t"], q["rbt"]] + ([q["ai"]] if more else [])
            res = _dot(jnp.concatenate(parts, axis=0).astype(BF16), bd(q["ai"]))
            q["t"] = q["t"] + res[:L]
            q["rbt"] = q["rbt"] + res[L:2 * L]
            if more:
                q["ai"] = res[2 * L:]
    for q in probs:
        y0 = jnp.concatenate([q["ah"], q["kv"][:L]], axis=1)
        q["ry"] = _dot(jnp.concatenate([q["t"], q["rbt"]], axis=0).astype(BF16), bd2(y0))
    for q in probs:
        p, rs, sl, ry = q["p"], q["rs"], q["sl"], q["ry"]
        end = q["ci"] * L if bwd else q["ci"] * L + L - 1
        tot = cum[end:end + 1, sl]
        e_rm = jnp.exp(tot - cum[rs, sl])
        q["rt"] = q["rh"] + ry[L:, :PW]
        q["ob"] = ry[L:, PW:] + q["kv"][L:]
        lhs_t = jnp.concatenate([pair_vals[p]["b"][rs] * e_rm, kmod[rs, sl] * e_rm], axis=0).astype(BF16)
        rhs_t = jnp.concatenate(
            [ry[:L], jnp.concatenate([jnp.zeros((L, PW), F32), q["v"]], axis=1)], axis=0).astype(BF16)
        mn = _dot_tn(lhs_t, rhs_t)
        q["mm"] = jnp.where(eye, jnp.exp(tot), 0.0) + jnp.where(same_head, mn[:, :PW], 0.0)
        q["nn"] = jnp.where(same_head, mn[:, PW:], 0.0)
    states = [state_ref[p] for p in range(n_pairs)]
    for q in probs:
        p = q["p"]
        res = _dot(jnp.concatenate([q["rt"], q["mm"]], axis=0).astype(BF16), states[p].astype(BF16))
        o_ref[q["rs"], q["sl"]] = res[:L] + q["ob"]
        states[p] = res[L:] + q["nn"]
    for p in range(n_pairs):
        state_ref[p] = states[p]


def _chunk_cumsum(x, bwd):
    L = WKV_CHUNK
    n = x.shape[0]
    rin = lax.broadcasted_iota(jnp.int32, (n, 1), 0) % L
    s = 1
    while s < L:
        if bwd:
            x = x + jnp.where(rin < L - s, pltpu.roll(x, n - s, 0), 0.0)
        else:
            x = x + jnp.where(rin >= s, pltpu.roll(x, s, 0), 0.0)
        s *= 2
    return x


def _wkv(z_rkv, z_lora, decay_w0, w2h, w2l, iclr_a0, a2h, a2l, k_k, k_a, r_k, n_seq, seq_len, bwd):
    T = z_rkv.shape[0]
    W = z_rkv.shape[1] // 3
    PW = 2 * RWKV_HEAD
    n_pairs = _tile(W // PW, (4, 2, 1))
    GW = n_pairs * PW
    n_groups = W // GW
    TT = _tile(seq_len, (256, 128, 64))
    nT = seq_len // TT
    d = int(bwd)

    def tb(b, c):
        return b * nT + (nT - 1 - c if bwd else c)

    zspec = lambda off: pl.BlockSpec((TT, GW), lambda b, g, c: (tb(b, c), off * n_groups + g))
    pspec = pl.BlockSpec((None, 1, GW), lambda b, g, c: (d, 0, g))
    lspec = pl.BlockSpec((None, LORA_PAD, GW), lambda b, g, c: (d, 0, g))
    cspec = pl.BlockSpec((1, GW), lambda b, g, c: (0, g))
    ospec = pl.BlockSpec((TT, GW), lambda b, g, c: (tb(b, c), g))
    return pl.pallas_call(
        functools.partial(_wkv_kernel, n_pairs=n_pairs, bwd=bwd),
        grid=(n_seq, n_groups, nT),
        in_specs=[
            zspec(0), zspec(1), zspec(2),
            pl.BlockSpec((TT, LORA_PAD), lambda b, g, c: (tb(b, c), d)),
            pl.BlockSpec((TT, LORA_PAD), lambda b, g, c: (tb(b, c), 2 + d)),
            pspec, lspec, lspec, pspec, lspec, lspec, cspec, cspec, cspec,
        ],
        out_specs=[ospec, ospec],
        out_shape=[jax.ShapeDtypeStruct((T, W), F32), jax.ShapeDtypeStruct((T, W), F32)],
        scratch_shapes=[pltpu.VMEM((n_pairs, PW, PW), F32)],
        compiler_params=_cparams(("parallel", "parallel", "arbitrary")),
        name="wkv_scan_bwd" if bwd else "wkv_scan_fwd",
    )(z_rkv, z_rkv, z_rkv, z_lora, z_lora, decay_w0, w2h, w2l, iclr_a0, a2h, a2l, k_k, k_a, r_k)


def _rwkv_post_kernel(of_ref, ob_ref, bf_ref, bb_ref, gd_ref, g2_ref, lg_ref, lb_ref, out_ref):
    W = out_ref.shape[1]
    PW = 2 * RWKV_HEAD
    srow = lax.broadcasted_iota(jnp.int32, (PW, PW), 0)
    scol = lax.broadcasted_iota(jnp.int32, (PW, PW), 1)
    head_mean = jnp.where((srow // RWKV_HEAD) == (scol // RWKV_HEAD), 1.0, 0.0).astype(BF16)
    inv = 1.0 / RWKV_HEAD
    gate = _dot(_sigmoid(gd_ref[...]).astype(BF16), g2_ref[...])
    for p in range(W // PW):
        sl = slice(p * PW, (p + 1) * PW)
        o = of_ref[:, sl] + ob_ref[:, sl]
        mean = _dot_exact_rhs(o, head_mean) * inv
        oc = o - mean
        var = _dot_exact_rhs(oc * oc, head_mean) * inv
        y = oc * lax.rsqrt(var + LNX_EPS) * lg_ref[:, sl] + lb_ref[:, sl]
        y = y + bf_ref[:, sl] + bb_ref[:, sl]
        out_ref[:, sl] = (y * gate[:, sl]).astype(out_ref.dtype)


def _rwkv_post(o_f, o_b, bonus_f, bonus_b, z_lora, g2, lnx_g, lnx_b):
    T, W = o_f.shape
    tm = _tile(T, (256, 128, 64, 32, 16, 8))
    GL = g2.shape[0]
    gd_blk = (4 * LORA_PAD) // GL
    tspec = pl.BlockSpec((tm, W), lambda i: (i, 0))
    return pl.pallas_call(
        _rwkv_post_kernel,
        grid=(T // tm,),
        in_specs=[
            tspec, tspec, tspec, tspec,
            pl.BlockSpec((tm, GL), lambda i: (i, gd_blk)),
            pl.BlockSpec((GL, W), lambda i: (0, 0)),
            pl.BlockSpec((1, W), lambda i: (0, 0)),
            pl.BlockSpec((1, W), lambda i: (0, 0)),
        ],
        out_specs=pl.BlockSpec((tm, W), lambda i: (i, 0)),
        out_shape=jax.ShapeDtypeStruct((T, W), BF16),
        compiler_params=_cparams(("parallel",)),
        name="rwkv_post",
    )(o_f, o_b, bonus_f, bonus_b, z_lora, g2, lnx_g, lnx_b)


def _conv_kernel(u_ref, up_ref, un_ref, w_ref, b_ref, lg_ref, lb_ref, o_ref, ext_scr, *, seq_len, halo):
    i = pl.program_id(0)
    tm = u_ref.shape[0]
    K = w_ref.shape[0]
    first = (i * tm) % seq_len == 0
    last = ((i + 1) * tm) % seq_len == 0
    ext_scr[0:halo, :] = jnp.where(first, 0.0, up_ref[...])
    ext_scr[halo:halo + tm, :] = u_ref[...]
    ext_scr[halo + tm:halo + tm + halo, :] = jnp.where(last, 0.0, un_ref[...])
    sub = min(tm, 32)
    base = halo - K // 2
    for s in range(tm // sub):
        acc = jnp.zeros((sub, u_ref.shape[1]), F32)
        for j in range(K):
            acc = acc + w_ref[j:j + 1, :] * ext_scr[base + s * sub + j:base + s * sub + j + sub, :]
        acc = acc + b_ref[...]
        mean = jnp.mean(acc, axis=-1, keepdims=True)
        xc = acc - mean
        var = jnp.mean(xc * xc, axis=-1, keepdims=True)
        y = xc * lax.rsqrt(var + LN_EPS) * lg_ref[...] + lb_ref[...]
        o_ref[s * sub:(s + 1) * sub, :] = (y * _sigmoid(y)).astype(o_ref.dtype)


def _conv(u, conv_w, conv_b, ln_g, ln_b, seq_len):
    T, C = u.shape
    K = conv_w.shape[0]
    halo = 16
    assert K // 2 <= halo
    tm = _tile(seq_len, (128, 64, 32, 16))
    rh = tm // halo
    nbh = T // halo
    return pl.pallas_call(
        functools.partial(_conv_kernel, seq_len=seq_len, halo=halo),
        grid=(T // tm,),
        in_specs=[
            pl.BlockSpec((tm, C), lambda i: (i, 0)),
            pl.BlockSpec((halo, C), lambda i: (jnp.maximum(i * rh - 1, 0), 0)),
            pl.BlockSpec((halo, C), lambda i: (jnp.minimum((i + 1) * rh, nbh - 1), 0)),
            pl.BlockSpec((K, C), lambda i: (0, 0)),
            pl.BlockSpec((1, C), lambda i: (0, 0)),
            pl.BlockSpec((1, C), lambda i: (0, 0)),
            pl.BlockSpec((1, C), lambda i: (0, 0)),
        ],
        out_specs=pl.BlockSpec((tm, C), lambda i: (i, 0)),
        out_shape=jax.ShapeDtypeStruct((T, C), BF16),
        scratch_shapes=[pltpu.VMEM((tm + 2 * halo, C), F32)],
        compiler_params=_cparams(("parallel",)),
        name="conformer_conv",
    )(u, u, u, conv_w, conv_b, ln_g, ln_b)


def _merge_kernel(x_ref, g_ref, orw_ref, ocv_ref, wg1_ref, wg2_ref, wb1_ref, wb2_ref, o_ref, h_scr):
    @pl.when(pl.program_id(1) == 0)
    def _():
        h_scr[...] = _rms(x_ref[...], g_ref[...]).astype(BF16)

    h = h_scr[...]
    g1 = _sigmoid(_dot(h, wg1_ref[...]))
    g2 = _sigmoid(_dot(h, wg2_ref[...]))
    y1 = _dot(orw_ref[...], wb1_ref[...])
    y2 = _dot(ocv_ref[...], wb2_ref[...])
    o_ref[...] = (g1 * y1 + g2 * y2).astype(o_ref.dtype)


def _merge(x, g, o_rwkv, o_conv, wg1, wg2, wb1, wb2):
    T, D = x.shape
    W = o_rwkv.shape[1]
    C = o_conv.shape[1]
    tm = _tile(T, (512, 256, 128, 64, 32, 16, 8))
    tn = _tile(D, (512, 256, 128))
    return pl.pallas_call(
        _merge_kernel,
        grid=(T // tm, D // tn),
        in_specs=[
            pl.BlockSpec((tm, D), lambda i, j: (i, 0)),
            pl.BlockSpec((1, D), lambda i, j: (0, 0)),
            pl.BlockSpec((tm, W), lambda i, j: (i, 0)),
            pl.BlockSpec((tm, C), lambda i, j: (i, 0)),
            pl.BlockSpec((D, tn), lambda i, j: (0, j)),
            pl.BlockSpec((D, tn), lambda i, j: (0, j)),
            pl.BlockSpec((W, tn), lambda i, j: (0, j)),
            pl.BlockSpec((C, tn), lambda i, j: (0, j)),
        ],
        out_specs=pl.BlockSpec((tm, tn), lambda i, j: (i, j)),
        out_shape=jax.ShapeDtypeStruct((T, D), BF16),
        scratch_shapes=[pltpu.VMEM((tm, D), BF16)],
        compiler_params=_cparams(("parallel", "arbitrary")),
        name="merge_gates",
    )(x, g, o_rwkv, o_conv, wg1, wg2, wb1, wb2)


def _mm_res_kernel(a_ref, w_ref, res_ref, o_ref):
    o_ref[...] = res_ref[...] + _dot(a_ref[...], w_ref[...])


def _mm_res(a, w, res):
    T, K = a.shape
    N = w.shape[1]
    tm = _tile(T, (1024, 512, 256, 128, 64, 32, 16, 8))
    tn = _tile(N, (1024, 512, 256, 128))
    return pl.pallas_call(
        _mm_res_kernel,
        grid=(T // tm, N // tn),
        in_specs=[
            pl.BlockSpec((tm, K), lambda i, j: (i, 0)),
            pl.BlockSpec((K, tn), lambda i, j: (0, j)),
            pl.BlockSpec((tm, tn), lambda i, j: (i, j)),
        ],
        out_specs=pl.BlockSpec((tm, tn), lambda i, j: (i, j)),
        out_shape=jax.ShapeDtypeStruct((T, N), F32),
        compiler_params=_cparams(("parallel", "parallel")),
        name="proj_residual",
    )(a, w, res)


def _mm_norm_kernel(x_ref, g_ref, w_ref, o_ref, h_scr):
    @pl.when(pl.program_id(1) == 0)
    def _():
        h_scr[...] = _rms(x_ref[...], g_ref[...]).astype(BF16)

    o_ref[...] = _dot(h_scr[...], w_ref[...]).astype(o_ref.dtype)


def _mm_norm(x, g, w):
    T, D = x.shape
    N = w.shape[1]
    tm = _tile(T, (512, 256, 128, 64, 32, 16, 8))
    tn = _tile(N, (1024, 512, 256, 128))
    return pl.pallas_call(
        _mm_norm_kernel,
        grid=(T // tm, N // tn),
        in_specs=[
            pl.BlockSpec((tm, D), lambda i, j: (i, 0)),
            pl.BlockSpec((1, D), lambda i, j: (0, 0)),
            pl.BlockSpec((D, tn), lambda i, j: (0, j)),
        ],
        out_specs=pl.BlockSpec((tm, tn), lambda i, j: (i, j)),
        out_shape=jax.ShapeDtypeStruct((T, N), BF16),
        scratch_shapes=[pltpu.VMEM((tm, D), BF16)],
        compiler_params=_cparams(("parallel", "arbitrary")),
        name="norm_proj",
    )(x, g, w)


def _xattn_kernel(q_ref, k_ref, v_ref, o_ref):
    D = q_ref.shape[1]
    hd = D // XATTN_HEADS
    scale = hd ** -0.5
    for h in range(XATTN_HEADS):
        sl = slice(h * hd, (h + 1) * hd)
        s = _dot_nt(q_ref[:, sl], k_ref[:, sl]) * scale
        s = s - jnp.max(s, axis=-1, keepdims=True)
        e = jnp.exp(s)
        p = e / jnp.sum(e, axis=-1, keepdims=True)
        o_ref[:, sl] = _dot(p.astype(BF16), v_ref[:, sl]).astype(o_ref.dtype)


def _xattn(q, kv, n_seq, seq_len, n_mem):
    T, D = q.shape
    tm = _tile(seq_len, (512, 256, 128, 64, 32, 16, 8))
    nT = seq_len // tm
    return pl.pallas_call(
        _xattn_kernel,
        grid=(n_seq, nT),
        in_specs=[
            pl.BlockSpec((tm, D), lambda b, i: (b * nT + i, 0)),
            pl.BlockSpec((n_mem, D), lambda b, i: (b, 0)),
            pl.BlockSpec((n_mem, D), lambda b, i: (b, 1)),
        ],
        out_specs=pl.BlockSpec((tm, D), lambda b, i: (b * nT + i, 0)),
        out_shape=jax.ShapeDtypeStruct((T, D), BF16),
        compiler_params=_cparams(("parallel", "parallel")),
        name="cross_attention",
    )(q, kv, kv)


def _router_kernel(x_ref, g_ref, wh_ref, wl_ref, b_ref, hf_ref, e_ref, gate_ref):
    hf = _rms(x_ref[...], g_ref[...])
    hf_ref[...] = hf.astype(hf_ref.dtype)
    logits = _dot3(hf, wh_ref[...], wl_ref[...]) + b_ref[...]
    E = logits.shape[1]
    eid = lax.broadcasted_iota(jnp.int32, logits.shape, 1).astype(F32)
    work = logits
    vals = []
    idxs = []
    for _ in range(TOP_K):
        m = jnp.max(work, axis=-1, keepdims=True)
        idx = jnp.min(jnp.where(work == m, eid, float(E)), axis=-1, keepdims=True)
        vals.append(m)
        idxs.append(idx.astype(jnp.int32))
        work = jnp.where(eid == idx, -jnp.inf, work)
    ex = [jnp.exp(vv - vals[0]) for vv in vals]
    den = ex[0]
    for t in ex[1:]:
        den = den + t
    kid = lax.broadcasted_iota(jnp.int32, (logits.shape[0], TOP_K), 1)
    e_out = jnp.zeros((logits.shape[0], TOP_K), jnp.int32)
    g_out = jnp.zeros((logits.shape[0], TOP_K), F32)
    for t in range(TOP_K):
        e_out = jnp.where(kid == t, idxs[t], e_out)
        g_out = jnp.where(kid == t, ex[t] / den, g_out)
    e_ref[...] = e_out
    gate_ref[...] = g_out


def _router(x, g, w_router, b_router):
    T, D = x.shape
    E = w_router.shape[1]
    wh, wl = _split2(w_router)
    tm = _tile(T, (256, 128, 64, 32, 16, 8))
    return pl.pallas_call(
        _router_kernel,
        grid=(T // tm,),
        in_specs=[
            pl.BlockSpec((tm, D), lambda i: (i, 0)),
            pl.BlockSpec((1, D), lambda i: (0, 0)),
            pl.BlockSpec((D, E), lambda i: (0, 0)),
            pl.BlockSpec((D, E), lambda i: (0, 0)),
            pl.BlockSpec((1, E), lambda i: (0, 0)),
        ],
        out_specs=[
            pl.BlockSpec((tm, D), lambda i: (i, 0)),
            pl.BlockSpec((tm, TOP_K), lambda i: (i, 0)),
            pl.BlockSpec((tm, TOP_K), lambda i: (i, 0)),
        ],
        out_shape=[jax.ShapeDtypeStruct((T, D), BF16),
                   jax.ShapeDtypeStruct((T, TOP_K), jnp.int32),
                   jax.ShapeDtypeStruct((T, TOP_K), F32)],
        compiler_params=_cparams(("parallel",)),
        name="router",
    )(x, g, wh, wl, b_router)


def _expert_kernel(te_ref, tv_ref, x_ref, wg_ref, wu_ref, bg_ref, bu_ref, wd_ref, bd_ref, o_ref, acc_ref):
    i = pl.program_id(0)
    f = pl.program_id(1)
    nf = pl.num_programs(1)

    @pl.when(tv_ref[i] > 0)
    def _():
        x = x_ref[...]
        g = _dot(x, wg_ref[...].astype(BF16)) + bg_ref[...]
        u = _dot(x, wu_ref[...].astype(BF16)) + bu_ref[...]
        g = jnp.minimum(g, SWIGLU_LIMIT)
        u = jnp.clip(u, -SWIGLU_LIMIT, SWIGLU_LIMIT)
        act = (u + 1.0) * (g * _sigmoid(SWIGLU_ALPHA * g))
        part = _dot(act.astype(BF16), wd_ref[...].astype(BF16))

        @pl.when(f == 0)
        def _():
            acc_ref[...] = part + bd_ref[...]

        @pl.when(f > 0)
        def _():
            acc_ref[...] += part

        @pl.when(f == nf - 1)
        def _():
            o_ref[...] = acc_ref[...].astype(o_ref.dtype)

    @pl.when((tv_ref[i] == 0) & (f == nf - 1))
    def _():
        o_ref[...] = jnp.zeros_like(o_ref)


def _experts(xb, tile_e, tile_valid, w_gu, b_gu, w_dn, b_dn, tm):
    R, D = xb.shape
    E, _, F2 = w_gu.shape
    F = F2 // 2
    tf = _tile(F, (256, 128))
    nf = F // tf
    n_tiles = R // tm
    b_gu3 = b_gu.reshape(E, 1, F2)
    b_dn3 = b_dn.reshape(E, 1, D)

    def ff(i, f, tv):
        return jnp.where(tv[i] > 0, f, nf - 1)

    grid_spec = pltpu.PrefetchScalarGridSpec(
        num_scalar_prefetch=2,
        grid=(n_tiles, nf),
        in_specs=[
            pl.BlockSpec((tm, D), lambda i, f, te, tv: (i, 0)),
            pl.BlockSpec((None, D, tf), lambda i, f, te, tv: (te[i], 0, ff(i, f, tv))),
            pl.BlockSpec((None, D, tf), lambda i, f, te, tv: (te[i], 0, nf + ff(i, f, tv))),
            pl.BlockSpec((None, 1, tf), lambda i, f, te, tv: (te[i], 0, ff(i, f, tv))),
            pl.BlockSpec((None, 1, tf), lambda i, f, te, tv: (te[i], 0, nf + ff(i, f, tv))),
            pl.BlockSpec((None, tf, D), lambda i, f, te, tv: (te[i], ff(i, f, tv), 0)),
            pl.BlockSpec((None, 1, D), lambda i, f, te, tv: (te[i], 0, 0)),
        ],
        out_specs=pl.BlockSpec((tm, D), lambda i, f, te, tv: (i, 0)),
        scratch_shapes=[pltpu.VMEM((tm, D), F32)],
    )
    return pl.pallas_call(
        _expert_kernel,
        grid_spec=grid_spec,
        out_shape=jax.ShapeDtypeStruct((R, D), BF16),
        compiler_params=_cparams(("arbitrary", "arbitrary")),
        name="moe_experts",
    )(tile_e, tile_valid, xb, w_gu, w_gu, b_gu3, b_gu3, w_dn, b_dn3)


def _combine_kernel(x_ref, y_ref, gate_ref, g_ref, o_ref):
    acc = x_ref[...]
    gate = gate_ref[...]
    for t in range(TOP_K):
        acc = acc + gate[:, t:t + 1] * y_ref[t].astype(F32)
    o_ref[...] = _rms(acc, g_ref[...])


def _combine(x, y4, gate, final_g):
    T, D = x.shape
    tm = _tile(T, (256, 128, 64, 32, 16, 8))
    return pl.pallas_call(
        _combine_kernel,
        grid=(T // tm,),
        in_specs=[
            pl.BlockSpec((tm, D), lambda i: (i, 0)),
            pl.BlockSpec((TOP_K, tm, D), lambda i: (0, i, 0)),
            pl.BlockSpec((tm, TOP_K), lambda i: (i, 0)),
            pl.BlockSpec((1, D), lambda i: (0, 0)),
        ],
        out_specs=pl.BlockSpec((tm, D), lambda i: (i, 0)),
        out_shape=jax.ShapeDtypeStruct((T, D), F32),
        compiler_params=_cparams(("parallel",)),
        name="moe_combine_norm",
    )(x, y4, gate, final_g)


def _pad_rows(w, n):
    return jnp.pad(w, ((0, 0),) * (w.ndim - 2) + ((0, n - w.shape[-2]), (0, 0)))


def _pad_cols(w, n):
    return jnp.pad(w, ((0, 0),) * (w.ndim - 1) + ((0, n - w.shape[-1]),))


def _moe_tile_rows(n_assign, n_experts):
    for tm in (1024, 512, 256, 128, 64, 32, 16, 8):
        if n_assign >= 4 * n_experts * tm or tm == 8:
            return tm


def _layer(x, mem, n_seq, seq_len, norm_mix_g, w_in, shift_mu, decay_w0, decay_w2, iclr_a0, iclr_a2,
           gate_g2, k_k, k_a, r_k, lnx_g, lnx_b, conv_w, conv_b, conv_ln_g, conv_ln_b, w_branch, w_o,
           norm_x_g, norm_mem_g, w_xq, w_xkv, w_xo, norm_ffn_g, w_router, b_router, w_gu, b_gu,
           w_dn, b_dn, final_g):
    T, D = x.shape
    W = k_k.shape[0]
    DL = decay_w2.shape[1]
    AL = iclr_a2.shape[1]
    GL = gate_g2.shape[0]
    C = conv_w.shape[1]
    E = w_router.shape[1]
    n_mem = mem.shape[0] // n_seq
    row = lambda v: v.reshape(1, -1)

    o3 = 3 * W
    o4 = o3 + 2 * DL
    o5 = o4 + 2 * AL
    o6 = o5 + GL
    seg = lambda m, a, b, n: _pad_cols(m[..., a:b], n)
    lora_cols = lambda m: jnp.concatenate(
        [seg(m, o3, o3 + DL, LORA_PAD), seg(m, o3 + DL, o4, LORA_PAD), seg(m, o4, o4 + AL, LORA_PAD),
         seg(m, o4 + AL, o5, LORA_PAD), m[..., o5:o6]], axis=-1)
    w_rkv = w_in[:, :o3].astype(BF16)
    w_lora = lora_cols(w_in).astype(BF16)
    mu_rkv = row(shift_mu[:o3])
    mu_lora = row(lora_cols(shift_mu))
    w_ca = w_in[:, o6:o6 + C].astype(BF16)
    w_cb = w_in[:, o6 + C:o6 + 2 * C].astype(BF16)
    w_g1 = w_in[:, o6 + 2 * C:o6 + 2 * C + D].astype(BF16)
    w_g2 = w_in[:, o6 + 2 * C + D:].astype(BF16)
    g_mix = row(norm_mix_g)

    z_rkv = _inproj_shift(x, g_mix, w_rkv, mu_rkv, seq_len, _tile(o3, (1024, 512, 256, 128)))
    z_lora = _inproj_shift(x, g_mix, w_lora, mu_lora, seq_len, w_lora.shape[1])
    u = _inproj_glu(x, g_mix, w_ca, w_cb)

    w2h, w2l = _split2(_pad_rows(decay_w2, LORA_PAD))
    a2h, a2l = _split2(_pad_rows(iclr_a2, LORA_PAD))
    wkv_args = (z_rkv, z_lora, decay_w0.reshape(2, 1, W), w2h, w2l, iclr_a0.reshape(2, 1, W), a2h, a2l,
                row(k_k), row(k_a), row(r_k), n_seq, seq_len)
    o_f, bonus_f = _wkv(*wkv_args, bwd=False)
    o_b, bonus_b = _wkv(*wkv_args, bwd=True)
    o_rwkv = _rwkv_post(o_f, o_b, bonus_f, bonus_b, z_lora, gate_g2.astype(BF16), row(lnx_g), row(lnx_b))
    o_conv = _conv(u, conv_w, row(conv_b), row(conv_ln_g), row(conv_ln_b), seq_len)

    merged = _merge(x, g_mix, o_rwkv, o_conv, w_g1, w_g2, w_branch[:W].astype(BF16),
                    w_branch[W:].astype(BF16))
    x1 = _mm_res(merged, w_o.astype(BF16), x)

    q = _mm_norm(x1, row(norm_x_g), w_xq.astype(BF16))
    kv = _mm_norm(mem, row(norm_mem_g), w_xkv.astype(BF16))
    att = _xattn(q, kv, n_seq, seq_len, n_mem)
    x2 = _mm_res(att, w_xo.astype(BF16), x1)

    hf, top_e, gate = _router(x2, row(norm_ffn_g), w_router, row(b_router))

    A = T * TOP_K
    tm_e = _moe_tile_rows(A, E)
    n_tiles = (A + E * (tm_e - 1) + tm_e - 1) // tm_e
    n_rows = n_tiles * tm_e
    flat_e = top_e.reshape(A)
    iota_a = jnp.arange(A, dtype=jnp.int32)
    sorted_e, order = lax.sort((flat_e, iota_a), num_keys=1)
    _, rank = lax.sort((order, iota_a), num_keys=1)
    experts = jnp.arange(E, dtype=jnp.int32)
    start = jnp.searchsorted(sorted_e, experts, side='left').astype(jnp.int32)
    counts = jnp.searchsorted(sorted_e, experts, side='right').astype(jnp.int32) - start
    padded = (counts + tm_e - 1) // tm_e * tm_e
    pad_end = jnp.cumsum(padded)
    pad_start = pad_end - padded
    dest_of = pad_start[flat_e] + rank - start[flat_e]
    tile_start = jnp.arange(n_tiles, dtype=jnp.int32) * tm_e
    tile_e = jnp.minimum(jnp.searchsorted(pad_end, tile_start, side='right'), E - 1).astype(jnp.int32)
    tile_valid = jnp.clip(pad_start[tile_e] + counts[tile_e] - tile_start, 0, tm_e).astype(jnp.int32)
    rows = jnp.arange(n_rows, dtype=jnp.int32)
    row_e = jnp.repeat(tile_e, tm_e)
    row_off = rows - pad_start[row_e]
    src = order[jnp.minimum(start[row_e] + row_off, A - 1)] // TOP_K
    row_tok = jnp.where(row_off < counts[row_e], src, rows % T)

    xb = hf.at[row_tok].get(mode='promise_in_bounds')
    yb = _experts(xb, tile_e, tile_valid, w_gu, b_gu, w_dn, b_dn, tm_e)
    dest_slot_major = dest_of.reshape(T, TOP_K).T.reshape(A)
    y4 = yb.at[dest_slot_major].get(mode='promise_in_bounds').reshape(TOP_K, T, D)
    return _combine(x2, y4, gate, row(final_g))


def kernel(x_prompt, x_sample, mem_prompt, mem_sample, norm_mix_g, w_in, shift_mu, decay_w0, decay_w2,
           iclr_a0, iclr_a2, gate_g2, k_k, k_a, r_k, lnx_g, lnx_b, conv_w, conv_b, conv_ln_g, conv_ln_b,
           w_branch, w_o, norm_x_g, norm_mem_g, w_xq, w_xkv, w_xo, norm_ffn_g, w_router, b_router,
           w_gu, b_gu, w_dn, b_dn, final_g):
    layer_params = (norm_mix_g, w_in, shift_mu, decay_w0, decay_w2, iclr_a0, iclr_a2, gate_g2, k_k, k_a,
                    r_k, lnx_g, lnx_b, conv_w, conv_b, conv_ln_g, conv_ln_b, w_branch, w_o, norm_x_g,
                    norm_mem_g, w_xq, w_xkv, w_xo, norm_ffn_g, w_router, b_router, w_gu, b_gu, w_dn, b_dn)
    assert all(p.shape[0] == 1 for p in layer_params), "single-layer stack expected"
    bp, seq_len, D = x_prompt.shape
    bs = x_sample.shape[0]
    assert x_sample.shape[1] == seq_len
    n_seq = bp + bs
    x = jnp.concatenate([x_prompt, x_sample], axis=0).reshape(n_seq * seq_len, D)
    mem = jnp.concatenate([mem_prompt, mem_sample], axis=0).reshape(-1, D)
    y = _layer(x, mem, n_seq, seq_len, *[p[0] for p in layer_params], final_g)
    y = y.reshape(n_seq, seq_len, D)
    return y[:bp], y[bp:]
```

```python
import functools
import math

import jax
import jax.numpy as jnp
from jax import lax
from jax.experimental import pallas as pl
from jax.experimental.pallas import tpu as pltpu

F32 = jnp.float32
BF16 = jnp.bfloat16

RWKV_HEAD = 64
DECAY_SCALE = math.exp(-0.5)
LNX_EPS = RWKV_HEAD * 1e-5
RMS_EPS = 1e-5
LN_EPS = 1e-5
XATTN_HEADS = 4
TOP_K = 4
SWIGLU_ALPHA = 1.702
SWIGLU_LIMIT = 7.0

LANES = 128
SUBLANES = 8
WKV_CHUNK = 64
LORA_PAD = 128
VMEM_LIMIT = 56 * 1024 * 1024


def _cparams(sem):
    return pltpu.CompilerParams(dimension_semantics=sem, vmem_limit_bytes=VMEM_LIMIT)


def _tile(n, prefs):
    for p in prefs:
        if n % p == 0:
            return p
    return n


def _dot(a, b):
    return jnp.dot(a, b, preferred_element_type=F32)


def _dot_nt(a, b):
    return lax.dot_general(a, b, (((1,), (1,)), ((), ())), preferred_element_type=F32)


def _dot_tn(a, b):
    return lax.dot_general(a, b, (((0,), (0,)), ((), ())), preferred_element_type=F32)


def _split2(x):
    hi = x.astype(BF16)
    lo = (x - hi.astype(F32)).astype(BF16)
    return hi, lo


def _split3(x):
    hi = x.astype(BF16)
    r1 = x - hi.astype(F32)
    mid = r1.astype(BF16)
    lo = (r1 - mid.astype(F32)).astype(BF16)
    return hi, mid, lo


def _dot_exact_rhs(x, w_bf16):
    h, m, l = _split3(x)
    return _dot(h, w_bf16) + _dot(m, w_bf16) + _dot(l, w_bf16)


def _dot_exact_lhs(w_bf16, x):
    h, m, l = _split3(x)
    return _dot(w_bf16, h) + _dot(w_bf16, m) + _dot(w_bf16, l)


def _dot3(x, w_hi, w_lo):
    xh, xl = _split2(x)
    return _dot(xh, w_hi) + _dot(xl, w_hi) + _dot(xh, w_lo)


def _rms(x, g):
    return x * lax.rsqrt(jnp.mean(x * x, axis=-1, keepdims=True) + RMS_EPS) * g


def _sigmoid(x):
    return 1.0 / (1.0 + jnp.exp(-x))


def _inproj_shift_kernel(x_ref, xp_ref, xn_ref, g_ref, w_ref, mu_ref, o_ref, h_scr, hp_scr, hn_scr,
                         *, seq_len):
    i = pl.program_id(0)
    j = pl.program_id(1)
    tm = x_ref.shape[0]

    @pl.when(j == 0)
    def _():
        g = g_ref[...]
        h_scr[...] = _rms(x_ref[...], g).astype(BF16)
        hp_scr[...] = _rms(xp_ref[...], g).astype(BF16)
        hn_scr[...] = _rms(xn_ref[...], g).astype(BF16)

    w = w_ref[...]
    p = _dot(h_scr[...], w)
    pp = _dot(hp_scr[...], w)[SUBLANES - 1:SUBLANES, :]
    pn = _dot(hn_scr[...], w)[0:1, :]
    first = (i * tm) % seq_len == 0
    last = ((i + 1) * tm) % seq_len == 0
    pp = jnp.where(first, 0.0, pp)
    pn = jnp.where(last, 0.0, pn)
    rid = lax.broadcasted_iota(jnp.int32, p.shape, 0)
    up = jnp.where(rid == 0, pp, pltpu.roll(p, 1, 0))
    dn = jnp.where(rid == tm - 1, pn, pltpu.roll(p, tm - 1, 0))
    o_ref[...] = p + mu_ref[...] * (0.5 * (up + dn) - p)


def _inproj_shift(x, g, w_bf16, mu, seq_len, tn):
    T, D = x.shape
    N = w_bf16.shape[1]
    tm = _tile(seq_len, (1024, 512, 256, 128, 64, 32, 16, 8))
    nb8 = T // SUBLANES
    r8 = tm // SUBLANES
    return pl.pallas_call(
        functools.partial(_inproj_shift_kernel, seq_len=seq_len),
        grid=(T // tm, N // tn),
        in_specs=[
            pl.BlockSpec((tm, D), lambda i, j: (i, 0)),
            pl.BlockSpec((SUBLANES, D), lambda i, j: (jnp.maximum(i * r8 - 1, 0), 0)),
            pl.BlockSpec((SUBLANES, D), lambda i, j: (jnp.minimum((i + 1) * r8, nb8 - 1), 0)),
            pl.BlockSpec((1, D), lambda i, j: (0, 0)),
            pl.BlockSpec((D, tn), lambda i, j: (0, j)),
            pl.BlockSpec((1, tn), lambda i, j: (0, j)),
        ],
        out_specs=pl.BlockSpec((tm, tn), lambda i, j: (i, j)),
        out_shape=jax.ShapeDtypeStruct((T, N), F32),
        scratch_shapes=[pltpu.VMEM((tm, D), BF16), pltpu.VMEM((SUBLANES, D), BF16),
                        pltpu.VMEM((SUBLANES, D), BF16)],
        compiler_params=_cparams(("parallel", "arbitrary")),
        name="inproj_shift",
    )(x, x, x, g, w_bf16, mu)


def _inproj_glu_kernel(x_ref, g_ref, wa_ref, wb_ref, o_ref, h_scr):
    @pl.when(pl.program_id(1) == 0)
    def _():
        h_scr[...] = _rms(x_ref[...], g_ref[...]).astype(BF16)

    h = h_scr[...]
    a = _dot(h, wa_ref[...])
    b = _dot(h, wb_ref[...])
    o_ref[...] = a * _sigmoid(b)


def _inproj_glu(x, g, wa, wb):
    T, D = x.shape
    N = wa.shape[1]
    tm = _tile(T, (512, 256, 128, 64, 32, 16, 8))
    tn = _tile(N, (512, 256, 128))
    return pl.pallas_call(
        _inproj_glu_kernel,
        grid=(T // tm, N // tn),
        in_specs=[
            pl.BlockSpec((tm, D), lambda i, j: (i, 0)),
            pl.BlockSpec((1, D), lambda i, j: (0, 0)),
            pl.BlockSpec((D, tn), lambda i, j: (0, j)),
            pl.BlockSpec((D, tn), lambda i, j: (0, j)),
        ],
        out_specs=pl.BlockSpec((tm, tn), lambda i, j: (i, j)),
        out_shape=jax.ShapeDtypeStruct((T, N), F32),
        scratch_shapes=[pltpu.VMEM((tm, D), BF16)],
        compiler_params=_cparams(("parallel", "arbitrary")),
        name="inproj_glu",
    )(x, g, wa, wb)


def _wkv_kernel(zr_ref, zk_ref, zv_ref, dd_ref, ad_ref, w0_ref, w2h_ref, w2l_ref, a0_ref, a2h_ref,
                a2l_ref, kk_ref, ka_ref, rk_ref, o_ref, bonus_ref, state_ref, *, n_pairs, bwd):
    L = WKV_CHUNK
    PW = 2 * RWKV_HEAD
    TT = zr_ref.shape[0]
    n_chunks = TT // L

    @pl.when(pl.program_id(2) == 0)
    def _():
        state_ref[...] = jnp.zeros_like(state_ref)

    row = lax.broadcasted_iota(jnp.int32, (L, 2 * L), 0)
    col = lax.broadcasted_iota(jnp.int32, (L, 2 * L), 1) % L
    strict = (col > row) if bwd else (col < row)
    incl = (col >= row) if bwd else (col <= row)
    ipk = jnp.where(col == row, 1.0, 0.0)
    lane = lax.broadcasted_iota(jnp.int32, (1, PW), 1)
    m0 = lane < RWKV_HEAD
    lane2 = lax.broadcasted_iota(jnp.int32, (1, 2 * PW), 1) % PW
    m0w = lane2 < RWKV_HEAD
    srow = lax.broadcasted_iota(jnp.int32, (PW, PW), 0)
    scol = lax.broadcasted_iota(jnp.int32, (PW, PW), 1)
    same_head = (srow // RWKV_HEAD) == (scol // RWKV_HEAD)
    eye = srow == scol
    head_ones = jnp.where(same_head, 1.0, 0.0).astype(BF16)

    def bd(x):
        return jnp.concatenate([jnp.where(m0, x, 0.0), jnp.where(m0, 0.0, x)], axis=0).astype(BF16)

    def bd2(x):
        return jnp.concatenate([jnp.where(m0w, x, 0.0), jnp.where(m0w, 0.0, x)], axis=0).astype(BF16)

    w0 = w0_ref[...]
    a0 = a0_ref[...]
    k_k = kk_ref[...]
    k_a = ka_ref[...]
    r_k = rk_ref[...]

    def head_sum(x):
        n = x.shape[0]
        hi, lo = _split2(x)
        res = _dot(jnp.concatenate([hi, lo], axis=0), head_ones)
        return res[:n] + res[n:]

    r = zr_ref[...]
    k = zk_ref[...]
    v = zv_ref[...]
    dlin = w0 + _dot3(jnp.tanh(dd_ref[...]), w2h_ref[...], w2l_ref[...])
    lw = -DECAY_SCALE * _sigmoid(dlin)
    iclr = _sigmoid(a0 + _dot3(ad_ref[...], a2h_ref[...], a2l_ref[...]))
    kkr = k * k_k
    kmod = k * (1.0 + (iclr - 1.0) * k_a)
    cum = _chunk_cumsum(lw, bwd)
    e_in = jnp.exp(cum)
    e_ex = jnp.exp(cum - lw)
    e_ng = jnp.exp(-cum)
    rkr = r * kmod * r_k

    chunk_order = list(range(n_chunks - 1, -1, -1) if bwd else range(n_chunks))
    pair_vals = []
    for p in range(n_pairs):
        sl = slice(p * PW, (p + 1) * PW)
        kkr_p = kkr[:, sl]
        kk = kkr_p / jnp.maximum(jnp.sqrt(head_sum(kkr_p * kkr_p)), 1e-12)
        bonus_ref[:, sl] = head_sum(rkr[:, sl]) * v[:, sl]
        b_t = kk * iclr[:, sl]
        pair_vals.append(dict(b=b_t, ah=-kk * e_ex[:, sl], rh=r[:, sl] * e_in[:, sl],
                              bh=b_t * e_ng[:, sl], kh=kmod[:, sl] * e_ng[:, sl]))

    probs = []
    for ci in chunk_order:
        for p in range(n_pairs):
            sl = slice(p * PW, (p + 1) * PW)
            rs = slice(ci * L, (ci + 1) * L)
            pv = pair_vals[p]
            q = dict(p=p, ci=ci, sl=sl, rs=rs, ah=pv["ah"][rs], rh=pv["rh"][rs], v=v[rs, sl])
            lhs = jnp.concatenate([q["ah"], q["rh"]], axis=0).astype(BF16)
            g = _dot_nt(lhs, jnp.concatenate([bd(pv["bh"][rs]), bd(pv["kh"][rs])], axis=0))
            q["a"] = jnp.where(strict, g[:L, :2 * L], 0.0)
            q["ak"] = jnp.where(strict, g[:L, 2 * L:], 0.0)
            q["rb"] = jnp.where(incl, g[L:, :2 * L], 0.0)
            q["rk"] = jnp.where(incl, g[L:, 2 * L:], 0.0)
            probs.append(q)
    for q in probs:
        q["kv"] = _dot(jnp.concatenate([q["ak"], q["rk"]], axis=0).astype(BF16), bd(q["v"]))
    for q in probs:
        sq = _dot(jnp.concatenate([q["a"], q["rb"]], axis=0).astype(BF16), bd(q["a"]))
        q["t"] = ipk + q["a"]
        q["rbt"] = q["rb"] + sq[L:]
        q["ai"] = sq[:L]
    for lvl in range(1, 6):
        more = lvl < 5
        for q in probs:
            parts = [q["t"], q["rbt"]] + ([q["ai"]] if more else [])
            res = _dot(jnp.concatenate(parts, axis=0).astype(BF16), bd(q["ai"]))
            q["t"] = q["t"] + res[:L]
            q["rbt"] = q["rbt"] + res[L:2 * L]
            if more:
                q["ai"] = res[2 * L:]
    for q in probs:
        y0 = jnp.concatenate([q["ah"], q["kv"][:L]], axis=1)
        q["ry"] = _dot(jnp.concatenate([q["t"], q["rbt"]], axis=0).astype(BF16), bd2(y0))
    for q in probs:
        p, rs, sl, ry = q["p"], q["rs"], q["sl"], q["ry"]
        end = q["ci"] * L if bwd else q["ci"] * L + L - 1
        tot = cum[end:end + 1, sl]
        e_rm = jnp.exp(tot - cum[rs, sl])
        q["rt"] = q["rh"] + ry[L:, :PW]
        q["ob"] = ry[L:, PW:] + q["kv"][L:]
        lhs_t = jnp.concatenate([pair_vals[p]["b"][rs] * e_rm, kmod[rs, sl] * e_rm], axis=0).astype(BF16)
        rhs_t = jnp.concatenate(
            [ry[:L], jnp.concatenate([jnp.zeros((L, PW), F32), q["v"]], axis=1)], axis=0).astype(BF16)
        mn = _dot_tn(lhs_t, rhs_t)
        q["mm"] = jnp.where(eye, jnp.exp(tot), 0.0) + jnp.where(same_head, mn[:, :PW], 0.0)
        q["nn"] = jnp.where(same_head, mn[:, PW:], 0.0)
    states = [state_ref[p] for p in range(n_pairs)]
    for q in probs:
        p = q["p"]
        res = _dot(jnp.concatenate([q["rt"], q["mm"]], axis=0).astype(BF16), states[p].astype(BF16))
        o_ref[q["rs"], q["sl"]] = res[:L] + q["ob"]
        states[p] = res[L:] + q["nn"]
    for p in range(n_pairs):
        state_ref[p] = states[p]


def _chunk_cumsum(x, bwd):
    L = WKV_CHUNK
    n = x.shape[0]
    rin = lax.broadcasted_iota(jnp.int32, (n, 1), 0) % L
    s = 1
    while s < L:
        if bwd:
            x = x + jnp.where(rin < L - s, pltpu.roll(x, n - s, 0), 0.0)
        else:
            x = x + jnp.where(rin >= s, pltpu.roll(x, s, 0), 0.0)
        s *= 2
    return x


def _wkv(z_rkv, z_lora, decay_w0, w2h, w2l, iclr_a0, a2h, a2l, k_k, k_a, r_k, n_seq, seq_len, bwd):
    T = z_rkv.shape[0]
    W = z_rkv.shape[1] // 3
    PW = 2 * RWKV_HEAD
    n_pairs = _tile(W // PW, (4, 2, 1))
    GW = n_pairs * PW
    n_groups = W // GW
    TT = _tile(seq_len, (256, 128, 64))
    nT = seq_len // TT
    d = int(bwd)

    def tb(b, c):
        return b * nT + (nT - 1 - c if bwd else c)

    zspec = lambda off: pl.BlockSpec((TT, GW), lambda b, g, c: (tb(b, c), off * n_groups + g))
    pspec = pl.BlockSpec((None, 1, GW), lambda b, g, c: (d, 0, g))
    lspec = pl.BlockSpec((None, LORA_PAD, GW), lambda b, g, c: (d, 0, g))
    cspec = pl.BlockSpec((1, GW), lambda b, g, c: (0, g))
    ospec = pl.BlockSpec((TT, GW), lambda b, g, c: (tb(b, c), g))
    return pl.pallas_call(
        functools.partial(_wkv_kernel, n_pairs=n_pairs, bwd=bwd),
        grid=(n_seq, n_groups, nT),
        in_specs=[
            zspec(0), zspec(1), zspec(2),
            pl.BlockSpec((TT, LORA_PAD), lambda b, g, c: (tb(b, c), d)),
            pl.BlockSpec((TT, LORA_PAD), lambda b, g, c: (tb(b, c), 2 + d)),
            pspec, lspec, lspec, pspec, lspec, lspec, cspec, cspec, cspec,
        ],
        out_specs=[ospec, ospec],
        out_shape=[jax.ShapeDtypeStruct((T, W), F32), jax.ShapeDtypeStruct((T, W), F32)],
        scratch_shapes=[pltpu.VMEM((n_pairs, PW, PW), F32)],
        compiler_params=_cparams(("parallel", "parallel", "arbitrary")),
        name="wkv_scan_bwd" if bwd else "wkv_scan_fwd",
    )(z_rkv, z_rkv, z_rkv, z_lora, z_lora, decay_w0, w2h, w2l, iclr_a0, a2h, a2l, k_k, k_a, r_k)


def _rwkv_post_kernel(of_ref, ob_ref, bf_ref, bb_ref, gd_ref, g2_ref, lg_ref, lb_ref, out_ref):
    W = out_ref.shape[1]
    PW = 2 * RWKV_HEAD
    srow = lax.broadcasted_iota(jnp.int32, (PW, PW), 0)
    scol = lax.broadcasted_iota(jnp.int32, (PW, PW), 1)
    head_mean = jnp.where((srow // RWKV_HEAD) == (scol // RWKV_HEAD), 1.0, 0.0).astype(BF16)
    inv = 1.0 / RWKV_HEAD
    gate = _dot(_sigmoid(gd_ref[...]).astype(BF16), g2_ref[...])
    for p in range(W // PW):
        sl = slice(p * PW, (p + 1) * PW)
        o = of_ref[:, sl] + ob_ref[:, sl]
        mean = _dot_exact_rhs(o, head_mean) * inv
        oc = o - mean
        var = _dot_exact_rhs(oc * oc, head_mean) * inv
        y = oc * lax.rsqrt(var + LNX_EPS) * lg_ref[:, sl] + lb_ref[:, sl]
        y = y + bf_ref[:, sl] + bb_ref[:, sl]
        out_ref[:, sl] = (y * gate[:, sl]).astype(out_ref.dtype)


def _rwkv_post(o_f, o_b, bonus_f, bonus_b, z_lora, g2, lnx_g, lnx_b):
    T, W = o_f.shape
    tm = _tile(T, (256, 128, 64, 32, 16, 8))
    GL = g2.shape[0]
    gd_blk = (4 * LORA_PAD) // GL
    tspec = pl.BlockSpec((tm, W), lambda i: (i, 0))
    return pl.pallas_call(
        _rwkv_post_kernel,
        grid=(T // tm,),
        in_specs=[
            tspec, tspec, tspec, tspec,
            pl.BlockSpec((tm, GL), lambda i: (i, gd_blk)),
            pl.BlockSpec((GL, W), lambda i: (0, 0)),
            pl.BlockSpec((1, W), lambda i: (0, 0)),
            pl.BlockSpec((1, W), lambda i: (0, 0)),
        ],
        out_specs=pl.BlockSpec((tm, W), lambda i: (i, 0)),
        out_shape=jax.ShapeDtypeStruct((T, W), BF16),
        compiler_params=_cparams(("parallel",)),
        name="rwkv_post",
    )(o_f, o_b, bonus_f, bonus_b, z_lora, g2, lnx_g, lnx_b)


def _conv_kernel(u_ref, up_ref, un_ref, w_ref, b_ref, lg_ref, lb_ref, o_ref, ext_scr, *, seq_len, halo):
    i = pl.program_id(0)
    tm = u_ref.shape[0]
    K = w_ref.shape[0]
    first = (i * tm) % seq_len == 0
    last = ((i + 1) * tm) % seq_len == 0
    ext_scr[0:halo, :] = jnp.where(first, 0.0, up_ref[...])
    ext_scr[halo:halo + tm, :] = u_ref[...]
    ext_scr[halo + tm:halo + tm + halo, :] = jnp.where(last, 0.0, un_ref[...])
    sub = min(tm, 32)
    base = halo - K // 2
    for s in range(tm // sub):
        acc = jnp.zeros((sub, u_ref.shape[1]), F32)
        for j in range(K):
            acc = acc + w_ref[j:j + 1, :] * ext_scr[base + s * sub + j:base + s * sub + j + sub, :]
        acc = acc + b_ref[...]
        mean = jnp.mean(acc, axis=-1, keepdims=True)
        xc = acc - mean
        var = jnp.mean(xc * xc, axis=-1, keepdims=True)
        y = xc * lax.rsqrt(var + LN_EPS) * lg_ref[...] + lb_ref[...]
        o_ref[s * sub:(s + 1) * sub, :] = (y * _sigmoid(y)).astype(o_ref.dtype)


def _conv(u, conv_w, conv_b, ln_g, ln_b, seq_len):
    T, C = u.shape
    K = conv_w.shape[0]
    halo = 16
    assert K // 2 <= halo
    tm = _tile(seq_len, (128, 64, 32, 16))
    rh = tm // halo
    nbh = T // halo
    return pl.pallas_call(
        functools.partial(_conv_kernel, seq_len=seq_len, halo=halo),
        grid=(T // tm,),
        in_specs=[
            pl.BlockSpec((tm, C), lambda i: (i, 0)),
            pl.BlockSpec((halo, C), lambda i: (jnp.maximum(i * rh - 1, 0), 0)),
            pl.BlockSpec((halo, C), lambda i: (jnp.minimum((i + 1) * rh, nbh - 1), 0)),
            pl.BlockSpec((K, C), lambda i: (0, 0)),
            pl.BlockSpec((1, C), lambda i: (0, 0)),
            pl.BlockSpec((1, C), lambda i: (0, 0)),
            pl.BlockSpec((1, C), lambda i: (0, 0)),
        ],
        out_specs=pl.BlockSpec((tm, C), lambda i: (i, 0)),
        out_shape=jax.ShapeDtypeStruct((T, C), BF16),
        scratch_shapes=[pltpu.VMEM((tm + 2 * halo, C), F32)],
        compiler_params=_cparams(("parallel",)),
        name="conformer_conv",
    )(u, u, u, conv_w, conv_b, ln_g, ln_b)


def _merge_kernel(x_ref, g_ref, orw_ref, ocv_ref, wg1_ref, wg2_ref, wb1_ref, wb2_ref, o_ref, h_scr):
    @pl.when(pl.program_id(1) == 0)
    def _():
        h_scr[...] = _rms(x_ref[...], g_ref[...]).astype(BF16)

    h = h_scr[...]
    g1 = _sigmoid(_dot(h, wg1_ref[...]))
    g2 = _sigmoid(_dot(h, wg2_ref[...]))
    y1 = _dot(orw_ref[...], wb1_ref[...])
    y2 = _dot(ocv_ref[...], wb2_ref[...])
    o_ref[...] = (g1 * y1 + g2 * y2).astype(o_ref.dtype)


def _merge(x, g, o_rwkv, o_conv, wg1, wg2, wb1, wb2):
    T, D = x.shape
    W = o_rwkv.shape[1]
    C = o_conv.shape[1]
    tm = _tile(T, (512, 256, 128, 64, 32, 16, 8))
    tn = _tile(D, (512, 256, 128))
    return pl.pallas_call(
        _merge_kernel,
        grid=(T // tm, D // tn),
        in_specs=[
            pl.BlockSpec((tm, D), lambda i, j: (i, 0)),
            pl.BlockSpec((1, D), lambda i, j: (0, 0)),
            pl.BlockSpec((tm, W), lambda i, j: (i, 0)),
            pl.BlockSpec((tm, C), lambda i, j: (i, 0)),
            pl.BlockSpec((D, tn), lambda i, j: (0, j)),
            pl.BlockSpec((D, tn), lambda i, j: (0, j)),
            pl.BlockSpec((W, tn), lambda i, j: (0, j)),
            pl.BlockSpec((C, tn), lambda i, j: (0, j)),
        ],
        out_specs=pl.BlockSpec((tm, tn), lambda i, j: (i, j)),
        out_shape=jax.ShapeDtypeStruct((T, D), BF16),
        scratch_shapes=[pltpu.VMEM((tm, D), BF16)],
        compiler_params=_cparams(("parallel", "arbitrary")),
        name="merge_gates",
    )(x, g, o_rwkv, o_conv, wg1, wg2, wb1, wb2)


def _mm_res_kernel(a_ref, w_ref, res_ref, o_ref):
    o_ref[...] = res_ref[...] + _dot(a_ref[...], w_ref[...])


def _mm_res(a, w, res):
    T, K = a.shape
    N = w.shape[1]
    tm = _tile(T, (1024, 512, 256, 128, 64, 32, 16, 8))
    tn = _tile(N, (1024, 512, 256, 128))
    return pl.pallas_call(
        _mm_res_kernel,
        grid=(T // tm, N // tn),
        in_specs=[
            pl.BlockSpec((tm, K), lambda i, j: (i, 0)),
            pl.BlockSpec((K, tn), lambda i, j: (0, j)),
            pl.BlockSpec((tm, tn), lambda i, j: (i, j)),
        ],
        out_specs=pl.BlockSpec((tm, tn), lambda i, j: (i, j)),
        out_shape=jax.ShapeDtypeStruct((T, N), F32),
        compiler_params=_cparams(("parallel", "parallel")),
        name="proj_residual",
    )(a, w, res)


def _mm_norm_kernel(x_ref, g_ref, w_ref, o_ref, h_scr):
    @pl.when(pl.program_id(1) == 0)
    def _():
        h_scr[...] = _rms(x_ref[...], g_ref[...]).astype(BF16)

    o_ref[...] = _dot(h_scr[...], w_ref[...]).astype(o_ref.dtype)


def _mm_norm(x, g, w):
    T, D = x.shape
    N = w.shape[1]
    tm = _tile(T, (512, 256, 128, 64, 32, 16, 8))
    tn = _tile(N, (1024, 512, 256, 128))
    return pl.pallas_call(
        _mm_norm_kernel,
        grid=(T // tm, N // tn),
        in_specs=[
            pl.BlockSpec((tm, D), lambda i, j: (i, 0)),
            pl.BlockSpec((1, D), lambda i, j: (0, 0)),
            pl.BlockSpec((D, tn), lambda i, j: (0, j)),
        ],
        out_specs=pl.BlockSpec((tm, tn), lambda i, j: (i, j)),
        out_shape=jax.ShapeDtypeStruct((T, N), BF16),
        scratch_shapes=[pltpu.VMEM((tm, D), BF16)],
        compiler_params=_cparams(("parallel", "arbitrary")),
        name="norm_proj",
    )(x, g, w)


def _xattn_kernel(q_ref, k_ref, v_ref, o_ref):
    D = q_ref.shape[1]
    hd = D // XATTN_HEADS
    scale = hd ** -0.5
    for h in range(XATTN_HEADS):
        sl = slice(h * hd, (h + 1) * hd)
        s = _dot_nt(q_ref[:, sl], k_ref[:, sl]) * scale
        s = s - jnp.max(s, axis=-1, keepdims=True)
        e = jnp.exp(s)
        p = e / jnp.sum(e, axis=-1, keepdims=True)
        o_ref[:, sl] = _dot(p.astype(BF16), v_ref[:, sl]).astype(o_ref.dtype)


def _xattn(q, kv, n_seq, seq_len, n_mem):
    T, D = q.shape
    tm = _tile(seq_len, (512, 256, 128, 64, 32, 16, 8))
    nT = seq_len // tm
    return pl.pallas_call(
        _xattn_kernel,
        grid=(n_seq, nT),
        in_specs=[
            pl.BlockSpec((tm, D), lambda b, i: (b * nT + i, 0)),
            pl.BlockSpec((n_mem, D), lambda b, i: (b, 0)),
            pl.BlockSpec((n_mem, D), lambda b, i: (b, 1)),
        ],
        out_specs=pl.BlockSpec((tm, D), lambda b, i: (b * nT + i, 0)),
        out_shape=jax.ShapeDtypeStruct((T, D), BF16),
        compiler_params=_cparams(("parallel", "parallel")),
        name="cross_attention",
    )(q, kv, kv)


def _router_kernel(x_ref, g_ref, wh_ref, wl_ref, b_ref, hf_ref, e_ref, gate_ref):
    hf = _rms(x_ref[...], g_ref[...])
    hf_ref[...] = hf.astype(hf_ref.dtype)
    logits = _dot3(hf, wh_ref[...], wl_ref[...]) + b_ref[...]
    E = logits.shape[1]
    eid = lax.broadcasted_iota(jnp.int32, logits.shape, 1).astype(F32)
    work = logits
    vals = []
    idxs = []
    for _ in range(TOP_K):
        m = jnp.max(work, axis=-1, keepdims=True)
        idx = jnp.min(jnp.where(work == m, eid, float(E)), axis=-1, keepdims=True)
        vals.append(m)
        idxs.append(idx.astype(jnp.int32))
        work = jnp.where(eid == idx, -jnp.inf, work)
    ex = [jnp.exp(vv - vals[0]) for vv in vals]
    den = ex[0]
    for t in ex[1:]:
        den = den + t
    kid = lax.broadcasted_iota(jnp.int32, (logits.shape[0], TOP_K), 1)
    e_out = jnp.zeros((logits.shape[0], TOP_K), jnp.int32)
    g_out = jnp.zeros((logits.shape[0], TOP_K), F32)
    for t in range(TOP_K):
        e_out = jnp.where(kid == t, idxs[t], e_out)
        g_out = jnp.where(kid == t, ex[t] / den, g_out)
    e_ref[...] = e_out
    gate_ref[...] = g_out


def _router(x, g, w_router, b_router):
    T, D = x.shape
    E = w_router.shape[1]
    wh, wl = _split2(w_router)
    tm = _tile(T, (256, 128, 64, 32, 16, 8))
    return pl.pallas_call(
        _router_kernel,
        grid=(T // tm,),
        in_specs=[
            pl.BlockSpec((tm, D), lambda i: (i, 0)),
            pl.BlockSpec((1, D), lambda i: (0, 0)),
            pl.BlockSpec((D, E), lambda i: (0, 0)),
            pl.BlockSpec((D, E), lambda i: (0, 0)),
            pl.BlockSpec((1, E), lambda i: (0, 0)),
        ],
        out_specs=[
            pl.BlockSpec((tm, D), lambda i: (i, 0)),
            pl.BlockSpec((tm, TOP_K), lambda i: (i, 0)),
            pl.BlockSpec((tm, TOP_K), lambda i: (i, 0)),
        ],
        out_shape=[jax.ShapeDtypeStruct((T, D), BF16),
                   jax.ShapeDtypeStruct((T, TOP_K), jnp.int32),
                   jax.ShapeDtypeStruct((T, TOP_K), F32)],
        compiler_params=_cparams(("parallel",)),
        name="router",
    )(x, g, wh, wl, b_router)


def _expert_kernel(te_ref, tv_ref, x_ref, wg_ref, wu_ref, bg_ref, bu_ref, wd_ref, bd_ref, o_ref, act_ref, *, nf):
    i = pl.program_id(0)
    s = pl.program_id(1)
    tf = wg_ref.shape[1]
    valid = tv_ref[i] > 0

    @pl.when(valid & (s < nf))
    def _():
        x = x_ref[...]
        g = _dot(x, wg_ref[...].astype(BF16)) + bg_ref[...]
        u = _dot(x, wu_ref[...].astype(BF16)) + bu_ref[...]
        g = jnp.minimum(g, SWIGLU_LIMIT)
        u = jnp.clip(u, -SWIGLU_LIMIT, SWIGLU_LIMIT)
        act = ((u + 1.0) * (g * _sigmoid(SWIGLU_ALPHA * g))).astype(BF16)
        for f in range(nf):
            @pl.when(s == f)
            def _(f=f):
                act_ref[:, f * tf:(f + 1) * tf] = act

    @pl.when(valid & (s >= nf))
    def _():
        y = _dot(act_ref[...], wd_ref[...].astype(BF16)) + bd_ref[...]
        o_ref[...] = y.astype(o_ref.dtype)

    @pl.when(jnp.logical_not(valid) & (s >= nf))
    def _():
        o_ref[...] = jnp.zeros_like(o_ref)


def _experts(xb, tile_e, tile_valid, w_gu, b_gu, w_dn, b_dn, tm):
    R, D = xb.shape
    E, _, F2 = w_gu.shape
    F = F2 // 2
    tf = _tile(F, (256, 128))
    tn = _tile(D, (512, 256, 128))
    nf = F // tf
    nn = D // tn
    n_tiles = R // tm
    b_gu3 = b_gu.reshape(E, 1, F2)
    b_dn3 = b_dn.reshape(E, 1, D)

    def fa(i, s, tv):
        return jnp.where(tv[i] > 0, jnp.minimum(s, nf - 1), nf - 1)

    def nb(i, s, tv):
        return jnp.where(tv[i] > 0, jnp.clip(s - nf, 0, nn - 1), nn - 1)

    grid_spec = pltpu.PrefetchScalarGridSpec(
        num_scalar_prefetch=2,
        grid=(n_tiles, nf + nn),
        in_specs=[
            pl.BlockSpec((tm, D), lambda i, s, te, tv: (i, 0)),
            pl.BlockSpec((None, D, tf), lambda i, s, te, tv: (te[i], 0, fa(i, s, tv))),
            pl.BlockSpec((None, D, tf), lambda i, s, te, tv: (te[i], 0, nf + fa(i, s, tv))),
            pl.BlockSpec((None, 1, tf), lambda i, s, te, tv: (te[i], 0, fa(i, s, tv))),
            pl.BlockSpec((None, 1, tf), lambda i, s, te, tv: (te[i], 0, nf + fa(i, s, tv))),
            pl.BlockSpec((None, F, tn), lambda i, s, te, tv: (te[i], 0, nb(i, s, tv))),
            pl.BlockSpec((None, 1, tn), lambda i, s, te, tv: (te[i], 0, nb(i, s, tv))),
        ],
        out_specs=pl.BlockSpec((tm, tn), lambda i, s, te, tv: (i, jnp.clip(s - nf, 0, nn - 1))),
        scratch_shapes=[pltpu.VMEM((tm, F), BF16)],
    )
    return pl.pallas_call(
        functools.partial(_expert_kernel, nf=nf),
        grid_spec=grid_spec,
        out_shape=jax.ShapeDtypeStruct((R, D), BF16),
        compiler_params=_cparams(("arbitrary", "arbitrary")),
        name="moe_experts",
    )(tile_e, tile_valid, xb, w_gu, w_gu, b_gu3, b_gu3, w_dn, b_dn3)


def _combine_kernel(x_ref, y_ref, gate_ref, g_ref, o_ref):
    acc = x_ref[...]
    gate = gate_ref[...]
    for t in range(TOP_K):
        acc = acc + gate[:, t:t + 1] * y_ref[t].astype(F32)
    o_ref[...] = _rms(acc, g_ref[...])


def _combine(x, y4, gate, final_g):
    T, D = x.shape
    tm = _tile(T, (256, 128, 64, 32, 16, 8))
    return pl.pallas_call(
        _combine_kernel,
        grid=(T // tm,),
        in_specs=[
            pl.BlockSpec((tm, D), lambda i: (i, 0)),
            pl.BlockSpec((TOP_K, tm, D), lambda i: (0, i, 0)),
            pl.BlockSpec((tm, TOP_K), lambda i: (i, 0)),
            pl.BlockSpec((1, D), lambda i: (0, 0)),
        ],
        out_specs=pl.BlockSpec((tm, D), lambda i: (i, 0)),
        out_shape=jax.ShapeDtypeStruct((T, D), F32),
        compiler_params=_cparams(("parallel",)),
        name="moe_combine_norm",
    )(x, y4, gate, final_g)


def _pad_rows(w, n):
    return jnp.pad(w, ((0, 0),) * (w.ndim - 2) + ((0, n - w.shape[-2]), (0, 0)))


def _pad_cols(w, n):
    return jnp.pad(w, ((0, 0),) * (w.ndim - 1) + ((0, n - w.shape[-1]),))


def _moe_tile_rows(n_assign, n_experts):
    for tm in (1024, 512, 256, 128, 64, 32, 16, 8):
        if n_assign >= 4 * n_experts * tm or tm == 8:
            return tm


def _layer(x, mem, n_seq, seq_len, norm_mix_g, w_in, shift_mu, decay_w0, decay_w2, iclr_a0, iclr_a2,
           gate_g2, k_k, k_a, r_k, lnx_g, lnx_b, conv_w, conv_b, conv_ln_g, conv_ln_b, w_branch, w_o,
           norm_x_g, norm_mem_g, w_xq, w_xkv, w_xo, norm_ffn_g, w_router, b_router, w_gu, b_gu,
           w_dn, b_dn, final_g):
    T, D = x.shape
    W = k_k.shape[0]
    DL = decay_w2.shape[1]
    AL = iclr_a2.shape[1]
    GL = gate_g2.shape[0]
    C = conv_w.shape[1]
    E = w_router.shape[1]
    n_mem = mem.shape[0] // n_seq
    row = lambda v: v.reshape(1, -1)

    o3 = 3 * W
    o4 = o3 + 2 * DL
    o5 = o4 + 2 * AL
    o6 = o5 + GL
    seg = lambda m, a, b, n: _pad_cols(m[..., a:b], n)
    lora_cols = lambda m: jnp.concatenate(
        [seg(m, o3, o3 + DL, LORA_PAD), seg(m, o3 + DL, o4, LORA_PAD), seg(m, o4, o4 + AL, LORA_PAD),
         seg(m, o4 + AL, o5, LORA_PAD), m[..., o5:o6]], axis=-1)
    w_rkv = w_in[:, :o3].astype(BF16)
    w_lora = lora_cols(w_in).astype(BF16)
    mu_rkv = row(shift_mu[:o3])
    mu_lora = row(lora_cols(shift_mu))
    w_ca = w_in[:, o6:o6 + C].astype(BF16)
    w_cb = w_in[:, o6 + C:o6 + 2 * C].astype(BF16)
    w_g1 = w_in[:, o6 + 2 * C:o6 + 2 * C + D].astype(BF16)
    w_g2 = w_in[:, o6 + 2 * C + D:].astype(BF16)
    g_mix = row(norm_mix_g)

    z_rkv = _inproj_shift(x, g_mix, w_rkv, mu_rkv, seq_len, _tile(o3, (1024, 512, 256, 128)))
    z_lora = _inproj_shift(x, g_mix, w_lora, mu_lora, seq_len, w_lora.shape[1])
    u = _inproj_glu(x, g_mix, w_ca, w_cb)

    w2h, w2l = _split2(_pad_rows(decay_w2, LORA_PAD))
    a2h, a2l = _split2(_pad_rows(iclr_a2, LORA_PAD))
    wkv_args = (z_rkv, z_lora, decay_w0.reshape(2, 1, W), w2h, w2l, iclr_a0.reshape(2, 1, W), a2h, a2l,
                row(k_k), row(k_a), row(r_k), n_seq, seq_len)
    o_f, bonus_f = _wkv(*wkv_args, bwd=False)
    o_b, bonus_b = _wkv(*wkv_args, bwd=True)
    o_rwkv = _rwkv_post(o_f, o_b, bonus_f, bonus_b, z_lora, gate_g2.astype(BF16), row(lnx_g), row(lnx_b))
    o_conv = _conv(u, conv_w, row(conv_b), row(conv_ln_g), row(conv_ln_b), seq_len)

    merged = _merge(x, g_mix, o_rwkv, o_conv, w_g1, w_g2, w_branch[:W].astype(BF16),
                    w_branch[W:].astype(BF16))
    x1 = _mm_res(merged, w_o.astype(BF16), x)

    q = _mm_norm(x1, row(norm_x_g), w_xq.astype(BF16))
    kv = _mm_norm(mem, row(norm_mem_g), w_xkv.astype(BF16))
    att = _xattn(q, kv, n_seq, seq_len, n_mem)
    x2 = _mm_res(att, w_xo.astype(BF16), x1)

    hf, top_e, gate = _router(x2, row(norm_ffn_g), w_router, row(b_router))

    A = T * TOP_K
    tm_e = _moe_tile_rows(A, E)
    n_tiles = (A + E * (tm_e - 1) + tm_e - 1) // tm_e
    n_rows = n_tiles * tm_e
    flat_e = top_e.reshape(A)
    iota_a = jnp.arange(A, dtype=jnp.int32)
    sorted_e, order = lax.sort((flat_e, iota_a), num_keys=1)
    _, rank = lax.sort((order, iota_a), num_keys=1)
    experts = jnp.arange(E, dtype=jnp.int32)
    start = jnp.searchsorted(sorted_e, experts, side='left').astype(jnp.int32)
    counts = jnp.searchsorted(sorted_e, experts, side='right').astype(jnp.int32) - start
    padded = (counts + tm_e - 1) // tm_e * tm_e
    pad_end = jnp.cumsum(padded)
    pad_start = pad_end - padded
    dest_of = pad_start[flat_e] + rank - start[flat_e]
    tile_start = jnp.arange(n_tiles, dtype=jnp.int32) * tm_e
    tile_e = jnp.minimum(jnp.searchsorted(pad_end, tile_start, side='right'), E - 1).astype(jnp.int32)
    tile_valid = jnp.clip(pad_start[tile_e] + counts[tile_e] - tile_start, 0, tm_e).astype(jnp.int32)
    rows = jnp.arange(n_rows, dtype=jnp.int32)
    row_e = jnp.repeat(tile_e, tm_e)
    row_off = rows - pad_start[row_e]
    src = order[jnp.minimum(start[row_e] + row_off, A - 1)] // TOP_K
    row_tok = jnp.where(row_off < counts[row_e], src, rows % T)

    xb = hf.at[row_tok].get(mode='promise_in_bounds')
    yb = _experts(xb, tile_e, tile_valid, w_gu, b_gu, w_dn, b_dn, tm_e)
    dest_slot_major = dest_of.reshape(T, TOP_K).T.reshape(A)
    y4 = yb.at[dest_slot_major].get(mode='promise_in_bounds').reshape(TOP_K, T, D)
    return _combine(x2, y4, gate, row(final_g))


def kernel(x_prompt, x_sample, mem_prompt, mem_sample, norm_mix_g, w_in, shift_mu, decay_w0, decay_w2,
           iclr_a0, iclr_a2, gate_g2, k_k, k_a, r_k, lnx_g, lnx_b, conv_w, conv_b, conv_ln_g, conv_ln_b,
           w_branch, w_o, norm_x_g, norm_mem_g, w_xq, w_xkv, w_xo, norm_ffn_g, w_router, b_router,
           w_gu, b_gu, w_dn, b_dn, final_g):
    layer_params = (norm_mix_g, w_in, shift_mu, decay_w0, decay_w2, iclr_a0, iclr_a2, gate_g2, k_k, k_a,
                    r_k, lnx_g, lnx_b, conv_w, conv_b, conv_ln_g, conv_ln_b, w_branch, w_o, norm_x_g,
                    norm_mem_g, w_xq, w_xkv, w_xo, norm_ffn_g, w_router, b_router, w_gu, b_gu, w_dn, b_dn)
    assert all(p.shape[0] == 1 for p in layer_params), "single-layer stack expected"
    bp, seq_len, D = x_prompt.shape
    bs = x_sample.shape[0]
    assert x_sample.shape[1] == seq_len
    n_seq = bp + bs
    x = jnp.concatenate([x_prompt, x_sample], axis=0).reshape(n_seq * seq_len, D)
    mem = jnp.concatenate([mem_prompt, mem_sample], axis=0).reshape(-1, D)
    y = _layer(x, mem, n_seq, seq_len, *[p[0] for p in layer_params], final_g)
    y = y.reshape(n_seq, seq_len, D)
    return y[:bp], y[bp:]
```

```python
import functools
import math

import jax
import jax.numpy as jnp
from jax import lax
from jax.experimental import pallas as pl
from jax.experimental.pallas import tpu as pltpu

F32 = jnp.float32
BF16 = jnp.bfloat16

RWKV_HEAD = 64
DECAY_SCALE = math.exp(-0.5)
LNX_EPS = RWKV_HEAD * 1e-5
RMS_EPS = 1e-5
LN_EPS = 1e-5
XATTN_HEADS = 4
TOP_K = 4
SWIGLU_ALPHA = 1.702
SWIGLU_LIMIT = 7.0

LANES = 128
SUBLANES = 8
WKV_CHUNK = 64
LORA_PAD = 128
VMEM_LIMIT = 56 * 1024 * 1024


def _cparams(sem):
    return pltpu.CompilerParams(dimension_semantics=sem, vmem_limit_bytes=VMEM_LIMIT)


def _tile(n, prefs):
    for p in prefs:
        if n % p == 0:
            return p
    return n


def _dot(a, b):
    return jnp.dot(a, b, preferred_element_type=F32)


def _dot_nt(a, b):
    return lax.dot_general(a, b, (((1,), (1,)), ((), ())), preferred_element_type=F32)


def _dot_tn(a, b):
    return lax.dot_general(a, b, (((0,), (0,)), ((), ())), preferred_element_type=F32)


def _split2(x):
    hi = x.astype(BF16)
    lo = (x - hi.astype(F32)).astype(BF16)
    return hi, lo


def _split3(x):
    hi = x.astype(BF16)
    r1 = x - hi.astype(F32)
    mid = r1.astype(BF16)
    lo = (r1 - mid.astype(F32)).astype(BF16)
    return hi, mid, lo


def _dot_exact_rhs(x, w_bf16):
    h, m, l = _split3(x)
    return _dot(h, w_bf16) + _dot(m, w_bf16) + _dot(l, w_bf16)


def _dot_exact_lhs(w_bf16, x):
    h, m, l = _split3(x)
    return _dot(w_bf16, h) + _dot(w_bf16, m) + _dot(w_bf16, l)


def _dot3(x, w_hi, w_lo):
    xh, xl = _split2(x)
    return _dot(xh, w_hi) + _dot(xl, w_hi) + _dot(xh, w_lo)


def _rms(x, g):
    return x * lax.rsqrt(jnp.mean(x * x, axis=-1, keepdims=True) + RMS_EPS) * g


def _sigmoid(x):
    return 0.5 * jnp.tanh(0.5 * x) + 0.5


def _inproj_shift_kernel(x_ref, xp_ref, xn_ref, g_ref, w_ref, mu_ref, o_ref, h_scr, hp_scr, hn_scr,
                         *, seq_len):
    i = pl.program_id(0)
    j = pl.program_id(1)
    tm = x_ref.shape[0]

    @pl.when(j == 0)
    def _():
        g = g_ref[...]
        h_scr[...] = _rms(x_ref[...], g).astype(BF16)
        hp_scr[...] = _rms(xp_ref[...], g).astype(BF16)
        hn_scr[...] = _rms(xn_ref[...], g).astype(BF16)

    w = w_ref[...]
    p = _dot(h_scr[...], w)
    pp = _dot(hp_scr[...], w)[SUBLANES - 1:SUBLANES, :]
    pn = _dot(hn_scr[...], w)[0:1, :]
    first = (i * tm) % seq_len == 0
    last = ((i + 1) * tm) % seq_len == 0
    pp = jnp.where(first, 0.0, pp)
    pn = jnp.where(last, 0.0, pn)
    rid = lax.broadcasted_iota(jnp.int32, p.shape, 0)
    up = jnp.where(rid == 0, pp, pltpu.roll(p, 1, 0))
    dn = jnp.where(rid == tm - 1, pn, pltpu.roll(p, tm - 1, 0))
    o_ref[...] = p + mu_ref[...] * (0.5 * (up + dn) - p)


def _inproj_shift(x, g, w_bf16, mu, seq_len, tn):
    T, D = x.shape
    N = w_bf16.shape[1]
    tm = _tile(seq_len, (1024, 512, 256, 128, 64, 32, 16, 8))
    nb8 = T // SUBLANES
    r8 = tm // SUBLANES
    return pl.pallas_call(
        functools.partial(_inproj_shift_kernel, seq_len=seq_len),
        grid=(T // tm, N // tn),
        in_specs=[
            pl.BlockSpec((tm, D), lambda i, j: (i, 0)),
            pl.BlockSpec((SUBLANES, D), lambda i, j: (jnp.maximum(i * r8 - 1, 0), 0)),
            pl.BlockSpec((SUBLANES, D), lambda i, j: (jnp.minimum((i + 1) * r8, nb8 - 1), 0)),
            pl.BlockSpec((1, D), lambda i, j: (0, 0)),
            pl.BlockSpec((D, tn), lambda i, j: (0, j)),
            pl.BlockSpec((1, tn), lambda i, j: (0, j)),
        ],
        out_specs=pl.BlockSpec((tm, tn), lambda i, j: (i, j)),
        out_shape=jax.ShapeDtypeStruct((T, N), F32),
        scratch_shapes=[pltpu.VMEM((tm, D), BF16), pltpu.VMEM((SUBLANES, D), BF16),
                        pltpu.VMEM((SUBLANES, D), BF16)],
        compiler_params=_cparams(("parallel", "arbitrary")),
        name="inproj_shift",
    )(x, x, x, g, w_bf16, mu)


def _inproj_glu_kernel(x_ref, g_ref, wa_ref, wb_ref, o_ref, h_scr):
    @pl.when(pl.program_id(1) == 0)
    def _():
        h_scr[...] = _rms(x_ref[...], g_ref[...]).astype(BF16)

    h = h_scr[...]
    a = _dot(h, wa_ref[...])
    b = _dot(h, wb_ref[...])
    o_ref[...] = a * _sigmoid(b)


def _inproj_glu(x, g, wa, wb):
    T, D = x.shape
    N = wa.shape[1]
    tm = _tile(T, (512, 256, 128, 64, 32, 16, 8))
    tn = _tile(N, (512, 256, 128))
    return pl.pallas_call(
        _inproj_glu_kernel,
        grid=(T // tm, N // tn),
        in_specs=[
            pl.BlockSpec((tm, D), lambda i, j: (i, 0)),
            pl.BlockSpec((1, D), lambda i, j: (0, 0)),
            pl.BlockSpec((D, tn), lambda i, j: (0, j)),
            pl.BlockSpec((D, tn), lambda i, j: (0, j)),
        ],
        out_specs=pl.BlockSpec((tm, tn), lambda i, j: (i, j)),
        out_shape=jax.ShapeDtypeStruct((T, N), F32),
        scratch_shapes=[pltpu.VMEM((tm, D), BF16)],
        compiler_params=_cparams(("parallel", "arbitrary")),
        name="inproj_glu",
    )(x, g, wa, wb)


def _wkv_kernel(zr_ref, zk_ref, zv_ref, dd_ref, ad_ref, w0_ref, w2h_ref, w2l_ref, a0_ref, a2h_ref,
                kk_ref, ka_ref, rk_ref, o_ref, bonus_ref, state_ref, *, n_pairs, bwd):
    L = WKV_CHUNK
    PW = 2 * RWKV_HEAD
    TT = zr_ref.shape[0]
    n_chunks = TT // L

    @pl.when(pl.program_id(2) == 0)
    def _():
        state_ref[...] = jnp.zeros_like(state_ref)

    row = lax.broadcasted_iota(jnp.int32, (L, 2 * L), 0)
    col = lax.broadcasted_iota(jnp.int32, (L, 2 * L), 1) % L
    strict = (col > row) if bwd else (col < row)
    incl = (col >= row) if bwd else (col <= row)
    ipk = jnp.where(col == row, 1.0, 0.0)
    lane = lax.broadcasted_iota(jnp.int32, (1, PW), 1)
    m0 = lane < RWKV_HEAD
    lane2 = lax.broadcasted_iota(jnp.int32, (1, 2 * PW), 1) % PW
    m0w = lane2 < RWKV_HEAD
    srow = lax.broadcasted_iota(jnp.int32, (PW, PW), 0)
    scol = lax.broadcasted_iota(jnp.int32, (PW, PW), 1)
    same_head = (srow // RWKV_HEAD) == (scol // RWKV_HEAD)
    eye = srow == scol
    HSW = min(zr_ref.shape[1], 2 * PW)
    hrow = lax.broadcasted_iota(jnp.int32, (HSW, HSW), 0)
    hcol = lax.broadcasted_iota(jnp.int32, (HSW, HSW), 1)
    head_ones_w = jnp.where((hrow // RWKV_HEAD) == (hcol // RWKV_HEAD), 1.0, 0.0).astype(BF16)

    def bd(x):
        return jnp.concatenate([jnp.where(m0, x, 0.0), jnp.where(m0, 0.0, x)], axis=0).astype(BF16)

    def bd2(x):
        return jnp.concatenate([jnp.where(m0w, x, 0.0), jnp.where(m0w, 0.0, x)], axis=0).astype(BF16)

    w0 = w0_ref[...]
    a0 = a0_ref[...]
    k_k = kk_ref[...]
    k_a = ka_ref[...]
    r_k = rk_ref[...]

    def head_sum(x, two_pass):
        n = x.shape[0]
        outs = []
        for c in range(x.shape[1] // HSW):
            xc = x[:, c * HSW:(c + 1) * HSW]
            if two_pass:
                hi, lo = _split2(xc)
                res = _dot(jnp.concatenate([hi, lo], axis=0), head_ones_w)
                outs.append(res[:n] + res[n:])
            else:
                outs.append(_dot(xc.astype(BF16), head_ones_w))
        return jnp.concatenate(outs, axis=1) if len(outs) > 1 else outs[0]

    r = zr_ref[...]
    k = zk_ref[...]
    v = zv_ref[...]
    dlin = w0 + _dot3(jnp.tanh(dd_ref[...]), w2h_ref[...], w2l_ref[...])
    lw = -DECAY_SCALE * _sigmoid(dlin)
    iclr = _sigmoid(a0 + _dot(ad_ref[...].astype(BF16), a2h_ref[...]))
    kkr = k * k_k
    kmod = k * (1.0 + (iclr - 1.0) * k_a)
    cum = _chunk_cumsum(lw, bwd)
    e_in = jnp.exp(cum)
    e_ex = jnp.exp(cum - lw)
    e_ng = jnp.exp(-cum)
    kk = kkr * lax.rsqrt(jnp.maximum(head_sum(kkr * kkr, False), 1e-24))
    bonus_ref[...] = head_sum(r * kmod * r_k, True) * v
    b_all = kk * iclr
    ah_all = -kk * e_ex
    rh_all = r * e_in
    bh_all = b_all * e_ng
    kh_all = kmod * e_ng

    chunk_order = list(range(n_chunks - 1, -1, -1) if bwd else range(n_chunks))
    pair_vals = []
    for p in range(n_pairs):
        sl = slice(p * PW, (p + 1) * PW)
        pair_vals.append(dict(b=b_all[:, sl], ah=ah_all[:, sl], rh=rh_all[:, sl],
                              bh=bh_all[:, sl], kh=kh_all[:, sl]))

    probs = []
    for ci in chunk_order:
        for p in range(n_pairs):
            sl = slice(p * PW, (p + 1) * PW)
            rs = slice(ci * L, (ci + 1) * L)
            pv = pair_vals[p]
            q = dict(p=p, ci=ci, sl=sl, rs=rs, ah=pv["ah"][rs], rh=pv["rh"][rs], v=v[rs, sl])
            lhs = jnp.concatenate([q["ah"], q["rh"]], axis=0).astype(BF16)
            g = _dot_nt(lhs, jnp.concatenate([bd(pv["bh"][rs]), bd(pv["kh"][rs])], axis=0))
            q["a"] = jnp.where(strict, g[:L, :2 * L], 0.0)
            q["ak"] = jnp.where(strict, g[:L, 2 * L:], 0.0)
            q["rb"] = jnp.where(incl, g[L:, :2 * L], 0.0)
            q["rk"] = jnp.where(incl, g[L:, 2 * L:], 0.0)
            probs.append(q)
    for q in probs:
        q["kv"] = _dot(jnp.concatenate([q["ak"], q["rk"]], axis=0).astype(BF16), bd(q["v"]))
    for q in probs:
        q["ai"] = _dot(q["a"].astype(BF16), bd(q["a"]))
        q["t"] = ipk + q["a"]
    for lvl in range(1, 6):
        more = lvl < 5
        for q in probs:
            parts = [q["t"]] + ([q["ai"]] if more else [])
            res = _dot(jnp.concatenate(parts, axis=0).astype(BF16), bd(q["ai"]))
            q["t"] = q["t"] + res[:L]
            if more:
                q["ai"] = res[L:]
    for q in probs:
        q["rbt"] = _dot(q["rb"].astype(BF16), bd(q["t"]))
    for q in probs:
        y0 = jnp.concatenate([q["ah"], q["kv"][:L]], axis=1)
        q["ry"] = _dot(jnp.concatenate([q["t"], q["rbt"]], axis=0).astype(BF16), bd2(y0))
    for q in probs:
        p, rs, sl, ry = q["p"], q["rs"], q["sl"], q["ry"]
        end = q["ci"] * L if bwd else q["ci"] * L + L - 1
        tot = cum[end:end + 1, sl]
        e_rm = jnp.exp(tot - cum[rs, sl])
        q["rt"] = q["rh"] + ry[L:, :PW]
        q["ob"] = ry[L:, PW:] + q["kv"][L:]
        lhs_t = jnp.concatenate([pair_vals[p]["b"][rs] * e_rm, kmod[rs, sl] * e_rm], axis=0).astype(BF16)
        rhs_t = jnp.concatenate(
            [ry[:L], jnp.concatenate([jnp.zeros((L, PW), F32), q["v"]], axis=1)], axis=0).astype(BF16)
        mn = _dot_tn(lhs_t, rhs_t)
        q["mm"] = jnp.where(eye, jnp.exp(tot), 0.0) + jnp.where(same_head, mn[:, :PW], 0.0)
        q["nn"] = jnp.where(same_head, mn[:, PW:], 0.0)
    states = [state_ref[p] for p in range(n_pairs)]
    for q in probs:
        p = q["p"]
        res = _dot(jnp.concatenate([q["rt"], q["mm"]], axis=0).astype(BF16), states[p].astype(BF16))
        o_ref[q["rs"], q["sl"]] = res[:L] + q["ob"]
        states[p] = res[L:] + q["nn"]
    for p in range(n_pairs):
        state_ref[p] = states[p]


def _chunk_cumsum(x, bwd):
    L = WKV_CHUNK
    n = x.shape[0]
    rin = lax.broadcasted_iota(jnp.int32, (n, 1), 0) % L
    s = 1
    while s < L:
        if bwd:
            x = x + jnp.where(rin < L - s, pltpu.roll(x, n - s, 0), 0.0)
        else:
            x = x + jnp.where(rin >= s, pltpu.roll(x, s, 0), 0.0)
        s *= 2
    return x


def _wkv(z_rkv, z_lora, decay_w0, w2h, w2l, iclr_a0, a2_bf16, k_k, k_a, r_k, n_seq, seq_len, bwd):
    T = z_rkv.shape[0]
    W = z_rkv.shape[1] // 3
    PW = 2 * RWKV_HEAD
    n_pairs = _tile(W // PW, (4, 2, 1))
    GW = n_pairs * PW
    n_groups = W // GW
    TT = _tile(seq_len, (256, 128, 64))
    nT = seq_len // TT
    d = int(bwd)

    def tb(b, c):
        return b * nT + (nT - 1 - c if bwd else c)

    zspec = lambda off: pl.BlockSpec((TT, GW), lambda b, g, c: (tb(b, c), off * n_groups + g))
    pspec = pl.BlockSpec((None, 1, GW), lambda b, g, c: (d, 0, g))
    lspec = pl.BlockSpec((None, LORA_PAD, GW), lambda b, g, c: (d, 0, g))
    cspec = pl.BlockSpec((1, GW), lambda b, g, c: (0, g))
    ospec = pl.BlockSpec((TT, GW), lambda b, g, c: (tb(b, c), g))
    return pl.pallas_call(
        functools.partial(_wkv_kernel, n_pairs=n_pairs, bwd=bwd),
        grid=(n_seq, n_groups, nT),
        in_specs=[
            zspec(0), zspec(1), zspec(2),
            pl.BlockSpec((TT, LORA_PAD), lambda b, g, c: (tb(b, c), d)),
            pl.BlockSpec((TT, LORA_PAD), lambda b, g, c: (tb(b, c), 2 + d)),
            pspec, lspec, lspec, pspec, lspec, cspec, cspec, cspec,
        ],
        out_specs=[ospec, ospec],
        out_shape=[jax.ShapeDtypeStruct((T, W), F32), jax.ShapeDtypeStruct((T, W), F32)],
        scratch_shapes=[pltpu.VMEM((n_pairs, PW, PW), F32)],
        compiler_params=_cparams(("parallel", "parallel", "arbitrary")),
        name="wkv_scan_bwd" if bwd else "wkv_scan_fwd",
    )(z_rkv, z_rkv, z_rkv, z_lora, z_lora, decay_w0, w2h, w2l, iclr_a0, a2_bf16, k_k, k_a, r_k)


def _rwkv_post_kernel(of_ref, ob_ref, bf_ref, bb_ref, gd_ref, g2_ref, lg_ref, lb_ref, out_ref):
    W = out_ref.shape[1]
    PW = 2 * RWKV_HEAD
    srow = lax.broadcasted_iota(jnp.int32, (PW, PW), 0)
    scol = lax.broadcasted_iota(jnp.int32, (PW, PW), 1)
    head_mean = jnp.where((srow // RWKV_HEAD) == (scol // RWKV_HEAD), 1.0, 0.0).astype(BF16)
    inv = 1.0 / RWKV_HEAD
    gate = _dot(_sigmoid(gd_ref[...]).astype(BF16), g2_ref[...])
    for p in range(W // PW):
        sl = slice(p * PW, (p + 1) * PW)
        o = of_ref[:, sl] + ob_ref[:, sl]
        mean = _dot_exact_rhs(o, head_mean) * inv
        oc = o - mean
        var = _dot_exact_rhs(oc * oc, head_mean) * inv
        y = oc * lax.rsqrt(var + LNX_EPS) * lg_ref[:, sl] + lb_ref[:, sl]
        y = y + bf_ref[:, sl] + bb_ref[:, sl]
        out_ref[:, sl] = (y * gate[:, sl]).astype(out_ref.dtype)


def _rwkv_post(o_f, o_b, bonus_f, bonus_b, z_lora, g2, lnx_g, lnx_b):
    T, W = o_f.shape
    tm = _tile(T, (256, 128, 64, 32, 16, 8))
    GL = g2.shape[0]
    gd_blk = (4 * LORA_PAD) // GL
    tspec = pl.BlockSpec((tm, W), lambda i: (i, 0))
    return pl.pallas_call(
        _rwkv_post_kernel,
        grid=(T // tm,),
        in_specs=[
            tspec, tspec, tspec, tspec,
            pl.BlockSpec((tm, GL), lambda i: (i, gd_blk)),
            pl.BlockSpec((GL, W), lambda i: (0, 0)),
            pl.BlockSpec((1, W), lambda i: (0, 0)),
            pl.BlockSpec((1, W), lambda i: (0, 0)),
        ],
        out_specs=pl.BlockSpec((tm, W), lambda i: (i, 0)),
        out_shape=jax.ShapeDtypeStruct((T, W), BF16),
        compiler_params=_cparams(("parallel",)),
        name="rwkv_post",
    )(o_f, o_b, bonus_f, bonus_b, z_lora, g2, lnx_g, lnx_b)


def _conv_kernel(u_ref, up_ref, un_ref, w_ref, b_ref, lg_ref, lb_ref, o_ref, ext_scr, *, seq_len, halo):
    i = pl.program_id(0)
    tm = u_ref.shape[0]
    K = w_ref.shape[0]
    first = (i * tm) % seq_len == 0
    last = ((i + 1) * tm) % seq_len == 0
    n_ext = tm + 2 * halo
    ext = jnp.concatenate([jnp.where(first, 0.0, up_ref[...]), u_ref[...],
                           jnp.where(last, 0.0, un_ref[...])], axis=0)
    ext_scr[0] = ext
    for r in range(1, SUBLANES):
        ext_scr[r] = pltpu.roll(ext, n_ext - r, 0)
    sub = min(tm, 32)
    base = halo - K // 2
    for s in range(tm // sub):
        acc = jnp.zeros((sub, u_ref.shape[1]), F32)
        for j in range(K):
            off = base + j
            row0 = s * sub + (off // SUBLANES) * SUBLANES
            acc = acc + w_ref[j:j + 1, :] * ext_scr[off % SUBLANES, row0:row0 + sub, :]
        acc = acc + b_ref[...]
        mean = jnp.mean(acc, axis=-1, keepdims=True)
        xc = acc - mean
        var = jnp.mean(xc * xc, axis=-1, keepdims=True)
        y = xc * lax.rsqrt(var + LN_EPS) * lg_ref[...] + lb_ref[...]
        o_ref[s * sub:(s + 1) * sub, :] = (y * _sigmoid(y)).astype(o_ref.dtype)


def _conv(u, conv_w, conv_b, ln_g, ln_b, seq_len):
    T, C = u.shape
    K = conv_w.shape[0]
    halo = 16
    assert K // 2 <= halo
    tm = _tile(seq_len, (128, 64, 32, 16))
    rh = tm // halo
    nbh = T // halo
    return pl.pallas_call(
        functools.partial(_conv_kernel, seq_len=seq_len, halo=halo),
        grid=(T // tm,),
        in_specs=[
            pl.BlockSpec((tm, C), lambda i: (i, 0)),
            pl.BlockSpec((halo, C), lambda i: (jnp.maximum(i * rh - 1, 0), 0)),
            pl.BlockSpec((halo, C), lambda i: (jnp.minimum((i + 1) * rh, nbh - 1), 0)),
            pl.BlockSpec((K, C), lambda i: (0, 0)),
            pl.BlockSpec((1, C), lambda i: (0, 0)),
            pl.BlockSpec((1, C), lambda i: (0, 0)),
            pl.BlockSpec((1, C), lambda i: (0, 0)),
        ],
        out_specs=pl.BlockSpec((tm, C), lambda i: (i, 0)),
        out_shape=jax.ShapeDtypeStruct((T, C), BF16),
        scratch_shapes=[pltpu.VMEM((SUBLANES, tm + 2 * halo, C), F32)],
        compiler_params=_cparams(("parallel",)),
        name="conformer_conv",
    )(u, u, u, conv_w, conv_b, ln_g, ln_b)


def _merge_kernel(x_ref, g_ref, orw_ref, ocv_ref, wg1_ref, wg2_ref, wb1_ref, wb2_ref, o_ref, h_scr):
    @pl.when(pl.program_id(1) == 0)
    def _():
        h_scr[...] = _rms(x_ref[...], g_ref[...]).astype(BF16)

    h = h_scr[...]
    g1 = _sigmoid(_dot(h, wg1_ref[...]))
    g2 = _sigmoid(_dot(h, wg2_ref[...]))
    y1 = _dot(orw_ref[...], wb1_ref[...])
    y2 = _dot(ocv_ref[...], wb2_ref[...])
    o_ref[...] = (g1 * y1 + g2 * y2).astype(o_ref.dtype)


def _merge(x, g, o_rwkv, o_conv, wg1, wg2, wb1, wb2):
    T, D = x.shape
    W = o_rwkv.shape[1]
    C = o_conv.shape[1]
    tm = _tile(T, (512, 256, 128, 64, 32, 16, 8))
    tn = _tile(D, (512, 256, 128))
    return pl.pallas_call(
        _merge_kernel,
        grid=(T // tm, D // tn),
        in_specs=[
            pl.BlockSpec((tm, D), lambda i, j: (i, 0)),
            pl.BlockSpec((1, D), lambda i, j: (0, 0)),
            pl.BlockSpec((tm, W), lambda i, j: (i, 0)),
            pl.BlockSpec((tm, C), lambda i, j: (i, 0)),
            pl.BlockSpec((D, tn), lambda i, j: (0, j)),
            pl.BlockSpec((D, tn), lambda i, j: (0, j)),
            pl.BlockSpec((W, tn), lambda i, j: (0, j)),
            pl.BlockSpec((C, tn), lambda i, j: (0, j)),
        ],
        out_specs=pl.BlockSpec((tm, tn), lambda i, j: (i, j)),
        out_shape=jax.ShapeDtypeStruct((T, D), BF16),
        scratch_shapes=[pltpu.VMEM((tm, D), BF16)],
        compiler_params=_cparams(("parallel", "arbitrary")),
        name="merge_gates",
    )(x, g, o_rwkv, o_conv, wg1, wg2, wb1, wb2)


def _mm_res_kernel(a_ref, w_ref, res_ref, o_ref):
    o_ref[...] = res_ref[...] + _dot(a_ref[...], w_ref[...])


def _mm_res(a, w, res):
    T, K = a.shape
    N = w.shape[1]
    tm = _tile(T, (1024, 512, 256, 128, 64, 32, 16, 8))
    tn = _tile(N, (1024, 512, 256, 128))
    return pl.pallas_call(
        _mm_res_kernel,
        grid=(T // tm, N // tn),
        in_specs=[
            pl.BlockSpec((tm, K), lambda i, j: (i, 0)),
            pl.BlockSpec((K, tn), lambda i, j: (0, j)),
            pl.BlockSpec((tm, tn), lambda i, j: (i, j)),
        ],
        out_specs=pl.BlockSpec((tm, tn), lambda i, j: (i, j)),
        out_shape=jax.ShapeDtypeStruct((T, N), F32),
        compiler_params=_cparams(("parallel", "parallel")),
        name="proj_residual",
    )(a, w, res)


def _mm_norm_kernel(x_ref, g_ref, w_ref, o_ref, h_scr):
    @pl.when(pl.program_id(1) == 0)
    def _():
        h_scr[...] = _rms(x_ref[...], g_ref[...]).astype(BF16)

    o_ref[...] = _dot(h_scr[...], w_ref[...]).astype(o_ref.dtype)


def _mm_norm(x, g, w):
    T, D = x.shape
    N = w.shape[1]
    tm = _tile(T, (512, 256, 128, 64, 32, 16, 8))
    tn = _tile(N, (1024, 512, 256, 128))
    return pl.pallas_call(
        _mm_norm_kernel,
        grid=(T // tm, N // tn),
        in_specs=[
            pl.BlockSpec((tm, D), lambda i, j: (i, 0)),
            pl.BlockSpec((1, D), lambda i, j: (0, 0)),
            pl.BlockSpec((D, tn), lambda i, j: (0, j)),
        ],
        out_specs=pl.BlockSpec((tm, tn), lambda i, j: (i, j)),
        out_shape=jax.ShapeDtypeStruct((T, N), BF16),
        scratch_shapes=[pltpu.VMEM((tm, D), BF16)],
        compiler_params=_cparams(("parallel", "arbitrary")),
        name="norm_proj",
    )(x, g, w)


def _xattn_kernel(q_ref, k_ref, v_ref, o_ref):
    D = q_ref.shape[1]
    hd = D // XATTN_HEADS
    scale = hd ** -0.5
    for h in range(XATTN_HEADS):
        sl = slice(h * hd, (h + 1) * hd)
        s = _dot_nt(q_ref[:, sl], k_ref[:, sl]) * scale
        s = s - jnp.max(s, axis=-1, keepdims=True)
        e = jnp.exp(s)
        p = e / jnp.sum(e, axis=-1, keepdims=True)
        o_ref[:, sl] = _dot(p.astype(BF16), v_ref[:, sl]).astype(o_ref.dtype)


def _xattn(q, kv, n_seq, seq_len, n_mem):
    T, D = q.shape
    tm = _tile(seq_len, (512, 256, 128, 64, 32, 16, 8))
    nT = seq_len // tm
    return pl.pallas_call(
        _xattn_kernel,
        grid=(n_seq, nT),
        in_specs=[
            pl.BlockSpec((tm, D), lambda b, i: (b * nT + i, 0)),
            pl.BlockSpec((n_mem, D), lambda b, i: (b, 0)),
            pl.BlockSpec((n_mem, D), lambda b, i: (b, 1)),
        ],
        out_specs=pl.BlockSpec((tm, D), lambda b, i: (b * nT + i, 0)),
        out_shape=jax.ShapeDtypeStruct((T, D), BF16),
        compiler_params=_cparams(("parallel", "parallel")),
        name="cross_attention",
    )(q, kv, kv)


def _router_kernel(x_ref, g_ref, wh_ref, wl_ref, b_ref, hf_ref, e_ref, gate_ref):
    hf = _rms(x_ref[...], g_ref[...])
    hf_ref[...] = hf.astype(hf_ref.dtype)
    logits = _dot3(hf, wh_ref[...], wl_ref[...]) + b_ref[...]
    E = logits.shape[1]
    eid = lax.broadcasted_iota(jnp.int32, logits.shape, 1).astype(F32)
    work = logits
    vals = []
    idxs = []
    for _ in range(TOP_K):
        m = jnp.max(work, axis=-1, keepdims=True)
        idx = jnp.min(jnp.where(work == m, eid, float(E)), axis=-1, keepdims=True)
        vals.append(m)
        idxs.append(idx.astype(jnp.int32))
        work = jnp.where(eid == idx, -jnp.inf, work)
    ex = [jnp.exp(vv - vals[0]) for vv in vals]
    den = ex[0]
    for t in ex[1:]:
        den = den + t
    kid = lax.broadcasted_iota(jnp.int32, (logits.shape[0], TOP_K), 1)
    e_out = jnp.zeros((logits.shape[0], TOP_K), jnp.int32)
    g_out = jnp.zeros((logits.shape[0], TOP_K), F32)
    for t in range(TOP_K):
        e_out = jnp.where(kid == t, idxs[t], e_out)
        g_out = jnp.where(kid == t, ex[t] / den, g_out)
    e_ref[...] = e_out
    gate_ref[...] = g_out


def _router(x, g, w_router, b_router):
    T, D = x.shape
    E = w_router.shape[1]
    wh, wl = _split2(w_router)
    tm = _tile(T, (256, 128, 64, 32, 16, 8))
    return pl.pallas_call(
        _router_kernel,
        grid=(T // tm,),
        in_specs=[
            pl.BlockSpec((tm, D), lambda i: (i, 0)),
            pl.BlockSpec((1, D), lambda i: (0, 0)),
            pl.BlockSpec((D, E), lambda i: (0, 0)),
            pl.BlockSpec((D, E), lambda i: (0, 0)),
            pl.BlockSpec((1, E), lambda i: (0, 0)),
        ],
        out_specs=[
            pl.BlockSpec((tm, D), lambda i: (i, 0)),
            pl.BlockSpec((tm, TOP_K), lambda i: (i, 0)),
            pl.BlockSpec((tm, TOP_K), lambda i: (i, 0)),
        ],
        out_shape=[jax.ShapeDtypeStruct((T, D), BF16),
                   jax.ShapeDtypeStruct((T, TOP_K), jnp.int32),
                   jax.ShapeDtypeStruct((T, TOP_K), F32)],
        compiler_params=_cparams(("parallel",)),
        name="router",
    )(x, g, wh, wl, b_router)


def _expert_kernel(te_ref, tv_ref, x_ref, wg_ref, wu_ref, bg_ref, bu_ref, wd_ref, bd_ref, o_ref, act_ref, *, nf):
    i = pl.program_id(0)
    s = pl.program_id(1)
    tf = wg_ref.shape[1]
    valid = tv_ref[i] > 0

    @pl.when(valid & (s < nf))
    def _():
        x = x_ref[...]
        g = _dot(x, wg_ref[...].astype(BF16)) + bg_ref[...]
        u = _dot(x, wu_ref[...].astype(BF16)) + bu_ref[...]
        g = jnp.minimum(g, SWIGLU_LIMIT)
        u = jnp.clip(u, -SWIGLU_LIMIT, SWIGLU_LIMIT)
        act = ((u + 1.0) * (g * _sigmoid(SWIGLU_ALPHA * g))).astype(BF16)
        for f in range(nf):
            @pl.when(s == f)
            def _(f=f):
                act_ref[:, f * tf:(f + 1) * tf] = act

    @pl.when(valid & (s >= nf))
    def _():
        y = _dot(act_ref[...], wd_ref[...].astype(BF16)) + bd_ref[...]
        o_ref[...] = y.astype(o_ref.dtype)

    @pl.when(jnp.logical_not(valid) & (s >= nf))
    def _():
        o_ref[...] = jnp.zeros_like(o_ref)


def _experts(xb, tile_e, tile_valid, w_gu, b_gu, w_dn, b_dn, tm):
    R, D = xb.shape
    E, _, F2 = w_gu.shape
    F = F2 // 2
    tf = _tile(F, (512, 256, 128))
    tn = _tile(D, (512, 256, 128))
    nf = F // tf
    nn = D // tn
    n_tiles = R // tm
    b_gu3 = b_gu.reshape(E, 1, F2)
    b_dn3 = b_dn.reshape(E, 1, D)

    def fa(i, s, tv):
        return jnp.where(tv[i] > 0, jnp.minimum(s, nf - 1), nf - 1)

    def nb(i, s, tv):
        return jnp.where(tv[i] > 0, jnp.clip(s - nf, 0, nn - 1), nn - 1)

    grid_spec = pltpu.PrefetchScalarGridSpec(
        num_scalar_prefetch=2,
        grid=(n_tiles, nf + nn),
        in_specs=[
            pl.BlockSpec((tm, D), lambda i, s, te, tv: (i, 0)),
            pl.BlockSpec((None, D, tf), lambda i, s, te, tv: (te[i], 0, fa(i, s, tv))),
            pl.BlockSpec((None, D, tf), lambda i, s, te, tv: (te[i], 0, nf + fa(i, s, tv))),
            pl.BlockSpec((None, 1, tf), lambda i, s, te, tv: (te[i], 0, fa(i, s, tv))),
            pl.BlockSpec((None, 1, tf), lambda i, s, te, tv: (te[i], 0, nf + fa(i, s, tv))),
            pl.BlockSpec((None, F, tn), lambda i, s, te, tv: (te[i], 0, nb(i, s, tv))),
            pl.BlockSpec((None, 1, tn), lambda i, s, te, tv: (te[i], 0, nb(i, s, tv))),
        ],
        out_specs=pl.BlockSpec((tm, tn), lambda i, s, te, tv: (i, jnp.clip(s - nf, 0, nn - 1))),
        scratch_shapes=[pltpu.VMEM((tm, F), BF16)],
    )
    return pl.pallas_call(
        functools.partial(_expert_kernel, nf=nf),
        grid_spec=grid_spec,
        out_shape=jax.ShapeDtypeStruct((R, D), BF16),
        compiler_params=_cparams(("arbitrary", "arbitrary")),
        name="moe_experts",
    )(tile_e, tile_valid, xb, w_gu, w_gu, b_gu3, b_gu3, w_dn, b_dn3)


def _combine_kernel(x_ref, y_ref, gate_ref, g_ref, o_head_ref, o_tail_ref, *, n_head_tiles):
    acc = x_ref[...]
    gate = gate_ref[...]
    for t in range(TOP_K):
        acc = acc + gate[:, t:t + 1] * y_ref[t].astype(F32)
    y = _rms(acc, g_ref[...])
    i = pl.program_id(0)

    @pl.when(i < n_head_tiles)
    def _():
        o_head_ref[...] = y

    @pl.when(i >= n_head_tiles)
    def _():
        o_tail_ref[...] = y


def _combine(x, y4, gate, final_g, t_head):
    T, D = x.shape
    tm = _tile(math.gcd(t_head, T - t_head), (256, 128, 64, 32, 16, 8))
    nh = t_head // tm
    return pl.pallas_call(
        functools.partial(_combine_kernel, n_head_tiles=nh),
        grid=(T // tm,),
        in_specs=[
            pl.BlockSpec((tm, D), lambda i: (i, 0)),
            pl.BlockSpec((TOP_K, tm, D), lambda i: (0, i, 0)),
            pl.BlockSpec((tm, TOP_K), lambda i: (i, 0)),
            pl.BlockSpec((1, D), lambda i: (0, 0)),
        ],
        out_specs=[pl.BlockSpec((tm, D), lambda i: (jnp.minimum(i, nh - 1), 0)),
                   pl.BlockSpec((tm, D), lambda i: (jnp.maximum(i - nh, 0), 0))],
        out_shape=[jax.ShapeDtypeStruct((t_head, D), F32), jax.ShapeDtypeStruct((T - t_head, D), F32)],
        compiler_params=_cparams(("arbitrary",)),
        name="moe_combine_norm",
    )(x, y4, gate, final_g)


def _pad_rows(w, n):
    return jnp.pad(w, ((0, 0),) * (w.ndim - 2) + ((0, n - w.shape[-2]), (0, 0)))


def _pad_cols(w, n):
    return jnp.pad(w, ((0, 0),) * (w.ndim - 1) + ((0, n - w.shape[-1]),))


def _moe_tile_rows(n_assign, n_experts):
    for tm in (1024, 512, 256, 128, 64, 32, 16, 8):
        if n_assign >= 4 * n_experts * tm or tm == 8:
            return tm


def _layer(x, mem, n_seq, seq_len, t_head, norm_mix_g, w_in, shift_mu, decay_w0, decay_w2, iclr_a0, iclr_a2,
           gate_g2, k_k, k_a, r_k, lnx_g, lnx_b, conv_w, conv_b, conv_ln_g, conv_ln_b, w_branch, w_o,
           norm_x_g, norm_mem_g, w_xq, w_xkv, w_xo, norm_ffn_g, w_router, b_router, w_gu, b_gu,
           w_dn, b_dn, final_g):
    T, D = x.shape
    W = k_k.shape[0]
    DL = decay_w2.shape[1]
    AL = iclr_a2.shape[1]
    GL = gate_g2.shape[0]
    C = conv_w.shape[1]
    E = w_router.shape[1]
    n_mem = mem.shape[0] // n_seq
    row = lambda v: v.reshape(1, -1)

    o3 = 3 * W
    o4 = o3 + 2 * DL
    o5 = o4 + 2 * AL
    o6 = o5 + GL
    seg = lambda m, a, b, n: _pad_cols(m[..., a:b], n)
    lora_cols = lambda m: jnp.concatenate(
        [seg(m, o3, o3 + DL, LORA_PAD), seg(m, o3 + DL, o4, LORA_PAD), seg(m, o4, o4 + AL, LORA_PAD),
         seg(m, o4 + AL, o5, LORA_PAD), m[..., o5:o6]], axis=-1)
    w_rkv = w_in[:, :o3].astype(BF16)
    w_lora = lora_cols(w_in).astype(BF16)
    mu_rkv = row(shift_mu[:o3])
    mu_lora = row(lora_cols(shift_mu))
    w_ca = w_in[:, o6:o6 + C].astype(BF16)
    w_cb = w_in[:, o6 + C:o6 + 2 * C].astype(BF16)
    w_g1 = w_in[:, o6 + 2 * C:o6 + 2 * C + D].astype(BF16)
    w_g2 = w_in[:, o6 + 2 * C + D:].astype(BF16)
    g_mix = row(norm_mix_g)

    z_rkv = _inproj_shift(x, g_mix, w_rkv, mu_rkv, seq_len, _tile(o3, (1024, 512, 256, 128)))
    z_lora = _inproj_shift(x, g_mix, w_lora, mu_lora, seq_len, w_lora.shape[1])
    u = _inproj_glu(x, g_mix, w_ca, w_cb)

    w2h, w2l = _split2(_pad_rows(decay_w2, LORA_PAD))
    a2_bf16 = _pad_rows(iclr_a2, LORA_PAD).astype(BF16)
    wkv_args = (z_rkv, z_lora, decay_w0.reshape(2, 1, W), w2h, w2l, iclr_a0.reshape(2, 1, W), a2_bf16,
                row(k_k), row(k_a), row(r_k), n_seq, seq_len)
    o_f, bonus_f = _wkv(*wkv_args, bwd=False)
    o_b, bonus_b = _wkv(*wkv_args, bwd=True)
    o_rwkv = _rwkv_post(o_f, o_b, bonus_f, bonus_b, z_lora, gate_g2.astype(BF16), row(lnx_g), row(lnx_b))
    o_conv = _conv(u, conv_w, row(conv_b), row(conv_ln_g), row(conv_ln_b), seq_len)

    merged = _merge(x, g_mix, o_rwkv, o_conv, w_g1, w_g2, w_branch[:W].astype(BF16),
                    w_branch[W:].astype(BF16))
    x1 = _mm_res(merged, w_o.astype(BF16), x)

    q = _mm_norm(x1, row(norm_x_g), w_xq.astype(BF16))
    kv = _mm_norm(mem, row(norm_mem_g), w_xkv.astype(BF16))
    att = _xattn(q, kv, n_seq, seq_len, n_mem)
    x2 = _mm_res(att, w_xo.astype(BF16), x1)

    hf, top_e, gate = _router(x2, row(norm_ffn_g), w_router, row(b_router))

    A = T * TOP_K
    tm_e = _moe_tile_rows(A, E)
    n_tiles = (A + E * (tm_e - 1) + tm_e - 1) // tm_e
    n_rows = n_tiles * tm_e
    flat_e = top_e.reshape(A)
    iota_a = jnp.arange(A, dtype=jnp.int32)
    sorted_e, order = lax.sort((flat_e, iota_a), num_keys=1)
    _, rank = lax.sort((order, iota_a), num_keys=1)
    experts = jnp.arange(E, dtype=jnp.int32)
    start = jnp.searchsorted(sorted_e, experts, side='left').astype(jnp.int32)
    counts = jnp.searchsorted(sorted_e, experts, side='right').astype(jnp.int32) - start
    padded = (counts + tm_e - 1) // tm_e * tm_e
    pad_end = jnp.cumsum(padded)
    pad_start = pad_end - padded
    dest_of = pad_start[flat_e] + rank - start[flat_e]
    tile_start = jnp.arange(n_tiles, dtype=jnp.int32) * tm_e
    tile_e = jnp.minimum(jnp.searchsorted(pad_end, tile_start, side='right'), E - 1).astype(jnp.int32)
    tile_valid = jnp.clip(pad_start[tile_e] + counts[tile_e] - tile_start, 0, tm_e).astype(jnp.int32)
    rows = jnp.arange(n_rows, dtype=jnp.int32)
    row_e = jnp.repeat(tile_e, tm_e)
    row_off = rows - pad_start[row_e]
    src = order[jnp.minimum(start[row_e] + row_off, A - 1)] // TOP_K
    row_tok = jnp.where(row_off < counts[row_e], src, rows % T)

    xb = hf.at[row_tok].get(mode='promise_in_bounds')
    yb = _experts(xb, tile_e, tile_valid, w_gu, b_gu, w_dn, b_dn, tm_e)
    dest_slot_major = dest_of.reshape(T, TOP_K).T.reshape(A)
    y4 = yb.at[dest_slot_major].get(mode='promise_in_bounds').reshape(TOP_K, T, D)
    return _combine(x2, y4, gate, row(final_g), t_head)


def kernel(x_prompt, x_sample, mem_prompt, mem_sample, norm_mix_g, w_in, shift_mu, decay_w0, decay_w2,
           iclr_a0, iclr_a2, gate_g2, k_k, k_a, r_k, lnx_g, lnx_b, conv_w, conv_b, conv_ln_g, conv_ln_b,
           w_branch, w_o, norm_x_g, norm_mem_g, w_xq, w_xkv, w_xo, norm_ffn_g, w_router, b_router,
           w_gu, b_gu, w_dn, b_dn, final_g):
    layer_params = (norm_mix_g, w_in, shift_mu, decay_w0, decay_w2, iclr_a0, iclr_a2, gate_g2, k_k, k_a,
                    r_k, lnx_g, lnx_b, conv_w, conv_b, conv_ln_g, conv_ln_b, w_branch, w_o, norm_x_g,
                    norm_mem_g, w_xq, w_xkv, w_xo, norm_ffn_g, w_router, b_router, w_gu, b_gu, w_dn, b_dn)
    assert all(p.shape[0] == 1 for p in layer_params), "single-layer stack expected"
    bp, seq_len, D = x_prompt.shape
    bs = x_sample.shape[0]
    assert x_sample.shape[1] == seq_len
    n_seq = bp + bs
    x = jnp.concatenate([x_prompt, x_sample], axis=0).reshape(n_seq * seq_len, D)
    mem = jnp.concatenate([mem_prompt, mem_sample], axis=0).reshape(-1, D)
    y_p, y_s = _layer(x, mem, n_seq, seq_len, bp * seq_len, *[p[0] for p in layer_params], final_g)
    return y_p.reshape(bp, seq_len, D), y_s.reshape(bs, seq_len, D)
```

```python
import functools
import math

import jax
import jax.numpy as jnp
from jax import lax
from jax.experimental import pallas as pl
from jax.experimental.pallas import tpu as pltpu

F32 = jnp.float32
BF16 = jnp.bfloat16

RWKV_HEAD = 64
DECAY_SCALE = math.exp(-0.5)
LNX_EPS = RWKV_HEAD * 1e-5
RMS_EPS = 1e-5
LN_EPS = 1e-5
XATTN_HEADS = 4
TOP_K = 4
SWIGLU_ALPHA = 1.702
SWIGLU_LIMIT = 7.0

LANES = 128
SUBLANES = 8
WKV_CHUNK = 64
LORA_PAD = 128
VMEM_LIMIT = 56 * 1024 * 1024


def _cparams(sem):
    return pltpu.CompilerParams(dimension_semantics=sem, vmem_limit_bytes=VMEM_LIMIT)


def _tile(n, prefs):
    for p in prefs:
        if n % p == 0:
            return p
    return n


def _dot(a, b):
    return jnp.dot(a, b, preferred_element_type=F32)


def _dot_nt(a, b):
    return lax.dot_general(a, b, (((1,), (1,)), ((), ())), preferred_element_type=F32)


def _dot_tn(a, b):
    return lax.dot_general(a, b, (((0,), (0,)), ((), ())), preferred_element_type=F32)


def _split2(x):
    hi = x.astype(BF16)
    lo = (x - hi.astype(F32)).astype(BF16)
    return hi, lo


def _split3(x):
    hi = x.astype(BF16)
    r1 = x - hi.astype(F32)
    mid = r1.astype(BF16)
    lo = (r1 - mid.astype(F32)).astype(BF16)
    return hi, mid, lo


def _dot_exact_rhs(x, w_bf16):
    h, m, l = _split3(x)
    return _dot(h, w_bf16) + _dot(m, w_bf16) + _dot(l, w_bf16)


def _dot3(x, w_hi, w_lo):
    xh, xl = _split2(x)
    return _dot(xh, w_hi) + _dot(xl, w_hi) + _dot(xh, w_lo)


def _rms(x, g):
    return x * lax.rsqrt(jnp.mean(x * x, axis=-1, keepdims=True) + RMS_EPS) * g


def _sigmoid(x):
    return 0.5 * jnp.tanh(0.5 * x) + 0.5


def _inproj_shift_kernel(x_ref, xp_ref, xn_ref, g_ref, w_ref, mu_ref, o_ref, h_scr, hp_scr, hn_scr,
                         *, seq_len):
    i = pl.program_id(0)
    j = pl.program_id(1)
    tm = x_ref.shape[0]

    @pl.when(j == 0)
    def _():
        g = g_ref[...]
        h_scr[...] = _rms(x_ref[...], g).astype(BF16)
        hp_scr[...] = _rms(xp_ref[...], g).astype(BF16)
        hn_scr[...] = _rms(xn_ref[...], g).astype(BF16)

    w = w_ref[...]
    p = _dot(h_scr[...], w)
    pp = _dot(hp_scr[...], w)[SUBLANES - 1:SUBLANES, :]
    pn = _dot(hn_scr[...], w)[0:1, :]
    first = (i * tm) % seq_len == 0
    last = ((i + 1) * tm) % seq_len == 0
    pp = jnp.where(first, 0.0, pp)
    pn = jnp.where(last, 0.0, pn)
    rid = lax.broadcasted_iota(jnp.int32, p.shape, 0)
    up = jnp.where(rid == 0, pp, pltpu.roll(p, 1, 0))
    dn = jnp.where(rid == tm - 1, pn, pltpu.roll(p, tm - 1, 0))
    o_ref[...] = (p + mu_ref[...] * (0.5 * (up + dn) - p)).astype(o_ref.dtype)


def _inproj_shift(x, g, w_bf16, mu, seq_len, tn, out_dtype):
    T, D = x.shape
    N = w_bf16.shape[1]
    tm = _tile(seq_len, (1024, 512, 256, 128, 64, 32, 16, 8))
    nb8 = T // SUBLANES
    r8 = tm // SUBLANES
    return pl.pallas_call(
        functools.partial(_inproj_shift_kernel, seq_len=seq_len),
        grid=(T // tm, N // tn),
        in_specs=[
            pl.BlockSpec((tm, D), lambda i, j: (i, 0)),
            pl.BlockSpec((SUBLANES, D), lambda i, j: (jnp.maximum(i * r8 - 1, 0), 0)),
            pl.BlockSpec((SUBLANES, D), lambda i, j: (jnp.minimum((i + 1) * r8, nb8 - 1), 0)),
            pl.BlockSpec((1, D), lambda i, j: (0, 0)),
            pl.BlockSpec((D, tn), lambda i, j: (0, j)),
            pl.BlockSpec((1, tn), lambda i, j: (0, j)),
        ],
        out_specs=pl.BlockSpec((tm, tn), lambda i, j: (i, j)),
        out_shape=jax.ShapeDtypeStruct((T, N), out_dtype),
        scratch_shapes=[pltpu.VMEM((tm, D), BF16), pltpu.VMEM((SUBLANES, D), BF16),
                        pltpu.VMEM((SUBLANES, D), BF16)],
        compiler_params=_cparams(("parallel", "arbitrary")),
        name="inproj_shift",
    )(x, x, x, g, w_bf16, mu)


def _inproj_glu_kernel(x_ref, g_ref, wa_ref, wb_ref, o_ref, h_scr):
    @pl.when(pl.program_id(1) == 0)
    def _():
        h_scr[...] = _rms(x_ref[...], g_ref[...]).astype(BF16)

    h = h_scr[...]
    a = _dot(h, wa_ref[...])
    b = _dot(h, wb_ref[...])
    o_ref[...] = (a * _sigmoid(b)).astype(o_ref.dtype)


def _inproj_glu(x, g, wa, wb):
    T, D = x.shape
    N = wa.shape[1]
    tm = _tile(T, (512, 256, 128, 64, 32, 16, 8))
    tn = _tile(N, (512, 256, 128))
    return pl.pallas_call(
        _inproj_glu_kernel,
        grid=(T // tm, N // tn),
        in_specs=[
            pl.BlockSpec((tm, D), lambda i, j: (i, 0)),
            pl.BlockSpec((1, D), lambda i, j: (0, 0)),
            pl.BlockSpec((D, tn), lambda i, j: (0, j)),
            pl.BlockSpec((D, tn), lambda i, j: (0, j)),
        ],
        out_specs=pl.BlockSpec((tm, tn), lambda i, j: (i, j)),
        out_shape=jax.ShapeDtypeStruct((T, N), BF16),
        scratch_shapes=[pltpu.VMEM((tm, D), BF16)],
        compiler_params=_cparams(("parallel", "arbitrary")),
        name="inproj_glu",
    )(x, g, wa, wb)


def _wkv_kernel(zr_ref, zk_ref, zv_ref, dd_ref, ad_ref, w0_ref, w2h_ref, w2l_ref, a0_ref, a2_ref,
                kk_ref, ka_ref, rk_ref, o_ref, bonus_ref, state_ref, *, n_pairs, bwd):
    L = WKV_CHUNK
    PW = 2 * RWKV_HEAD
    TT = zr_ref.shape[0]
    n_chunks = TT // L

    @pl.when(pl.program_id(2) == 0)
    def _():
        state_ref[...] = jnp.zeros_like(state_ref)

    row = lax.broadcasted_iota(jnp.int32, (L, 2 * L), 0)
    col = lax.broadcasted_iota(jnp.int32, (L, 2 * L), 1) % L
    strict = (col > row) if bwd else (col < row)
    incl = (col >= row) if bwd else (col <= row)
    ipk = jnp.where(col == row, 1.0, 0.0)
    lane = lax.broadcasted_iota(jnp.int32, (1, PW), 1)
    m0 = lane < RWKV_HEAD
    lane2 = lax.broadcasted_iota(jnp.int32, (1, 2 * PW), 1) % PW
    m0w = lane2 < RWKV_HEAD
    srow = lax.broadcasted_iota(jnp.int32, (PW, PW), 0)
    scol = lax.broadcasted_iota(jnp.int32, (PW, PW), 1)
    same_head = (srow // RWKV_HEAD) == (scol // RWKV_HEAD)
    eye = srow == scol
    HSW = min(zr_ref.shape[1], 2 * PW)
    hrow = lax.broadcasted_iota(jnp.int32, (HSW, HSW), 0)
    hcol = lax.broadcasted_iota(jnp.int32, (HSW, HSW), 1)
    head_ones_w = jnp.where((hrow // RWKV_HEAD) == (hcol // RWKV_HEAD), 1.0, 0.0).astype(BF16)

    def bd(x):
        return jnp.concatenate([jnp.where(m0, x, 0.0), jnp.where(m0, 0.0, x)], axis=0).astype(BF16)

    def bd2(x):
        return jnp.concatenate([jnp.where(m0w, x, 0.0), jnp.where(m0w, 0.0, x)], axis=0).astype(BF16)

    w0 = w0_ref[...]
    a0 = a0_ref[...]
    k_k = kk_ref[...]
    k_a = ka_ref[...]
    r_k = rk_ref[...]

    def head_sum(x, two_pass):
        n = x.shape[0]
        outs = []
        for c in range(x.shape[1] // HSW):
            xc = x[:, c * HSW:(c + 1) * HSW]
            if two_pass:
                hi, lo = _split2(xc)
                res = _dot(jnp.concatenate([hi, lo], axis=0), head_ones_w)
                outs.append(res[:n] + res[n:])
            else:
                outs.append(_dot(xc.astype(BF16), head_ones_w))
        return jnp.concatenate(outs, axis=1) if len(outs) > 1 else outs[0]

    r = zr_ref[...].astype(F32)
    k = zk_ref[...].astype(F32)
    v = zv_ref[...].astype(F32)
    dlin = w0 + _dot3(jnp.tanh(dd_ref[...]), w2h_ref[...], w2l_ref[...])
    lw = -DECAY_SCALE * _sigmoid(dlin)
    iclr = _sigmoid(a0 + _dot(ad_ref[...].astype(BF16), a2_ref[...]))
    kkr = k * k_k
    kmod = k * (1.0 + (iclr - 1.0) * k_a)
    cum = _chunk_cumsum(lw, bwd)
    e_in = jnp.exp(cum)
    e_ex = jnp.exp(cum - lw)
    e_ng = jnp.exp(-cum)
    kk = kkr * lax.rsqrt(jnp.maximum(head_sum(kkr * kkr, False), 1e-24))
    bonus_ref[...] = (head_sum(r * kmod * r_k, True) * v).astype(bonus_ref.dtype)
    b_all = kk * iclr
    ah_all = -kk * e_ex
    rh_all = r * e_in
    bh_all = b_all * e_ng
    kh_all = kmod * e_ng

    chunk_order = list(range(n_chunks - 1, -1, -1) if bwd else range(n_chunks))
    pair_vals = []
    for p in range(n_pairs):
        sl = slice(p * PW, (p + 1) * PW)
        pair_vals.append(dict(b=b_all[:, sl], ah=ah_all[:, sl], rh=rh_all[:, sl],
                              bh=bh_all[:, sl], kh=kh_all[:, sl]))

    probs = []
    for ci in chunk_order:
        for p in range(n_pairs):
            sl = slice(p * PW, (p + 1) * PW)
            rs = slice(ci * L, (ci + 1) * L)
            pv = pair_vals[p]
            q = dict(p=p, ci=ci, sl=sl, rs=rs, ah=pv["ah"][rs], rh=pv["rh"][rs], v=v[rs, sl])
            lhs = jnp.concatenate([q["ah"], q["rh"]], axis=0).astype(BF16)
            g = _dot_nt(lhs, jnp.concatenate([bd(pv["bh"][rs]), bd(pv["kh"][rs])], axis=0))
            a32 = jnp.where(strict, g[:L, :2 * L], 0.0)
            q["a"] = a32.astype(BF16)
            q["t"] = (ipk + a32).astype(BF16)
            q["rb"] = jnp.where(incl, g[L:, :2 * L], 0.0).astype(BF16)
            q["akrk"] = jnp.concatenate([jnp.where(strict, g[:L, 2 * L:], 0.0),
                                         jnp.where(incl, g[L:, 2 * L:], 0.0)], axis=0).astype(BF16)
            probs.append(q)
    for q in probs:
        q["kv"] = _dot(q["akrk"], bd(q["v"]))
    for q in probs:
        q["ai"] = _dot(q["a"], bd(q["a"])).astype(BF16)
    for lvl in range(1, 6):
        more = lvl < 5
        for q in probs:
            lhs = jnp.concatenate([q["t"], q["ai"]], axis=0) if more else q["t"]
            res = _dot(lhs, bd(q["ai"]))
            q["t"] = (q["t"].astype(F32) + res[:L]).astype(BF16)
            if more:
                q["ai"] = res[L:].astype(BF16)
    for q in probs:
        q["rbt"] = _dot(q["rb"], bd(q["t"])).astype(BF16)
    for q in probs:
        y0 = jnp.concatenate([q["ah"], q["kv"][:L]], axis=1)
        q["ry"] = _dot(jnp.concatenate([q["t"], q["rbt"]], axis=0), bd2(y0))
    for q in probs:
        p, rs, sl, ry = q["p"], q["rs"], q["sl"], q["ry"]
        end = q["ci"] * L if bwd else q["ci"] * L + L - 1
        tot = cum[end:end + 1, sl]
        e_rm = jnp.exp(tot - cum[rs, sl])
        q["rt"] = q["rh"] + ry[L:, :PW]
        q["ob"] = ry[L:, PW:] + q["kv"][L:]
        lhs_t = jnp.concatenate([pair_vals[p]["b"][rs] * e_rm, kmod[rs, sl] * e_rm], axis=0).astype(BF16)
        rhs_t = jnp.concatenate(
            [ry[:L], jnp.concatenate([jnp.zeros((L, PW), F32), q["v"]], axis=1)], axis=0).astype(BF16)
        mn = _dot_tn(lhs_t, rhs_t)
        q["mm"] = jnp.where(eye, jnp.exp(tot), 0.0) + jnp.where(same_head, mn[:, :PW], 0.0)
        q["nn"] = jnp.where(same_head, mn[:, PW:], 0.0)
    states = [state_ref[p] for p in range(n_pairs)]
    for q in probs:
        p = q["p"]
        res = _dot(jnp.concatenate([q["rt"], q["mm"]], axis=0).astype(BF16), states[p].astype(BF16))
        o_ref[q["rs"], q["sl"]] = (res[:L] + q["ob"]).astype(o_ref.dtype)
        states[p] = res[L:] + q["nn"]
    for p in range(n_pairs):
        state_ref[p] = states[p]


def _chunk_cumsum(x, bwd):
    L = WKV_CHUNK
    n = x.shape[0]
    rin = lax.broadcasted_iota(jnp.int32, (n, 1), 0) % L
    s = 1
    while s < L:
        if bwd:
            x = x + jnp.where(rin < L - s, pltpu.roll(x, n - s, 0), 0.0)
        else:
            x = x + jnp.where(rin >= s, pltpu.roll(x, s, 0), 0.0)
        s *= 2
    return x


def _wkv(z_rkv, z_lora, decay_w0, w2h, w2l, iclr_a0, a2_bf16, k_k, k_a, r_k, n_seq, seq_len, bwd):
    T = z_rkv.shape[0]
    W = z_rkv.shape[1] // 3
    PW = 2 * RWKV_HEAD
    n_pairs = _tile(W // PW, (4, 2, 1))
    GW = n_pairs * PW
    n_groups = W // GW
    TT = _tile(seq_len, (256, 128, 64))
    nT = seq_len // TT
    d = int(bwd)

    def tb(b, c):
        return b * nT + (nT - 1 - c if bwd else c)

    zspec = lambda off: pl.BlockSpec((TT, GW), lambda b, g, c: (tb(b, c), off * n_groups + g))
    pspec = pl.BlockSpec((None, 1, GW), lambda b, g, c: (d, 0, g))
    lspec = pl.BlockSpec((None, LORA_PAD, GW), lambda b, g, c: (d, 0, g))
    cspec = pl.BlockSpec((1, GW), lambda b, g, c: (0, g))
    ospec = pl.BlockSpec((TT, GW), lambda b, g, c: (tb(b, c), g))
    return pl.pallas_call(
        functools.partial(_wkv_kernel, n_pairs=n_pairs, bwd=bwd),
        grid=(n_seq, n_groups, nT),
        in_specs=[
            zspec(0), zspec(1), zspec(2),
            pl.BlockSpec((TT, LORA_PAD), lambda b, g, c: (tb(b, c), d)),
            pl.BlockSpec((TT, LORA_PAD), lambda b, g, c: (tb(b, c), 2 + d)),
            pspec, lspec, lspec, pspec, lspec, cspec, cspec, cspec,
        ],
        out_specs=[ospec, ospec],
        out_shape=[jax.ShapeDtypeStruct((T, W), BF16), jax.ShapeDtypeStruct((T, W), BF16)],
        scratch_shapes=[pltpu.VMEM((n_pairs, PW, PW), F32)],
        compiler_params=_cparams(("parallel", "parallel", "arbitrary")),
        name="wkv_scan_bwd" if bwd else "wkv_scan_fwd",
    )(z_rkv, z_rkv, z_rkv, z_lora, z_lora, decay_w0, w2h, w2l, iclr_a0, a2_bf16, k_k, k_a, r_k)


def _rwkv_post_kernel(of_ref, ob_ref, bf_ref, bb_ref, gd_ref, g2_ref, lg_ref, lb_ref, out_ref):
    W = out_ref.shape[1]
    PW = 2 * RWKV_HEAD
    srow = lax.broadcasted_iota(jnp.int32, (PW, PW), 0)
    scol = lax.broadcasted_iota(jnp.int32, (PW, PW), 1)
    head_mean = jnp.where((srow // RWKV_HEAD) == (scol // RWKV_HEAD), 1.0, 0.0).astype(BF16)
    inv = 1.0 / RWKV_HEAD
    gate = _dot(_sigmoid(gd_ref[...]).astype(BF16), g2_ref[...])
    for p in range(W // PW):
        sl = slice(p * PW, (p + 1) * PW)
        o = of_ref[:, sl].astype(F32) + ob_ref[:, sl].astype(F32)
        mean = _dot_exact_rhs(o, head_mean) * inv
        oc = o - mean
        var = _dot_exact_rhs(oc * oc, head_mean) * inv
        y = oc * lax.rsqrt(var + LNX_EPS) * lg_ref[:, sl] + lb_ref[:, sl]
        y = y + bf_ref[:, sl].astype(F32) + bb_ref[:, sl].astype(F32)
        out_ref[:, sl] = (y * gate[:, sl]).astype(out_ref.dtype)


def _rwkv_post(o_f, o_b, bonus_f, bonus_b, z_lora, g2, lnx_g, lnx_b):
    T, W = o_f.shape
    tm = _tile(T, (256, 128, 64, 32, 16, 8))
    GL = g2.shape[0]
    gd_blk = (4 * LORA_PAD) // GL
    tspec = pl.BlockSpec((tm, W), lambda i: (i, 0))
    return pl.pallas_call(
        _rwkv_post_kernel,
        grid=(T // tm,),
        in_specs=[
            tspec, tspec, tspec, tspec,
            pl.BlockSpec((tm, GL), lambda i: (i, gd_blk)),
            pl.BlockSpec((GL, W), lambda i: (0, 0)),
            pl.BlockSpec((1, W), lambda i: (0, 0)),
            pl.BlockSpec((1, W), lambda i: (0, 0)),
        ],
        out_specs=pl.BlockSpec((tm, W), lambda i: (i, 0)),
        out_shape=jax.ShapeDtypeStruct((T, W), BF16),
        compiler_params=_cparams(("parallel",)),
        name="rwkv_post",
    )(o_f, o_b, bonus_f, bonus_b, z_lora, g2, lnx_g, lnx_b)


def _conv_kernel(u_ref, up_ref, un_ref, w_ref, b_ref, lg_ref, lb_ref, o_ref, ext_scr, *, seq_len, halo):
    i = pl.program_id(0)
    tm = u_ref.shape[0]
    K = w_ref.shape[0]
    first = (i * tm) % seq_len == 0
    last = ((i + 1) * tm) % seq_len == 0
    n_ext = tm + 2 * halo
    ext = jnp.concatenate([jnp.where(first, 0.0, up_ref[...].astype(F32)), u_ref[...].astype(F32),
                           jnp.where(last, 0.0, un_ref[...].astype(F32))], axis=0)
    ext_scr[0] = ext
    for r in range(1, SUBLANES):
        ext_scr[r] = pltpu.roll(ext, n_ext - r, 0)
    sub = min(tm, 32)
    base = halo - K // 2
    for s in range(tm // sub):
        acc = jnp.zeros((sub, u_ref.shape[1]), F32)
        for j in range(K):
            off = base + j
            row0 = s * sub + (off // SUBLANES) * SUBLANES
            acc = acc + w_ref[j:j + 1, :] * ext_scr[off % SUBLANES, row0:row0 + sub, :]
        acc = acc + b_ref[...]
        mean = jnp.mean(acc, axis=-1, keepdims=True)
        xc = acc - mean
        var = jnp.mean(xc * xc, axis=-1, keepdims=True)
        y = xc * lax.rsqrt(var + LN_EPS) * lg_ref[...] + lb_ref[...]
        o_ref[s * sub:(s + 1) * sub, :] = (y * _sigmoid(y)).astype(o_ref.dtype)


def _conv(u, conv_w, conv_b, ln_g, ln_b, seq_len):
    T, C = u.shape
    K = conv_w.shape[0]
    halo = 16
    assert K // 2 <= halo
    tm = _tile(seq_len, (128, 64, 32, 16))
    rh = tm // halo
    nbh = T // halo
    return pl.pallas_call(
        functools.partial(_conv_kernel, seq_len=seq_len, halo=halo),
        grid=(T // tm,),
        in_specs=[
            pl.BlockSpec((tm, C), lambda i: (i, 0)),
            pl.BlockSpec((halo, C), lambda i: (jnp.maximum(i * rh - 1, 0), 0)),
            pl.BlockSpec((halo, C), lambda i: (jnp.minimum((i + 1) * rh, nbh - 1), 0)),
            pl.BlockSpec((K, C), lambda i: (0, 0)),
            pl.BlockSpec((1, C), lambda i: (0, 0)),
            pl.BlockSpec((1, C), lambda i: (0, 0)),
            pl.BlockSpec((1, C), lambda i: (0, 0)),
        ],
        out_specs=pl.BlockSpec((tm, C), lambda i: (i, 0)),
        out_shape=jax.ShapeDtypeStruct((T, C), BF16),
        scratch_shapes=[pltpu.VMEM((SUBLANES, tm + 2 * halo, C), F32)],
        compiler_params=_cparams(("parallel",)),
        name="conformer_conv",
    )(u, u, u, conv_w, conv_b, ln_g, ln_b)


def _merge_kernel(x_ref, g_ref, orw_ref, ocv_ref, wg1_ref, wg2_ref, wb1_ref, wb2_ref, o_ref, h_scr):
    @pl.when(pl.program_id(1) == 0)
    def _():
        h_scr[...] = _rms(x_ref[...], g_ref[...]).astype(BF16)

    h = h_scr[...]
    g1 = _sigmoid(_dot(h, wg1_ref[...]))
    g2 = _sigmoid(_dot(h, wg2_ref[...]))
    y1 = _dot(orw_ref[...], wb1_ref[...])
    y2 = _dot(ocv_ref[...], wb2_ref[...])
    o_ref[...] = (g1 * y1 + g2 * y2).astype(o_ref.dtype)


def _merge(x, g, o_rwkv, o_conv, wg1, wg2, wb1, wb2):
    T, D = x.shape
    W = o_rwkv.shape[1]
    C = o_conv.shape[1]
    tm = _tile(T, (512, 256, 128, 64, 32, 16, 8))
    tn = _tile(D, (512, 256, 128))
    return pl.pallas_call(
        _merge_kernel,
        grid=(T // tm, D // tn),
        in_specs=[
            pl.BlockSpec((tm, D), lambda i, j: (i, 0)),
            pl.BlockSpec((1, D), lambda i, j: (0, 0)),
            pl.BlockSpec((tm, W), lambda i, j: (i, 0)),
            pl.BlockSpec((tm, C), lambda i, j: (i, 0)),
            pl.BlockSpec((D, tn), lambda i, j: (0, j)),
            pl.BlockSpec((D, tn), lambda i, j: (0, j)),
            pl.BlockSpec((W, tn), lambda i, j: (0, j)),
            pl.BlockSpec((C, tn), lambda i, j: (0, j)),
        ],
        out_specs=pl.BlockSpec((tm, tn), lambda i, j: (i, j)),
        out_shape=jax.ShapeDtypeStruct((T, D), BF16),
        scratch_shapes=[pltpu.VMEM((tm, D), BF16)],
        compiler_params=_cparams(("parallel", "arbitrary")),
        name="merge_gates",
    )(x, g, o_rwkv, o_conv, wg1, wg2, wb1, wb2)


def _mm_res_kernel(a_ref, w_ref, res_ref, o_ref):
    o_ref[...] = res_ref[...] + _dot(a_ref[...], w_ref[...])


def _mm_res(a, w, res):
    T, K = a.shape
    N = w.shape[1]
    tm = _tile(T, (1024, 512, 256, 128, 64, 32, 16, 8))
    tn = _tile(N, (1024, 512, 256, 128))
    return pl.pallas_call(
        _mm_res_kernel,
        grid=(T // tm, N // tn),
        in_specs=[
            pl.BlockSpec((tm, K), lambda i, j: (i, 0)),
            pl.BlockSpec((K, tn), lambda i, j: (0, j)),
            pl.BlockSpec((tm, tn), lambda i, j: (i, j)),
        ],
        out_specs=pl.BlockSpec((tm, tn), lambda i, j: (i, j)),
        out_shape=jax.ShapeDtypeStruct((T, N), F32),
        compiler_params=_cparams(("parallel", "parallel")),
        name="proj_residual",
    )(a, w, res)


def _mm_norm_kernel(x_ref, g_ref, w_ref, o_ref, h_scr):
    @pl.when(pl.program_id(1) == 0)
    def _():
        h_scr[...] = _rms(x_ref[...], g_ref[...]).astype(BF16)

    o_ref[...] = _dot(h_scr[...], w_ref[...]).astype(o_ref.dtype)


def _mm_norm(x, g, w):
    T, D = x.shape
    N = w.shape[1]
    tm = _tile(T, (512, 256, 128, 64, 32, 16, 8))
    tn = _tile(N, (1024, 512, 256, 128))
    return pl.pallas_call(
        _mm_norm_kernel,
        grid=(T // tm, N // tn),
        in_specs=[
            pl.BlockSpec((tm, D), lambda i, j: (i, 0)),
            pl.BlockSpec((1, D), lambda i, j: (0, 0)),
            pl.BlockSpec((D, tn), lambda i, j: (0, j)),
        ],
        out_specs=pl.BlockSpec((tm, tn), lambda i, j: (i, j)),
        out_shape=jax.ShapeDtypeStruct((T, N), BF16),
        scratch_shapes=[pltpu.VMEM((tm, D), BF16)],
        compiler_params=_cparams(("parallel", "arbitrary")),
        name="norm_proj",
    )(x, g, w)


def _xattn_kernel(q_ref, k_ref, v_ref, o_ref):
    D = q_ref.shape[1]
    hd = D // XATTN_HEADS
    scale = hd ** -0.5
    for h in range(XATTN_HEADS):
        sl = slice(h * hd, (h + 1) * hd)
        s = _dot_nt(q_ref[:, sl], k_ref[:, sl]) * scale
        s = s - jnp.max(s, axis=-1, keepdims=True)
        e = jnp.exp(s)
        p = e / jnp.sum(e, axis=-1, keepdims=True)
        o_ref[:, sl] = _dot(p.astype(BF16), v_ref[:, sl]).astype(o_ref.dtype)


def _xattn(q, kv, n_seq, seq_len, n_mem):
    T, D = q.shape
    tm = _tile(seq_len, (512, 256, 128, 64, 32, 16, 8))
    nT = seq_len // tm
    return pl.pallas_call(
        _xattn_kernel,
        grid=(n_seq, nT),
        in_specs=[
            pl.BlockSpec((tm, D), lambda b, i: (b * nT + i, 0)),
            pl.BlockSpec((n_mem, D), lambda b, i: (b, 0)),
            pl.BlockSpec((n_mem, D), lambda b, i: (b, 1)),
        ],
        out_specs=pl.BlockSpec((tm, D), lambda b, i: (b * nT + i, 0)),
        out_shape=jax.ShapeDtypeStruct((T, D), BF16),
        compiler_params=_cparams(("parallel", "parallel")),
        name="cross_attention",
    )(q, kv, kv)


def _router_kernel(x_ref, g_ref, wh_ref, wl_ref, b_ref, hf_ref, e_ref, gate_ref):
    hf = _rms(x_ref[...], g_ref[...])
    hf_ref[...] = hf.astype(hf_ref.dtype)
    logits = _dot3(hf, wh_ref[...], wl_ref[...]) + b_ref[...]
    E = logits.shape[1]
    eid = lax.broadcasted_iota(jnp.int32, logits.shape, 1).astype(F32)
    work = logits
    vals = []
    idxs = []
    for _ in range(TOP_K):
        m = jnp.max(work, axis=-1, keepdims=True)
        idx = jnp.min(jnp.where(work == m, eid, float(E)), axis=-1, keepdims=True)
        vals.append(m)
        idxs.append(idx.astype(jnp.int32))
        work = jnp.where(eid == idx, -jnp.inf, work)
    ex = [jnp.exp(vv - vals[0]) for vv in vals]
    den = ex[0]
    for t in ex[1:]:
        den = den + t
    kid = lax.broadcasted_iota(jnp.int32, (logits.shape[0], TOP_K), 1)
    e_out = jnp.zeros((logits.shape[0], TOP_K), jnp.int32)
    g_out = jnp.zeros((logits.shape[0], TOP_K), F32)
    for t in range(TOP_K):
        e_out = jnp.where(kid == t, idxs[t], e_out)
        g_out = jnp.where(kid == t, ex[t] / den, g_out)
    e_ref[...] = e_out
    gate_ref[...] = g_out


def _router(x, g, w_router, b_router):
    T, D = x.shape
    E = w_router.shape[1]
    wh, wl = _split2(w_router)
    tm = _tile(T, (256, 128, 64, 32, 16, 8))
    return pl.pallas_call(
        _router_kernel,
        grid=(T // tm,),
        in_specs=[
            pl.BlockSpec((tm, D), lambda i: (i, 0)),
            pl.BlockSpec((1, D), lambda i: (0, 0)),
            pl.BlockSpec((D, E), lambda i: (0, 0)),
            pl.BlockSpec((D, E), lambda i: (0, 0)),
            pl.BlockSpec((1, E), lambda i: (0, 0)),
        ],
        out_specs=[
            pl.BlockSpec((tm, D), lambda i: (i, 0)),
            pl.BlockSpec((tm, TOP_K), lambda i: (i, 0)),
            pl.BlockSpec((tm, TOP_K), lambda i: (i, 0)),
        ],
        out_shape=[jax.ShapeDtypeStruct((T, D), BF16),
                   jax.ShapeDtypeStruct((T, TOP_K), jnp.int32),
                   jax.ShapeDtypeStruct((T, TOP_K), F32)],
        compiler_params=_cparams(("parallel",)),
        name="router",
    )(x, g, wh, wl, b_router)


def _expert_kernel(te_ref, tv_ref, x_ref, wg_ref, wu_ref, bg_ref, bu_ref, wd_ref, bd_ref, o_ref, act_ref, *, nf):
    i = pl.program_id(0)
    s = pl.program_id(1)
    tf = wg_ref.shape[1]
    valid = tv_ref[i] > 0

    @pl.when(valid & (s < nf))
    def _():
        x = x_ref[...]
        g = _dot(x, wg_ref[...].astype(BF16)) + bg_ref[...]
        u = _dot(x, wu_ref[...].astype(BF16)) + bu_ref[...]
        g = jnp.minimum(g, SWIGLU_LIMIT)
        u = jnp.clip(u, -SWIGLU_LIMIT, SWIGLU_LIMIT)
        act = ((u + 1.0) * (g * _sigmoid(SWIGLU_ALPHA * g))).astype(BF16)
        for f in range(nf):
            @pl.when(s == f)
            def _(f=f):
                act_ref[:, f * tf:(f + 1) * tf] = act

    @pl.when(valid & (s >= nf))
    def _():
        y = _dot(act_ref[...], wd_ref[...].astype(BF16)) + bd_ref[...]
        o_ref[...] = y.astype(o_ref.dtype)

    @pl.when(jnp.logical_not(valid) & (s >= nf))
    def _():
        o_ref[...] = jnp.zeros_like(o_ref)


def _experts(xb, tile_e, tile_valid, w_gu, b_gu, w_dn, b_dn, tm):
    R, D = xb.shape
    E, _, F2 = w_gu.shape
    F = F2 // 2
    tf = _tile(F, (512, 256, 128))
    tn = _tile(D, (512, 256, 128))
    nf = F // tf
    nn = D // tn
    n_tiles = R // tm
    b_gu3 = b_gu.reshape(E, 1, F2)
    b_dn3 = b_dn.reshape(E, 1, D)

    def fa(i, s, tv):
        return jnp.where(tv[i] > 0, jnp.minimum(s, nf - 1), nf - 1)

    def nb(i, s, tv):
        return jnp.where(tv[i] > 0, jnp.clip(s - nf, 0, nn - 1), nn - 1)

    grid_spec = pltpu.PrefetchScalarGridSpec(
        num_scalar_prefetch=2,
        grid=(n_tiles, nf + nn),
        in_specs=[
            pl.BlockSpec((tm, D), lambda i, s, te, tv: (i, 0)),
            pl.BlockSpec((None, D, tf), lambda i, s, te, tv: (te[i], 0, fa(i, s, tv))),
            pl.BlockSpec((None, D, tf), lambda i, s, te, tv: (te[i], 0, nf + fa(i, s, tv))),
            pl.BlockSpec((None, 1, tf), lambda i, s, te, tv: (te[i], 0, fa(i, s, tv))),
            pl.BlockSpec((None, 1, tf), lambda i, s, te, tv: (te[i], 0, nf + fa(i, s, tv))),
            pl.BlockSpec((None, F, tn), lambda i, s, te, tv: (te[i], 0, nb(i, s, tv))),
            pl.BlockSpec((None, 1, tn), lambda i, s, te, tv: (te[i], 0, nb(i, s, tv))),
        ],
        out_specs=pl.BlockSpec((tm, tn), lambda i, s, te, tv: (i, jnp.clip(s - nf, 0, nn - 1))),
        scratch_shapes=[pltpu.VMEM((tm, F), BF16)],
    )
    return pl.pallas_call(
        functools.partial(_expert_kernel, nf=nf),
        grid_spec=grid_spec,
        out_shape=jax.ShapeDtypeStruct((R, D), BF16),
        compiler_params=_cparams(("arbitrary", "arbitrary")),
        name="moe_experts",
    )(tile_e, tile_valid, xb, w_gu, w_gu, b_gu3, b_gu3, w_dn, b_dn3)


def _combine_kernel(x_ref, y_ref, gate_ref, g_ref, o_head_ref, o_tail_ref, *, n_head_tiles):
    acc = x_ref[...]
    gate = gate_ref[...]
    for t in range(TOP_K):
        acc = acc + gate[:, t:t + 1] * y_ref[t].astype(F32)
    y = _rms(acc, g_ref[...])
    i = pl.program_id(0)

    @pl.when(i < n_head_tiles)
    def _():
        o_head_ref[...] = y

    @pl.when(i >= n_head_tiles)
    def _():
        o_tail_ref[...] = y


def _combine(x, y4, gate, final_g, t_head):
    T, D = x.shape
    tm = _tile(math.gcd(t_head, T - t_head), (256, 128, 64, 32, 16, 8))
    nh = t_head // tm
    return pl.pallas_call(
        functools.partial(_combine_kernel, n_head_tiles=nh),
        grid=(T // tm,),
        in_specs=[
            pl.BlockSpec((tm, D), lambda i: (i, 0)),
            pl.BlockSpec((TOP_K, tm, D), lambda i: (0, i, 0)),
            pl.BlockSpec((tm, TOP_K), lambda i: (i, 0)),
            pl.BlockSpec((1, D), lambda i: (0, 0)),
        ],
        out_specs=[pl.BlockSpec((tm, D), lambda i: (jnp.minimum(i, nh - 1), 0)),
                   pl.BlockSpec((tm, D), lambda i: (jnp.maximum(i - nh, 0), 0))],
        out_shape=[jax.ShapeDtypeStruct((t_head, D), F32), jax.ShapeDtypeStruct((T - t_head, D), F32)],
        compiler_params=_cparams(("arbitrary",)),
        name="moe_combine_norm",
    )(x, y4, gate, final_g)


def _pad_rows(w, n):
    return jnp.pad(w, ((0, 0),) * (w.ndim - 2) + ((0, n - w.shape[-2]), (0, 0)))


def _pad_cols(w, n):
    return jnp.pad(w, ((0, 0),) * (w.ndim - 1) + ((0, n - w.shape[-1]),))


def _moe_tile_rows(n_assign, n_experts):
    for tm in (1024, 512, 256, 128, 64, 32, 16, 8):
        if n_assign >= 4 * n_experts * tm or tm == 8:
            return tm


def _layer(x, mem, n_seq, seq_len, t_head, norm_mix_g, w_in, shift_mu, decay_w0, decay_w2, iclr_a0, iclr_a2,
           gate_g2, k_k, k_a, r_k, lnx_g, lnx_b, conv_w, conv_b, conv_ln_g, conv_ln_b, w_branch, w_o,
           norm_x_g, norm_mem_g, w_xq, w_xkv, w_xo, norm_ffn_g, w_router, b_router, w_gu, b_gu,
           w_dn, b_dn, final_g):
    T, D = x.shape
    W = k_k.shape[0]
    DL = decay_w2.shape[1]
    AL = iclr_a2.shape[1]
    GL = gate_g2.shape[0]
    C = conv_w.shape[1]
    E = w_router.shape[1]
    n_mem = mem.shape[0] // n_seq
    row = lambda v: v.reshape(1, -1)

    o3 = 3 * W
    o4 = o3 + 2 * DL
    o5 = o4 + 2 * AL
    o6 = o5 + GL
    seg = lambda m, a, b, n: _pad_cols(m[..., a:b], n)
    lora_cols = lambda m: jnp.concatenate(
        [seg(m, o3, o3 + DL, LORA_PAD), seg(m, o3 + DL, o4, LORA_PAD), seg(m, o4, o4 + AL, LORA_PAD),
         seg(m, o4 + AL, o5, LORA_PAD), m[..., o5:o6]], axis=-1)
    w_rkv = w_in[:, :o3].astype(BF16)
    w_lora = lora_cols(w_in).astype(BF16)
    mu_rkv = row(shift_mu[:o3])
    mu_lora = row(lora_cols(shift_mu))
    w_ca = w_in[:, o6:o6 + C].astype(BF16)
    w_cb = w_in[:, o6 + C:o6 + 2 * C].astype(BF16)
    w_g1 = w_in[:, o6 + 2 * C:o6 + 2 * C + D].astype(BF16)
    w_g2 = w_in[:, o6 + 2 * C + D:].astype(BF16)
    g_mix = row(norm_mix_g)

    z_rkv = _inproj_shift(x, g_mix, w_rkv, mu_rkv, seq_len, _tile(o3, (1024, 512, 256, 128)), BF16)
    z_lora = _inproj_shift(x, g_mix, w_lora, mu_lora, seq_len, w_lora.shape[1], F32)
    u = _inproj_glu(x, g_mix, w_ca, w_cb)

    w2h, w2l = _split2(_pad_rows(decay_w2, LORA_PAD))
    a2_bf16 = _pad_rows(iclr_a2, LORA_PAD).astype(BF16)
    wkv_args = (z_rkv, z_lora, decay_w0.reshape(2, 1, W), w2h, w2l, iclr_a0.reshape(2, 1, W), a2_bf16,
                row(k_k), row(k_a), row(r_k), n_seq, seq_len)
    o_f, bonus_f = _wkv(*wkv_args, bwd=False)
    o_b, bonus_b = _wkv(*wkv_args, bwd=True)
    o_rwkv = _rwkv_post(o_f, o_b, bonus_f, bonus_b, z_lora, gate_g2.astype(BF16), row(lnx_g), row(lnx_b))
    o_conv = _conv(u, conv_w, row(conv_b), row(conv_ln_g), row(conv_ln_b), seq_len)

    merged = _merge(x, g_mix, o_rwkv, o_conv, w_g1, w_g2, w_branch[:W].astype(BF16),
                    w_branch[W:].astype(BF16))
    x1 = _mm_res(merged, w_o.astype(BF16), x)

    q = _mm_norm(x1, row(norm_x_g), w_xq.astype(BF16))
    kv = _mm_norm(mem, row(norm_mem_g), w_xkv.astype(BF16))
    att = _xattn(q, kv, n_seq, seq_len, n_mem)
    x2 = _mm_res(att, w_xo.astype(BF16), x1)

    hf, top_e, gate = _router(x2, row(norm_ffn_g), w_router, row(b_router))

    A = T * TOP_K
    tm_e = _moe_tile_rows(A, E)
    n_tiles = (A + E * (tm_e - 1) + tm_e - 1) // tm_e
    n_rows = n_tiles * tm_e
    flat_e = top_e.reshape(A)
    iota_a = jnp.arange(A, dtype=jnp.int32)
    sorted_e, order = lax.sort((flat_e, iota_a), num_keys=1)
    _, rank = lax.sort((order, iota_a), num_keys=1)
    experts = jnp.arange(E, dtype=jnp.int32)
    start = jnp.searchsorted(sorted_e, experts, side='left').astype(jnp.int32)
    counts = jnp.searchsorted(sorted_e, experts, side='right').astype(jnp.int32) - start
    padded = (counts + tm_e - 1) // tm_e * tm_e
    pad_end = jnp.cumsum(padded)
    pad_start = pad_end - padded
    dest_of = pad_start[flat_e] + rank - start[flat_e]
    tile_start = jnp.arange(n_tiles, dtype=jnp.int32) * tm_e
    tile_e = jnp.minimum(jnp.searchsorted(pad_end, tile_start, side='right'), E - 1).astype(jnp.int32)
    tile_valid = jnp.clip(pad_start[tile_e] + counts[tile_e] - tile_start, 0, tm_e).astype(jnp.int32)
    rows = jnp.arange(n_rows, dtype=jnp.int32)
    row_e = jnp.repeat(tile_e, tm_e)
    row_off = rows - pad_start[row_e]
    src = order[jnp.minimum(start[row_e] + row_off, A - 1)] // TOP_K
    row_tok = jnp.where(row_off < counts[row_e], src, rows % T)

    xb = hf.at[row_tok].get(mode='promise_in_bounds')
    yb = _experts(xb, tile_e, tile_valid, w_gu, b_gu, w_dn, b_dn, tm_e)
    dest_slot_major = dest_of.reshape(T, TOP_K).T.reshape(A)
    y4 = yb.at[dest_slot_major].get(mode='promise_in_bounds').reshape(TOP_K, T, D)
    return _combine(x2, y4, gate, row(final_g), t_head)


def kernel(x_prompt, x_sample, mem_prompt, mem_sample, norm_mix_g, w_in, shift_mu, decay_w0, decay_w2,
           iclr_a0, iclr_a2, gate_g2, k_k, k_a, r_k, lnx_g, lnx_b, conv_w, conv_b, conv_ln_g, conv_ln_b,
           w_branch, w_o, norm_x_g, norm_mem_g, w_xq, w_xkv, w_xo, norm_ffn_g, w_router, b_router,
           w_gu, b_gu, w_dn, b_dn, final_g):
    layer_params = (norm_mix_g, w_in, shift_mu, decay_w0, decay_w2, iclr_a0, iclr_a2, gate_g2, k_k, k_a,
                    r_k, lnx_g, lnx_b, conv_w, conv_b, conv_ln_g, conv_ln_b, w_branch, w_o, norm_x_g,
                    norm_mem_g, w_xq, w_xkv, w_xo, norm_ffn_g, w_router, b_router, w_gu, b_gu, w_dn, b_dn)
    assert all(p.shape[0] == 1 for p in layer_params), "single-layer stack expected"
    bp, seq_len, D = x_prompt.shape
    bs = x_sample.shape[0]
    assert x_sample.shape[1] == seq_len
    n_seq = bp + bs
    x = jnp.concatenate([x_prompt, x_sample], axis=0).reshape(n_seq * seq_len, D)
    mem = jnp.concatenate([mem_prompt, mem_sample], axis=0).reshape(-1, D)
    y_p, y_s = _layer(x, mem, n_seq, seq_len, bp * seq_len, *[p[0] for p in layer_params], final_g)
    return y_p.reshape(bp, seq_len, D), y_s.reshape(bs, seq_len, D)
```

```python
import functools
import math

import jax
import jax.numpy as jnp
from jax import lax
from jax.experimental import pallas as pl
from jax.experimental.pallas import tpu as pltpu

F32 = jnp.float32
BF16 = jnp.bfloat16

RWKV_HEAD = 64
DECAY_SCALE = math.exp(-0.5)
LNX_EPS = RWKV_HEAD * 1e-5
RMS_EPS = 1e-5
LN_EPS = 1e-5
XATTN_HEADS = 4
TOP_K = 4
SWIGLU_ALPHA = 1.702
SWIGLU_LIMIT = 7.0

LANES = 128
SUBLANES = 8
WKV_CHUNK = 64
LORA_PAD = 128
MOE_SUBS_PER_TILE = 2
VMEM_LIMIT = 56 * 1024 * 1024


def _cparams(sem):
    return pltpu.CompilerParams(dimension_semantics=sem, vmem_limit_bytes=VMEM_LIMIT)


def _tile(n, prefs):
    for p in prefs:
        if n % p == 0:
            return p
    return n


def _dot(a, b):
    return jnp.dot(a, b, preferred_element_type=F32)


def _dot_nt(a, b):
    return lax.dot_general(a, b, (((1,), (1,)), ((), ())), preferred_element_type=F32)


def _dot_tn(a, b):
    return lax.dot_general(a, b, (((0,), (0,)), ((), ())), preferred_element_type=F32)


def _split2(x):
    hi = x.astype(BF16)
    lo = (x - hi.astype(F32)).astype(BF16)
    return hi, lo


def _split3(x):
    hi = x.astype(BF16)
    r1 = x - hi.astype(F32)
    mid = r1.astype(BF16)
    lo = (r1 - mid.astype(F32)).astype(BF16)
    return hi, mid, lo


def _dot_exact_rhs(x, w_bf16):
    h, m, l = _split3(x)
    return _dot(h, w_bf16) + _dot(m, w_bf16) + _dot(l, w_bf16)


def _dot3(x, w_hi, w_lo):
    xh, xl = _split2(x)
    return _dot(xh, w_hi) + _dot(xl, w_hi) + _dot(xh, w_lo)


def _rms(x, g):
    return x * lax.rsqrt(jnp.mean(x * x, axis=-1, keepdims=True) + RMS_EPS) * g


def _sigmoid(x):
    return 0.5 * jnp.tanh(0.5 * x) + 0.5


def _inproj_shift_kernel(x_ref, xp_ref, xn_ref, g_ref, w_ref, mu_ref, o_ref, h_scr, hp_scr, hn_scr,
                         *, seq_len):
    i = pl.program_id(0)
    j = pl.program_id(1)
    tm = x_ref.shape[0]

    @pl.when(j == 0)
    def _():
        g = g_ref[...]
        h_scr[...] = _rms(x_ref[...], g).astype(BF16)
        hp_scr[...] = _rms(xp_ref[...], g).astype(BF16)
        hn_scr[...] = _rms(xn_ref[...], g).astype(BF16)

    w = w_ref[...]
    mu = mu_ref[...]
    pp = _dot(hp_scr[...], w)[SUBLANES - 1:SUBLANES, :]
    pn = _dot(hn_scr[...], w)[0:1, :]
    first = (i * tm) % seq_len == 0
    last = ((i + 1) * tm) % seq_len == 0
    pp = jnp.where(first, 0.0, pp)
    pn = jnp.where(last, 0.0, pn)
    sb = _tile(tm, (256,))
    n_sb = tm // sb
    rid = lax.broadcasted_iota(jnp.int32, (sb, w.shape[1]), 0)
    prods = []

    def shift_piece(k):
        p = prods[k]
        above = pp if k == 0 else prods[k - 1][sb - 1:sb, :]
        below = pn if k == n_sb - 1 else prods[k + 1][0:1, :]
        up = jnp.where(rid == 0, above, pltpu.roll(p, 1, 0))
        dn = jnp.where(rid == sb - 1, below, pltpu.roll(p, sb - 1, 0))
        o_ref[k * sb:(k + 1) * sb, :] = (p + mu * (0.5 * (up + dn) - p)).astype(o_ref.dtype)

    for k in range(n_sb):
        prods.append(_dot(h_scr[k * sb:(k + 1) * sb, :], w))
        if k >= 1:
            shift_piece(k - 1)
    shift_piece(n_sb - 1)


def _inproj_shift(x, g, w_bf16, mu, seq_len, tn, out_dtype):
    T, D = x.shape
    N = w_bf16.shape[1]
    tm = _tile(seq_len, (1024, 512, 256, 128, 64, 32, 16, 8))
    nb8 = T // SUBLANES
    r8 = tm // SUBLANES
    return pl.pallas_call(
        functools.partial(_inproj_shift_kernel, seq_len=seq_len),
        grid=(T // tm, N // tn),
        in_specs=[
            pl.BlockSpec((tm, D), lambda i, j: (i, 0)),
            pl.BlockSpec((SUBLANES, D), lambda i, j: (jnp.maximum(i * r8 - 1, 0), 0)),
            pl.BlockSpec((SUBLANES, D), lambda i, j: (jnp.minimum((i + 1) * r8, nb8 - 1), 0)),
            pl.BlockSpec((1, D), lambda i, j: (0, 0)),
            pl.BlockSpec((D, tn), lambda i, j: (0, j)),
            pl.BlockSpec((1, tn), lambda i, j: (0, j)),
        ],
        out_specs=pl.BlockSpec((tm, tn), lambda i, j: (i, j)),
        out_shape=jax.ShapeDtypeStruct((T, N), out_dtype),
        scratch_shapes=[pltpu.VMEM((tm, D), BF16), pltpu.VMEM((SUBLANES, D), BF16),
                        pltpu.VMEM((SUBLANES, D), BF16)],
        compiler_params=_cparams(("parallel", "arbitrary")),
        name="inproj_shift",
    )(x, x, x, g, w_bf16, mu)


def _inproj_glu_kernel(x_ref, g_ref, wa_ref, wb_ref, o_ref, h_scr):
    @pl.when(pl.program_id(1) == 0)
    def _():
        h_scr[...] = _rms(x_ref[...], g_ref[...]).astype(BF16)

    h = h_scr[...]
    a = _dot(h, wa_ref[...])
    b = _dot(h, wb_ref[...])
    o_ref[...] = (a * _sigmoid(b)).astype(o_ref.dtype)


def _inproj_glu(x, g, wa, wb):
    T, D = x.shape
    N = wa.shape[1]
    tm = _tile(T, (512, 256, 128, 64, 32, 16, 8))
    tn = _tile(N, (512, 256, 128))
    return pl.pallas_call(
        _inproj_glu_kernel,
        grid=(T // tm, N // tn),
        in_specs=[
            pl.BlockSpec((tm, D), lambda i, j: (i, 0)),
            pl.BlockSpec((1, D), lambda i, j: (0, 0)),
            pl.BlockSpec((D, tn), lambda i, j: (0, j)),
            pl.BlockSpec((D, tn), lambda i, j: (0, j)),
        ],
        out_specs=pl.BlockSpec((tm, tn), lambda i, j: (i, j)),
        out_shape=jax.ShapeDtypeStruct((T, N), BF16),
        scratch_shapes=[pltpu.VMEM((tm, D), BF16)],
        compiler_params=_cparams(("parallel", "arbitrary")),
        name="inproj_glu",
    )(x, g, wa, wb)


def _wkv_kernel(zr_ref, zk_ref, zv_ref, dd_ref, ad_ref, w0_ref, w2h_ref, w2l_ref, a0_ref, a2_ref,
                kk_ref, ka_ref, rk_ref, o_ref, bonus_ref, state_ref, *, n_pairs, bwd):
    L = WKV_CHUNK
    PW = 2 * RWKV_HEAD
    TT = zr_ref.shape[0]
    n_chunks = TT // L

    @pl.when(pl.program_id(2) == 0)
    def _():
        state_ref[...] = jnp.zeros_like(state_ref)

    row = lax.broadcasted_iota(jnp.int32, (L, 2 * L), 0)
    col = lax.broadcasted_iota(jnp.int32, (L, 2 * L), 1) % L
    strict = (col > row) if bwd else (col < row)
    incl = (col >= row) if bwd else (col <= row)
    ipk = jnp.where(col == row, 1.0, 0.0)
    lane = lax.broadcasted_iota(jnp.int32, (1, PW), 1)
    m0 = lane < RWKV_HEAD
    lane2 = lax.broadcasted_iota(jnp.int32, (1, 2 * PW), 1) % PW
    m0w = lane2 < RWKV_HEAD
    srow = lax.broadcasted_iota(jnp.int32, (PW, PW), 0)
    scol = lax.broadcasted_iota(jnp.int32, (PW, PW), 1)
    same_head = (srow // RWKV_HEAD) == (scol // RWKV_HEAD)
    eye = srow == scol
    HSW = min(zr_ref.shape[1], 2 * PW)
    hrow = lax.broadcasted_iota(jnp.int32, (HSW, HSW), 0)
    hcol = lax.broadcasted_iota(jnp.int32, (HSW, HSW), 1)
    head_ones_w = jnp.where((hrow // RWKV_HEAD) == (hcol // RWKV_HEAD), 1.0, 0.0).astype(BF16)

    def bd(x):
        return jnp.concatenate([jnp.where(m0, x, 0.0), jnp.where(m0, 0.0, x)], axis=0).astype(BF16)

    def bd2(x):
        return jnp.concatenate([jnp.where(m0w, x, 0.0), jnp.where(m0w, 0.0, x)], axis=0).astype(BF16)

    w0 = w0_ref[...]
    a0 = a0_ref[...]
    k_k = kk_ref[...]
    k_a = ka_ref[...]
    r_k = rk_ref[...]

    def head_sum(x, two_pass):
        n = x.shape[0]
        outs = []
        for c in range(x.shape[1] // HSW):
            xc = x[:, c * HSW:(c + 1) * HSW]
            if two_pass:
                hi, lo = _split2(xc)
                res = _dot(jnp.concatenate([hi, lo], axis=0), head_ones_w)
                outs.append(res[:n] + res[n:])
            else:
                outs.append(_dot(xc.astype(BF16), head_ones_w))
        return jnp.concatenate(outs, axis=1) if len(outs) > 1 else outs[0]

    r = zr_ref[...].astype(F32)
    k = zk_ref[...].astype(F32)
    v = zv_ref[...].astype(F32)
    dlin = w0 + _dot3(jnp.tanh(dd_ref[...]), w2h_ref[...], w2l_ref[...])
    lw = -DECAY_SCALE * _sigmoid(dlin)
    iclr = _sigmoid(a0 + _dot(ad_ref[...].astype(BF16), a2_ref[...]))
    kkr = k * k_k
    kmod = k * (1.0 + (iclr - 1.0) * k_a)
    cum = _chunk_cumsum(lw, bwd)
    e_in = jnp.exp(cum)
    e_ex = jnp.exp(cum - lw)
    e_ng = jnp.exp(-cum)
    kk = kkr * lax.rsqrt(jnp.maximum(head_sum(kkr * kkr, False), 1e-24))
    bonus_ref[...] = (head_sum(r * kmod * r_k, True) * v).astype(bonus_ref.dtype)
    b_all = kk * iclr
    ah_all = -kk * e_ex
    rh_all = r * e_in
    bh_all = b_all * e_ng
    kh_all = kmod * e_ng

    chunk_order = list(range(n_chunks - 1, -1, -1) if bwd else range(n_chunks))
    pair_vals = []
    for p in range(n_pairs):
        sl = slice(p * PW, (p + 1) * PW)
        pair_vals.append(dict(b=b_all[:, sl], ah=ah_all[:, sl], rh=rh_all[:, sl],
                              bh=bh_all[:, sl], kh=kh_all[:, sl]))

    probs = []
    for ci in chunk_order:
        for p in range(n_pairs):
            sl = slice(p * PW, (p + 1) * PW)
            rs = slice(ci * L, (ci + 1) * L)
            pv = pair_vals[p]
            q = dict(p=p, ci=ci, sl=sl, rs=rs, ah=pv["ah"][rs], rh=pv["rh"][rs], v=v[rs, sl])
            lhs = jnp.concatenate([q["ah"], q["rh"]], axis=0).astype(BF16)
            g = _dot_nt(lhs, jnp.concatenate([bd(pv["bh"][rs]), bd(pv["kh"][rs])], axis=0))
            a32 = jnp.where(strict, g[:L, :2 * L], 0.0)
            q["a"] = a32.astype(BF16)
            q["t"] = (ipk + a32).astype(BF16)
            q["rb"] = jnp.where(incl, g[L:, :2 * L], 0.0).astype(BF16)
            q["akrk"] = jnp.concatenate([jnp.where(strict, g[:L, 2 * L:], 0.0),
                                         jnp.where(incl, g[L:, 2 * L:], 0.0)], axis=0).astype(BF16)
            probs.append(q)
    for q in probs:
        q["kv"] = _dot(q["akrk"], bd(q["v"]))
    for q in probs:
        q["ai"] = _dot(q["a"], bd(q["a"])).astype(BF16)
    for lvl in range(1, 6):
        more = lvl < 5
        for q in probs:
            lhs = jnp.concatenate([q["t"], q["ai"]], axis=0) if more else q["t"]
            res = _dot(lhs, bd(q["ai"]))
            q["t"] = (q["t"].astype(F32) + res[:L]).astype(BF16)
            if more:
                q["ai"] = res[L:].astype(BF16)
    for q in probs:
        q["rbt"] = _dot(q["rb"], bd(q["t"])).astype(BF16)
    for q in probs:
        y0 = jnp.concatenate([q["ah"], q["kv"][:L]], axis=1)
        q["ry"] = _dot(jnp.concatenate([q["t"], q["rbt"]], axis=0), bd2(y0))
    for q in probs:
        p, rs, sl, ry = q["p"], q["rs"], q["sl"], q["ry"]
        end = q["ci"] * L if bwd else q["ci"] * L + L - 1
        tot = cum[end:end + 1, sl]
        e_rm = jnp.exp(tot - cum[rs, sl])
        q["rt"] = q["rh"] + ry[L:, :PW]
        q["ob"] = ry[L:, PW:] + q["kv"][L:]
        lhs_t = jnp.concatenate([pair_vals[p]["b"][rs] * e_rm, kmod[rs, sl] * e_rm], axis=0).astype(BF16)
        rhs_t = jnp.concatenate(
            [ry[:L], jnp.concatenate([jnp.zeros((L, PW), F32), q["v"]], axis=1)], axis=0).astype(BF16)
        mn = _dot_tn(lhs_t, rhs_t)
        q["mm"] = jnp.where(eye, jnp.exp(tot), 0.0) + jnp.where(same_head, mn[:, :PW], 0.0)
        q["nn"] = jnp.where(same_head, mn[:, PW:], 0.0)
    states = [state_ref[p] for p in range(n_pairs)]
    for q in probs:
        p = q["p"]
        res = _dot(jnp.concatenate([q["rt"], q["mm"]], axis=0).astype(BF16), states[p].astype(BF16))
        o_ref[q["rs"], q["sl"]] = (res[:L] + q["ob"]).astype(o_ref.dtype)
        states[p] = res[L:] + q["nn"]
    for p in range(n_pairs):
        state_ref[p] = states[p]


def _chunk_cumsum(x, bwd):
    L = WKV_CHUNK
    n = x.shape[0]
    rin = lax.broadcasted_iota(jnp.int32, (n, 1), 0) % L
    s = 1
    while s < L:
        if bwd:
            x = x + jnp.where(rin < L - s, pltpu.roll(x, n - s, 0), 0.0)
        else:
            x = x + jnp.where(rin >= s, pltpu.roll(x, s, 0), 0.0)
        s *= 2
    return x


def _wkv(z_rkv, z_lora, decay_w0, w2h, w2l, iclr_a0, a2_bf16, k_k, k_a, r_k, n_seq, seq_len, bwd):
    T = z_rkv.shape[0]
    W = z_rkv.shape[1] // 3
    PW = 2 * RWKV_HEAD
    n_pairs = _tile(W // PW, (4, 2, 1))
    GW = n_pairs * PW
    n_groups = W // GW
    TT = _tile(seq_len, (256, 128, 64))
    nT = seq_len // TT
    d = int(bwd)

    def tb(b, c):
        return b * nT + (nT - 1 - c if bwd else c)

    zspec = lambda off: pl.BlockSpec((TT, GW), lambda b, g, c: (tb(b, c), off * n_groups + g))
    pspec = pl.BlockSpec((None, 1, GW), lambda b, g, c: (d, 0, g))
    lspec = pl.BlockSpec((None, LORA_PAD, GW), lambda b, g, c: (d, 0, g))
    cspec = pl.BlockSpec((1, GW), lambda b, g, c: (0, g))
    ospec = pl.BlockSpec((TT, GW), lambda b, g, c: (tb(b, c), g))
    return pl.pallas_call(
        functools.partial(_wkv_kernel, n_pairs=n_pairs, bwd=bwd),
        grid=(n_seq, n_groups, nT),
        in_specs=[
            zspec(0), zspec(1), zspec(2),
            pl.BlockSpec((TT, LORA_PAD), lambda b, g, c: (tb(b, c), d)),
            pl.BlockSpec((TT, LORA_PAD), lambda b, g, c: (tb(b, c), 2 + d)),
            pspec, lspec, lspec, pspec, lspec, cspec, cspec, cspec,
        ],
        out_specs=[ospec, ospec],
        out_shape=[jax.ShapeDtypeStruct((T, W), BF16), jax.ShapeDtypeStruct((T, W), BF16)],
        scratch_shapes=[pltpu.VMEM((n_pairs, PW, PW), F32)],
        compiler_params=_cparams(("parallel", "parallel", "arbitrary")),
        name="wkv_scan_bwd" if bwd else "wkv_scan_fwd",
    )(z_rkv, z_rkv, z_rkv, z_lora, z_lora, decay_w0, w2h, w2l, iclr_a0, a2_bf16, k_k, k_a, r_k)


def _rwkv_post_kernel(of_ref, ob_ref, bf_ref, bb_ref, gd_ref, g2_ref, lg_ref, lb_ref, out_ref):
    W = out_ref.shape[1]
    PW = 2 * RWKV_HEAD
    srow = lax.broadcasted_iota(jnp.int32, (PW, PW), 0)
    scol = lax.broadcasted_iota(jnp.int32, (PW, PW), 1)
    head_mean = jnp.where((srow // RWKV_HEAD) == (scol // RWKV_HEAD), 1.0, 0.0).astype(BF16)
    inv = 1.0 / RWKV_HEAD
    gate = _dot(_sigmoid(gd_ref[...]).astype(BF16), g2_ref[...])
    for p in range(W // PW):
        sl = slice(p * PW, (p + 1) * PW)
        o = of_ref[:, sl].astype(F32) + ob_ref[:, sl].astype(F32)
        mean = _dot_exact_rhs(o, head_mean) * inv
        oc = o - mean
        var = _dot_exact_rhs(oc * oc, head_mean) * inv
        y = oc * lax.rsqrt(var + LNX_EPS) * lg_ref[:, sl] + lb_ref[:, sl]
        y = y + bf_ref[:, sl].astype(F32) + bb_ref[:, sl].astype(F32)
        out_ref[:, sl] = (y * gate[:, sl]).astype(out_ref.dtype)


def _rwkv_post(o_f, o_b, bonus_f, bonus_b, z_lora, g2, lnx_g, lnx_b):
    T, W = o_f.shape
    tm = _tile(T, (256, 128, 64, 32, 16, 8))
    GL = g2.shape[0]
    gd_blk = (4 * LORA_PAD) // GL
    tspec = pl.BlockSpec((tm, W), lambda i: (i, 0))
    return pl.pallas_call(
        _rwkv_post_kernel,
        grid=(T // tm,),
        in_specs=[
            tspec, tspec, tspec, tspec,
            pl.BlockSpec((tm, GL), lambda i: (i, gd_blk)),
            pl.BlockSpec((GL, W), lambda i: (0, 0)),
            pl.BlockSpec((1, W), lambda i: (0, 0)),
            pl.BlockSpec((1, W), lambda i: (0, 0)),
        ],
        out_specs=pl.BlockSpec((tm, W), lambda i: (i, 0)),
        out_shape=jax.ShapeDtypeStruct((T, W), BF16),
        compiler_params=_cparams(("parallel",)),
        name="rwkv_post",
    )(o_f, o_b, bonus_f, bonus_b, z_lora, g2, lnx_g, lnx_b)


def _conv_kernel(u_ref, up_ref, un_ref, w_ref, b_ref, lg_ref, lb_ref, o_ref, ext_scr, *, seq_len, halo):
    i = pl.program_id(0)
    tm = u_ref.shape[0]
    K = w_ref.shape[0]
    first = (i * tm) % seq_len == 0
    last = ((i + 1) * tm) % seq_len == 0
    n_ext = tm + 2 * halo
    ext = jnp.concatenate([jnp.where(first, 0.0, up_ref[...].astype(F32)), u_ref[...].astype(F32),
                           jnp.where(last, 0.0, un_ref[...].astype(F32))], axis=0)
    ext_scr[0] = ext
    for r in range(1, SUBLANES):
        ext_scr[r] = pltpu.roll(ext, n_ext - r, 0)
    sub = min(tm, 32)
    base = halo - K // 2
    for s in range(tm // sub):
        acc = jnp.zeros((sub, u_ref.shape[1]), F32)
        for j in range(K):
            off = base + j
            row0 = s * sub + (off // SUBLANES) * SUBLANES
            acc = acc + w_ref[j:j + 1, :] * ext_scr[off % SUBLANES, row0:row0 + sub, :]
        acc = acc + b_ref[...]
        mean = jnp.mean(acc, axis=-1, keepdims=True)
        xc = acc - mean
        var = jnp.mean(xc * xc, axis=-1, keepdims=True)
        y = xc * lax.rsqrt(var + LN_EPS) * lg_ref[...] + lb_ref[...]
        o_ref[s * sub:(s + 1) * sub, :] = (y * _sigmoid(y)).astype(o_ref.dtype)


def _conv(u, conv_w, conv_b, ln_g, ln_b, seq_len):
    T, C = u.shape
    K = conv_w.shape[0]
    halo = 16
    assert K // 2 <= halo
    tm = _tile(seq_len, (128, 64, 32, 16))
    rh = tm // halo
    nbh = T // halo
    return pl.pallas_call(
        functools.partial(_conv_kernel, seq_len=seq_len, halo=halo),
        grid=(T // tm,),
        in_specs=[
            pl.BlockSpec((tm, C), lambda i: (i, 0)),
            pl.BlockSpec((halo, C), lambda i: (jnp.maximum(i * rh - 1, 0), 0)),
            pl.BlockSpec((halo, C), lambda i: (jnp.minimum((i + 1) * rh, nbh - 1), 0)),
            pl.BlockSpec((K, C), lambda i: (0, 0)),
            pl.BlockSpec((1, C), lambda i: (0, 0)),
            pl.BlockSpec((1, C), lambda i: (0, 0)),
            pl.BlockSpec((1, C), lambda i: (0, 0)),
        ],
        out_specs=pl.BlockSpec((tm, C), lambda i: (i, 0)),
        out_shape=jax.ShapeDtypeStruct((T, C), BF16),
        scratch_shapes=[pltpu.VMEM((SUBLANES, tm + 2 * halo, C), F32)],
        compiler_params=_cparams(("parallel",)),
        name="conformer_conv",
    )(u, u, u, conv_w, conv_b, ln_g, ln_b)


def _merge_kernel(x_ref, g_ref, orw_ref, ocv_ref, wg1_ref, wg2_ref, wb1_ref, wb2_ref, o_ref, h_scr):
    @pl.when(pl.program_id(1) == 0)
    def _():
        h_scr[...] = _rms(x_ref[...], g_ref[...]).astype(BF16)

    h = h_scr[...]
    g1 = _sigmoid(_dot(h, wg1_ref[...]))
    g2 = _sigmoid(_dot(h, wg2_ref[...]))
    y1 = _dot(orw_ref[...], wb1_ref[...])
    y2 = _dot(ocv_ref[...], wb2_ref[...])
    o_ref[...] = (g1 * y1 + g2 * y2).astype(o_ref.dtype)


def _merge(x, g, o_rwkv, o_conv, wg1, wg2, wb1, wb2):
    T, D = x.shape
    W = o_rwkv.shape[1]
    C = o_conv.shape[1]
    tm = _tile(T, (512, 256, 128, 64, 32, 16, 8))
    tn = _tile(D, (512, 256, 128))
    return pl.pallas_call(
        _merge_kernel,
        grid=(T // tm, D // tn),
        in_specs=[
            pl.BlockSpec((tm, D), lambda i, j: (i, 0)),
            pl.BlockSpec((1, D), lambda i, j: (0, 0)),
            pl.BlockSpec((tm, W), lambda i, j: (i, 0)),
            pl.BlockSpec((tm, C), lambda i, j: (i, 0)),
            pl.BlockSpec((D, tn), lambda i, j: (0, j)),
            pl.BlockSpec((D, tn), lambda i, j: (0, j)),
            pl.BlockSpec((W, tn), lambda i, j: (0, j)),
            pl.BlockSpec((C, tn), lambda i, j: (0, j)),
        ],
        out_specs=pl.BlockSpec((tm, tn), lambda i, j: (i, j)),
        out_shape=jax.ShapeDtypeStruct((T, D), BF16),
        scratch_shapes=[pltpu.VMEM((tm, D), BF16)],
        compiler_params=_cparams(("parallel", "arbitrary")),
        name="merge_gates",
    )(x, g, o_rwkv, o_conv, wg1, wg2, wb1, wb2)


def _mm_res_kernel(a_ref, w_ref, res_ref, o_ref):
    o_ref[...] = res_ref[...] + _dot(a_ref[...], w_ref[...])


def _mm_res(a, w, res):
    T, K = a.shape
    N = w.shape[1]
    tm = _tile(T, (1024, 512, 256, 128, 64, 32, 16, 8))
    tn = _tile(N, (1024, 512, 256, 128))
    return pl.pallas_call(
        _mm_res_kernel,
        grid=(T // tm, N // tn),
        in_specs=[
            pl.BlockSpec((tm, K), lambda i, j: (i, 0)),
            pl.BlockSpec((K, tn), lambda i, j: (0, j)),
            pl.BlockSpec((tm, tn), lambda i, j: (i, j)),
        ],
        out_specs=pl.BlockSpec((tm, tn), lambda i, j: (i, j)),
        out_shape=jax.ShapeDtypeStruct((T, N), F32),
        compiler_params=_cparams(("parallel", "parallel")),
        name="proj_residual",
    )(a, w, res)


def _mm_norm_kernel(x_ref, g_ref, w_ref, o_ref, h_scr):
    @pl.when(pl.program_id(1) == 0)
    def _():
        h_scr[...] = _rms(x_ref[...], g_ref[...]).astype(BF16)

    o_ref[...] = _dot(h_scr[...], w_ref[...]).astype(o_ref.dtype)


def _mm_norm(x, g, w):
    T, D = x.shape
    N = w.shape[1]
    tm = _tile(T, (512, 256, 128, 64, 32, 16, 8))
    tn = _tile(N, (1024, 512, 256, 128))
    return pl.pallas_call(
        _mm_norm_kernel,
        grid=(T // tm, N // tn),
        in_specs=[
            pl.BlockSpec((tm, D), lambda i, j: (i, 0)),
            pl.BlockSpec((1, D), lambda i, j: (0, 0)),
            pl.BlockSpec((D, tn), lambda i, j: (0, j)),
        ],
        out_specs=pl.BlockSpec((tm, tn), lambda i, j: (i, j)),
        out_shape=jax.ShapeDtypeStruct((T, N), BF16),
        scratch_shapes=[pltpu.VMEM((tm, D), BF16)],
        compiler_params=_cparams(("parallel", "arbitrary")),
        name="norm_proj",
    )(x, g, w)


def _xattn_kernel(q_ref, k_ref, v_ref, o_ref):
    D = q_ref.shape[1]
    hd = D // XATTN_HEADS
    scale = hd ** -0.5
    for h in range(XATTN_HEADS):
        sl = slice(h * hd, (h + 1) * hd)
        s = _dot_nt(q_ref[:, sl], k_ref[:, sl]) * scale
        s = s - jnp.max(s, axis=-1, keepdims=True)
        e = jnp.exp(s)
        p = e / jnp.sum(e, axis=-1, keepdims=True)
        o_ref[:, sl] = _dot(p.astype(BF16), v_ref[:, sl]).astype(o_ref.dtype)


def _xattn(q, kv, n_seq, seq_len, n_mem):
    T, D = q.shape
    tm = _tile(seq_len, (512, 256, 128, 64, 32, 16, 8))
    nT = seq_len // tm
    return pl.pallas_call(
        _xattn_kernel,
        grid=(n_seq, nT),
        in_specs=[
            pl.BlockSpec((tm, D), lambda b, i: (b * nT + i, 0)),
            pl.BlockSpec((n_mem, D), lambda b, i: (b, 0)),
            pl.BlockSpec((n_mem, D), lambda b, i: (b, 1)),
        ],
        out_specs=pl.BlockSpec((tm, D), lambda b, i: (b * nT + i, 0)),
        out_shape=jax.ShapeDtypeStruct((T, D), BF16),
        compiler_params=_cparams(("parallel", "parallel")),
        name="cross_attention",
    )(q, kv, kv)


def _router_kernel(x_ref, g_ref, wh_ref, wl_ref, b_ref, hf_ref, e_ref, gate_ref):
    hf = _rms(x_ref[...], g_ref[...])
    hf_ref[...] = hf.astype(hf_ref.dtype)
    logits = _dot3(hf, wh_ref[...], wl_ref[...]) + b_ref[...]
    E = logits.shape[1]
    eid = lax.broadcasted_iota(jnp.int32, logits.shape, 1).astype(F32)
    work = logits
    vals = []
    idxs = []
    for _ in range(TOP_K):
        m = jnp.max(work, axis=-1, keepdims=True)
        idx = jnp.min(jnp.where(work == m, eid, float(E)), axis=-1, keepdims=True)
        vals.append(m)
        idxs.append(idx.astype(jnp.int32))
        work = jnp.where(eid == idx, -jnp.inf, work)
    ex = [jnp.exp(vv - vals[0]) for vv in vals]
    den = ex[0]
    for t in ex[1:]:
        den = den + t
    kid = lax.broadcasted_iota(jnp.int32, (logits.shape[0], TOP_K), 1)
    e_out = jnp.zeros((logits.shape[0], TOP_K), jnp.int32)
    g_out = jnp.zeros((logits.shape[0], TOP_K), F32)
    for t in range(TOP_K):
        e_out = jnp.where(kid == t, idxs[t], e_out)
        g_out = jnp.where(kid == t, ex[t] / den, g_out)
    e_ref[...] = e_out
    gate_ref[...] = g_out


def _router(x, g, w_router, b_router):
    T, D = x.shape
    E = w_router.shape[1]
    wh, wl = _split2(w_router)
    tm = _tile(T, (256, 128, 64, 32, 16, 8))
    return pl.pallas_call(
        _router_kernel,
        grid=(T // tm,),
        in_specs=[
            pl.BlockSpec((tm, D), lambda i: (i, 0)),
            pl.BlockSpec((1, D), lambda i: (0, 0)),
            pl.BlockSpec((D, E), lambda i: (0, 0)),
            pl.BlockSpec((D, E), lambda i: (0, 0)),
            pl.BlockSpec((1, E), lambda i: (0, 0)),
        ],
        out_specs=[
            pl.BlockSpec((tm, D), lambda i: (i, 0)),
            pl.BlockSpec((tm, TOP_K), lambda i: (i, 0)),
            pl.BlockSpec((tm, TOP_K), lambda i: (i, 0)),
        ],
        out_shape=[jax.ShapeDtypeStruct((T, D), BF16),
                   jax.ShapeDtypeStruct((T, TOP_K), jnp.int32),
                   jax.ShapeDtypeStruct((T, TOP_K), F32)],
        compiler_params=_cparams(("parallel",)),
        name="router",
    )(x, g, wh, wl, b_router)


def _expert_kernel(te_ref, tv_ref, x_ref, wg_ref, wu_ref, bg_ref, bu_ref, wd_ref, bd_ref, o_ref, act_ref,
                   *, nf, sub):
    i = pl.program_id(0)
    s = pl.program_id(1)
    tf = wg_ref.shape[1]
    n_valid = tv_ref[i]
    for r0 in range(0, x_ref.shape[0], sub):
        rows = slice(r0, r0 + sub)
        live = n_valid > r0

        @pl.when(live & (s < nf))
        def _(rows=rows):
            x = x_ref[rows, :]
            g = _dot(x, wg_ref[...].astype(BF16)) + bg_ref[...]
            u = _dot(x, wu_ref[...].astype(BF16)) + bu_ref[...]
            g = jnp.minimum(g, SWIGLU_LIMIT)
            u = jnp.clip(u, -SWIGLU_LIMIT, SWIGLU_LIMIT)
            act = ((u + 1.0) * (g * _sigmoid(SWIGLU_ALPHA * g))).astype(BF16)
            for f in range(nf):
                @pl.when(s == f)
                def _(f=f):
                    act_ref[rows, f * tf:(f + 1) * tf] = act

        @pl.when(live & (s >= nf))
        def _(rows=rows):
            y = _dot(act_ref[rows, :], wd_ref[...].astype(BF16)) + bd_ref[...]
            o_ref[rows, :] = y.astype(o_ref.dtype)

        @pl.when(jnp.logical_not(live) & (s >= nf))
        def _(rows=rows):
            o_ref[rows, :] = jnp.zeros((sub, o_ref.shape[1]), o_ref.dtype)


def _experts(xb, tile_e, tile_valid, w_gu, b_gu, w_dn, b_dn, tm, sub):
    R, D = xb.shape
    E, _, F2 = w_gu.shape
    F = F2 // 2
    tf = _tile(F, (256, 128))
    tn = _tile(D, (256, 128))
    nf = F // tf
    nn = D // tn
    n_tiles = R // tm
    b_gu3 = b_gu.reshape(E, 1, F2)
    b_dn3 = b_dn.reshape(E, 1, D)

    def fa(i, s, tv):
        return jnp.where(tv[i] > 0, jnp.minimum(s, nf - 1), nf - 1)

    def nb(i, s, tv):
        return jnp.where(tv[i] > 0, jnp.clip(s - nf, 0, nn - 1), nn - 1)

    grid_spec = pltpu.PrefetchScalarGridSpec(
        num_scalar_prefetch=2,
        grid=(n_tiles, nf + nn),
        in_specs=[
            pl.BlockSpec((tm, D), lambda i, s, te, tv: (i, 0)),
            pl.BlockSpec((None, D, tf), lambda i, s, te, tv: (te[i], 0, fa(i, s, tv))),
            pl.BlockSpec((None, D, tf), lambda i, s, te, tv: (te[i], 0, nf + fa(i, s, tv))),
            pl.BlockSpec((None, 1, tf), lambda i, s, te, tv: (te[i], 0, fa(i, s, tv))),
            pl.BlockSpec((None, 1, tf), lambda i, s, te, tv: (te[i], 0, nf + fa(i, s, tv))),
            pl.BlockSpec((None, F, tn), lambda i, s, te, tv: (te[i], 0, nb(i, s, tv))),
            pl.BlockSpec((None, 1, tn), lambda i, s, te, tv: (te[i], 0, nb(i, s, tv))),
        ],
        out_specs=pl.BlockSpec((tm, tn), lambda i, s, te, tv: (i, jnp.clip(s - nf, 0, nn - 1))),
        scratch_shapes=[pltpu.VMEM((tm, F), BF16)],
    )
    return pl.pallas_call(
        functools.partial(_expert_kernel, nf=nf, sub=sub),
        grid_spec=grid_spec,
        out_shape=jax.ShapeDtypeStruct((R, D), BF16),
        compiler_params=_cparams(("arbitrary", "arbitrary")),
        name="moe_experts",
    )(tile_e, tile_valid, xb, w_gu, w_gu, b_gu3, b_gu3, w_dn, b_dn3)


def _combine_kernel(x_ref, y_ref, gate_ref, g_ref, o_head_ref, o_tail_ref, *, n_head_tiles):
    acc = x_ref[...]
    gate = gate_ref[...]
    for t in range(TOP_K):
        acc = acc + gate[:, t:t + 1] * y_ref[t].astype(F32)
    y = _rms(acc, g_ref[...])
    i = pl.program_id(0)

    @pl.when(i < n_head_tiles)
    def _():
        o_head_ref[...] = y

    @pl.when(i >= n_head_tiles)
    def _():
        o_tail_ref[...] = y


def _combine(x, y4, gate, final_g, t_head):
    T, D = x.shape
    tm = _tile(math.gcd(t_head, T - t_head), (256, 128, 64, 32, 16, 8))
    nh = t_head // tm
    return pl.pallas_call(
        functools.partial(_combine_kernel, n_head_tiles=nh),
        grid=(T // tm,),
        in_specs=[
            pl.BlockSpec((tm, D), lambda i: (i, 0)),
            pl.BlockSpec((TOP_K, tm, D), lambda i: (0, i, 0)),
            pl.BlockSpec((tm, TOP_K), lambda i: (i, 0)),
            pl.BlockSpec((1, D), lambda i: (0, 0)),
        ],
        out_specs=[pl.BlockSpec((tm, D), lambda i: (jnp.minimum(i, nh - 1), 0)),
                   pl.BlockSpec((tm, D), lambda i: (jnp.maximum(i - nh, 0), 0))],
        out_shape=[jax.ShapeDtypeStruct((t_head, D), F32), jax.ShapeDtypeStruct((T - t_head, D), F32)],
        compiler_params=_cparams(("arbitrary",)),
        name="moe_combine_norm",
    )(x, y4, gate, final_g)


def _pad_rows(w, n):
    return jnp.pad(w, ((0, 0),) * (w.ndim - 2) + ((0, n - w.shape[-2]), (0, 0)))


def _pad_cols(w, n):
    return jnp.pad(w, ((0, 0),) * (w.ndim - 1) + ((0, n - w.shape[-1]),))


def _moe_tile_rows(n_assign, n_experts):
    for tm in (1024, 512, 256, 128, 64, 32, 16, 8):
        if n_assign >= 4 * n_experts * tm or tm == 8:
            return tm


def _layer(x, mem, n_seq, seq_len, t_head, norm_mix_g, w_in, shift_mu, decay_w0, decay_w2, iclr_a0, iclr_a2,
           gate_g2, k_k, k_a, r_k, lnx_g, lnx_b, conv_w, conv_b, conv_ln_g, conv_ln_b, w_branch, w_o,
           norm_x_g, norm_mem_g, w_xq, w_xkv, w_xo, norm_ffn_g, w_router, b_router, w_gu, b_gu,
           w_dn, b_dn, final_g):
    T, D = x.shape
    W = k_k.shape[0]
    DL = decay_w2.shape[1]
    AL = iclr_a2.shape[1]
    GL = gate_g2.shape[0]
    C = conv_w.shape[1]
    E = w_router.shape[1]
    n_mem = mem.shape[0] // n_seq
    row = lambda v: v.reshape(1, -1)

    o3 = 3 * W
    o4 = o3 + 2 * DL
    o5 = o4 + 2 * AL
    o6 = o5 + GL
    seg = lambda m, a, b, n: _pad_cols(m[..., a:b], n)
    lora_cols = lambda m: jnp.concatenate(
        [seg(m, o3, o3 + DL, LORA_PAD), seg(m, o3 + DL, o4, LORA_PAD), seg(m, o4, o4 + AL, LORA_PAD),
         seg(m, o4 + AL, o5, LORA_PAD), m[..., o5:o6]], axis=-1)
    w_rkv = w_in[:, :o3].astype(BF16)
    w_lora = lora_cols(w_in).astype(BF16)
    mu_rkv = row(shift_mu[:o3])
    mu_lora = row(lora_cols(shift_mu))
    w_ca = w_in[:, o6:o6 + C].astype(BF16)
    w_cb = w_in[:, o6 + C:o6 + 2 * C].astype(BF16)
    w_g1 = w_in[:, o6 + 2 * C:o6 + 2 * C + D].astype(BF16)
    w_g2 = w_in[:, o6 + 2 * C + D:].astype(BF16)
    g_mix = row(norm_mix_g)

    z_rkv = _inproj_shift(x, g_mix, w_rkv, mu_rkv, seq_len, _tile(o3, (1024, 512, 256, 128)), BF16)
    z_lora = _inproj_shift(x, g_mix, w_lora, mu_lora, seq_len, w_lora.shape[1], F32)
    u = _inproj_glu(x, g_mix, w_ca, w_cb)

    w2h, w2l = _split2(_pad_rows(decay_w2, LORA_PAD))
    a2_bf16 = _pad_rows(iclr_a2, LORA_PAD).astype(BF16)
    wkv_args = (z_rkv, z_lora, decay_w0.reshape(2, 1, W), w2h, w2l, iclr_a0.reshape(2, 1, W), a2_bf16,
                row(k_k), row(k_a), row(r_k), n_seq, seq_len)
    o_f, bonus_f = _wkv(*wkv_args, bwd=False)
    o_b, bonus_b = _wkv(*wkv_args, bwd=True)
    o_rwkv = _rwkv_post(o_f, o_b, bonus_f, bonus_b, z_lora, gate_g2.astype(BF16), row(lnx_g), row(lnx_b))
    o_conv = _conv(u, conv_w, row(conv_b), row(conv_ln_g), row(conv_ln_b), seq_len)

    merged = _merge(x, g_mix, o_rwkv, o_conv, w_g1, w_g2, w_branch[:W].astype(BF16),
                    w_branch[W:].astype(BF16))
    x1 = _mm_res(merged, w_o.astype(BF16), x)

    q = _mm_norm(x1, row(norm_x_g), w_xq.astype(BF16))
    kv = _mm_norm(mem, row(norm_mem_g), w_xkv.astype(BF16))
    att = _xattn(q, kv, n_seq, seq_len, n_mem)
    x2 = _mm_res(att, w_xo.astype(BF16), x1)

    hf, top_e, gate = _router(x2, row(norm_ffn_g), w_router, row(b_router))

    A = T * TOP_K
    sub_e = _moe_tile_rows(A, E)
    tm_e = MOE_SUBS_PER_TILE * sub_e
    n_tiles = (A + E * (tm_e - 1) + tm_e - 1) // tm_e
    n_rows = n_tiles * tm_e
    flat_e = top_e.reshape(A)
    iota_a = jnp.arange(A, dtype=jnp.int32)
    sorted_e, order = lax.sort((flat_e, iota_a), num_keys=1)
    _, rank = lax.sort((order, iota_a), num_keys=1)
    experts = jnp.arange(E, dtype=jnp.int32)
    start = jnp.searchsorted(sorted_e, experts, side='left').astype(jnp.int32)
    counts = jnp.searchsorted(sorted_e, experts, side='right').astype(jnp.int32) - start
    padded = (counts + tm_e - 1) // tm_e * tm_e
    pad_end = jnp.cumsum(padded)
    pad_start = pad_end - padded
    dest_of = pad_start[flat_e] + rank - start[flat_e]
    tile_start = jnp.arange(n_tiles, dtype=jnp.int32) * tm_e
    tile_e = jnp.minimum(jnp.searchsorted(pad_end, tile_start, side='right'), E - 1).astype(jnp.int32)
    tile_valid = jnp.clip(pad_start[tile_e] + counts[tile_e] - tile_start, 0, tm_e).astype(jnp.int32)
    rows = jnp.arange(n_rows, dtype=jnp.int32)
    row_e = jnp.repeat(tile_e, tm_e)
    row_off = rows - pad_start[row_e]
    src = order[jnp.minimum(start[row_e] + row_off, A - 1)] // TOP_K
    row_tok = jnp.where(row_off < counts[row_e], src, rows % T)

    xb = hf.at[row_tok].get(mode='promise_in_bounds')
    yb = _experts(xb, tile_e, tile_valid, w_gu, b_gu, w_dn, b_dn, tm_e, sub_e)
    dest_slot_major = dest_of.reshape(T, TOP_K).T.reshape(A)
    y4 = yb.at[dest_slot_major].get(mode='promise_in_bounds').reshape(TOP_K, T, D)
    return _combine(x2, y4, gate, row(final_g), t_head)


def kernel(x_prompt, x_sample, mem_prompt, mem_sample, norm_mix_g, w_in, shift_mu, decay_w0, decay_w2,
           iclr_a0, iclr_a2, gate_g2, k_k, k_a, r_k, lnx_g, lnx_b, conv_w, conv_b, conv_ln_g, conv_ln_b,
           w_branch, w_o, norm_x_g, norm_mem_g, w_xq, w_xkv, w_xo, norm_ffn_g, w_router, b_router,
           w_gu, b_gu, w_dn, b_dn, final_g):
    layer_params = (norm_mix_g, w_in, shift_mu, decay_w0, decay_w2, iclr_a0, iclr_a2, gate_g2, k_k, k_a,
                    r_k, lnx_g, lnx_b, conv_w, conv_b, conv_ln_g, conv_ln_b, w_branch, w_o, norm_x_g,
                    norm_mem_g, w_xq, w_xkv, w_xo, norm_ffn_g, w_router, b_router, w_gu, b_gu, w_dn, b_dn)
    assert all(p.shape[0] == 1 for p in layer_params), "single-layer stack expected"
    bp, seq_len, D = x_prompt.shape
    bs = x_sample.shape[0]
    assert x_sample.shape[1] == seq_len
    n_seq = bp + bs
    x = jnp.concatenate([x_prompt, x_sample], axis=0).reshape(n_seq * seq_len, D)
    mem = jnp.concatenate([mem_prompt, mem_sample], axis=0).reshape(-1, D)
    y_p, y_s = _layer(x, mem, n_seq, seq_len, bp * seq_len, *[p[0] for p in layer_params], final_g)
    return y_p.reshape(bp, seq_len, D), y_s.reshape(bs, seq_len, D)
```

```python
import functools
import math

import jax
import jax.numpy as jnp
from jax import lax
from jax.experimental import pallas as pl
from jax.experimental.pallas import tpu as pltpu

F32 = jnp.float32
BF16 = jnp.bfloat16

RWKV_HEAD = 64
DECAY_SCALE = math.exp(-0.5)
LNX_EPS = RWKV_HEAD * 1e-5
RMS_EPS = 1e-5
LN_EPS = 1e-5
XATTN_HEADS = 4
TOP_K = 4
SWIGLU_ALPHA = 1.702
SWIGLU_LIMIT = 7.0

LANES = 128
SUBLANES = 8
WKV_CHUNK = 64
LORA_PAD = 128
MOE_SUBS_PER_TILE = 2
VMEM_LIMIT = 56 * 1024 * 1024


def _cparams(sem):
    return pltpu.CompilerParams(dimension_semantics=sem, vmem_limit_bytes=VMEM_LIMIT)


def _tile(n, prefs):
    for p in prefs:
        if n % p == 0:
            return p
    return n


def _dot(a, b):
    return jnp.dot(a, b, preferred_element_type=F32)


def _dot_nt(a, b):
    return lax.dot_general(a, b, (((1,), (1,)), ((), ())), preferred_element_type=F32)


def _dot_tn(a, b):
    return lax.dot_general(a, b, (((0,), (0,)), ((), ())), preferred_element_type=F32)


def _split2(x):
    hi = x.astype(BF16)
    lo = (x - hi.astype(F32)).astype(BF16)
    return hi, lo


def _dot3(x, w_hi, w_lo):
    xh, xl = _split2(x)
    return _dot(xh, w_hi) + _dot(xl, w_hi) + _dot(xh, w_lo)


def _rms(x, g):
    return x * lax.rsqrt(jnp.mean(x * x, axis=-1, keepdims=True) + RMS_EPS) * g


def _sigmoid(x):
    return 0.5 * jnp.tanh(0.5 * x) + 0.5


def _inproj_shift_kernel(x_ref, xp_ref, xn_ref, g_ref, w_ref, mu_ref, o_ref, h_scr, hp_scr, hn_scr,
                         *, seq_len):
    i = pl.program_id(0)
    j = pl.program_id(1)
    tm = x_ref.shape[0]

    @pl.when(j == 0)
    def _():
        g = g_ref[...]
        h_scr[...] = _rms(x_ref[...], g).astype(BF16)
        hp_scr[...] = _rms(xp_ref[...], g).astype(BF16)
        hn_scr[...] = _rms(xn_ref[...], g).astype(BF16)

    w = w_ref[...]
    p = _dot(h_scr[...], w)
    pp = _dot(hp_scr[...], w)[SUBLANES - 1:SUBLANES, :]
    pn = _dot(hn_scr[...], w)[0:1, :]
    first = (i * tm) % seq_len == 0
    last = ((i + 1) * tm) % seq_len == 0
    pp = jnp.where(first, 0.0, pp)
    pn = jnp.where(last, 0.0, pn)
    rid = lax.broadcasted_iota(jnp.int32, p.shape, 0)
    up = jnp.where(rid == 0, pp, pltpu.roll(p, 1, 0))
    dn = jnp.where(rid == tm - 1, pn, pltpu.roll(p, tm - 1, 0))
    o_ref[...] = (p + mu_ref[...] * (0.5 * (up + dn) - p)).astype(o_ref.dtype)


def _inproj_shift(x, g, w_bf16, mu, seq_len, tn, out_dtype):
    T, D = x.shape
    N = w_bf16.shape[1]
    tm = _tile(seq_len, (1024, 512, 256, 128, 64, 32, 16, 8))
    nb8 = T // SUBLANES
    r8 = tm // SUBLANES
    return pl.pallas_call(
        functools.partial(_inproj_shift_kernel, seq_len=seq_len),
        grid=(T // tm, N // tn),
        in_specs=[
            pl.BlockSpec((tm, D), lambda i, j: (i, 0)),
            pl.BlockSpec((SUBLANES, D), lambda i, j: (jnp.maximum(i * r8 - 1, 0), 0)),
            pl.BlockSpec((SUBLANES, D), lambda i, j: (jnp.minimum((i + 1) * r8, nb8 - 1), 0)),
            pl.BlockSpec((1, D), lambda i, j: (0, 0)),
            pl.BlockSpec((D, tn), lambda i, j: (0, j)),
            pl.BlockSpec((1, tn), lambda i, j: (0, j)),
        ],
        out_specs=pl.BlockSpec((tm, tn), lambda i, j: (i, j)),
        out_shape=jax.ShapeDtypeStruct((T, N), out_dtype),
        scratch_shapes=[pltpu.VMEM((tm, D), BF16), pltpu.VMEM((SUBLANES, D), BF16),
                        pltpu.VMEM((SUBLANES, D), BF16)],
        compiler_params=_cparams(("parallel", "arbitrary")),
        name="inproj_shift",
    )(x, x, x, g, w_bf16, mu)


def _inproj_glu_kernel(x_ref, g_ref, wa_ref, wb_ref, o_ref, h_scr):
    @pl.when(pl.program_id(1) == 0)
    def _():
        h_scr[...] = _rms(x_ref[...], g_ref[...]).astype(BF16)

    h = h_scr[...]
    a = _dot(h, wa_ref[...])
    b = _dot(h, wb_ref[...])
    o_ref[...] = (a * _sigmoid(b)).astype(o_ref.dtype)


def _inproj_glu(x, g, wa, wb):
    T, D = x.shape
    N = wa.shape[1]
    tm = _tile(T, (512, 256, 128, 64, 32, 16, 8))
    tn = _tile(N, (512, 256, 128))
    return pl.pallas_call(
        _inproj_glu_kernel,
        grid=(T // tm, N // tn),
        in_specs=[
            pl.BlockSpec((tm, D), lambda i, j: (i, 0)),
            pl.BlockSpec((1, D), lambda i, j: (0, 0)),
            pl.BlockSpec((D, tn), lambda i, j: (0, j)),
            pl.BlockSpec((D, tn), lambda i, j: (0, j)),
        ],
        out_specs=pl.BlockSpec((tm, tn), lambda i, j: (i, j)),
        out_shape=jax.ShapeDtypeStruct((T, N), BF16),
        scratch_shapes=[pltpu.VMEM((tm, D), BF16)],
        compiler_params=_cparams(("parallel", "arbitrary")),
        name="inproj_glu",
    )(x, g, wa, wb)


def _wkv_kernel(zr_ref, zk_ref, zv_ref, dd_ref, ad_ref, w0_ref, w2h_ref, w2l_ref, a0_ref, a2_ref,
                kk_ref, ka_ref, rk_ref, o_ref, bonus_ref, state_ref, *, n_pairs, bwd):
    L = WKV_CHUNK
    PW = 2 * RWKV_HEAD
    TT = zr_ref.shape[0]
    n_chunks = TT // L

    @pl.when(pl.program_id(2) == 0)
    def _():
        state_ref[...] = jnp.zeros_like(state_ref)

    row = lax.broadcasted_iota(jnp.int32, (L, 2 * L), 0)
    col = lax.broadcasted_iota(jnp.int32, (L, 2 * L), 1) % L
    strict = (col > row) if bwd else (col < row)
    incl = (col >= row) if bwd else (col <= row)
    ipk = jnp.where(col == row, 1.0, 0.0)
    lane = lax.broadcasted_iota(jnp.int32, (1, PW), 1)
    m0 = lane < RWKV_HEAD
    lane2 = lax.broadcasted_iota(jnp.int32, (1, 2 * PW), 1) % PW
    m0w = lane2 < RWKV_HEAD
    srow = lax.broadcasted_iota(jnp.int32, (PW, PW), 0)
    scol = lax.broadcasted_iota(jnp.int32, (PW, PW), 1)
    same_head = (srow // RWKV_HEAD) == (scol // RWKV_HEAD)
    eye = srow == scol
    HSW = min(zr_ref.shape[1], 2 * PW)
    hrow = lax.broadcasted_iota(jnp.int32, (HSW, HSW), 0)
    hcol = lax.broadcasted_iota(jnp.int32, (HSW, HSW), 1)
    head_ones_w = jnp.where((hrow // RWKV_HEAD) == (hcol // RWKV_HEAD), 1.0, 0.0).astype(BF16)

    def bd(x):
        return jnp.concatenate([jnp.where(m0, x, 0.0), jnp.where(m0, 0.0, x)], axis=0).astype(BF16)

    def bd2(x):
        return jnp.concatenate([jnp.where(m0w, x, 0.0), jnp.where(m0w, 0.0, x)], axis=0).astype(BF16)

    w0 = w0_ref[...]
    a0 = a0_ref[...]
    k_k = kk_ref[...]
    k_a = ka_ref[...]
    r_k = rk_ref[...]

    def head_sum(x, two_pass):
        n = x.shape[0]
        outs = []
        for c in range(x.shape[1] // HSW):
            xc = x[:, c * HSW:(c + 1) * HSW]
            if two_pass:
                hi, lo = _split2(xc)
                res = _dot(jnp.concatenate([hi, lo], axis=0), head_ones_w)
                outs.append(res[:n] + res[n:])
            else:
                outs.append(_dot(xc.astype(BF16), head_ones_w))
        return jnp.concatenate(outs, axis=1) if len(outs) > 1 else outs[0]

    r = zr_ref[...].astype(F32)
    k = zk_ref[...].astype(F32)
    v = zv_ref[...].astype(F32)
    dlin = w0 + _dot3(jnp.tanh(dd_ref[...]), w2h_ref[...], w2l_ref[...])
    lw = -DECAY_SCALE * _sigmoid(dlin)
    iclr = _sigmoid(a0 + _dot(ad_ref[...].astype(BF16), a2_ref[...]))
    kkr = k * k_k
    kmod = k * (1.0 + (iclr - 1.0) * k_a)
    cum = _chunk_cumsum(lw, bwd)
    e_in = jnp.exp(cum)
    e_ex = jnp.exp(cum - lw)
    e_ng = jnp.exp(-cum)
    kk = kkr * lax.rsqrt(jnp.maximum(head_sum(kkr * kkr, False), 1e-24))
    bonus_ref[...] = (head_sum(r * kmod * r_k, True) * v).astype(bonus_ref.dtype)
    b_all = kk * iclr
    ah_all = -kk * e_ex
    rh_all = r * e_in
    bh_all = b_all * e_ng
    kh_all = kmod * e_ng

    chunk_order = list(range(n_chunks - 1, -1, -1) if bwd else range(n_chunks))
    pair_vals = []
    for p in range(n_pairs):
        sl = slice(p * PW, (p + 1) * PW)
        pair_vals.append(dict(b=b_all[:, sl], ah=ah_all[:, sl], rh=rh_all[:, sl],
                              bh=bh_all[:, sl], kh=kh_all[:, sl]))

    probs = []
    for ci in chunk_order:
        for p in range(n_pairs):
            sl = slice(p * PW, (p + 1) * PW)
            rs = slice(ci * L, (ci + 1) * L)
            pv = pair_vals[p]
            q = dict(p=p, ci=ci, sl=sl, rs=rs, ah=pv["ah"][rs], rh=pv["rh"][rs], v=v[rs, sl])
            lhs = jnp.concatenate([q["ah"], q["rh"]], axis=0).astype(BF16)
            g = _dot_nt(lhs, jnp.concatenate([bd(pv["bh"][rs]), bd(pv["kh"][rs])], axis=0))
            a32 = jnp.where(strict, g[:L, :2 * L], 0.0)
            q["a"] = a32.astype(BF16)
            q["t"] = (ipk + a32).astype(BF16)
            q["rb"] = jnp.where(incl, g[L:, :2 * L], 0.0).astype(BF16)
            q["akrk"] = jnp.concatenate([jnp.where(strict, g[:L, 2 * L:], 0.0),
                                         jnp.where(incl, g[L:, 2 * L:], 0.0)], axis=0).astype(BF16)
            probs.append(q)
    for q in probs:
        q["kv"] = _dot(q["akrk"], bd(q["v"]))
    for q in probs:
        q["ai"] = _dot(q["a"], bd(q["a"])).astype(BF16)
    for lvl in range(1, 6):
        more = lvl < 5
        for q in probs:
            lhs = jnp.concatenate([q["t"], q["ai"]], axis=0) if more else q["t"]
            res = _dot(lhs, bd(q["ai"]))
            q["t"] = (q["t"].astype(F32) + res[:L]).astype(BF16)
            if more:
                q["ai"] = res[L:].astype(BF16)
    for q in probs:
        q["rbt"] = _dot(q["rb"], bd(q["t"])).astype(BF16)
    for q in probs:
        y0 = jnp.concatenate([q["ah"], q["kv"][:L]], axis=1)
        q["ry"] = _dot(jnp.concatenate([q["t"], q["rbt"]], axis=0), bd2(y0))
    for q in probs:
        p, rs, sl, ry = q["p"], q["rs"], q["sl"], q["ry"]
        end = q["ci"] * L if bwd else q["ci"] * L + L - 1
        tot = cum[end:end + 1, sl]
        e_rm = jnp.exp(tot - cum[rs, sl])
        q["rt"] = q["rh"] + ry[L:, :PW]
        q["ob"] = ry[L:, PW:] + q["kv"][L:]
        lhs_t = jnp.concatenate([pair_vals[p]["b"][rs] * e_rm, kmod[rs, sl] * e_rm], axis=0).astype(BF16)
        rhs_t = jnp.concatenate(
            [ry[:L], jnp.concatenate([jnp.zeros((L, PW), F32), q["v"]], axis=1)], axis=0).astype(BF16)
        mn = _dot_tn(lhs_t, rhs_t)
        q["mm"] = jnp.where(eye, jnp.exp(tot), 0.0) + jnp.where(same_head, mn[:, :PW], 0.0)
        q["nn"] = jnp.where(same_head, mn[:, PW:], 0.0)
    states = [state_ref[p] for p in range(n_pairs)]
    for q in probs:
        p = q["p"]
        res = _dot(jnp.concatenate([q["rt"], q["mm"]], axis=0).astype(BF16), states[p].astype(BF16))
        o_ref[q["rs"], q["sl"]] = (res[:L] + q["ob"]).astype(o_ref.dtype)
        states[p] = res[L:] + q["nn"]
    for p in range(n_pairs):
        state_ref[p] = states[p]


def _chunk_cumsum(x, bwd):
    L = WKV_CHUNK
    n = x.shape[0]
    rin = lax.broadcasted_iota(jnp.int32, (n, 1), 0) % L
    s = 1
    while s < L:
        if bwd:
            x = x + jnp.where(rin < L - s, pltpu.roll(x, n - s, 0), 0.0)
        else:
            x = x + jnp.where(rin >= s, pltpu.roll(x, s, 0), 0.0)
        s *= 2
    return x


def _wkv(z_rkv, z_lora, decay_w0, w2h, w2l, iclr_a0, a2_bf16, k_k, k_a, r_k, n_seq, seq_len, bwd):
    T = z_rkv.shape[0]
    W = z_rkv.shape[1] // 3
    PW = 2 * RWKV_HEAD
    n_pairs = _tile(W // PW, (4, 2, 1))
    GW = n_pairs * PW
    n_groups = W // GW
    TT = _tile(seq_len, (256, 128, 64))
    nT = seq_len // TT
    d = int(bwd)

    def tb(b, c):
        return b * nT + (nT - 1 - c if bwd else c)

    zspec = lambda off: pl.BlockSpec((TT, GW), lambda b, g, c: (tb(b, c), off * n_groups + g))
    pspec = pl.BlockSpec((None, 1, GW), lambda b, g, c: (d, 0, g))
    lspec = pl.BlockSpec((None, LORA_PAD, GW), lambda b, g, c: (d, 0, g))
    cspec = pl.BlockSpec((1, GW), lambda b, g, c: (0, g))
    ospec = pl.BlockSpec((TT, GW), lambda b, g, c: (tb(b, c), g))
    return pl.pallas_call(
        functools.partial(_wkv_kernel, n_pairs=n_pairs, bwd=bwd),
        grid=(n_seq, n_groups, nT),
        in_specs=[
            zspec(0), zspec(1), zspec(2),
            pl.BlockSpec((TT, LORA_PAD), lambda b, g, c: (tb(b, c), d)),
            pl.BlockSpec((TT, LORA_PAD), lambda b, g, c: (tb(b, c), 2 + d)),
            pspec, lspec, lspec, pspec, lspec, cspec, cspec, cspec,
        ],
        out_specs=[ospec, ospec],
        out_shape=[jax.ShapeDtypeStruct((T, W), BF16), jax.ShapeDtypeStruct((T, W), BF16)],
        scratch_shapes=[pltpu.VMEM((n_pairs, PW, PW), F32)],
        compiler_params=_cparams(("parallel", "parallel", "arbitrary")),
        name="wkv_scan_bwd" if bwd else "wkv_scan_fwd",
    )(z_rkv, z_rkv, z_rkv, z_lora, z_lora, decay_w0, w2h, w2l, iclr_a0, a2_bf16, k_k, k_a, r_k)


def _rwkv_post_kernel(of_ref, ob_ref, bf_ref, bb_ref, gd_ref, g2_ref, lg_ref, lb_ref, out_ref):
    W = out_ref.shape[1]
    PW = min(W, 4 * RWKV_HEAD)
    srow = lax.broadcasted_iota(jnp.int32, (PW, PW), 0)
    scol = lax.broadcasted_iota(jnp.int32, (PW, PW), 1)
    head_ones = jnp.where((srow // RWKV_HEAD) == (scol // RWKV_HEAD), 1.0, 0.0).astype(BF16)
    inv = 1.0 / RWKV_HEAD
    n = out_ref.shape[0]

    def head_mean(x):
        hi, lo = _split2(x)
        res = _dot(jnp.concatenate([hi, lo], axis=0), head_ones)
        return (res[:n] + res[n:]) * inv

    gate = _dot(_sigmoid(gd_ref[...]).astype(BF16), g2_ref[...])
    for p in range(W // PW):
        sl = slice(p * PW, (p + 1) * PW)
        o = of_ref[:, sl].astype(F32) + ob_ref[:, sl].astype(F32)
        oc = o - head_mean(o)
        var = head_mean(oc * oc)
        y = oc * lax.rsqrt(var + LNX_EPS) * lg_ref[:, sl] + lb_ref[:, sl]
        y = y + bf_ref[:, sl].astype(F32) + bb_ref[:, sl].astype(F32)
        out_ref[:, sl] = (y * gate[:, sl]).astype(out_ref.dtype)


def _rwkv_post(o_f, o_b, bonus_f, bonus_b, z_lora, g2, lnx_g, lnx_b):
    T, W = o_f.shape
    tm = _tile(T, (256, 128, 64, 32, 16, 8))
    GL = g2.shape[0]
    gd_blk = (4 * LORA_PAD) // GL
    tspec = pl.BlockSpec((tm, W), lambda i: (i, 0))
    return pl.pallas_call(
        _rwkv_post_kernel,
        grid=(T // tm,),
        in_specs=[
            tspec, tspec, tspec, tspec,
            pl.BlockSpec((tm, GL), lambda i: (i, gd_blk)),
            pl.BlockSpec((GL, W), lambda i: (0, 0)),
            pl.BlockSpec((1, W), lambda i: (0, 0)),
            pl.BlockSpec((1, W), lambda i: (0, 0)),
        ],
        out_specs=pl.BlockSpec((tm, W), lambda i: (i, 0)),
        out_shape=jax.ShapeDtypeStruct((T, W), BF16),
        compiler_params=_cparams(("parallel",)),
        name="rwkv_post",
    )(o_f, o_b, bonus_f, bonus_b, z_lora, g2, lnx_g, lnx_b)


def _conv_kernel(u_ref, up_ref, un_ref, w_ref, b_ref, lg_ref, lb_ref, o_ref, ext_scr, *, seq_len, halo):
    i = pl.program_id(0)
    tm = u_ref.shape[0]
    K = w_ref.shape[0]
    first = (i * tm) % seq_len == 0
    last = ((i + 1) * tm) % seq_len == 0
    n_ext = tm + 2 * halo
    ext = jnp.concatenate([jnp.where(first, 0.0, up_ref[...].astype(F32)), u_ref[...].astype(F32),
                           jnp.where(last, 0.0, un_ref[...].astype(F32))], axis=0)
    ext_scr[0] = ext
    for r in range(1, SUBLANES):
        ext_scr[r] = pltpu.roll(ext, n_ext - r, 0)
    sub = min(tm, 32)
    base = halo - K // 2
    for s in range(tm // sub):
        acc = jnp.zeros((sub, u_ref.shape[1]), F32)
        for j in range(K):
            off = base + j
            row0 = s * sub + (off // SUBLANES) * SUBLANES
            acc = acc + w_ref[j:j + 1, :] * ext_scr[off % SUBLANES, row0:row0 + sub, :]
        acc = acc + b_ref[...]
        mean = jnp.mean(acc, axis=-1, keepdims=True)
        xc = acc - mean
        var = jnp.mean(xc * xc, axis=-1, keepdims=True)
        y = xc * lax.rsqrt(var + LN_EPS) * lg_ref[...] + lb_ref[...]
        o_ref[s * sub:(s + 1) * sub, :] = (y * _sigmoid(y)).astype(o_ref.dtype)


def _conv(u, conv_w, conv_b, ln_g, ln_b, seq_len):
    T, C = u.shape
    K = conv_w.shape[0]
    halo = 16
    assert K // 2 <= halo
    tm = _tile(seq_len, (128, 64, 32, 16))
    rh = tm // halo
    nbh = T // halo
    return pl.pallas_call(
        functools.partial(_conv_kernel, seq_len=seq_len, halo=halo),
        grid=(T // tm,),
        in_specs=[
            pl.BlockSpec((tm, C), lambda i: (i, 0)),
            pl.BlockSpec((halo, C), lambda i: (jnp.maximum(i * rh - 1, 0), 0)),
            pl.BlockSpec((halo, C), lambda i: (jnp.minimum((i + 1) * rh, nbh - 1), 0)),
            pl.BlockSpec((K, C), lambda i: (0, 0)),
            pl.BlockSpec((1, C), lambda i: (0, 0)),
            pl.BlockSpec((1, C), lambda i: (0, 0)),
            pl.BlockSpec((1, C), lambda i: (0, 0)),
        ],
        out_specs=pl.BlockSpec((tm, C), lambda i: (i, 0)),
        out_shape=jax.ShapeDtypeStruct((T, C), BF16),
        scratch_shapes=[pltpu.VMEM((SUBLANES, tm + 2 * halo, C), F32)],
        compiler_params=_cparams(("parallel",)),
        name="conformer_conv",
    )(u, u, u, conv_w, conv_b, ln_g, ln_b)


def _merge_kernel(x_ref, g_ref, orw_ref, ocv_ref, wg1_ref, wg2_ref, wb1_ref, wb2_ref, o_ref, h_scr):
    @pl.when(pl.program_id(1) == 0)
    def _():
        h_scr[...] = _rms(x_ref[...], g_ref[...]).astype(BF16)

    h = h_scr[...]
    g1 = _sigmoid(_dot(h, wg1_ref[...]))
    g2 = _sigmoid(_dot(h, wg2_ref[...]))
    y1 = _dot(orw_ref[...], wb1_ref[...])
    y2 = _dot(ocv_ref[...], wb2_ref[...])
    o_ref[...] = (g1 * y1 + g2 * y2).astype(o_ref.dtype)


def _merge(x, g, o_rwkv, o_conv, wg1, wg2, wb1, wb2):
    T, D = x.shape
    W = o_rwkv.shape[1]
    C = o_conv.shape[1]
    tm = _tile(T, (512, 256, 128, 64, 32, 16, 8))
    tn = _tile(D, (512, 256, 128))
    return pl.pallas_call(
        _merge_kernel,
        grid=(T // tm, D // tn),
        in_specs=[
            pl.BlockSpec((tm, D), lambda i, j: (i, 0)),
            pl.BlockSpec((1, D), lambda i, j: (0, 0)),
            pl.BlockSpec((tm, W), lambda i, j: (i, 0)),
            pl.BlockSpec((tm, C), lambda i, j: (i, 0)),
            pl.BlockSpec((D, tn), lambda i, j: (0, j)),
            pl.BlockSpec((D, tn), lambda i, j: (0, j)),
            pl.BlockSpec((W, tn), lambda i, j: (0, j)),
            pl.BlockSpec((C, tn), lambda i, j: (0, j)),
        ],
        out_specs=pl.BlockSpec((tm, tn), lambda i, j: (i, j)),
        out_shape=jax.ShapeDtypeStruct((T, D), BF16),
        scratch_shapes=[pltpu.VMEM((tm, D), BF16)],
        compiler_params=_cparams(("parallel", "arbitrary")),
        name="merge_gates",
    )(x, g, o_rwkv, o_conv, wg1, wg2, wb1, wb2)


def _mm_res_kernel(a_ref, w_ref, res_ref, o_ref):
    o_ref[...] = res_ref[...] + _dot(a_ref[...], w_ref[...])


def _mm_res(a, w, res):
    T, K = a.shape
    N = w.shape[1]
    tm = _tile(T, (1024, 512, 256, 128, 64, 32, 16, 8))
    tn = _tile(N, (1024, 512, 256, 128))
    return pl.pallas_call(
        _mm_res_kernel,
        grid=(T // tm, N // tn),
        in_specs=[
            pl.BlockSpec((tm, K), lambda i, j: (i, 0)),
            pl.BlockSpec((K, tn), lambda i, j: (0, j)),
            pl.BlockSpec((tm, tn), lambda i, j: (i, j)),
        ],
        out_specs=pl.BlockSpec((tm, tn), lambda i, j: (i, j)),
        out_shape=jax.ShapeDtypeStruct((T, N), F32),
        compiler_params=_cparams(("parallel", "parallel")),
        name="proj_residual",
    )(a, w, res)


def _mm_norm_kernel(x_ref, g_ref, w_ref, o_ref, h_scr):
    @pl.when(pl.program_id(1) == 0)
    def _():
        h_scr[...] = _rms(x_ref[...], g_ref[...]).astype(BF16)

    o_ref[...] = _dot(h_scr[...], w_ref[...]).astype(o_ref.dtype)


def _mm_norm(x, g, w):
    T, D = x.shape
    N = w.shape[1]
    tm = _tile(T, (512, 256, 128, 64, 32, 16, 8))
    tn = _tile(N, (1024, 512, 256, 128))
    return pl.pallas_call(
        _mm_norm_kernel,
        grid=(T // tm, N // tn),
        in_specs=[
            pl.BlockSpec((tm, D), lambda i, j: (i, 0)),
            pl.BlockSpec((1, D), lambda i, j: (0, 0)),
            pl.BlockSpec((D, tn), lambda i, j: (0, j)),
        ],
        out_specs=pl.BlockSpec((tm, tn), lambda i, j: (i, j)),
        out_shape=jax.ShapeDtypeStruct((T, N), BF16),
        scratch_shapes=[pltpu.VMEM((tm, D), BF16)],
        compiler_params=_cparams(("parallel", "arbitrary")),
        name="norm_proj",
    )(x, g, w)


def _xattn_kernel(q_ref, k_ref, v_ref, o_ref):
    D = q_ref.shape[1]
    hd = D // XATTN_HEADS
    scale = hd ** -0.5
    for h in range(XATTN_HEADS):
        sl = slice(h * hd, (h + 1) * hd)
        s = _dot_nt(q_ref[:, sl], k_ref[:, sl]) * scale
        s = s - jnp.max(s, axis=-1, keepdims=True)
        e = jnp.exp(s)
        p = e / jnp.sum(e, axis=-1, keepdims=True)
        o_ref[:, sl] = _dot(p.astype(BF16), v_ref[:, sl]).astype(o_ref.dtype)


def _xattn(q, kv, n_seq, seq_len, n_mem):
    T, D = q.shape
    tm = _tile(seq_len, (512, 256, 128, 64, 32, 16, 8))
    nT = seq_len // tm
    return pl.pallas_call(
        _xattn_kernel,
        grid=(n_seq, nT),
        in_specs=[
            pl.BlockSpec((tm, D), lambda b, i: (b * nT + i, 0)),
            pl.BlockSpec((n_mem, D), lambda b, i: (b, 0)),
            pl.BlockSpec((n_mem, D), lambda b, i: (b, 1)),
        ],
        out_specs=pl.BlockSpec((tm, D), lambda b, i: (b * nT + i, 0)),
        out_shape=jax.ShapeDtypeStruct((T, D), BF16),
        compiler_params=_cparams(("parallel", "parallel")),
        name="cross_attention",
    )(q, kv, kv)


def _router_kernel(x_ref, g_ref, wh_ref, wl_ref, b_ref, hf_ref, e_ref, gate_ref):
    hf = _rms(x_ref[...], g_ref[...])
    hf_ref[...] = hf.astype(hf_ref.dtype)
    logits = _dot3(hf, wh_ref[...], wl_ref[...]) + b_ref[...]
    E = logits.shape[1]
    eid = lax.broadcasted_iota(jnp.int32, logits.shape, 1).astype(F32)
    work = logits
    vals = []
    idxs = []
    for _ in range(TOP_K):
        m = jnp.max(work, axis=-1, keepdims=True)
        idx = jnp.min(jnp.where(work == m, eid, float(E)), axis=-1, keepdims=True)
        vals.append(m)
        idxs.append(idx.astype(jnp.int32))
        work = jnp.where(eid == idx, -jnp.inf, work)
    ex = [jnp.exp(vv - vals[0]) for vv in vals]
    den = ex[0]
    for t in ex[1:]:
        den = den + t
    kid = lax.broadcasted_iota(jnp.int32, (logits.shape[0], TOP_K), 1)
    e_out = jnp.zeros((logits.shape[0], TOP_K), jnp.int32)
    g_out = jnp.zeros((logits.shape[0], TOP_K), F32)
    for t in range(TOP_K):
        e_out = jnp.where(kid == t, idxs[t], e_out)
        g_out = jnp.where(kid == t, ex[t] / den, g_out)
    e_ref[...] = e_out
    gate_ref[...] = g_out


def _router(x, g, w_router, b_router):
    T, D = x.shape
    E = w_router.shape[1]
    wh, wl = _split2(w_router)
    tm = _tile(T, (256, 128, 64, 32, 16, 8))
    return pl.pallas_call(
        _router_kernel,
        grid=(T // tm,),
        in_specs=[
            pl.BlockSpec((tm, D), lambda i: (i, 0)),
            pl.BlockSpec((1, D), lambda i: (0, 0)),
            pl.BlockSpec((D, E), lambda i: (0, 0)),
            pl.BlockSpec((D, E), lambda i: (0, 0)),
            pl.BlockSpec((1, E), lambda i: (0, 0)),
        ],
        out_specs=[
            pl.BlockSpec((tm, D), lambda i: (i, 0)),
            pl.BlockSpec((tm, TOP_K), lambda i: (i, 0)),
            pl.BlockSpec((tm, TOP_K), lambda i: (i, 0)),
        ],
        out_shape=[jax.ShapeDtypeStruct((T, D), BF16),
                   jax.ShapeDtypeStruct((T, TOP_K), jnp.int32),
                   jax.ShapeDtypeStruct((T, TOP_K), F32)],
        compiler_params=_cparams(("parallel",)),
        name="router",
    )(x, g, wh, wl, b_router)


def _expert_kernel(te_ref, tv_ref, x_ref, wg_ref, wu_ref, bg_ref, bu_ref, wd_ref, bd_ref, o_ref, act_ref,
                   *, nf, sub):
    i = pl.program_id(0)
    s = pl.program_id(1)
    tf = wg_ref.shape[1]
    n_valid = tv_ref[i]
    for r0 in range(0, x_ref.shape[0], sub):
        rows = slice(r0, r0 + sub)
        live = n_valid > r0

        @pl.when(live & (s < nf))
        def _(rows=rows):
            x = x_ref[rows, :]
            g = _dot(x, wg_ref[...].astype(BF16)) + bg_ref[...]
            u = _dot(x, wu_ref[...].astype(BF16)) + bu_ref[...]
            g = jnp.minimum(g, SWIGLU_LIMIT)
            u = jnp.clip(u, -SWIGLU_LIMIT, SWIGLU_LIMIT)
            act = ((u + 1.0) * (g * _sigmoid(SWIGLU_ALPHA * g))).astype(BF16)
            for f in range(nf):
                @pl.when(s == f)
                def _(f=f):
                    act_ref[rows, f * tf:(f + 1) * tf] = act

        @pl.when(live & (s >= nf))
        def _(rows=rows):
            y = _dot(act_ref[rows, :], wd_ref[...].astype(BF16)) + bd_ref[...]
            o_ref[rows, :] = y.astype(o_ref.dtype)

        @pl.when(jnp.logical_not(live) & (s >= nf))
        def _(rows=rows):
            o_ref[rows, :] = jnp.zeros((sub, o_ref.shape[1]), o_ref.dtype)


def _experts(xb, tile_e, tile_valid, w_gu, b_gu, w_dn, b_dn, tm, sub):
    R, D = xb.shape
    E, _, F2 = w_gu.shape
    F = F2 // 2
    tf = _tile(F, (512, 256, 128))
    tn = _tile(D, (512, 256, 128))
    nf = F // tf
    nn = D // tn
    n_tiles = R // tm
    b_gu3 = b_gu.reshape(E, 1, F2)
    b_dn3 = b_dn.reshape(E, 1, D)

    def fa(i, s, tv):
        return jnp.where(tv[i] > 0, jnp.minimum(s, nf - 1), nf - 1)

    def nb(i, s, tv):
        return jnp.where(tv[i] > 0, jnp.clip(s - nf, 0, nn - 1), nn - 1)

    grid_spec = pltpu.PrefetchScalarGridSpec(
        num_scalar_prefetch=2,
        grid=(n_tiles, nf + nn),
        in_specs=[
            pl.BlockSpec((tm, D), lambda i, s, te, tv: (i, 0)),
            pl.BlockSpec((None, D, tf), lambda i, s, te, tv: (te[i], 0, fa(i, s, tv))),
            pl.BlockSpec((None, D, tf), lambda i, s, te, tv: (te[i], 0, nf + fa(i, s, tv))),
            pl.BlockSpec((None, 1, tf), lambda i, s, te, tv: (te[i], 0, fa(i, s, tv))),
            pl.BlockSpec((None, 1, tf), lambda i, s, te, tv: (te[i], 0, nf + fa(i, s, tv))),
            pl.BlockSpec((None, F, tn), lambda i, s, te, tv: (te[i], 0, nb(i, s, tv))),
            pl.BlockSpec((None, 1, tn), lambda i, s, te, tv: (te[i], 0, nb(i, s, tv))),
        ],
        out_specs=pl.BlockSpec((tm, tn), lambda i, s, te, tv: (i, jnp.clip(s - nf, 0, nn - 1))),
        scratch_shapes=[pltpu.VMEM((tm, F), BF16)],
    )
    return pl.pallas_call(
        functools.partial(_expert_kernel, nf=nf, sub=sub),
        grid_spec=grid_spec,
        out_shape=jax.ShapeDtypeStruct((R, D), BF16),
        compiler_params=_cparams(("arbitrary", "arbitrary")),
        name="moe_experts",
    )(tile_e, tile_valid, xb, w_gu, w_gu, b_gu3, b_gu3, w_dn, b_dn3)


def _combine_kernel(x_ref, y_ref, gate_ref, g_ref, o_head_ref, o_tail_ref, *, n_head_tiles):
    acc = x_ref[...]
    gate = gate_ref[...]
    for t in range(TOP_K):
        acc = acc + gate[:, t:t + 1] * y_ref[t].astype(F32)
    y = _rms(acc, g_ref[...])
    i = pl.program_id(0)

    @pl.when(i < n_head_tiles)
    def _():
        o_head_ref[...] = y

    @pl.when(i >= n_head_tiles)
    def _():
        o_tail_ref[...] = y


def _combine(x, y4, gate, final_g, t_head):
    T, D = x.shape
    tm = _tile(math.gcd(t_head, T - t_head), (256, 128, 64, 32, 16, 8))
    nh = t_head // tm
    return pl.pallas_call(
        functools.partial(_combine_kernel, n_head_tiles=nh),
        grid=(T // tm,),
        in_specs=[
            pl.BlockSpec((tm, D), lambda i: (i, 0)),
            pl.BlockSpec((TOP_K, tm, D), lambda i: (0, i, 0)),
            pl.BlockSpec((tm, TOP_K), lambda i: (i, 0)),
            pl.BlockSpec((1, D), lambda i: (0, 0)),
        ],
        out_specs=[pl.BlockSpec((tm, D), lambda i: (jnp.minimum(i, nh - 1), 0)),
                   pl.BlockSpec((tm, D), lambda i: (jnp.maximum(i - nh, 0), 0))],
        out_shape=[jax.ShapeDtypeStruct((t_head, D), F32), jax.ShapeDtypeStruct((T - t_head, D), F32)],
        compiler_params=_cparams(("arbitrary",)),
        name="moe_combine_norm",
    )(x, y4, gate, final_g)


def _pad_rows(w, n):
    return jnp.pad(w, ((0, 0),) * (w.ndim - 2) + ((0, n - w.shape[-2]), (0, 0)))


def _pad_cols(w, n):
    return jnp.pad(w, ((0, 0),) * (w.ndim - 1) + ((0, n - w.shape[-1]),))


def _moe_tile_rows(n_assign, n_experts):
    for tm in (1024, 512, 256, 128, 64, 32, 16, 8):
        if n_assign >= 4 * n_experts * tm or tm == 8:
            return tm


def _layer(x, mem, n_seq, seq_len, t_head, norm_mix_g, w_in, shift_mu, decay_w0, decay_w2, iclr_a0, iclr_a2,
           gate_g2, k_k, k_a, r_k, lnx_g, lnx_b, conv_w, conv_b, conv_ln_g, conv_ln_b, w_branch, w_o,
           norm_x_g, norm_mem_g, w_xq, w_xkv, w_xo, norm_ffn_g, w_router, b_router, w_gu, b_gu,
           w_dn, b_dn, final_g):
    T, D = x.shape
    W = k_k.shape[0]
    DL = decay_w2.shape[1]
    AL = iclr_a2.shape[1]
    GL = gate_g2.shape[0]
    C = conv_w.shape[1]
    E = w_router.shape[1]
    n_mem = mem.shape[0] // n_seq
    row = lambda v: v.reshape(1, -1)

    o3 = 3 * W
    o4 = o3 + 2 * DL
    o5 = o4 + 2 * AL
    o6 = o5 + GL
    seg = lambda m, a, b, n: _pad_cols(m[..., a:b], n)
    lora_cols = lambda m: jnp.concatenate(
        [seg(m, o3, o3 + DL, LORA_PAD), seg(m, o3 + DL, o4, LORA_PAD), seg(m, o4, o4 + AL, LORA_PAD),
         seg(m, o4 + AL, o5, LORA_PAD), m[..., o5:o6]], axis=-1)
    w_rkv = w_in[:, :o3].astype(BF16)
    w_lora = lora_cols(w_in).astype(BF16)
    mu_rkv = row(shift_mu[:o3])
    mu_lora = row(lora_cols(shift_mu))
    w_ca = w_in[:, o6:o6 + C].astype(BF16)
    w_cb = w_in[:, o6 + C:o6 + 2 * C].astype(BF16)
    w_g1 = w_in[:, o6 + 2 * C:o6 + 2 * C + D].astype(BF16)
    w_g2 = w_in[:, o6 + 2 * C + D:].astype(BF16)
    g_mix = row(norm_mix_g)

    z_rkv = _inproj_shift(x, g_mix, w_rkv, mu_rkv, seq_len, _tile(o3, (1024, 512, 256, 128)), BF16)
    z_lora = _inproj_shift(x, g_mix, w_lora, mu_lora, seq_len, w_lora.shape[1], F32)
    u = _inproj_glu(x, g_mix, w_ca, w_cb)

    w2h, w2l = _split2(_pad_rows(decay_w2, LORA_PAD))
    a2_bf16 = _pad_rows(iclr_a2, LORA_PAD).astype(BF16)
    wkv_args = (z_rkv, z_lora, decay_w0.reshape(2, 1, W), w2h, w2l, iclr_a0.reshape(2, 1, W), a2_bf16,
                row(k_k), row(k_a), row(r_k), n_seq, seq_len)
    o_f, bonus_f = _wkv(*wkv_args, bwd=False)
    o_b, bonus_b = _wkv(*wkv_args, bwd=True)
    o_rwkv = _rwkv_post(o_f, o_b, bonus_f, bonus_b, z_lora, gate_g2.astype(BF16), row(lnx_g), row(lnx_b))
    o_conv = _conv(u, conv_w, row(conv_b), row(conv_ln_g), row(conv_ln_b), seq_len)

    merged = _merge(x, g_mix, o_rwkv, o_conv, w_g1, w_g2, w_branch[:W].astype(BF16),
                    w_branch[W:].astype(BF16))
    x1 = _mm_res(merged, w_o.astype(BF16), x)

    q = _mm_norm(x1, row(norm_x_g), w_xq.astype(BF16))
    kv = _mm_norm(mem, row(norm_mem_g), w_xkv.astype(BF16))
    att = _xattn(q, kv, n_seq, seq_len, n_mem)
    x2 = _mm_res(att, w_xo.astype(BF16), x1)

    hf, top_e, gate = _router(x2, row(norm_ffn_g), w_router, row(b_router))

    A = T * TOP_K
    tm_e = _moe_tile_rows(A, E)
    sub_e = max(tm_e // MOE_SUBS_PER_TILE, SUBLANES)
    n_tiles = (A + E * (tm_e - 1) + tm_e - 1) // tm_e
    n_rows = n_tiles * tm_e
    flat_e = top_e.reshape(A)
    iota_a = jnp.arange(A, dtype=jnp.int32)
    sorted_e, order = lax.sort((flat_e, iota_a), num_keys=1)
    _, rank = lax.sort((order, iota_a), num_keys=1)
    experts = jnp.arange(E, dtype=jnp.int32)
    start = jnp.searchsorted(sorted_e, experts, side='left').astype(jnp.int32)
    counts = jnp.searchsorted(sorted_e, experts, side='right').astype(jnp.int32) - start
    padded = (counts + tm_e - 1) // tm_e * tm_e
    pad_end = jnp.cumsum(padded)
    pad_start = pad_end - padded
    dest_of = pad_start[flat_e] + rank - start[flat_e]
    tile_start = jnp.arange(n_tiles, dtype=jnp.int32) * tm_e
    tile_e = jnp.minimum(jnp.searchsorted(pad_end, tile_start, side='right'), E - 1).astype(jnp.int32)
    tile_valid = jnp.clip(pad_start[tile_e] + counts[tile_e] - tile_start, 0, tm_e).astype(jnp.int32)
    rows = jnp.arange(n_rows, dtype=jnp.int32)
    row_e = jnp.repeat(tile_e, tm_e)
    row_off = rows - pad_start[row_e]
    src = order[jnp.minimum(start[row_e] + row_off, A - 1)] // TOP_K
    row_tok = jnp.where(row_off < counts[row_e], src, rows % T)

    xb = hf.at[row_tok].get(mode='promise_in_bounds')
    yb = _experts(xb, tile_e, tile_valid, w_gu, b_gu, w_dn, b_dn, tm_e, sub_e)
    dest_slot_major = dest_of.reshape(T, TOP_K).T.reshape(A)
    y4 = yb.at[dest_slot_major].get(mode='promise_in_bounds').reshape(TOP_K, T, D)
    return _combine(x2, y4, gate, row(final_g), t_head)


def kernel(x_prompt, x_sample, mem_prompt, mem_sample, norm_mix_g, w_in, shift_mu, decay_w0, decay_w2,
           iclr_a0, iclr_a2, gate_g2, k_k, k_a, r_k, lnx_g, lnx_b, conv_w, conv_b, conv_ln_g, conv_ln_b,
           w_branch, w_o, norm_x_g, norm_mem_g, w_xq, w_xkv, w_xo, norm_ffn_g, w_router, b_router,
           w_gu, b_gu, w_dn, b_dn, final_g):
    layer_params = (norm_mix_g, w_in, shift_mu, decay_w0, decay_w2, iclr_a0, iclr_a2, gate_g2, k_k, k_a,
                    r_k, lnx_g, lnx_b, conv_w, conv_b, conv_ln_g, conv_ln_b, w_branch, w_o, norm_x_g,
                    norm_mem_g, w_xq, w_xkv, w_xo, norm_ffn_g, w_router, b_router, w_gu, b_gu, w_dn, b_dn)
    assert all(p.shape[0] == 1 for p in layer_params), "single-layer stack expected"
    bp, seq_len, D = x_prompt.shape
    bs = x_sample.shape[0]
    assert x_sample.shape[1] == seq_len
    n_seq = bp + bs
    x = jnp.concatenate([x_prompt, x_sample], axis=0).reshape(n_seq * seq_len, D)
    mem = jnp.concatenate([mem_prompt, mem_sample], axis=0).reshape(-1, D)
    y_p, y_s = _layer(x, mem, n_seq, seq_len, bp * seq_len, *[p[0] for p in layer_params], final_g)
    return y_p.reshape(bp, seq_len, D), y_s.reshape(bs, seq_len, D)
```

```python
import functools
import math

import jax
import jax.numpy as jnp
from jax import lax
from jax.experimental import pallas as pl
from jax.experimental.pallas import tpu as pltpu

F32 = jnp.float32
BF16 = jnp.bfloat16

RWKV_HEAD = 64
DECAY_SCALE = math.exp(-0.5)
LNX_EPS = RWKV_HEAD * 1e-5
RMS_EPS = 1e-5
LN_EPS = 1e-5
XATTN_HEADS = 4
TOP_K = 4
SWIGLU_ALPHA = 1.702
SWIGLU_LIMIT = 7.0

LANES = 128
SUBLANES = 8
WKV_CHUNK = 64
LORA_PAD = 128
VMEM_LIMIT = 56 * 1024 * 1024


def _cparams(sem):
    return pltpu.CompilerParams(dimension_semantics=sem, vmem_limit_bytes=VMEM_LIMIT)


def _tile(n, prefs):
    for p in prefs:
        if n % p == 0:
            return p
    return n


def _dot(a, b):
    return jnp.dot(a, b, preferred_element_type=F32)


def _dot_nt(a, b):
    return lax.dot_general(a, b, (((1,), (1,)), ((), ())), preferred_element_type=F32)


def _dot_tn(a, b):
    return lax.dot_general(a, b, (((0,), (0,)), ((), ())), preferred_element_type=F32)


def _split2(x):
    hi = x.astype(BF16)
    lo = (x - hi.astype(F32)).astype(BF16)
    return hi, lo


def _dot3(x, w_hi, w_lo):
    xh, xl = _split2(x)
    return _dot(xh, w_hi) + _dot(xl, w_hi) + _dot(xh, w_lo)


def _rms(x, g):
    return x * lax.rsqrt(jnp.mean(x * x, axis=-1, keepdims=True) + RMS_EPS) * g


def _sigmoid(x):
    return 0.5 * jnp.tanh(0.5 * x) + 0.5


def _inproj_shift_kernel(x_ref, xp_ref, xn_ref, g_ref, w_ref, mu_ref, o_ref, h_scr, hp_scr, hn_scr,
                         *, seq_len):
    i = pl.program_id(0)
    j = pl.program_id(1)
    tm = x_ref.shape[0]

    @pl.when(j == 0)
    def _():
        g = g_ref[...]
        h_scr[...] = _rms(x_ref[...], g).astype(BF16)
        hp_scr[...] = _rms(xp_ref[...], g).astype(BF16)
        hn_scr[...] = _rms(xn_ref[...], g).astype(BF16)

    w = w_ref[...]
    p = _dot(h_scr[...], w)
    pp = _dot(hp_scr[...], w)[SUBLANES - 1:SUBLANES, :]
    pn = _dot(hn_scr[...], w)[0:1, :]
    first = (i * tm) % seq_len == 0
    last = ((i + 1) * tm) % seq_len == 0
    pp = jnp.where(first, 0.0, pp)
    pn = jnp.where(last, 0.0, pn)
    rid = lax.broadcasted_iota(jnp.int32, p.shape, 0)
    up = jnp.where(rid == 0, pp, pltpu.roll(p, 1, 0))
    dn = jnp.where(rid == tm - 1, pn, pltpu.roll(p, tm - 1, 0))
    o_ref[...] = (p + mu_ref[...] * (0.5 * (up + dn) - p)).astype(o_ref.dtype)


def _inproj_shift(x, g, w_bf16, mu, seq_len, tn, out_dtype):
    T, D = x.shape
    N = w_bf16.shape[1]
    tm = _tile(seq_len, (1024, 512, 256, 128, 64, 32, 16, 8))
    nb8 = T // SUBLANES
    r8 = tm // SUBLANES
    return pl.pallas_call(
        functools.partial(_inproj_shift_kernel, seq_len=seq_len),
        grid=(T // tm, N // tn),
        in_specs=[
            pl.BlockSpec((tm, D), lambda i, j: (i, 0)),
            pl.BlockSpec((SUBLANES, D), lambda i, j: (jnp.maximum(i * r8 - 1, 0), 0)),
            pl.BlockSpec((SUBLANES, D), lambda i, j: (jnp.minimum((i + 1) * r8, nb8 - 1), 0)),
            pl.BlockSpec((1, D), lambda i, j: (0, 0)),
            pl.BlockSpec((D, tn), lambda i, j: (0, j)),
            pl.BlockSpec((1, tn), lambda i, j: (0, j)),
        ],
        out_specs=pl.BlockSpec((tm, tn), lambda i, j: (i, j)),
        out_shape=jax.ShapeDtypeStruct((T, N), out_dtype),
        scratch_shapes=[pltpu.VMEM((tm, D), BF16), pltpu.VMEM((SUBLANES, D), BF16),
                        pltpu.VMEM((SUBLANES, D), BF16)],
        compiler_params=_cparams(("parallel", "arbitrary")),
        name="inproj_shift",
    )(x, x, x, g, w_bf16, mu)


def _inproj_glu_kernel(x_ref, g_ref, wa_ref, wb_ref, o_ref, h_scr):
    @pl.when(pl.program_id(1) == 0)
    def _():
        h_scr[...] = _rms(x_ref[...], g_ref[...]).astype(BF16)

    h = h_scr[...]
    a = _dot(h, wa_ref[...])
    b = _dot(h, wb_ref[...])
    o_ref[...] = (a * _sigmoid(b)).astype(o_ref.dtype)


def _inproj_glu(x, g, wa, wb):
    T, D = x.shape
    N = wa.shape[1]
    tm = _tile(T, (512, 256, 128, 64, 32, 16, 8))
    tn = _tile(N, (512, 256, 128))
    return pl.pallas_call(
        _inproj_glu_kernel,
        grid=(T // tm, N // tn),
        in_specs=[
            pl.BlockSpec((tm, D), lambda i, j: (i, 0)),
            pl.BlockSpec((1, D), lambda i, j: (0, 0)),
            pl.BlockSpec((D, tn), lambda i, j: (0, j)),
            pl.BlockSpec((D, tn), lambda i, j: (0, j)),
        ],
        out_specs=pl.BlockSpec((tm, tn), lambda i, j: (i, j)),
        out_shape=jax.ShapeDtypeStruct((T, N), BF16),
        scratch_shapes=[pltpu.VMEM((tm, D), BF16)],
        compiler_params=_cparams(("parallel", "arbitrary")),
        name="inproj_glu",
    )(x, g, wa, wb)


def _wkv_kernel(zr_ref, zk_ref, zv_ref, dd_ref, ad_ref, w0_ref, w2h_ref, w2l_ref, a0_ref, a2_ref,
                kk_ref, ka_ref, rk_ref, o_ref, bonus_ref, state_ref, *, n_pairs, bwd):
    L = WKV_CHUNK
    PW = 2 * RWKV_HEAD
    TT = zr_ref.shape[0]
    n_chunks = TT // L

    @pl.when(pl.program_id(2) == 0)
    def _():
        state_ref[...] = jnp.zeros_like(state_ref)

    row = lax.broadcasted_iota(jnp.int32, (L, 2 * L), 0)
    col = lax.broadcasted_iota(jnp.int32, (L, 2 * L), 1) % L
    strict = (col > row) if bwd else (col < row)
    incl = (col >= row) if bwd else (col <= row)
    ipk = jnp.where(col == row, 1.0, 0.0)
    lane = lax.broadcasted_iota(jnp.int32, (1, PW), 1)
    m0 = lane < RWKV_HEAD
    lane2 = lax.broadcasted_iota(jnp.int32, (1, 2 * PW), 1) % PW
    m0w = lane2 < RWKV_HEAD
    srow = lax.broadcasted_iota(jnp.int32, (PW, PW), 0)
    scol = lax.broadcasted_iota(jnp.int32, (PW, PW), 1)
    same_head = (srow // RWKV_HEAD) == (scol // RWKV_HEAD)
    eye = srow == scol
    HSW = min(zr_ref.shape[1], 2 * PW)
    hrow = lax.broadcasted_iota(jnp.int32, (HSW, HSW), 0)
    hcol = lax.broadcasted_iota(jnp.int32, (HSW, HSW), 1)
    head_ones_w = jnp.where((hrow // RWKV_HEAD) == (hcol // RWKV_HEAD), 1.0, 0.0).astype(BF16)

    def bd(x):
        return jnp.concatenate([jnp.where(m0, x, 0.0), jnp.where(m0, 0.0, x)], axis=0).astype(BF16)

    def bd2(x):
        return jnp.concatenate([jnp.where(m0w, x, 0.0), jnp.where(m0w, 0.0, x)], axis=0).astype(BF16)

    w0 = w0_ref[...]
    a0 = a0_ref[...]
    k_k = kk_ref[...]
    k_a = ka_ref[...]
    r_k = rk_ref[...]

    def head_sum(x, two_pass):
        n = x.shape[0]
        outs = []
        for c in range(x.shape[1] // HSW):
            xc = x[:, c * HSW:(c + 1) * HSW]
            if two_pass:
                hi, lo = _split2(xc)
                res = _dot(jnp.concatenate([hi, lo], axis=0), head_ones_w)
                outs.append(res[:n] + res[n:])
            else:
                outs.append(_dot(xc.astype(BF16), head_ones_w))
        return jnp.concatenate(outs, axis=1) if len(outs) > 1 else outs[0]

    r = zr_ref[...].astype(F32)
    k = zk_ref[...].astype(F32)
    v = zv_ref[...].astype(F32)
    dlin = w0 + _dot3(jnp.tanh(dd_ref[...]), w2h_ref[...], w2l_ref[...])
    lw = -DECAY_SCALE * _sigmoid(dlin)
    iclr = _sigmoid(a0 + _dot(ad_ref[...].astype(BF16), a2_ref[...]))
    kkr = k * k_k
    kmod = k * (1.0 + (iclr - 1.0) * k_a)
    cum = _chunk_cumsum(lw, bwd)
    e_in = jnp.exp(cum)
    e_ex = jnp.exp(cum - lw)
    e_ng = jnp.exp(-cum)
    kk = kkr * lax.rsqrt(jnp.maximum(head_sum(kkr * kkr, False), 1e-24))
    bonus_ref[...] = (head_sum(r * kmod * r_k, True) * v).astype(bonus_ref.dtype)
    b_all = kk * iclr
    ah_all = -kk * e_ex
    rh_all = r * e_in
    bh_all = b_all * e_ng
    kh_all = kmod * e_ng

    chunk_order = list(range(n_chunks - 1, -1, -1) if bwd else range(n_chunks))
    pair_vals = []
    for p in range(n_pairs):
        sl = slice(p * PW, (p + 1) * PW)
        pair_vals.append(dict(b=b_all[:, sl], ah=ah_all[:, sl], rh=rh_all[:, sl],
                              bh=bh_all[:, sl], kh=kh_all[:, sl]))

    probs = []
    for ci in chunk_order:
        for p in range(n_pairs):
            sl = slice(p * PW, (p + 1) * PW)
            rs = slice(ci * L, (ci + 1) * L)
            pv = pair_vals[p]
            q = dict(p=p, ci=ci, sl=sl, rs=rs, ah=pv["ah"][rs], rh=pv["rh"][rs], v=v[rs, sl])
            lhs = jnp.concatenate([q["ah"], q["rh"]], axis=0).astype(BF16)
            g = _dot_nt(lhs, jnp.concatenate([bd(pv["bh"][rs]), bd(pv["kh"][rs])], axis=0))
            a32 = jnp.where(strict, g[:L, :2 * L], 0.0)
            q["a"] = a32.astype(BF16)
            q["t"] = (ipk + a32).astype(BF16)
            q["rb"] = jnp.where(incl, g[L:, :2 * L], 0.0).astype(BF16)
            q["akrk"] = jnp.concatenate([jnp.where(strict, g[:L, 2 * L:], 0.0),
                                         jnp.where(incl, g[L:, 2 * L:], 0.0)], axis=0).astype(BF16)
            probs.append(q)
    for q in probs:
        q["kv"] = _dot(q["akrk"], bd(q["v"]))
    for q in probs:
        q["ai"] = _dot(q["a"], bd(q["a"])).astype(BF16)
    for lvl in range(1, 6):
        more = lvl < 5
        for q in probs:
            lhs = jnp.concatenate([q["t"], q["ai"]], axis=0) if more else q["t"]
            res = _dot(lhs, bd(q["ai"]))
            q["t"] = (q["t"].astype(F32) + res[:L]).astype(BF16)
            if more:
                q["ai"] = res[L:].astype(BF16)
    for q in probs:
        q["rbt"] = _dot(q["rb"], bd(q["t"])).astype(BF16)
    for q in probs:
        y0 = jnp.concatenate([q["ah"], q["kv"][:L]], axis=1)
        q["ry"] = _dot(jnp.concatenate([q["t"], q["rbt"]], axis=0), bd2(y0))
    for q in probs:
        p, rs, sl, ry = q["p"], q["rs"], q["sl"], q["ry"]
        end = q["ci"] * L if bwd else q["ci"] * L + L - 1
        tot = cum[end:end + 1, sl]
        e_rm = jnp.exp(tot - cum[rs, sl])
        q["rt"] = q["rh"] + ry[L:, :PW]
        q["ob"] = ry[L:, PW:] + q["kv"][L:]
        lhs_t = jnp.concatenate([pair_vals[p]["b"][rs] * e_rm, kmod[rs, sl] * e_rm], axis=0).astype(BF16)
        rhs_t = jnp.concatenate(
            [ry[:L], jnp.concatenate([jnp.zeros((L, PW), F32), q["v"]], axis=1)], axis=0).astype(BF16)
        mn = _dot_tn(lhs_t, rhs_t)
        q["mm"] = jnp.where(eye, jnp.exp(tot), 0.0) + jnp.where(same_head, mn[:, :PW], 0.0)
        q["nn"] = jnp.where(same_head, mn[:, PW:], 0.0)
    states = [state_ref[p] for p in range(n_pairs)]
    for q in probs:
        p = q["p"]
        res = _dot(jnp.concatenate([q["rt"], q["mm"]], axis=0).astype(BF16), states[p].astype(BF16))
        o_ref[q["rs"], q["sl"]] = (res[:L] + q["ob"]).astype(o_ref.dtype)
        states[p] = res[L:] + q["nn"]
    for p in range(n_pairs):
        state_ref[p] = states[p]


def _chunk_cumsum(x, bwd):
    L = WKV_CHUNK
    n = x.shape[0]
    rin = lax.broadcasted_iota(jnp.int32, (n, 1), 0) % L
    s = 1
    while s < L:
        if bwd:
            x = x + jnp.where(rin < L - s, pltpu.roll(x, n - s, 0), 0.0)
        else:
            x = x + jnp.where(rin >= s, pltpu.roll(x, s, 0), 0.0)
        s *= 2
    return x


def _wkv(z_rkv, z_lora, decay_w0, w2h, w2l, iclr_a0, a2_bf16, k_k, k_a, r_k, n_seq, seq_len, bwd):
    T = z_rkv.shape[0]
    W = z_rkv.shape[1] // 3
    PW = 2 * RWKV_HEAD
    n_pairs = _tile(W // PW, (4, 2, 1))
    GW = n_pairs * PW
    n_groups = W // GW
    TT = _tile(seq_len, (256, 128, 64))
    nT = seq_len // TT
    d = int(bwd)

    def tb(b, c):
        return b * nT + (nT - 1 - c if bwd else c)

    zspec = lambda off: pl.BlockSpec((TT, GW), lambda b, g, c: (tb(b, c), off * n_groups + g))
    pspec = pl.BlockSpec((None, 1, GW), lambda b, g, c: (d, 0, g))
    lspec = pl.BlockSpec((None, LORA_PAD, GW), lambda b, g, c: (d, 0, g))
    cspec = pl.BlockSpec((1, GW), lambda b, g, c: (0, g))
    ospec = pl.BlockSpec((TT, GW), lambda b, g, c: (tb(b, c), g))
    return pl.pallas_call(
        functools.partial(_wkv_kernel, n_pairs=n_pairs, bwd=bwd),
        grid=(n_seq, n_groups, nT),
        in_specs=[
            zspec(0), zspec(1), zspec(2),
            pl.BlockSpec((TT, LORA_PAD), lambda b, g, c: (tb(b, c), d)),
            pl.BlockSpec((TT, LORA_PAD), lambda b, g, c: (tb(b, c), 2 + d)),
            pspec, lspec, lspec, pspec, lspec, cspec, cspec, cspec,
        ],
        out_specs=[ospec, ospec],
        out_shape=[jax.ShapeDtypeStruct((T, W), BF16), jax.ShapeDtypeStruct((T, W), BF16)],
        scratch_shapes=[pltpu.VMEM((n_pairs, PW, PW), F32)],
        compiler_params=_cparams(("parallel", "parallel", "arbitrary")),
        name="wkv_scan_bwd" if bwd else "wkv_scan_fwd",
    )(z_rkv, z_rkv, z_rkv, z_lora, z_lora, decay_w0, w2h, w2l, iclr_a0, a2_bf16, k_k, k_a, r_k)


WKV_ISSUE_ORDER = "fffffbffbffffff" + "bf" * 4


def _wkv_bidir_kernel(zr_f, zk_f, zv_f, dd_f, ad_f, zr_b, zk_b, zv_b, dd_b, ad_b, w0_ref, w2h_ref, w2l_ref,
                      a0_ref, a2_ref, kk_ref, ka_ref, rk_ref, of_ref, bf_ref, ob_ref, bb_ref, state_ref,
                      *, n_pairs):
    L = WKV_CHUNK
    PW = 2 * RWKV_HEAD
    TT = zr_f.shape[0]
    n_chunks = TT // L

    @pl.when(pl.program_id(2) == 0)
    def _():
        state_ref[...] = jnp.zeros_like(state_ref)

    row = lax.broadcasted_iota(jnp.int32, (L, 2 * L), 0)
    col = lax.broadcasted_iota(jnp.int32, (L, 2 * L), 1) % L
    ipk = jnp.where(col == row, 1.0, 0.0)
    lane = lax.broadcasted_iota(jnp.int32, (1, PW), 1)
    m0 = lane < RWKV_HEAD
    lane2 = lax.broadcasted_iota(jnp.int32, (1, 2 * PW), 1) % PW
    m0w = lane2 < RWKV_HEAD
    srow = lax.broadcasted_iota(jnp.int32, (PW, PW), 0)
    scol = lax.broadcasted_iota(jnp.int32, (PW, PW), 1)
    same_head = (srow // RWKV_HEAD) == (scol // RWKV_HEAD)
    eye = srow == scol
    HSW = min(zr_f.shape[1], 2 * PW)
    hrow = lax.broadcasted_iota(jnp.int32, (HSW, HSW), 0)
    hcol = lax.broadcasted_iota(jnp.int32, (HSW, HSW), 1)
    head_ones_w = jnp.where((hrow // RWKV_HEAD) == (hcol // RWKV_HEAD), 1.0, 0.0).astype(BF16)

    def bd(x):
        return jnp.concatenate([jnp.where(m0, x, 0.0), jnp.where(m0, 0.0, x)], axis=0).astype(BF16)

    def bd2(x):
        return jnp.concatenate([jnp.where(m0w, x, 0.0), jnp.where(m0w, 0.0, x)], axis=0).astype(BF16)

    def head_sum(x, two_pass):
        n = x.shape[0]
        outs = []
        for c in range(x.shape[1] // HSW):
            xc = x[:, c * HSW:(c + 1) * HSW]
            if two_pass:
                hi, lo = _split2(xc)
                res = _dot(jnp.concatenate([hi, lo], axis=0), head_ones_w)
                outs.append(res[:n] + res[n:])
            else:
                outs.append(_dot(xc.astype(BF16), head_ones_w))
        return jnp.concatenate(outs, axis=1) if len(outs) > 1 else outs[0]

    def direction(d, zr_ref, zk_ref, zv_ref, dd_ref, ad_ref, o_ref, bonus_ref):
        bwd = d == 1
        strict = (col > row) if bwd else (col < row)
        incl = (col >= row) if bwd else (col <= row)
        r = zr_ref[...].astype(F32)
        k = zk_ref[...].astype(F32)
        v = zv_ref[...].astype(F32)
        dlin = w0_ref[d] + _dot3(jnp.tanh(dd_ref[...]), w2h_ref[d], w2l_ref[d])
        iclr_lin = a0_ref[d] + _dot(ad_ref[...].astype(BF16), a2_ref[d])
        yield
        lw = -DECAY_SCALE * _sigmoid(dlin)
        iclr = _sigmoid(iclr_lin)
        kkr = k * kk_ref[...]
        kmod = k * (1.0 + (iclr - 1.0) * ka_ref[...])
        ss = head_sum(kkr * kkr, False)
        bonus_ref[...] = (head_sum(r * kmod * rk_ref[...], True) * v).astype(bonus_ref.dtype)
        yield
        cum = _chunk_cumsum(lw, bwd)
        e_ng = jnp.exp(-cum)
        kk = kkr * lax.rsqrt(jnp.maximum(ss, 1e-24))
        b_all = kk * iclr
        ah_all = -kk * jnp.exp(cum - lw)
        rh_all = r * jnp.exp(cum)
        bh_all = b_all * e_ng
        kh_all = kmod * e_ng

        probs = []
        for ci in (range(n_chunks - 1, -1, -1) if bwd else range(n_chunks)):
            for p in range(n_pairs):
                sl = slice(p * PW, (p + 1) * PW)
                rs = slice(ci * L, (ci + 1) * L)
                q = dict(p=p, ci=ci, sl=sl, rs=rs, ah=ah_all[rs, sl], rh=rh_all[rs, sl], v=v[rs, sl])
                lhs = jnp.concatenate([q["ah"], q["rh"]], axis=0).astype(BF16)
                g = _dot_nt(lhs, jnp.concatenate([bd(bh_all[rs, sl]), bd(kh_all[rs, sl])], axis=0))
                a32 = jnp.where(strict, g[:L, :2 * L], 0.0)
                q["a"] = a32.astype(BF16)
                q["t"] = (ipk + a32).astype(BF16)
                q["rb"] = jnp.where(incl, g[L:, :2 * L], 0.0).astype(BF16)
                q["akrk"] = jnp.concatenate([jnp.where(strict, g[:L, 2 * L:], 0.0),
                                             jnp.where(incl, g[L:, 2 * L:], 0.0)], axis=0).astype(BF16)
                probs.append(q)
        yield
        for q in probs:
            q["kv"] = _dot(q["akrk"], bd(q["v"]))
        yield
        for q in probs:
            q["ai"] = _dot(q["a"], bd(q["a"])).astype(BF16)
        yield
        for lvl in range(1, 6):
            more = lvl < 5
            for q in probs:
                lhs = jnp.concatenate([q["t"], q["ai"]], axis=0) if more else q["t"]
                res = _dot(lhs, bd(q["ai"]))
                q["t"] = (q["t"].astype(F32) + res[:L]).astype(BF16)
                if more:
                    q["ai"] = res[L:].astype(BF16)
            yield
        for q in probs:
            q["rbt"] = _dot(q["rb"], bd(q["t"])).astype(BF16)
        yield
        for q in probs:
            y0 = jnp.concatenate([q["ah"], q["kv"][:L]], axis=1)
            q["ry"] = _dot(jnp.concatenate([q["t"], q["rbt"]], axis=0), bd2(y0))
        yield
        for q in probs:
            rs, sl, ry = q["rs"], q["sl"], q["ry"]
            end = q["ci"] * L if bwd else q["ci"] * L + L - 1
            tot = cum[end:end + 1, sl]
            e_rm = jnp.exp(tot - cum[rs, sl])
            q["rt"] = q["rh"] + ry[L:, :PW]
            q["ob"] = ry[L:, PW:] + q["kv"][L:]
            lhs_t = jnp.concatenate([b_all[rs, sl] * e_rm, kmod[rs, sl] * e_rm], axis=0).astype(BF16)
            rhs_t = jnp.concatenate(
                [ry[:L], jnp.concatenate([jnp.zeros((L, PW), F32), q["v"]], axis=1)], axis=0).astype(BF16)
            mn = _dot_tn(lhs_t, rhs_t)
            q["mm"] = jnp.where(eye, jnp.exp(tot), 0.0) + jnp.where(same_head, mn[:, :PW], 0.0)
            q["nn"] = jnp.where(same_head, mn[:, PW:], 0.0)
        yield
        states = [state_ref[d, p] for p in range(n_pairs)]
        for n, q in enumerate(probs):
            p = q["p"]
            res = _dot(jnp.concatenate([q["rt"], q["mm"]], axis=0).astype(BF16), states[p].astype(BF16))
            o_ref[q["rs"], q["sl"]] = (res[:L] + q["ob"]).astype(o_ref.dtype)
            states[p] = res[L:] + q["nn"]
            if p == n_pairs - 1 and n < len(probs) - 1:
                yield
        for p in range(n_pairs):
            state_ref[d, p] = states[p]

    gens = {"f": direction(0, zr_f, zk_f, zv_f, dd_f, ad_f, of_ref, bf_ref),
            "b": direction(1, zr_b, zk_b, zv_b, dd_b, ad_b, ob_ref, bb_ref)}
    order = list(WKV_ISSUE_ORDER)
    while gens:
        key = order.pop(0) if order else next(iter(gens))
        if key in gens and next(gens[key], "done") == "done":
            del gens[key]


def _wkv_bidir(z_rkv, z_lora, decay_w0, w2h, w2l, iclr_a0, a2_bf16, k_k, k_a, r_k, n_seq, seq_len):
    T = z_rkv.shape[0]
    W = z_rkv.shape[1] // 3
    PW = 2 * RWKV_HEAD
    n_pairs = _tile(W // PW, (4, 2, 1))
    GW = n_pairs * PW
    n_groups = W // GW
    TT = _tile(seq_len, (256, 128, 64))
    nT = seq_len // TT
    fwd_t = lambda b, c: b * nT + c
    bwd_t = lambda b, c: b * nT + nT - 1 - c
    in_specs = []
    for tile, d in ((fwd_t, 0), (bwd_t, 1)):
        in_specs += [pl.BlockSpec((TT, GW), lambda b, g, c, tile=tile, off=off: (tile(b, c), off * n_groups + g))
                     for off in range(3)]
        in_specs += [pl.BlockSpec((TT, LORA_PAD), lambda b, g, c, tile=tile, d=d: (tile(b, c), d)),
                     pl.BlockSpec((TT, LORA_PAD), lambda b, g, c, tile=tile, d=d: (tile(b, c), 2 + d))]
    pspec = pl.BlockSpec((2, 1, GW), lambda b, g, c: (0, 0, g))
    lspec = pl.BlockSpec((2, LORA_PAD, GW), lambda b, g, c: (0, 0, g))
    cspec = pl.BlockSpec((1, GW), lambda b, g, c: (0, g))
    in_specs += [pspec, lspec, lspec, pspec, lspec, cspec, cspec, cspec]
    ospec_f = pl.BlockSpec((TT, GW), lambda b, g, c: (fwd_t(b, c), g))
    ospec_b = pl.BlockSpec((TT, GW), lambda b, g, c: (bwd_t(b, c), g))
    out_t = jax.ShapeDtypeStruct((T, W), BF16)
    return pl.pallas_call(
        functools.partial(_wkv_bidir_kernel, n_pairs=n_pairs),
        grid=(n_seq, n_groups, nT),
        in_specs=in_specs,
        out_specs=[ospec_f, ospec_f, ospec_b, ospec_b],
        out_shape=[out_t, out_t, out_t, out_t],
        scratch_shapes=[pltpu.VMEM((2, n_pairs, PW, PW), F32)],
        compiler_params=_cparams(("parallel", "parallel", "arbitrary")),
        name="wkv_scan_bidir",
    )(*([z_rkv] * 3 + [z_lora] * 2) * 2, decay_w0, w2h, w2l, iclr_a0, a2_bf16, k_k, k_a, r_k)


def _rwkv_post_kernel(of_ref, ob_ref, bf_ref, bb_ref, gd_ref, g2_ref, lg_ref, lb_ref, out_ref):
    W = out_ref.shape[1]
    PW = min(W, 4 * RWKV_HEAD)
    srow = lax.broadcasted_iota(jnp.int32, (PW, PW), 0)
    scol = lax.broadcasted_iota(jnp.int32, (PW, PW), 1)
    head_ones = jnp.where((srow // RWKV_HEAD) == (scol // RWKV_HEAD), 1.0, 0.0).astype(BF16)
    inv = 1.0 / RWKV_HEAD
    n = out_ref.shape[0]

    def head_mean(x):
        hi, lo = _split2(x)
        res = _dot(jnp.concatenate([hi, lo], axis=0), head_ones)
        return (res[:n] + res[n:]) * inv

    gate = _dot(_sigmoid(gd_ref[...]).astype(BF16), g2_ref[...])
    for p in range(W // PW):
        sl = slice(p * PW, (p + 1) * PW)
        o = of_ref[:, sl].astype(F32) + ob_ref[:, sl].astype(F32)
        oc = o - head_mean(o)
        var = head_mean(oc * oc)
        y = oc * lax.rsqrt(var + LNX_EPS) * lg_ref[:, sl] + lb_ref[:, sl]
        y = y + bf_ref[:, sl].astype(F32) + bb_ref[:, sl].astype(F32)
        out_ref[:, sl] = (y * gate[:, sl]).astype(out_ref.dtype)


def _rwkv_post(o_f, o_b, bonus_f, bonus_b, z_lora, g2, lnx_g, lnx_b):
    T, W = o_f.shape
    tm = _tile(T, (256, 128, 64, 32, 16, 8))
    GL = g2.shape[0]
    gd_blk = (4 * LORA_PAD) // GL
    tspec = pl.BlockSpec((tm, W), lambda i: (i, 0))
    return pl.pallas_call(
        _rwkv_post_kernel,
        grid=(T // tm,),
        in_specs=[
            tspec, tspec, tspec, tspec,
            pl.BlockSpec((tm, GL), lambda i: (i, gd_blk)),
            pl.BlockSpec((GL, W), lambda i: (0, 0)),
            pl.BlockSpec((1, W), lambda i: (0, 0)),
            pl.BlockSpec((1, W), lambda i: (0, 0)),
        ],
        out_specs=pl.BlockSpec((tm, W), lambda i: (i, 0)),
        out_shape=jax.ShapeDtypeStruct((T, W), BF16),
        compiler_params=_cparams(("parallel",)),
        name="rwkv_post",
    )(o_f, o_b, bonus_f, bonus_b, z_lora, g2, lnx_g, lnx_b)


def _conv_kernel(u_ref, up_ref, un_ref, w_ref, b_ref, lg_ref, lb_ref, o_ref, ext_scr, *, seq_len, halo):
    i = pl.program_id(0)
    tm = u_ref.shape[0]
    K = w_ref.shape[0]
    first = (i * tm) % seq_len == 0
    last = ((i + 1) * tm) % seq_len == 0
    n_ext = tm + 2 * halo
    ext = jnp.concatenate([jnp.where(first, 0.0, up_ref[...].astype(F32)), u_ref[...].astype(F32),
                           jnp.where(last, 0.0, un_ref[...].astype(F32))], axis=0)
    ext_scr[0] = ext
    for r in range(1, SUBLANES):
        ext_scr[r] = pltpu.roll(ext, n_ext - r, 0)
    sub = min(tm, 32)
    base = halo - K // 2
    for s in range(tm // sub):
        acc = jnp.zeros((sub, u_ref.shape[1]), F32)
        for j in range(K):
            off = base + j
            row0 = s * sub + (off // SUBLANES) * SUBLANES
            acc = acc + w_ref[j:j + 1, :] * ext_scr[off % SUBLANES, row0:row0 + sub, :]
        acc = acc + b_ref[...]
        mean = jnp.mean(acc, axis=-1, keepdims=True)
        xc = acc - mean
        var = jnp.mean(xc * xc, axis=-1, keepdims=True)
        y = xc * lax.rsqrt(var + LN_EPS) * lg_ref[...] + lb_ref[...]
        o_ref[s * sub:(s + 1) * sub, :] = (y * _sigmoid(y)).astype(o_ref.dtype)


def _conv(u, conv_w, conv_b, ln_g, ln_b, seq_len):
    T, C = u.shape
    K = conv_w.shape[0]
    halo = 16
    assert K // 2 <= halo
    tm = _tile(seq_len, (128, 64, 32, 16))
    rh = tm // halo
    nbh = T // halo
    return pl.pallas_call(
        functools.partial(_conv_kernel, seq_len=seq_len, halo=halo),
        grid=(T // tm,),
        in_specs=[
            pl.BlockSpec((tm, C), lambda i: (i, 0)),
            pl.BlockSpec((halo, C), lambda i: (jnp.maximum(i * rh - 1, 0), 0)),
            pl.BlockSpec((halo, C), lambda i: (jnp.minimum((i + 1) * rh, nbh - 1), 0)),
            pl.BlockSpec((K, C), lambda i: (0, 0)),
            pl.BlockSpec((1, C), lambda i: (0, 0)),
            pl.BlockSpec((1, C), lambda i: (0, 0)),
            pl.BlockSpec((1, C), lambda i: (0, 0)),
        ],
        out_specs=pl.BlockSpec((tm, C), lambda i: (i, 0)),
        out_shape=jax.ShapeDtypeStruct((T, C), BF16),
        scratch_shapes=[pltpu.VMEM((SUBLANES, tm + 2 * halo, C), F32)],
        compiler_params=_cparams(("parallel",)),
        name="conformer_conv",
    )(u, u, u, conv_w, conv_b, ln_g, ln_b)


def _merge_kernel(x_ref, g_ref, orw_ref, ocv_ref, wg1_ref, wg2_ref, wb1_ref, wb2_ref, o_ref, h_scr):
    @pl.when(pl.program_id(1) == 0)
    def _():
        h_scr[...] = _rms(x_ref[...], g_ref[...]).astype(BF16)

    h = h_scr[...]
    g1 = _sigmoid(_dot(h, wg1_ref[...]))
    g2 = _sigmoid(_dot(h, wg2_ref[...]))
    y1 = _dot(orw_ref[...], wb1_ref[...])
    y2 = _dot(ocv_ref[...], wb2_ref[...])
    o_ref[...] = (g1 * y1 + g2 * y2).astype(o_ref.dtype)


def _merge(x, g, o_rwkv, o_conv, wg1, wg2, wb1, wb2):
    T, D = x.shape
    W = o_rwkv.shape[1]
    C = o_conv.shape[1]
    tm = _tile(T, (512, 256, 128, 64, 32, 16, 8))
    tn = _tile(D, (512, 256, 128))
    return pl.pallas_call(
        _merge_kernel,
        grid=(T // tm, D // tn),
        in_specs=[
            pl.BlockSpec((tm, D), lambda i, j: (i, 0)),
            pl.BlockSpec((1, D), lambda i, j: (0, 0)),
            pl.BlockSpec((tm, W), lambda i, j: (i, 0)),
            pl.BlockSpec((tm, C), lambda i, j: (i, 0)),
            pl.BlockSpec((D, tn), lambda i, j: (0, j)),
            pl.BlockSpec((D, tn), lambda i, j: (0, j)),
            pl.BlockSpec((W, tn), lambda i, j: (0, j)),
            pl.BlockSpec((C, tn), lambda i, j: (0, j)),
        ],
        out_specs=pl.BlockSpec((tm, tn), lambda i, j: (i, j)),
        out_shape=jax.ShapeDtypeStruct((T, D), BF16),
        scratch_shapes=[pltpu.VMEM((tm, D), BF16)],
        compiler_params=_cparams(("parallel", "arbitrary")),
        name="merge_gates",
    )(x, g, o_rwkv, o_conv, wg1, wg2, wb1, wb2)


def _mm_res_kernel(a_ref, w_ref, res_ref, o_ref):
    o_ref[...] = res_ref[...] + _dot(a_ref[...], w_ref[...])


def _mm_res(a, w, res):
    T, K = a.shape
    N = w.shape[1]
    tm = _tile(T, (1024, 512, 256, 128, 64, 32, 16, 8))
    tn = _tile(N, (1024, 512, 256, 128))
    return pl.pallas_call(
        _mm_res_kernel,
        grid=(T // tm, N // tn),
        in_specs=[
            pl.BlockSpec((tm, K), lambda i, j: (i, 0)),
            pl.BlockSpec((K, tn), lambda i, j: (0, j)),
            pl.BlockSpec((tm, tn), lambda i, j: (i, j)),
        ],
        out_specs=pl.BlockSpec((tm, tn), lambda i, j: (i, j)),
        out_shape=jax.ShapeDtypeStruct((T, N), F32),
        compiler_params=_cparams(("parallel", "parallel")),
        name="proj_residual",
    )(a, w, res)


def _mm_norm_kernel(x_ref, g_ref, w_ref, o_ref, h_scr):
    @pl.when(pl.program_id(1) == 0)
    def _():
        h_scr[...] = _rms(x_ref[...], g_ref[...]).astype(BF16)

    o_ref[...] = _dot(h_scr[...], w_ref[...]).astype(o_ref.dtype)


def _mm_norm(x, g, w):
    T, D = x.shape
    N = w.shape[1]
    tm = _tile(T, (512, 256, 128, 64, 32, 16, 8))
    tn = _tile(N, (1024, 512, 256, 128))
    return pl.pallas_call(
        _mm_norm_kernel,
        grid=(T // tm, N // tn),
        in_specs=[
            pl.BlockSpec((tm, D), lambda i, j: (i, 0)),
            pl.BlockSpec((1, D), lambda i, j: (0, 0)),
            pl.BlockSpec((D, tn), lambda i, j: (0, j)),
        ],
        out_specs=pl.BlockSpec((tm, tn), lambda i, j: (i, j)),
        out_shape=jax.ShapeDtypeStruct((T, N), BF16),
        scratch_shapes=[pltpu.VMEM((tm, D), BF16)],
        compiler_params=_cparams(("parallel", "arbitrary")),
        name="norm_proj",
    )(x, g, w)


def _xattn_kernel(q_ref, k_ref, v_ref, o_ref):
    D = q_ref.shape[1]
    hd = D // XATTN_HEADS
    scale = hd ** -0.5
    for h in range(XATTN_HEADS):
        sl = slice(h * hd, (h + 1) * hd)
        s = _dot_nt(q_ref[:, sl], k_ref[:, sl]) * scale
        s = s - jnp.max(s, axis=-1, keepdims=True)
        e = jnp.exp(s)
        p = e / jnp.sum(e, axis=-1, keepdims=True)
        o_ref[:, sl] = _dot(p.astype(BF16), v_ref[:, sl]).astype(o_ref.dtype)


def _xattn(q, kv, n_seq, seq_len, n_mem):
    T, D = q.shape
    tm = _tile(seq_len, (512, 256, 128, 64, 32, 16, 8))
    nT = seq_len // tm
    return pl.pallas_call(
        _xattn_kernel,
        grid=(n_seq, nT),
        in_specs=[
            pl.BlockSpec((tm, D), lambda b, i: (b * nT + i, 0)),
            pl.BlockSpec((n_mem, D), lambda b, i: (b, 0)),
            pl.BlockSpec((n_mem, D), lambda b, i: (b, 1)),
        ],
        out_specs=pl.BlockSpec((tm, D), lambda b, i: (b * nT + i, 0)),
        out_shape=jax.ShapeDtypeStruct((T, D), BF16),
        compiler_params=_cparams(("parallel", "parallel")),
        name="cross_attention",
    )(q, kv, kv)


def _router_kernel(x_ref, g_ref, wh_ref, wl_ref, b_ref, hf_ref, e_ref, gate_ref):
    hf = _rms(x_ref[...], g_ref[...])
    hf_ref[...] = hf.astype(hf_ref.dtype)
    logits = _dot3(hf, wh_ref[...], wl_ref[...]) + b_ref[...]
    E = logits.shape[1]
    eid = lax.broadcasted_iota(jnp.int32, logits.shape, 1).astype(F32)
    work = logits
    vals = []
    idxs = []
    for _ in range(TOP_K):
        m = jnp.max(work, axis=-1, keepdims=True)
        idx = jnp.min(jnp.where(work == m, eid, float(E)), axis=-1, keepdims=True)
        vals.append(m)
        idxs.append(idx.astype(jnp.int32))
        work = jnp.where(eid == idx, -jnp.inf, work)
    ex = [jnp.exp(vv - vals[0]) for vv in vals]
    den = ex[0]
    for t in ex[1:]:
        den = den + t
    kid = lax.broadcasted_iota(jnp.int32, (logits.shape[0], TOP_K), 1)
    e_out = jnp.zeros((logits.shape[0], TOP_K), jnp.int32)
    g_out = jnp.zeros((logits.shape[0], TOP_K), F32)
    for t in range(TOP_K):
        e_out = jnp.where(kid == t, idxs[t], e_out)
        g_out = jnp.where(kid == t, ex[t] / den, g_out)
    e_ref[...] = e_out
    gate_ref[...] = g_out


def _router(x, g, w_router, b_router):
    T, D = x.shape
    E = w_router.shape[1]
    wh, wl = _split2(w_router)
    tm = _tile(T, (256, 128, 64, 32, 16, 8))
    return pl.pallas_call(
        _router_kernel,
        grid=(T // tm,),
        in_specs=[
            pl.BlockSpec((tm, D), lambda i: (i, 0)),
            pl.BlockSpec((1, D), lambda i: (0, 0)),
            pl.BlockSpec((D, E), lambda i: (0, 0)),
            pl.BlockSpec((D, E), lambda i: (0, 0)),
            pl.BlockSpec((1, E), lambda i: (0, 0)),
        ],
        out_specs=[
            pl.BlockSpec((tm, D), lambda i: (i, 0)),
            pl.BlockSpec((tm, TOP_K), lambda i: (i, 0)),
            pl.BlockSpec((tm, TOP_K), lambda i: (i, 0)),
        ],
        out_shape=[jax.ShapeDtypeStruct((T, D), BF16),
                   jax.ShapeDtypeStruct((T, TOP_K), jnp.int32),
                   jax.ShapeDtypeStruct((T, TOP_K), F32)],
        compiler_params=_cparams(("parallel",)),
        name="router",
    )(x, g, wh, wl, b_router)


def _expert_kernel(te_ref, tv_ref, x_ref, wg_ref, wu_ref, bg_ref, bu_ref, wd_ref, bd_ref, o_ref, act_ref, *, nf):
    i = pl.program_id(0)
    s = pl.program_id(1)
    tf = wg_ref.shape[1]
    valid = tv_ref[i] > 0

    @pl.when(valid & (s < nf))
    def _():
        x = x_ref[...]
        g = _dot(x, wg_ref[...].astype(BF16)) + bg_ref[...]
        u = _dot(x, wu_ref[...].astype(BF16)) + bu_ref[...]
        g = jnp.minimum(g, SWIGLU_LIMIT)
        u = jnp.clip(u, -SWIGLU_LIMIT, SWIGLU_LIMIT)
        act = ((u + 1.0) * (g * _sigmoid(SWIGLU_ALPHA * g))).astype(BF16)
        for f in range(nf):
            @pl.when(s == f)
            def _(f=f):
                act_ref[:, f * tf:(f + 1) * tf] = act

    @pl.when(valid & (s >= nf))
    def _():
        y = _dot(act_ref[...], wd_ref[...].astype(BF16)) + bd_ref[...]
        o_ref[...] = y.astype(o_ref.dtype)

    @pl.when(jnp.logical_not(valid) & (s >= nf))
    def _():
        o_ref[...] = jnp.zeros_like(o_ref)


def _experts(xb, tile_e, tile_valid, w_gu, b_gu, w_dn, b_dn, tm):
    R, D = xb.shape
    E, _, F2 = w_gu.shape
    F = F2 // 2
    tf = _tile(F, (512, 256, 128))
    tn = _tile(D, (512, 256, 128))
    nf = F // tf
    nn = D // tn
    n_tiles = R // tm
    b_gu3 = b_gu.reshape(E, 1, F2)
    b_dn3 = b_dn.reshape(E, 1, D)

    def fa(i, s, tv):
        return jnp.where(tv[i] > 0, jnp.minimum(s, nf - 1), nf - 1)

    def nb(i, s, tv):
        return jnp.where(tv[i] > 0, jnp.clip(s - nf, 0, nn - 1), nn - 1)

    grid_spec = pltpu.PrefetchScalarGridSpec(
        num_scalar_prefetch=2,
        grid=(n_tiles, nf + nn),
        in_specs=[
            pl.BlockSpec((tm, D), lambda i, s, te, tv: (i, 0)),
            pl.BlockSpec((None, D, tf), lambda i, s, te, tv: (te[i], 0, fa(i, s, tv))),
            pl.BlockSpec((None, D, tf), lambda i, s, te, tv: (te[i], 0, nf + fa(i, s, tv))),
            pl.BlockSpec((None, 1, tf), lambda i, s, te, tv: (te[i], 0, fa(i, s, tv))),
            pl.BlockSpec((None, 1, tf), lambda i, s, te, tv: (te[i], 0, nf + fa(i, s, tv))),
            pl.BlockSpec((None, F, tn), lambda i, s, te, tv: (te[i], 0, nb(i, s, tv))),
            pl.BlockSpec((None, 1, tn), lambda i, s, te, tv: (te[i], 0, nb(i, s, tv))),
        ],
        out_specs=pl.BlockSpec((tm, tn), lambda i, s, te, tv: (i, jnp.clip(s - nf, 0, nn - 1))),
        scratch_shapes=[pltpu.VMEM((tm, F), BF16)],
    )
    return pl.pallas_call(
        functools.partial(_expert_kernel, nf=nf),
        grid_spec=grid_spec,
        out_shape=jax.ShapeDtypeStruct((R, D), BF16),
        compiler_params=_cparams(("arbitrary", "arbitrary")),
        name="moe_experts",
    )(tile_e, tile_valid, xb, w_gu, w_gu, b_gu3, b_gu3, w_dn, b_dn3)


def _combine_kernel(x_ref, y_ref, gate_ref, g_ref, o_head_ref, o_tail_ref, *, n_head_tiles):
    acc = x_ref[...]
    gate = gate_ref[...]
    for t in range(TOP_K):
        acc = acc + gate[:, t:t + 1] * y_ref[t].astype(F32)
    y = _rms(acc, g_ref[...])
    i = pl.program_id(0)

    @pl.when(i < n_head_tiles)
    def _():
        o_head_ref[...] = y

    @pl.when(i >= n_head_tiles)
    def _():
        o_tail_ref[...] = y


def _combine(x, y4, gate, final_g, t_head):
    T, D = x.shape
    tm = _tile(math.gcd(t_head, T - t_head), (256, 128, 64, 32, 16, 8))
    nh = t_head // tm
    return pl.pallas_call(
        functools.partial(_combine_kernel, n_head_tiles=nh),
        grid=(T // tm,),
        in_specs=[
            pl.BlockSpec((tm, D), lambda i: (i, 0)),
            pl.BlockSpec((TOP_K, tm, D), lambda i: (0, i, 0)),
            pl.BlockSpec((tm, TOP_K), lambda i: (i, 0)),
            pl.BlockSpec((1, D), lambda i: (0, 0)),
        ],
        out_specs=[pl.BlockSpec((tm, D), lambda i: (jnp.minimum(i, nh - 1), 0)),
                   pl.BlockSpec((tm, D), lambda i: (jnp.maximum(i - nh, 0), 0))],
        out_shape=[jax.ShapeDtypeStruct((t_head, D), F32), jax.ShapeDtypeStruct((T - t_head, D), F32)],
        compiler_params=_cparams(("arbitrary",)),
        name="moe_combine_norm",
    )(x, y4, gate, final_g)


def _pad_rows(w, n):
    return jnp.pad(w, ((0, 0),) * (w.ndim - 2) + ((0, n - w.shape[-2]), (0, 0)))


def _pad_cols(w, n):
    return jnp.pad(w, ((0, 0),) * (w.ndim - 1) + ((0, n - w.shape[-1]),))


def _moe_tile_rows(n_assign, n_experts):
    for tm in (1024, 512, 256, 128, 64, 32, 16, 8):
        if n_assign >= 4 * n_experts * tm or tm == 8:
            return tm


def _layer(x, mem, n_seq, seq_len, t_head, norm_mix_g, w_in, shift_mu, decay_w0, decay_w2, iclr_a0, iclr_a2,
           gate_g2, k_k, k_a, r_k, lnx_g, lnx_b, conv_w, conv_b, conv_ln_g, conv_ln_b, w_branch, w_o,
           norm_x_g, norm_mem_g, w_xq, w_xkv, w_xo, norm_ffn_g, w_router, b_router, w_gu, b_gu,
           w_dn, b_dn, final_g):
    T, D = x.shape
    W = k_k.shape[0]
    DL = decay_w2.shape[1]
    AL = iclr_a2.shape[1]
    GL = gate_g2.shape[0]
    C = conv_w.shape[1]
    E = w_router.shape[1]
    n_mem = mem.shape[0] // n_seq
    row = lambda v: v.reshape(1, -1)

    o3 = 3 * W
    o4 = o3 + 2 * DL
    o5 = o4 + 2 * AL
    o6 = o5 + GL
    seg = lambda m, a, b, n: _pad_cols(m[..., a:b], n)
    lora_cols = lambda m: jnp.concatenate(
        [seg(m, o3, o3 + DL, LORA_PAD), seg(m, o3 + DL, o4, LORA_PAD), seg(m, o4, o4 + AL, LORA_PAD),
         seg(m, o4 + AL, o5, LORA_PAD), m[..., o5:o6]], axis=-1)
    w_rkv = w_in[:, :o3].astype(BF16)
    w_lora = lora_cols(w_in).astype(BF16)
    mu_rkv = row(shift_mu[:o3])
    mu_lora = row(lora_cols(shift_mu))
    w_ca = w_in[:, o6:o6 + C].astype(BF16)
    w_cb = w_in[:, o6 + C:o6 + 2 * C].astype(BF16)
    w_g1 = w_in[:, o6 + 2 * C:o6 + 2 * C + D].astype(BF16)
    w_g2 = w_in[:, o6 + 2 * C + D:].astype(BF16)
    g_mix = row(norm_mix_g)

    z_rkv = _inproj_shift(x, g_mix, w_rkv, mu_rkv, seq_len, _tile(o3, (1024, 512, 256, 128)), BF16)
    z_lora = _inproj_shift(x, g_mix, w_lora, mu_lora, seq_len, w_lora.shape[1], F32)
    u = _inproj_glu(x, g_mix, w_ca, w_cb)

    w2h, w2l = _split2(_pad_rows(decay_w2, LORA_PAD))
    a2_bf16 = _pad_rows(iclr_a2, LORA_PAD).astype(BF16)
    wkv_args = (z_rkv, z_lora, decay_w0.reshape(2, 1, W), w2h, w2l, iclr_a0.reshape(2, 1, W), a2_bf16,
                row(k_k), row(k_a), row(r_k), n_seq, seq_len)
    o_f, bonus_f, o_b, bonus_b = _wkv_bidir(*wkv_args)
    o_rwkv = _rwkv_post(o_f, o_b, bonus_f, bonus_b, z_lora, gate_g2.astype(BF16), row(lnx_g), row(lnx_b))
    o_conv = _conv(u, conv_w, row(conv_b), row(conv_ln_g), row(conv_ln_b), seq_len)

    merged = _merge(x, g_mix, o_rwkv, o_conv, w_g1, w_g2, w_branch[:W].astype(BF16),
                    w_branch[W:].astype(BF16))
    x1 = _mm_res(merged, w_o.astype(BF16), x)

    q = _mm_norm(x1, row(norm_x_g), w_xq.astype(BF16))
    kv = _mm_norm(mem, row(norm_mem_g), w_xkv.astype(BF16))
    att = _xattn(q, kv, n_seq, seq_len, n_mem)
    x2 = _mm_res(att, w_xo.astype(BF16), x1)

    hf, top_e, gate = _router(x2, row(norm_ffn_g), w_router, row(b_router))

    A = T * TOP_K
    tm_e = _moe_tile_rows(A, E)
    n_tiles = (A + E * (tm_e - 1) + tm_e - 1) // tm_e
    n_rows = n_tiles * tm_e
    flat_e = top_e.reshape(A)
    iota_a = jnp.arange(A, dtype=jnp.int32)
    sorted_e, order = lax.sort((flat_e, iota_a), num_keys=1)
    _, rank = lax.sort((order, iota_a), num_keys=1)
    experts = jnp.arange(E, dtype=jnp.int32)
    start = jnp.searchsorted(sorted_e, experts, side='left').astype(jnp.int32)
    counts = jnp.searchsorted(sorted_e, experts, side='right').astype(jnp.int32) - start
    padded = (counts + tm_e - 1) // tm_e * tm_e
    pad_end = jnp.cumsum(padded)
    pad_start = pad_end - padded
    dest_of = pad_start[flat_e] + rank - start[flat_e]
    tile_start = jnp.arange(n_tiles, dtype=jnp.int32) * tm_e
    tile_e = jnp.minimum(jnp.searchsorted(pad_end, tile_start, side='right'), E - 1).astype(jnp.int32)
    tile_valid = jnp.clip(pad_start[tile_e] + counts[tile_e] - tile_start, 0, tm_e).astype(jnp.int32)
    rows = jnp.arange(n_rows, dtype=jnp.int32)
    row_e = jnp.repeat(tile_e, tm_e)
    row_off = rows - pad_start[row_e]
    src = order[jnp.minimum(start[row_e] + row_off, A - 1)] // TOP_K
    row_tok = jnp.where(row_off < counts[row_e], src, rows % T)

    xb = hf.at[row_tok].get(mode='promise_in_bounds')
    yb = _experts(xb, tile_e, tile_valid, w_gu, b_gu, w_dn, b_dn, tm_e)
    dest_slot_major = dest_of.reshape(T, TOP_K).T.reshape(A)
    y4 = yb.at[dest_slot_major].get(mode='promise_in_bounds').reshape(TOP_K, T, D)
    return _combine(x2, y4, gate, row(final_g), t_head)


def kernel(x_prompt, x_sample, mem_prompt, mem_sample, norm_mix_g, w_in, shift_mu, decay_w0, decay_w2,
           iclr_a0, iclr_a2, gate_g2, k_k, k_a, r_k, lnx_g, lnx_b, conv_w, conv_b, conv_ln_g, conv_ln_b,
           w_branch, w_o, norm_x_g, norm_mem_g, w_xq, w_xkv, w_xo, norm_ffn_g, w_router, b_router,
           w_gu, b_gu, w_dn, b_dn, final_g):
    layer_params = (norm_mix_g, w_in, shift_mu, decay_w0, decay_w2, iclr_a0, iclr_a2, gate_g2, k_k, k_a,
                    r_k, lnx_g, lnx_b, conv_w, conv_b, conv_ln_g, conv_ln_b, w_branch, w_o, norm_x_g,
                    norm_mem_g, w_xq, w_xkv, w_xo, norm_ffn_g, w_router, b_router, w_gu, b_gu, w_dn, b_dn)
    assert all(p.shape[0] == 1 for p in layer_params), "single-layer stack expected"
    bp, seq_len, D = x_prompt.shape
    bs = x_sample.shape[0]
    assert x_sample.shape[1] == seq_len
    n_seq = bp + bs
    x = jnp.concatenate([x_prompt, x_sample], axis=0).reshape(n_seq * seq_len, D)
    mem = jnp.concatenate([mem_prompt, mem_sample], axis=0).reshape(-1, D)
    y_p, y_s = _layer(x, mem, n_seq, seq_len, bp * seq_len, *[p[0] for p in layer_params], final_g)
    return y_p.reshape(bp, seq_len, D), y_s.reshape(bs, seq_len, D)
```

```python
import functools
import math

import jax
import jax.numpy as jnp
from jax import lax
from jax.experimental import pallas as pl
from jax.experimental.pallas import tpu as pltpu

F32 = jnp.float32
BF16 = jnp.bfloat16

RWKV_HEAD = 64
DECAY_SCALE = math.exp(-0.5)
LNX_EPS = RWKV_HEAD * 1e-5
RMS_EPS = 1e-5
LN_EPS = 1e-5
XATTN_HEADS = 4
TOP_K = 4
SWIGLU_ALPHA = 1.702
SWIGLU_LIMIT = 7.0

LANES = 128
SUBLANES = 8
WKV_CHUNK = 64
LORA_PAD = 128
VMEM_LIMIT = 56 * 1024 * 1024


def _cparams(sem):
    return pltpu.CompilerParams(dimension_semantics=sem, vmem_limit_bytes=VMEM_LIMIT)


def _tile(n, prefs):
    for p in prefs:
        if n % p == 0:
            return p
    return n


def _dot(a, b):
    return jnp.dot(a, b, preferred_element_type=F32)


def _dot_nt(a, b):
    return lax.dot_general(a, b, (((1,), (1,)), ((), ())), preferred_element_type=F32)


def _dot_tn(a, b):
    return lax.dot_general(a, b, (((0,), (0,)), ((), ())), preferred_element_type=F32)


def _split2(x):
    hi = x.astype(BF16)
    lo = (x - hi.astype(F32)).astype(BF16)
    return hi, lo


def _dot3(x, w_hi, w_lo):
    xh, xl = _split2(x)
    return _dot(xh, w_hi) + _dot(xl, w_hi) + _dot(xh, w_lo)


def _rms(x, g):
    return x * lax.rsqrt(jnp.mean(x * x, axis=-1, keepdims=True) + RMS_EPS) * g


def _sigmoid(x):
    return 0.5 * jnp.tanh(0.5 * x) + 0.5


def _inproj_shift_kernel(x_ref, xp_ref, xn_ref, g_ref, w_ref, mu_ref, o_ref, h_scr, hp_scr, hn_scr,
                         *, seq_len):
    i = pl.program_id(0)
    j = pl.program_id(1)
    tm = x_ref.shape[0]

    @pl.when(j == 0)
    def _():
        g = g_ref[...]
        h_scr[...] = _rms(x_ref[...], g).astype(BF16)
        hp_scr[...] = _rms(xp_ref[...], g).astype(BF16)
        hn_scr[...] = _rms(xn_ref[...], g).astype(BF16)

    w = w_ref[...]
    p = _dot(h_scr[...], w)
    pp = _dot(hp_scr[...], w)[SUBLANES - 1:SUBLANES, :]
    pn = _dot(hn_scr[...], w)[0:1, :]
    first = (i * tm) % seq_len == 0
    last = ((i + 1) * tm) % seq_len == 0
    pp = jnp.where(first, 0.0, pp)
    pn = jnp.where(last, 0.0, pn)
    rid = lax.broadcasted_iota(jnp.int32, p.shape, 0)
    up = jnp.where(rid == 0, pp, pltpu.roll(p, 1, 0))
    dn = jnp.where(rid == tm - 1, pn, pltpu.roll(p, tm - 1, 0))
    o_ref[...] = (p + mu_ref[...] * (0.5 * (up + dn) - p)).astype(o_ref.dtype)


def _inproj_shift(x, g, w_bf16, mu, seq_len, tn, out_dtype):
    T, D = x.shape
    N = w_bf16.shape[1]
    tm = _tile(seq_len, (1024, 512, 256, 128, 64, 32, 16, 8))
    nb8 = T // SUBLANES
    r8 = tm // SUBLANES
    return pl.pallas_call(
        functools.partial(_inproj_shift_kernel, seq_len=seq_len),
        grid=(T // tm, N // tn),
        in_specs=[
            pl.BlockSpec((tm, D), lambda i, j: (i, 0)),
            pl.BlockSpec((SUBLANES, D), lambda i, j: (jnp.maximum(i * r8 - 1, 0), 0)),
            pl.BlockSpec((SUBLANES, D), lambda i, j: (jnp.minimum((i + 1) * r8, nb8 - 1), 0)),
            pl.BlockSpec((1, D), lambda i, j: (0, 0)),
            pl.BlockSpec((D, tn), lambda i, j: (0, j)),
            pl.BlockSpec((1, tn), lambda i, j: (0, j)),
        ],
        out_specs=pl.BlockSpec((tm, tn), lambda i, j: (i, j)),
        out_shape=jax.ShapeDtypeStruct((T, N), out_dtype),
        scratch_shapes=[pltpu.VMEM((tm, D), BF16), pltpu.VMEM((SUBLANES, D), BF16),
                        pltpu.VMEM((SUBLANES, D), BF16)],
        compiler_params=_cparams(("parallel", "arbitrary")),
        name="inproj_shift",
    )(x, x, x, g, w_bf16, mu)


def _inproj_glu_kernel(x_ref, g_ref, wa_ref, wb_ref, o_ref, h_scr):
    @pl.when(pl.program_id(1) == 0)
    def _():
        h_scr[...] = _rms(x_ref[...], g_ref[...]).astype(BF16)

    h = h_scr[...]
    a = _dot(h, wa_ref[...])
    b = _dot(h, wb_ref[...])
    o_ref[...] = (a * _sigmoid(b)).astype(o_ref.dtype)


def _inproj_glu(x, g, wa, wb):
    T, D = x.shape
    N = wa.shape[1]
    tm = _tile(T, (1024, 512, 256, 128, 64, 32, 16, 8))
    tn = _tile(N, (512, 256, 128))
    return pl.pallas_call(
        _inproj_glu_kernel,
        grid=(T // tm, N // tn),
        in_specs=[
            pl.BlockSpec((tm, D), lambda i, j: (i, 0)),
            pl.BlockSpec((1, D), lambda i, j: (0, 0)),
            pl.BlockSpec((D, tn), lambda i, j: (0, j)),
            pl.BlockSpec((D, tn), lambda i, j: (0, j)),
        ],
        out_specs=pl.BlockSpec((tm, tn), lambda i, j: (i, j)),
        out_shape=jax.ShapeDtypeStruct((T, N), BF16),
        scratch_shapes=[pltpu.VMEM((tm, D), BF16)],
        compiler_params=_cparams(("parallel", "arbitrary")),
        name="inproj_glu",
    )(x, g, wa, wb)


def _chunk_cumsum(x, bwd):
    L = WKV_CHUNK
    n = x.shape[0]
    rin = lax.broadcasted_iota(jnp.int32, (n, 1), 0) % L
    s = 1
    while s < L:
        if bwd:
            x = x + jnp.where(rin < L - s, pltpu.roll(x, n - s, 0), 0.0)
        else:
            x = x + jnp.where(rin >= s, pltpu.roll(x, s, 0), 0.0)
        s *= 2
    return x


WKV_ISSUE_ORDER = "fffffbffbffffff" + "bf" * 4


def _wkv_bidir_kernel(zr_f, zk_f, zv_f, dd_f, ad_f, zr_b, zk_b, zv_b, dd_b, ad_b, w0_ref, w2h_ref, w2l_ref,
                      a0_ref, a2_ref, kk_ref, ka_ref, rk_ref, of_ref, bf_ref, ob_ref, bb_ref, state_ref,
                      *, n_pairs):
    L = WKV_CHUNK
    PW = 2 * RWKV_HEAD
    TT = zr_f.shape[0]
    n_chunks = TT // L

    @pl.when(pl.program_id(2) == 0)
    def _():
        state_ref[...] = jnp.zeros_like(state_ref)

    row = lax.broadcasted_iota(jnp.int32, (L, 2 * L), 0)
    col = lax.broadcasted_iota(jnp.int32, (L, 2 * L), 1) % L
    ipk = jnp.where(col == row, 1.0, 0.0)
    lane = lax.broadcasted_iota(jnp.int32, (1, PW), 1)
    m0 = lane < RWKV_HEAD
    lane2 = lax.broadcasted_iota(jnp.int32, (1, 2 * PW), 1) % PW
    m0w = lane2 < RWKV_HEAD
    srow = lax.broadcasted_iota(jnp.int32, (PW, PW), 0)
    scol = lax.broadcasted_iota(jnp.int32, (PW, PW), 1)
    same_head = (srow // RWKV_HEAD) == (scol // RWKV_HEAD)
    eye = srow == scol
    HSW = min(zr_f.shape[1], 2 * PW)
    hrow = lax.broadcasted_iota(jnp.int32, (HSW, HSW), 0)
    hcol = lax.broadcasted_iota(jnp.int32, (HSW, HSW), 1)
    head_ones_w = jnp.where((hrow // RWKV_HEAD) == (hcol // RWKV_HEAD), 1.0, 0.0).astype(BF16)

    def bd(x):
        return jnp.concatenate([jnp.where(m0, x, 0.0), jnp.where(m0, 0.0, x)], axis=0).astype(BF16)

    def bd2(x):
        return jnp.concatenate([jnp.where(m0w, x, 0.0), jnp.where(m0w, 0.0, x)], axis=0).astype(BF16)

    def head_sum(x, two_pass):
        n = x.shape[0]
        outs = []
        for c in range(x.shape[1] // HSW):
            xc = x[:, c * HSW:(c + 1) * HSW]
            if two_pass:
                hi, lo = _split2(xc)
                res = _dot(jnp.concatenate([hi, lo], axis=0), head_ones_w)
                outs.append(res[:n] + res[n:])
            else:
                outs.append(_dot(xc.astype(BF16), head_ones_w))
        return jnp.concatenate(outs, axis=1) if len(outs) > 1 else outs[0]

    def direction(d, zr_ref, zk_ref, zv_ref, dd_ref, ad_ref, o_ref, bonus_ref):
        bwd = d == 1
        strict = (col > row) if bwd else (col < row)
        incl = (col >= row) if bwd else (col <= row)
        r = zr_ref[...].astype(F32)
        k = zk_ref[...].astype(F32)
        v = zv_ref[...].astype(F32)
        dlin = w0_ref[d] + _dot3(jnp.tanh(dd_ref[...]), w2h_ref[d], w2l_ref[d])
        iclr_lin = a0_ref[d] + _dot(ad_ref[...].astype(BF16), a2_ref[d])
        yield
        lw = -DECAY_SCALE * _sigmoid(dlin)
        iclr = _sigmoid(iclr_lin)
        kkr = k * kk_ref[...]
        kmod = k * (1.0 + (iclr - 1.0) * ka_ref[...])
        ss = head_sum(kkr * kkr, False)
        bonus_ref[...] = (head_sum(r * kmod * rk_ref[...], True) * v).astype(bonus_ref.dtype)
        yield
        cum = _chunk_cumsum(lw, bwd)
        e_ng = jnp.exp(-cum)
        kk = kkr * lax.rsqrt(jnp.maximum(ss, 1e-24))
        b_all = kk * iclr
        ah_all = -kk * jnp.exp(cum - lw)
        rh_all = r * jnp.exp(cum)
        bh_all = b_all * e_ng
        kh_all = kmod * e_ng

        probs = []
        for ci in (range(n_chunks - 1, -1, -1) if bwd else range(n_chunks)):
            for p in range(n_pairs):
                sl = slice(p * PW, (p + 1) * PW)
                rs = slice(ci * L, (ci + 1) * L)
                q = dict(p=p, ci=ci, sl=sl, rs=rs, ah=ah_all[rs, sl], rh=rh_all[rs, sl], v=v[rs, sl])
                lhs = jnp.concatenate([q["ah"], q["rh"]], axis=0).astype(BF16)
                g = _dot_nt(lhs, jnp.concatenate([bd(bh_all[rs, sl]), bd(kh_all[rs, sl])], axis=0))
                a32 = jnp.where(strict, g[:L, :2 * L], 0.0)
                q["a"] = a32.astype(BF16)
                q["t"] = (ipk + a32).astype(BF16)
                q["rb"] = jnp.where(incl, g[L:, :2 * L], 0.0).astype(BF16)
                q["akrk"] = jnp.concatenate([jnp.where(strict, g[:L, 2 * L:], 0.0),
                                             jnp.where(incl, g[L:, 2 * L:], 0.0)], axis=0).astype(BF16)
                probs.append(q)
        yield
        for q in probs:
            q["kv"] = _dot(q["akrk"], bd(q["v"]))
        yield
        for q in probs:
            q["ai"] = _dot(q["a"], bd(q["a"])).astype(BF16)
        yield
        for lvl in range(1, 6):
            more = lvl < 5
            for q in probs:
                lhs = jnp.concatenate([q["t"], q["ai"]], axis=0) if more else q["t"]
                res = _dot(lhs, bd(q["ai"]))
                q["t"] = (q["t"].astype(F32) + res[:L]).astype(BF16)
                if more:
                    q["ai"] = res[L:].astype(BF16)
            yield
        for q in probs:
            q["rbt"] = _dot(q["rb"], bd(q["t"])).astype(BF16)
        yield
        for q in probs:
            y0 = jnp.concatenate([q["ah"], q["kv"][:L]], axis=1)
            q["ry"] = _dot(jnp.concatenate([q["t"], q["rbt"]], axis=0), bd2(y0))
        yield
        for q in probs:
            rs, sl, ry = q["rs"], q["sl"], q["ry"]
            end = q["ci"] * L if bwd else q["ci"] * L + L - 1
            tot = cum[end:end + 1, sl]
            e_rm = jnp.exp(tot - cum[rs, sl])
            q["rt"] = q["rh"] + ry[L:, :PW]
            q["ob"] = ry[L:, PW:] + q["kv"][L:]
            lhs_t = jnp.concatenate([b_all[rs, sl] * e_rm, kmod[rs, sl] * e_rm], axis=0).astype(BF16)
            rhs_t = jnp.concatenate(
                [ry[:L], jnp.concatenate([jnp.zeros((L, PW), F32), q["v"]], axis=1)], axis=0).astype(BF16)
            mn = _dot_tn(lhs_t, rhs_t)
            q["mm"] = jnp.where(eye, jnp.exp(tot), 0.0) + jnp.where(same_head, mn[:, :PW], 0.0)
            q["nn"] = jnp.where(same_head, mn[:, PW:], 0.0)
        yield
        states = [state_ref[d, p] for p in range(n_pairs)]
        for n, q in enumerate(probs):
            p = q["p"]
            res = _dot(jnp.concatenate([q["rt"], q["mm"]], axis=0).astype(BF16), states[p].astype(BF16))
            o_ref[q["rs"], q["sl"]] = (res[:L] + q["ob"]).astype(o_ref.dtype)
            states[p] = res[L:] + q["nn"]
            if p == n_pairs - 1 and n < len(probs) - 1:
                yield
        for p in range(n_pairs):
            state_ref[d, p] = states[p]

    gens = {"f": direction(0, zr_f, zk_f, zv_f, dd_f, ad_f, of_ref, bf_ref),
            "b": direction(1, zr_b, zk_b, zv_b, dd_b, ad_b, ob_ref, bb_ref)}
    order = list(WKV_ISSUE_ORDER)
    while gens:
        key = order.pop(0) if order else next(iter(gens))
        if key in gens and next(gens[key], "done") == "done":
            del gens[key]


def _wkv_bidir(z_rkv, z_lora, decay_w0, w2h, w2l, iclr_a0, a2_bf16, k_k, k_a, r_k, n_seq, seq_len):
    T = z_rkv.shape[0]
    W = z_rkv.shape[1] // 3
    PW = 2 * RWKV_HEAD
    n_pairs = _tile(W // PW, (4, 2, 1))
    GW = n_pairs * PW
    n_groups = W // GW
    TT = _tile(seq_len, (512, 256, 128, 64))
    nT = seq_len // TT
    fwd_t = lambda b, c: b * nT + c
    bwd_t = lambda b, c: b * nT + nT - 1 - c
    in_specs = []
    for tile, d in ((fwd_t, 0), (bwd_t, 1)):
        in_specs += [pl.BlockSpec((TT, GW), lambda b, g, c, tile=tile, off=off: (tile(b, c), off * n_groups + g))
                     for off in range(3)]
        in_specs += [pl.BlockSpec((TT, LORA_PAD), lambda b, g, c, tile=tile, d=d: (tile(b, c), d)),
                     pl.BlockSpec((TT, LORA_PAD), lambda b, g, c, tile=tile, d=d: (tile(b, c), 2 + d))]
    pspec = pl.BlockSpec((2, 1, GW), lambda b, g, c: (0, 0, g))
    lspec = pl.BlockSpec((2, LORA_PAD, GW), lambda b, g, c: (0, 0, g))
    cspec = pl.BlockSpec((1, GW), lambda b, g, c: (0, g))
    in_specs += [pspec, lspec, lspec, pspec, lspec, cspec, cspec, cspec]
    ospec_f = pl.BlockSpec((TT, GW), lambda b, g, c: (fwd_t(b, c), g))
    ospec_b = pl.BlockSpec((TT, GW), lambda b, g, c: (bwd_t(b, c), g))
    out_t = jax.ShapeDtypeStruct((T, W), BF16)
    return pl.pallas_call(
        functools.partial(_wkv_bidir_kernel, n_pairs=n_pairs),
        grid=(n_seq, n_groups, nT),
        in_specs=in_specs,
        out_specs=[ospec_f, ospec_f, ospec_b, ospec_b],
        out_shape=[out_t, out_t, out_t, out_t],
        scratch_shapes=[pltpu.VMEM((2, n_pairs, PW, PW), F32)],
        compiler_params=_cparams(("parallel", "parallel", "arbitrary")),
        name="wkv_scan_bidir",
    )(*([z_rkv] * 3 + [z_lora] * 2) * 2, decay_w0, w2h, w2l, iclr_a0, a2_bf16, k_k, k_a, r_k)


def _rwkv_post_kernel(of_ref, ob_ref, bf_ref, bb_ref, gd_ref, g2_ref, lg_ref, lb_ref, out_ref):
    W = out_ref.shape[1]
    PW = min(W, 4 * RWKV_HEAD)
    srow = lax.broadcasted_iota(jnp.int32, (PW, PW), 0)
    scol = lax.broadcasted_iota(jnp.int32, (PW, PW), 1)
    head_ones = jnp.where((srow // RWKV_HEAD) == (scol // RWKV_HEAD), 1.0, 0.0).astype(BF16)
    inv = 1.0 / RWKV_HEAD
    n = out_ref.shape[0]

    def head_mean(x):
        hi, lo = _split2(x)
        res = _dot(jnp.concatenate([hi, lo], axis=0), head_ones)
        return (res[:n] + res[n:]) * inv

    gate = _dot(_sigmoid(gd_ref[...]).astype(BF16), g2_ref[...])
    for p in range(W // PW):
        sl = slice(p * PW, (p + 1) * PW)
        o = of_ref[:, sl].astype(F32) + ob_ref[:, sl].astype(F32)
        oc = o - head_mean(o)
        var = head_mean(oc * oc)
        y = oc * lax.rsqrt(var + LNX_EPS) * lg_ref[:, sl] + lb_ref[:, sl]
        y = y + bf_ref[:, sl].astype(F32) + bb_ref[:, sl].astype(F32)
        out_ref[:, sl] = (y * gate[:, sl]).astype(out_ref.dtype)


def _rwkv_post(o_f, o_b, bonus_f, bonus_b, z_lora, g2, lnx_g, lnx_b):
    T, W = o_f.shape
    tm = _tile(T, (512, 256, 128, 64, 32, 16, 8))
    GL = g2.shape[0]
    gd_blk = (4 * LORA_PAD) // GL
    tspec = pl.BlockSpec((tm, W), lambda i: (i, 0))
    return pl.pallas_call(
        _rwkv_post_kernel,
        grid=(T // tm,),
        in_specs=[
            tspec, tspec, tspec, tspec,
            pl.BlockSpec((tm, GL), lambda i: (i, gd_blk)),
            pl.BlockSpec((GL, W), lambda i: (0, 0)),
            pl.BlockSpec((1, W), lambda i: (0, 0)),
            pl.BlockSpec((1, W), lambda i: (0, 0)),
        ],
        out_specs=pl.BlockSpec((tm, W), lambda i: (i, 0)),
        out_shape=jax.ShapeDtypeStruct((T, W), BF16),
        compiler_params=_cparams(("parallel",)),
        name="rwkv_post",
    )(o_f, o_b, bonus_f, bonus_b, z_lora, g2, lnx_g, lnx_b)


def _conv_kernel(u_ref, up_ref, un_ref, w_ref, b_ref, lg_ref, lb_ref, o_ref, ext_scr, *, seq_len, halo):
    i = pl.program_id(0)
    tm = u_ref.shape[0]
    K = w_ref.shape[0]
    first = (i * tm) % seq_len == 0
    last = ((i + 1) * tm) % seq_len == 0
    n_ext = tm + 2 * halo
    ext = jnp.concatenate([jnp.where(first, 0.0, up_ref[...].astype(F32)), u_ref[...].astype(F32),
                           jnp.where(last, 0.0, un_ref[...].astype(F32))], axis=0)
    ext_scr[0] = ext
    for r in range(1, SUBLANES):
        ext_scr[r] = pltpu.roll(ext, n_ext - r, 0)
    sub = min(tm, 32)
    base = halo - K // 2
    for s in range(tm // sub):
        acc = jnp.zeros((sub, u_ref.shape[1]), F32)
        for j in range(K):
            off = base + j
            row0 = s * sub + (off // SUBLANES) * SUBLANES
            acc = acc + w_ref[j:j + 1, :] * ext_scr[off % SUBLANES, row0:row0 + sub, :]
        acc = acc + b_ref[...]
        mean = jnp.mean(acc, axis=-1, keepdims=True)
        xc = acc - mean
        var = jnp.mean(xc * xc, axis=-1, keepdims=True)
        y = xc * lax.rsqrt(var + LN_EPS) * lg_ref[...] + lb_ref[...]
        o_ref[s * sub:(s + 1) * sub, :] = (y * _sigmoid(y)).astype(o_ref.dtype)


def _conv(u, conv_w, conv_b, ln_g, ln_b, seq_len):
    T, C = u.shape
    K = conv_w.shape[0]
    halo = 16
    assert K // 2 <= halo
    tm = _tile(seq_len, (256, 128, 64, 32, 16))
    rh = tm // halo
    nbh = T // halo
    return pl.pallas_call(
        functools.partial(_conv_kernel, seq_len=seq_len, halo=halo),
        grid=(T // tm,),
        in_specs=[
            pl.BlockSpec((tm, C), lambda i: (i, 0)),
            pl.BlockSpec((halo, C), lambda i: (jnp.maximum(i * rh - 1, 0), 0)),
            pl.BlockSpec((halo, C), lambda i: (jnp.minimum((i + 1) * rh, nbh - 1), 0)),
            pl.BlockSpec((K, C), lambda i: (0, 0)),
            pl.BlockSpec((1, C), lambda i: (0, 0)),
            pl.BlockSpec((1, C), lambda i: (0, 0)),
            pl.BlockSpec((1, C), lambda i: (0, 0)),
        ],
        out_specs=pl.BlockSpec((tm, C), lambda i: (i, 0)),
        out_shape=jax.ShapeDtypeStruct((T, C), BF16),
        scratch_shapes=[pltpu.VMEM((SUBLANES, tm + 2 * halo, C), F32)],
        compiler_params=_cparams(("parallel",)),
        name="conformer_conv",
    )(u, u, u, conv_w, conv_b, ln_g, ln_b)


def _merge_kernel(x_ref, g_ref, orw_ref, ocv_ref, wg1_ref, wg2_ref, wb1_ref, wb2_ref, o_ref, h_scr):
    @pl.when(pl.program_id(1) == 0)
    def _():
        h_scr[...] = _rms(x_ref[...], g_ref[...]).astype(BF16)

    h = h_scr[...]
    g1 = _sigmoid(_dot(h, wg1_ref[...]))
    g2 = _sigmoid(_dot(h, wg2_ref[...]))
    y1 = _dot(orw_ref[...], wb1_ref[...])
    y2 = _dot(ocv_ref[...], wb2_ref[...])
    o_ref[...] = (g1 * y1 + g2 * y2).astype(o_ref.dtype)


def _merge(x, g, o_rwkv, o_conv, wg1, wg2, wb1, wb2):
    T, D = x.shape
    W = o_rwkv.shape[1]
    C = o_conv.shape[1]
    tm = _tile(T, (512, 256, 128, 64, 32, 16, 8))
    tn = _tile(D, (512, 256, 128))
    return pl.pallas_call(
        _merge_kernel,
        grid=(T // tm, D // tn),
        in_specs=[
            pl.BlockSpec((tm, D), lambda i, j: (i, 0)),
            pl.BlockSpec((1, D), lambda i, j: (0, 0)),
            pl.BlockSpec((tm, W), lambda i, j: (i, 0)),
            pl.BlockSpec((tm, C), lambda i, j: (i, 0)),
            pl.BlockSpec((D, tn), lambda i, j: (0, j)),
            pl.BlockSpec((D, tn), lambda i, j: (0, j)),
            pl.BlockSpec((W, tn), lambda i, j: (0, j)),
            pl.BlockSpec((C, tn), lambda i, j: (0, j)),
        ],
        out_specs=pl.BlockSpec((tm, tn), lambda i, j: (i, j)),
        out_shape=jax.ShapeDtypeStruct((T, D), BF16),
        scratch_shapes=[pltpu.VMEM((tm, D), BF16)],
        compiler_params=_cparams(("parallel", "arbitrary")),
        name="merge_gates",
    )(x, g, o_rwkv, o_conv, wg1, wg2, wb1, wb2)


def _mm_res_kernel(a_ref, w_ref, res_ref, o_ref):
    o_ref[...] = res_ref[...] + _dot(a_ref[...], w_ref[...])


def _mm_res(a, w, res):
    T, K = a.shape
    N = w.shape[1]
    tm = _tile(T, (1024, 512, 256, 128, 64, 32, 16, 8))
    tn = _tile(N, (1024, 512, 256, 128))
    return pl.pallas_call(
        _mm_res_kernel,
        grid=(T // tm, N // tn),
        in_specs=[
            pl.BlockSpec((tm, K), lambda i, j: (i, 0)),
            pl.BlockSpec((K, tn), lambda i, j: (0, j)),
            pl.BlockSpec((tm, tn), lambda i, j: (i, j)),
        ],
        out_specs=pl.BlockSpec((tm, tn), lambda i, j: (i, j)),
        out_shape=jax.ShapeDtypeStruct((T, N), F32),
        compiler_params=_cparams(("parallel", "parallel")),
        name="proj_residual",
    )(a, w, res)


def _mm_norm_kernel(x_ref, g_ref, w_ref, o_ref, h_scr):
    @pl.when(pl.program_id(1) == 0)
    def _():
        h_scr[...] = _rms(x_ref[...], g_ref[...]).astype(BF16)

    o_ref[...] = _dot(h_scr[...], w_ref[...]).astype(o_ref.dtype)


def _mm_norm(x, g, w):
    T, D = x.shape
    N = w.shape[1]
    tm = _tile(T, (1024, 512, 256, 128, 64, 32, 16, 8))
    tn = _tile(N, (1024, 512, 256, 128))
    return pl.pallas_call(
        _mm_norm_kernel,
        grid=(T // tm, N // tn),
        in_specs=[
            pl.BlockSpec((tm, D), lambda i, j: (i, 0)),
            pl.BlockSpec((1, D), lambda i, j: (0, 0)),
            pl.BlockSpec((D, tn), lambda i, j: (0, j)),
        ],
        out_specs=pl.BlockSpec((tm, tn), lambda i, j: (i, j)),
        out_shape=jax.ShapeDtypeStruct((T, N), BF16),
        scratch_shapes=[pltpu.VMEM((tm, D), BF16)],
        compiler_params=_cparams(("parallel", "arbitrary")),
        name="norm_proj",
    )(x, g, w)


def _xattn_kernel(q_ref, k_ref, v_ref, o_ref):
    D = q_ref.shape[1]
    hd = D // XATTN_HEADS
    scale = hd ** -0.5
    for h in range(XATTN_HEADS):
        sl = slice(h * hd, (h + 1) * hd)
        s = _dot_nt(q_ref[:, sl], k_ref[:, sl]) * scale
        s = s - jnp.max(s, axis=-1, keepdims=True)
        e = jnp.exp(s)
        p = e / jnp.sum(e, axis=-1, keepdims=True)
        o_ref[:, sl] = _dot(p.astype(BF16), v_ref[:, sl]).astype(o_ref.dtype)


def _xattn(q, kv, n_seq, seq_len, n_mem):
    T, D = q.shape
    tm = _tile(seq_len, (512, 256, 128, 64, 32, 16, 8))
    nT = seq_len // tm
    return pl.pallas_call(
        _xattn_kernel,
        grid=(n_seq, nT),
        in_specs=[
            pl.BlockSpec((tm, D), lambda b, i: (b * nT + i, 0)),
            pl.BlockSpec((n_mem, D), lambda b, i: (b, 0)),
            pl.BlockSpec((n_mem, D), lambda b, i: (b, 1)),
        ],
        out_specs=pl.BlockSpec((tm, D), lambda b, i: (b * nT + i, 0)),
        out_shape=jax.ShapeDtypeStruct((T, D), BF16),
        compiler_params=_cparams(("parallel", "parallel")),
        name="cross_attention",
    )(q, kv, kv)


def _router_kernel(x_ref, g_ref, wh_ref, wl_ref, b_ref, hf_ref, e_ref, gate_ref):
    hf = _rms(x_ref[...], g_ref[...])
    hf_ref[...] = hf.astype(hf_ref.dtype)
    logits = _dot3(hf, wh_ref[...], wl_ref[...]) + b_ref[...]
    E = logits.shape[1]
    eid = lax.broadcasted_iota(jnp.int32, logits.shape, 1).astype(F32)
    work = logits
    vals = []
    idxs = []
    for _ in range(TOP_K):
        m = jnp.max(work, axis=-1, keepdims=True)
        idx = jnp.min(jnp.where(work == m, eid, float(E)), axis=-1, keepdims=True)
        vals.append(m)
        idxs.append(idx.astype(jnp.int32))
        work = jnp.where(eid == idx, -jnp.inf, work)
    ex = [jnp.exp(vv - vals[0]) for vv in vals]
    den = ex[0]
    for t in ex[1:]:
        den = den + t
    kid = lax.broadcasted_iota(jnp.int32, (logits.shape[0], TOP_K), 1)
    e_out = jnp.zeros((logits.shape[0], TOP_K), jnp.int32)
    g_out = jnp.zeros((logits.shape[0], TOP_K), F32)
    for t in range(TOP_K):
        e_out = jnp.where(kid == t, idxs[t], e_out)
        g_out = jnp.where(kid == t, ex[t] / den, g_out)
    e_ref[...] = e_out
    gate_ref[...] = g_out


def _router(x, g, w_router, b_router):
    T, D = x.shape
    E = w_router.shape[1]
    wh, wl = _split2(w_router)
    tm = _tile(T, (512, 256, 128, 64, 32, 16, 8))
    return pl.pallas_call(
        _router_kernel,
        grid=(T // tm,),
        in_specs=[
            pl.BlockSpec((tm, D), lambda i: (i, 0)),
            pl.BlockSpec((1, D), lambda i: (0, 0)),
            pl.BlockSpec((D, E), lambda i: (0, 0)),
            pl.BlockSpec((D, E), lambda i: (0, 0)),
            pl.BlockSpec((1, E), lambda i: (0, 0)),
        ],
        out_specs=[
            pl.BlockSpec((tm, D), lambda i: (i, 0)),
            pl.BlockSpec((tm, TOP_K), lambda i: (i, 0)),
            pl.BlockSpec((tm, TOP_K), lambda i: (i, 0)),
        ],
        out_shape=[jax.ShapeDtypeStruct((T, D), BF16),
                   jax.ShapeDtypeStruct((T, TOP_K), jnp.int32),
                   jax.ShapeDtypeStruct((T, TOP_K), F32)],
        compiler_params=_cparams(("parallel",)),
        name="router",
    )(x, g, wh, wl, b_router)


def _expert_kernel(te_ref, tv_ref, x_ref, wg_ref, wu_ref, bg_ref, bu_ref, wd_ref, bd_ref, o_ref, act_ref, *, nf):
    i = pl.program_id(0)
    s = pl.program_id(1)
    tf = wg_ref.shape[1]
    valid = tv_ref[i] > 0

    @pl.when(valid & (s < nf))
    def _():
        x = x_ref[...]
        g = _dot(x, wg_ref[...].astype(BF16)) + bg_ref[...]
        u = _dot(x, wu_ref[...].astype(BF16)) + bu_ref[...]
        g = jnp.minimum(g, SWIGLU_LIMIT)
        u = jnp.clip(u, -SWIGLU_LIMIT, SWIGLU_LIMIT)
        act = ((u + 1.0) * (g * _sigmoid(SWIGLU_ALPHA * g))).astype(BF16)
        for f in range(nf):
            @pl.when(s == f)
            def _(f=f):
                act_ref[:, f * tf:(f + 1) * tf] = act

    @pl.when(valid & (s >= nf))
    def _():
        y = _dot(act_ref[...], wd_ref[...].astype(BF16)) + bd_ref[...]
        o_ref[...] = y.astype(o_ref.dtype)

    @pl.when(jnp.logical_not(valid) & (s >= nf))
    def _():
        o_ref[...] = jnp.zeros_like(o_ref)


def _experts(xb, tile_e, tile_valid, w_gu, b_gu, w_dn, b_dn, tm):
    R, D = xb.shape
    E, _, F2 = w_gu.shape
    F = F2 // 2
    tf = _tile(F, (512, 256, 128))
    tn = _tile(D, (512, 256, 128))
    nf = F // tf
    nn = D // tn
    n_tiles = R // tm
    b_gu3 = b_gu.reshape(E, 1, F2)
    b_dn3 = b_dn.reshape(E, 1, D)

    def fa(i, s, tv):
        return jnp.where(tv[i] > 0, jnp.minimum(s, nf - 1), nf - 1)

    def nb(i, s, tv):
        return jnp.where(tv[i] > 0, jnp.clip(s - nf, 0, nn - 1), nn - 1)

    grid_spec = pltpu.PrefetchScalarGridSpec(
        num_scalar_prefetch=2,
        grid=(n_tiles, nf + nn),
        in_specs=[
            pl.BlockSpec((tm, D), lambda i, s, te, tv: (i, 0)),
            pl.BlockSpec((None, D, tf), lambda i, s, te, tv: (te[i], 0, fa(i, s, tv))),
            pl.BlockSpec((None, D, tf), lambda i, s, te, tv: (te[i], 0, nf + fa(i, s, tv))),
            pl.BlockSpec((None, 1, tf), lambda i, s, te, tv: (te[i], 0, fa(i, s, tv))),
            pl.BlockSpec((None, 1, tf), lambda i, s, te, tv: (te[i], 0, nf + fa(i, s, tv))),
            pl.BlockSpec((None, F, tn), lambda i, s, te, tv: (te[i], 0, nb(i, s, tv))),
            pl.BlockSpec((None, 1, tn), lambda i, s, te, tv: (te[i], 0, nb(i, s, tv))),
        ],
        out_specs=pl.BlockSpec((tm, tn), lambda i, s, te, tv: (i, jnp.clip(s - nf, 0, nn - 1))),
        scratch_shapes=[pltpu.VMEM((tm, F), BF16)],
    )
    return pl.pallas_call(
        functools.partial(_expert_kernel, nf=nf),
        grid_spec=grid_spec,
        out_shape=jax.ShapeDtypeStruct((R, D), BF16),
        compiler_params=_cparams(("arbitrary", "arbitrary")),
        name="moe_experts",
    )(tile_e, tile_valid, xb, w_gu, w_gu, b_gu3, b_gu3, w_dn, b_dn3)


def _combine_kernel(x_ref, y_ref, gate_ref, g_ref, o_head_ref, o_tail_ref, *, n_head_tiles):
    acc = x_ref[...]
    gate = gate_ref[...]
    for t in range(TOP_K):
        acc = acc + gate[:, t:t + 1] * y_ref[t].astype(F32)
    y = _rms(acc, g_ref[...])
    i = pl.program_id(0)

    @pl.when(i < n_head_tiles)
    def _():
        o_head_ref[...] = y

    @pl.when(i >= n_head_tiles)
    def _():
        o_tail_ref[...] = y


def _combine(x, y4, gate, final_g, t_head):
    T, D = x.shape
    tm = _tile(math.gcd(t_head, T - t_head), (512, 256, 128, 64, 32, 16, 8))
    nh = t_head // tm
    return pl.pallas_call(
        functools.partial(_combine_kernel, n_head_tiles=nh),
        grid=(T // tm,),
        in_specs=[
            pl.BlockSpec((tm, D), lambda i: (i, 0)),
            pl.BlockSpec((TOP_K, tm, D), lambda i: (0, i, 0)),
            pl.BlockSpec((tm, TOP_K), lambda i: (i, 0)),
            pl.BlockSpec((1, D), lambda i: (0, 0)),
        ],
        out_specs=[pl.BlockSpec((tm, D), lambda i: (jnp.minimum(i, nh - 1), 0)),
                   pl.BlockSpec((tm, D), lambda i: (jnp.maximum(i - nh, 0), 0))],
        out_shape=[jax.ShapeDtypeStruct((t_head, D), F32), jax.ShapeDtypeStruct((T - t_head, D), F32)],
        compiler_params=_cparams(("arbitrary",)),
        name="moe_combine_norm",
    )(x, y4, gate, final_g)


def _pad_rows(w, n):
    return jnp.pad(w, ((0, 0),) * (w.ndim - 2) + ((0, n - w.shape[-2]), (0, 0)))


def _pad_cols(w, n):
    return jnp.pad(w, ((0, 0),) * (w.ndim - 1) + ((0, n - w.shape[-1]),))


def _moe_tile_rows(n_assign, n_experts):
    for tm in (1024, 512, 256, 128, 64, 32, 16, 8):
        if n_assign >= 4 * n_experts * tm or tm == 8:
            return tm


def _layer(x, mem, n_seq, seq_len, t_head, norm_mix_g, w_in, shift_mu, decay_w0, decay_w2, iclr_a0, iclr_a2,
           gate_g2, k_k, k_a, r_k, lnx_g, lnx_b, conv_w, conv_b, conv_ln_g, conv_ln_b, w_branch, w_o,
           norm_x_g, norm_mem_g, w_xq, w_xkv, w_xo, norm_ffn_g, w_router, b_router, w_gu, b_gu,
           w_dn, b_dn, final_g):
    T, D = x.shape
    W = k_k.shape[0]
    DL = decay_w2.shape[1]
    AL = iclr_a2.shape[1]
    GL = gate_g2.shape[0]
    C = conv_w.shape[1]
    E = w_router.shape[1]
    n_mem = mem.shape[0] // n_seq
    row = lambda v: v.reshape(1, -1)

    o3 = 3 * W
    o4 = o3 + 2 * DL
    o5 = o4 + 2 * AL
    o6 = o5 + GL
    seg = lambda m, a, b, n: _pad_cols(m[..., a:b], n)
    lora_cols = lambda m: jnp.concatenate(
        [seg(m, o3, o3 + DL, LORA_PAD), seg(m, o3 + DL, o4, LORA_PAD), seg(m, o4, o4 + AL, LORA_PAD),
         seg(m, o4 + AL, o5, LORA_PAD), m[..., o5:o6]], axis=-1)
    w_rkv = w_in[:, :o3].astype(BF16)
    w_lora = lora_cols(w_in).astype(BF16)
    mu_rkv = row(shift_mu[:o3])
    mu_lora = row(lora_cols(shift_mu))
    w_ca = w_in[:, o6:o6 + C].astype(BF16)
    w_cb = w_in[:, o6 + C:o6 + 2 * C].astype(BF16)
    w_g1 = w_in[:, o6 + 2 * C:o6 + 2 * C + D].astype(BF16)
    w_g2 = w_in[:, o6 + 2 * C + D:].astype(BF16)
    g_mix = row(norm_mix_g)

    z_rkv = _inproj_shift(x, g_mix, w_rkv, mu_rkv, seq_len, _tile(o3, (1024, 512, 256, 128)), BF16)
    z_lora = _inproj_shift(x, g_mix, w_lora, mu_lora, seq_len, w_lora.shape[1], F32)
    u = _inproj_glu(x, g_mix, w_ca, w_cb)

    w2h, w2l = _split2(_pad_rows(decay_w2, LORA_PAD))
    a2_bf16 = _pad_rows(iclr_a2, LORA_PAD).astype(BF16)
    wkv_args = (z_rkv, z_lora, decay_w0.reshape(2, 1, W), w2h, w2l, iclr_a0.reshape(2, 1, W), a2_bf16,
                row(k_k), row(k_a), row(r_k), n_seq, seq_len)
    o_f, bonus_f, o_b, bonus_b = _wkv_bidir(*wkv_args)
    o_rwkv = _rwkv_post(o_f, o_b, bonus_f, bonus_b, z_lora, gate_g2.astype(BF16), row(lnx_g), row(lnx_b))
    o_conv = _conv(u, conv_w, row(conv_b), row(conv_ln_g), row(conv_ln_b), seq_len)

    merged = _merge(x, g_mix, o_rwkv, o_conv, w_g1, w_g2, w_branch[:W].astype(BF16),
                    w_branch[W:].astype(BF16))
    x1 = _mm_res(merged, w_o.astype(BF16), x)

    q = _mm_norm(x1, row(norm_x_g), w_xq.astype(BF16))
    kv = _mm_norm(mem, row(norm_mem_g), w_xkv.astype(BF16))
    att = _xattn(q, kv, n_seq, seq_len, n_mem)
    x2 = _mm_res(att, w_xo.astype(BF16), x1)

    hf, top_e, gate = _router(x2, row(norm_ffn_g), w_router, row(b_router))

    A = T * TOP_K
    tm_e = _moe_tile_rows(A, E)
    n_tiles = (A + E * (tm_e - 1) + tm_e - 1) // tm_e
    n_rows = n_tiles * tm_e
    flat_e = top_e.reshape(A)
    iota_a = jnp.arange(A, dtype=jnp.int32)
    sorted_e, order = lax.sort((flat_e, iota_a), num_keys=1)
    _, rank = lax.sort((order, iota_a), num_keys=1)
    experts = jnp.arange(E, dtype=jnp.int32)
    start = jnp.searchsorted(sorted_e, experts, side='left').astype(jnp.int32)
    counts = jnp.searchsorted(sorted_e, experts, side='right').astype(jnp.int32) - start
    padded = (counts + tm_e - 1) // tm_e * tm_e
    pad_end = jnp.cumsum(padded)
    pad_start = pad_end - padded
    dest_of = pad_start[flat_e] + rank - start[flat_e]
    tile_start = jnp.arange(n_tiles, dtype=jnp.int32) * tm_e
    tile_e = jnp.minimum(jnp.searchsorted(pad_end, tile_start, side='right'), E - 1).astype(jnp.int32)
    tile_valid = jnp.clip(pad_start[tile_e] + counts[tile_e] - tile_start, 0, tm_e).astype(jnp.int32)
    rows = jnp.arange(n_rows, dtype=jnp.int32)
    row_e = jnp.repeat(tile_e, tm_e)
    row_off = rows - pad_start[row_e]
    src = order[jnp.minimum(start[row_e] + row_off, A - 1)] // TOP_K
    row_tok = jnp.where(row_off < counts[row_e], src, rows % T)

    xb = hf.at[row_tok].get(mode='promise_in_bounds')
    yb = _experts(xb, tile_e, tile_valid, w_gu, b_gu, w_dn, b_dn, tm_e)
    dest_slot_major = dest_of.reshape(T, TOP_K).T.reshape(A)
    y4 = yb.at[dest_slot_major].get(mode='promise_in_bounds').reshape(TOP_K, T, D)
    return _combine(x2, y4, gate, row(final_g), t_head)


def kernel(x_prompt, x_sample, mem_prompt, mem_sample, norm_mix_g, w_in, shift_mu, decay_w0, decay_w2,
           iclr_a0, iclr_a2, gate_g2, k_k, k_a, r_k, lnx_g, lnx_b, conv_w, conv_b, conv_ln_g, conv_ln_b,
           w_branch, w_o, norm_x_g, norm_mem_g, w_xq, w_xkv, w_xo, norm_ffn_g, w_router, b_router,
           w_gu, b_gu, w_dn, b_dn, final_g):
    layer_params = (norm_mix_g, w_in, shift_mu, decay_w0, decay_w2, iclr_a0, iclr_a2, gate_g2, k_k, k_a,
                    r_k, lnx_g, lnx_b, conv_w, conv_b, conv_ln_g, conv_ln_b, w_branch, w_o, norm_x_g,
                    norm_mem_g, w_xq, w_xkv, w_xo, norm_ffn_g, w_router, b_router, w_gu, b_gu, w_dn, b_dn)
    assert all(p.shape[0] == 1 for p in layer_params), "single-layer stack expected"
    bp, seq_len, D = x_prompt.shape
    bs = x_sample.shape[0]
    assert x_sample.shape[1] == seq_len
    n_seq = bp + bs
    x = jnp.concatenate([x_prompt, x_sample], axis=0).reshape(n_seq * seq_len, D)
    mem = jnp.concatenate([mem_prompt, mem_sample], axis=0).reshape(-1, D)
    y_p, y_s = _layer(x, mem, n_seq, seq_len, bp * seq_len, *[p[0] for p in layer_params], final_g)
    return y_p.reshape(bp, seq_len, D), y_s.reshape(bs, seq_len, D)
```

```python
import functools
import math

import jax
import jax.numpy as jnp
from jax import lax
from jax.experimental import pallas as pl
from jax.experimental.pallas import tpu as pltpu

F32 = jnp.float32
BF16 = jnp.bfloat16

RWKV_HEAD = 64
DECAY_SCALE = math.exp(-0.5)
LNX_EPS = RWKV_HEAD * 1e-5
RMS_EPS = 1e-5
LN_EPS = 1e-5
XATTN_HEADS = 4
TOP_K = 4
SWIGLU_ALPHA = 1.702
SWIGLU_LIMIT = 7.0

LANES = 128
SUBLANES = 8
WKV_CHUNK = 64
LORA_PAD = 128
VMEM_LIMIT = 56 * 1024 * 1024


def _cparams(sem):
    return pltpu.CompilerParams(dimension_semantics=sem, vmem_limit_bytes=VMEM_LIMIT)


def _tile(n, prefs):
    for p in prefs:
        if n % p == 0:
            return p
    return n


def _dot(a, b):
    return jnp.dot(a, b, preferred_element_type=F32)


def _dot_nt(a, b):
    return lax.dot_general(a, b, (((1,), (1,)), ((), ())), preferred_element_type=F32)


def _dot_tn(a, b):
    return lax.dot_general(a, b, (((0,), (0,)), ((), ())), preferred_element_type=F32)


def _split2(x):
    hi = x.astype(BF16)
    lo = (x - hi.astype(F32)).astype(BF16)
    return hi, lo


def _dot3(x, w_hi, w_lo):
    xh, xl = _split2(x)
    return _dot(xh, w_hi) + _dot(xl, w_hi) + _dot(xh, w_lo)


def _rms(x, g):
    return x * lax.rsqrt(jnp.mean(x * x, axis=-1, keepdims=True) + RMS_EPS) * g


def _sigmoid(x):
    return 0.5 * jnp.tanh(0.5 * x) + 0.5


def _inproj_shift_kernel(x_ref, xp_ref, xn_ref, g_ref, w_ref, mu_ref, o_ref, h_scr, hp_scr, hn_scr,
                         *, seq_len):
    i = pl.program_id(0)
    j = pl.program_id(1)
    tm = x_ref.shape[0]

    @pl.when(j == 0)
    def _():
        g = g_ref[...]
        h_scr[...] = _rms(x_ref[...], g).astype(BF16)
        hp_scr[...] = _rms(xp_ref[...], g).astype(BF16)
        hn_scr[...] = _rms(xn_ref[...], g).astype(BF16)

    w = w_ref[...]
    p = _dot(h_scr[...], w)
    pp = _dot(hp_scr[...], w)[SUBLANES - 1:SUBLANES, :]
    pn = _dot(hn_scr[...], w)[0:1, :]
    first = (i * tm) % seq_len == 0
    last = ((i + 1) * tm) % seq_len == 0
    pp = jnp.where(first, 0.0, pp)
    pn = jnp.where(last, 0.0, pn)
    rid = lax.broadcasted_iota(jnp.int32, p.shape, 0)
    up = jnp.where(rid == 0, pp, pltpu.roll(p, 1, 0))
    dn = jnp.where(rid == tm - 1, pn, pltpu.roll(p, tm - 1, 0))
    o_ref[...] = (p + mu_ref[...] * (0.5 * (up + dn) - p)).astype(o_ref.dtype)


def _inproj_shift(x, g, w_bf16, mu, seq_len, tn, out_dtype):
    T, D = x.shape
    N = w_bf16.shape[1]
    tm = _tile(seq_len, (1024, 512, 256, 128, 64, 32, 16, 8))
    nb8 = T // SUBLANES
    r8 = tm // SUBLANES
    return pl.pallas_call(
        functools.partial(_inproj_shift_kernel, seq_len=seq_len),
        grid=(T // tm, N // tn),
        in_specs=[
            pl.BlockSpec((tm, D), lambda i, j: (i, 0)),
            pl.BlockSpec((SUBLANES, D), lambda i, j: (jnp.maximum(i * r8 - 1, 0), 0)),
            pl.BlockSpec((SUBLANES, D), lambda i, j: (jnp.minimum((i + 1) * r8, nb8 - 1), 0)),
            pl.BlockSpec((1, D), lambda i, j: (0, 0)),
            pl.BlockSpec((D, tn), lambda i, j: (0, j)),
            pl.BlockSpec((1, tn), lambda i, j: (0, j)),
        ],
        out_specs=pl.BlockSpec((tm, tn), lambda i, j: (i, j)),
        out_shape=jax.ShapeDtypeStruct((T, N), out_dtype),
        scratch_shapes=[pltpu.VMEM((tm, D), BF16), pltpu.VMEM((SUBLANES, D), BF16),
                        pltpu.VMEM((SUBLANES, D), BF16)],
        compiler_params=_cparams(("parallel", "arbitrary")),
        name="inproj_shift",
    )(x, x, x, g, w_bf16, mu)


def _inproj_glu_kernel(x_ref, g_ref, wa_ref, wb_ref, o_ref, h_scr):
    @pl.when(pl.program_id(1) == 0)
    def _():
        h_scr[...] = _rms(x_ref[...], g_ref[...]).astype(BF16)

    h = h_scr[...]
    a = _dot(h, wa_ref[...])
    b = _dot(h, wb_ref[...])
    o_ref[...] = (a * _sigmoid(b)).astype(o_ref.dtype)


def _inproj_glu(x, g, wa, wb):
    T, D = x.shape
    N = wa.shape[1]
    tm = _tile(T, (1024, 512, 256, 128, 64, 32, 16, 8))
    tn = _tile(N, (512, 256, 128))
    return pl.pallas_call(
        _inproj_glu_kernel,
        grid=(T // tm, N // tn),
        in_specs=[
            pl.BlockSpec((tm, D), lambda i, j: (i, 0)),
            pl.BlockSpec((1, D), lambda i, j: (0, 0)),
            pl.BlockSpec((D, tn), lambda i, j: (0, j)),
            pl.BlockSpec((D, tn), lambda i, j: (0, j)),
        ],
        out_specs=pl.BlockSpec((tm, tn), lambda i, j: (i, j)),
        out_shape=jax.ShapeDtypeStruct((T, N), BF16),
        scratch_shapes=[pltpu.VMEM((tm, D), BF16)],
        compiler_params=_cparams(("parallel", "arbitrary")),
        name="inproj_glu",
    )(x, g, wa, wb)


def _chunk_cumsum(x, bwd):
    L = WKV_CHUNK
    n = x.shape[0]
    rin = lax.broadcasted_iota(jnp.int32, (n, 1), 0) % L
    s = 1
    while s < L:
        if bwd:
            x = x + jnp.where(rin < L - s, pltpu.roll(x, n - s, 0), 0.0)
        else:
            x = x + jnp.where(rin >= s, pltpu.roll(x, s, 0), 0.0)
        s *= 2
    return x


WKV_ISSUE_ORDER = "fffffbffbffffff" + "bf" * 4


def _wkv_bidir_kernel(zr_f, zk_f, zv_f, dd_f, ad_f, zr_b, zk_b, zv_b, dd_b, ad_b, w0_ref, w2h_ref, w2l_ref,
                      a0_ref, a2_ref, kk_ref, ka_ref, rk_ref, of_ref, bf_ref, ob_ref, bb_ref, state_ref,
                      *, n_pairs):
    L = WKV_CHUNK
    PW = 2 * RWKV_HEAD
    TT = zr_f.shape[0]
    n_chunks = TT // L

    @pl.when(pl.program_id(2) == 0)
    def _():
        state_ref[...] = jnp.zeros_like(state_ref)

    row = lax.broadcasted_iota(jnp.int32, (L, 2 * L), 0)
    col = lax.broadcasted_iota(jnp.int32, (L, 2 * L), 1) % L
    ipk = jnp.where(col == row, 1.0, 0.0)
    lane = lax.broadcasted_iota(jnp.int32, (1, PW), 1)
    m0 = lane < RWKV_HEAD
    lane2 = lax.broadcasted_iota(jnp.int32, (1, 2 * PW), 1) % PW
    m0w = lane2 < RWKV_HEAD
    srow = lax.broadcasted_iota(jnp.int32, (PW, PW), 0)
    scol = lax.broadcasted_iota(jnp.int32, (PW, PW), 1)
    same_head = (srow // RWKV_HEAD) == (scol // RWKV_HEAD)
    eye = srow == scol
    HSW = min(zr_f.shape[1], 2 * PW)
    hrow = lax.broadcasted_iota(jnp.int32, (HSW, HSW), 0)
    hcol = lax.broadcasted_iota(jnp.int32, (HSW, HSW), 1)
    head_ones_w = jnp.where((hrow // RWKV_HEAD) == (hcol // RWKV_HEAD), 1.0, 0.0).astype(BF16)

    def bd(x):
        return jnp.concatenate([jnp.where(m0, x, 0.0), jnp.where(m0, 0.0, x)], axis=0).astype(BF16)

    def bd2(x):
        return jnp.concatenate([jnp.where(m0w, x, 0.0), jnp.where(m0w, 0.0, x)], axis=0).astype(BF16)

    def head_sum(x, two_pass):
        n = x.shape[0]
        outs = []
        for c in range(x.shape[1] // HSW):
            xc = x[:, c * HSW:(c + 1) * HSW]
            if two_pass:
                hi, lo = _split2(xc)
                res = _dot(jnp.concatenate([hi, lo], axis=0), head_ones_w)
                outs.append(res[:n] + res[n:])
            else:
                outs.append(_dot(xc.astype(BF16), head_ones_w))
        return jnp.concatenate(outs, axis=1) if len(outs) > 1 else outs[0]

    def direction(d, zr_ref, zk_ref, zv_ref, dd_ref, ad_ref, o_ref, bonus_ref):
        bwd = d == 1
        strict = (col > row) if bwd else (col < row)
        incl = (col >= row) if bwd else (col <= row)
        r = zr_ref[...].astype(F32)
        k = zk_ref[...].astype(F32)
        v = zv_ref[...].astype(F32)
        dlin = w0_ref[d] + _dot3(jnp.tanh(dd_ref[...]), w2h_ref[d], w2l_ref[d])
        iclr_lin = a0_ref[d] + _dot(ad_ref[...].astype(BF16), a2_ref[d])
        yield
        lw = -DECAY_SCALE * _sigmoid(dlin)
        iclr = _sigmoid(iclr_lin)
        kkr = k * kk_ref[...]
        kmod = k * (1.0 + (iclr - 1.0) * ka_ref[...])
        ss = head_sum(kkr * kkr, False)
        bonus_ref[...] = (head_sum(r * kmod * rk_ref[...], True) * v).astype(bonus_ref.dtype)
        yield
        cum = _chunk_cumsum(lw, bwd)
        e_ng = jnp.exp(-cum)
        kk = kkr * lax.rsqrt(jnp.maximum(ss, 1e-24))
        b_all = kk * iclr
        ah_all = -kk * jnp.exp(cum - lw)
        rh_all = r * jnp.exp(cum)
        bh_all = b_all * e_ng
        kh_all = kmod * e_ng

        probs = []
        for ci in (range(n_chunks - 1, -1, -1) if bwd else range(n_chunks)):
            for p in range(n_pairs):
                sl = slice(p * PW, (p + 1) * PW)
                rs = slice(ci * L, (ci + 1) * L)
                q = dict(p=p, ci=ci, sl=sl, rs=rs, ah=ah_all[rs, sl], rh=rh_all[rs, sl], v=v[rs, sl])
                lhs = jnp.concatenate([q["ah"], q["rh"]], axis=0).astype(BF16)
                g = _dot_nt(lhs, jnp.concatenate([bd(bh_all[rs, sl]), bd(kh_all[rs, sl])], axis=0))
                a32 = jnp.where(strict, g[:L, :2 * L], 0.0)
                q["a"] = a32.astype(BF16)
                q["t"] = (ipk + a32).astype(BF16)
                q["rb"] = jnp.where(incl, g[L:, :2 * L], 0.0).astype(BF16)
                q["akrk"] = jnp.concatenate([jnp.where(strict, g[:L, 2 * L:], 0.0),
                                             jnp.where(incl, g[L:, 2 * L:], 0.0)], axis=0).astype(BF16)
                probs.append(q)
        yield
        for q in probs:
            q["kv"] = _dot(q["akrk"], bd(q["v"]))
        yield
        for q in probs:
            q["ai"] = _dot(q["a"], bd(q["a"])).astype(BF16)
        yield
        for lvl in range(1, 6):
            more = lvl < 5
            for q in probs:
                lhs = jnp.concatenate([q["t"], q["ai"]], axis=0) if more else q["t"]
                res = _dot(lhs, bd(q["ai"]))
                q["t"] = (q["t"].astype(F32) + res[:L]).astype(BF16)
                if more:
                    q["ai"] = res[L:].astype(BF16)
            yield
        for q in probs:
            q["rbt"] = _dot(q["rb"], bd(q["t"])).astype(BF16)
        yield
        for q in probs:
            y0 = jnp.concatenate([q["ah"], q["kv"][:L]], axis=1)
            q["ry"] = _dot(jnp.concatenate([q["t"], q["rbt"]], axis=0), bd2(y0))
        yield
        for q in probs:
            rs, sl, ry = q["rs"], q["sl"], q["ry"]
            end = q["ci"] * L if bwd else q["ci"] * L + L - 1
            tot = cum[end:end + 1, sl]
            e_rm = jnp.exp(tot - cum[rs, sl])
            q["rt"] = q["rh"] + ry[L:, :PW]
            q["ob"] = ry[L:, PW:] + q["kv"][L:]
            lhs_t = jnp.concatenate([b_all[rs, sl] * e_rm, kmod[rs, sl] * e_rm], axis=0).astype(BF16)
            rhs_t = jnp.concatenate(
                [ry[:L], jnp.concatenate([jnp.zeros((L, PW), F32), q["v"]], axis=1)], axis=0).astype(BF16)
            mn = _dot_tn(lhs_t, rhs_t)
            q["mm"] = jnp.where(eye, jnp.exp(tot), 0.0) + jnp.where(same_head, mn[:, :PW], 0.0)
            q["nn"] = jnp.where(same_head, mn[:, PW:], 0.0)
        yield
        states = [state_ref[d, p] for p in range(n_pairs)]
        for n, q in enumerate(probs):
            p = q["p"]
            res = _dot(jnp.concatenate([q["rt"], q["mm"]], axis=0).astype(BF16), states[p].astype(BF16))
            o_ref[q["rs"], q["sl"]] = (res[:L] + q["ob"]).astype(o_ref.dtype)
            states[p] = res[L:] + q["nn"]
            if p == n_pairs - 1 and n < len(probs) - 1:
                yield
        for p in range(n_pairs):
            state_ref[d, p] = states[p]

    gens = {"f": direction(0, zr_f, zk_f, zv_f, dd_f, ad_f, of_ref, bf_ref),
            "b": direction(1, zr_b, zk_b, zv_b, dd_b, ad_b, ob_ref, bb_ref)}
    order = list(WKV_ISSUE_ORDER)
    while gens:
        key = order.pop(0) if order else next(iter(gens))
        if key in gens and next(gens[key], "done") == "done":
            del gens[key]


def _wkv_bidir(z_rkv, z_lora, decay_w0, w2h, w2l, iclr_a0, a2_bf16, k_k, k_a, r_k, n_seq, seq_len):
    T = z_rkv.shape[0]
    W = z_rkv.shape[1] // 3
    PW = 2 * RWKV_HEAD
    n_pairs = _tile(W // PW, (4, 2, 1))
    GW = n_pairs * PW
    n_groups = W // GW
    TT = _tile(seq_len, (512, 256, 128, 64))
    nT = seq_len // TT
    fwd_t = lambda b, c: b * nT + c
    bwd_t = lambda b, c: b * nT + nT - 1 - c
    in_specs = []
    for tile, d in ((fwd_t, 0), (bwd_t, 1)):
        in_specs += [pl.BlockSpec((TT, GW), lambda b, g, c, tile=tile, off=off: (tile(b, c), off * n_groups + g))
                     for off in range(3)]
        in_specs += [pl.BlockSpec((TT, LORA_PAD), lambda b, g, c, tile=tile, d=d: (tile(b, c), d)),
                     pl.BlockSpec((TT, LORA_PAD), lambda b, g, c, tile=tile, d=d: (tile(b, c), 2 + d))]
    pspec = pl.BlockSpec((2, 1, GW), lambda b, g, c: (0, 0, g))
    lspec = pl.BlockSpec((2, LORA_PAD, GW), lambda b, g, c: (0, 0, g))
    cspec = pl.BlockSpec((1, GW), lambda b, g, c: (0, g))
    in_specs += [pspec, lspec, lspec, pspec, lspec, cspec, cspec, cspec]
    ospec_f = pl.BlockSpec((TT, GW), lambda b, g, c: (fwd_t(b, c), g))
    ospec_b = pl.BlockSpec((TT, GW), lambda b, g, c: (bwd_t(b, c), g))
    out_t = jax.ShapeDtypeStruct((T, W), BF16)
    return pl.pallas_call(
        functools.partial(_wkv_bidir_kernel, n_pairs=n_pairs),
        grid=(n_seq, n_groups, nT),
        in_specs=in_specs,
        out_specs=[ospec_f, ospec_f, ospec_b, ospec_b],
        out_shape=[out_t, out_t, out_t, out_t],
        scratch_shapes=[pltpu.VMEM((2, n_pairs, PW, PW), F32)],
        compiler_params=_cparams(("parallel", "parallel", "arbitrary")),
        name="wkv_scan_bidir",
    )(*([z_rkv] * 3 + [z_lora] * 2) * 2, decay_w0, w2h, w2l, iclr_a0, a2_bf16, k_k, k_a, r_k)


def _rwkv_post_kernel(of_ref, ob_ref, bf_ref, bb_ref, gd_ref, g2_ref, lg_ref, lb_ref, out_ref):
    W = out_ref.shape[1]
    PW = min(W, 4 * RWKV_HEAD)
    srow = lax.broadcasted_iota(jnp.int32, (PW, PW), 0)
    scol = lax.broadcasted_iota(jnp.int32, (PW, PW), 1)
    head_ones = jnp.where((srow // RWKV_HEAD) == (scol // RWKV_HEAD), 1.0, 0.0).astype(BF16)
    inv = 1.0 / RWKV_HEAD
    n = out_ref.shape[0]

    def head_mean(x):
        hi, lo = _split2(x)
        res = _dot(jnp.concatenate([hi, lo], axis=0), head_ones)
        return (res[:n] + res[n:]) * inv

    gate = _dot(_sigmoid(gd_ref[...]).astype(BF16), g2_ref[...])
    for p in range(W // PW):
        sl = slice(p * PW, (p + 1) * PW)
        o = of_ref[:, sl].astype(F32) + ob_ref[:, sl].astype(F32)
        oc = o - head_mean(o)
        var = head_mean(oc * oc)
        y = oc * lax.rsqrt(var + LNX_EPS) * lg_ref[:, sl] + lb_ref[:, sl]
        y = y + bf_ref[:, sl].astype(F32) + bb_ref[:, sl].astype(F32)
        out_ref[:, sl] = (y * gate[:, sl]).astype(out_ref.dtype)


def _rwkv_post(o_f, o_b, bonus_f, bonus_b, z_lora, g2, lnx_g, lnx_b):
    T, W = o_f.shape
    tm = _tile(T, (512, 256, 128, 64, 32, 16, 8))
    GL = g2.shape[0]
    gd_blk = (4 * LORA_PAD) // GL
    tspec = pl.BlockSpec((tm, W), lambda i: (i, 0))
    return pl.pallas_call(
        _rwkv_post_kernel,
        grid=(T // tm,),
        in_specs=[
            tspec, tspec, tspec, tspec,
            pl.BlockSpec((tm, GL), lambda i: (i, gd_blk)),
            pl.BlockSpec((GL, W), lambda i: (0, 0)),
            pl.BlockSpec((1, W), lambda i: (0, 0)),
            pl.BlockSpec((1, W), lambda i: (0, 0)),
        ],
        out_specs=pl.BlockSpec((tm, W), lambda i: (i, 0)),
        out_shape=jax.ShapeDtypeStruct((T, W), BF16),
        compiler_params=_cparams(("parallel",)),
        name="rwkv_post",
    )(o_f, o_b, bonus_f, bonus_b, z_lora, g2, lnx_g, lnx_b)


def _conv_kernel(u_ref, up_ref, un_ref, w_ref, b_ref, lg_ref, lb_ref, o_ref, ext_scr, *, seq_len, halo):
    i = pl.program_id(0)
    tm = u_ref.shape[0]
    K = w_ref.shape[0]
    first = (i * tm) % seq_len == 0
    last = ((i + 1) * tm) % seq_len == 0
    n_ext = tm + 2 * halo
    ext = jnp.concatenate([jnp.where(first, 0.0, up_ref[...].astype(F32)), u_ref[...].astype(F32),
                           jnp.where(last, 0.0, un_ref[...].astype(F32))], axis=0)
    ext_scr[0] = ext
    for r in range(1, SUBLANES):
        ext_scr[r] = pltpu.roll(ext, n_ext - r, 0)
    sub = min(tm, 32)
    base = halo - K // 2
    for s in range(tm // sub):
        acc = jnp.zeros((sub, u_ref.shape[1]), F32)
        for j in range(K):
            off = base + j
            row0 = s * sub + (off // SUBLANES) * SUBLANES
            acc = acc + w_ref[j:j + 1, :] * ext_scr[off % SUBLANES, row0:row0 + sub, :]
        acc = acc + b_ref[...]
        mean = jnp.mean(acc, axis=-1, keepdims=True)
        xc = acc - mean
        var = jnp.mean(xc * xc, axis=-1, keepdims=True)
        y = xc * lax.rsqrt(var + LN_EPS) * lg_ref[...] + lb_ref[...]
        o_ref[s * sub:(s + 1) * sub, :] = (y * _sigmoid(y)).astype(o_ref.dtype)


def _conv(u, conv_w, conv_b, ln_g, ln_b, seq_len):
    T, C = u.shape
    K = conv_w.shape[0]
    halo = 16
    assert K // 2 <= halo
    tm = _tile(seq_len, (256, 128, 64, 32, 16))
    rh = tm // halo
    nbh = T // halo
    return pl.pallas_call(
        functools.partial(_conv_kernel, seq_len=seq_len, halo=halo),
        grid=(T // tm,),
        in_specs=[
            pl.BlockSpec((tm, C), lambda i: (i, 0)),
            pl.BlockSpec((halo, C), lambda i: (jnp.maximum(i * rh - 1, 0), 0)),
            pl.BlockSpec((halo, C), lambda i: (jnp.minimum((i + 1) * rh, nbh - 1), 0)),
            pl.BlockSpec((K, C), lambda i: (0, 0)),
            pl.BlockSpec((1, C), lambda i: (0, 0)),
            pl.BlockSpec((1, C), lambda i: (0, 0)),
            pl.BlockSpec((1, C), lambda i: (0, 0)),
        ],
        out_specs=pl.BlockSpec((tm, C), lambda i: (i, 0)),
        out_shape=jax.ShapeDtypeStruct((T, C), BF16),
        scratch_shapes=[pltpu.VMEM((SUBLANES, tm + 2 * halo, C), F32)],
        compiler_params=_cparams(("parallel",)),
        name="conformer_conv",
    )(u, u, u, conv_w, conv_b, ln_g, ln_b)


def _merge_kernel(x_ref, g_ref, orw_ref, ocv_ref, wg1_ref, wg2_ref, wb1_ref, wb2_ref, o_ref, h_scr):
    @pl.when(pl.program_id(1) == 0)
    def _():
        h_scr[...] = _rms(x_ref[...], g_ref[...]).astype(BF16)

    h = h_scr[...]
    g1 = _sigmoid(_dot(h, wg1_ref[...]))
    g2 = _sigmoid(_dot(h, wg2_ref[...]))
    y1 = _dot(orw_ref[...], wb1_ref[...])
    y2 = _dot(ocv_ref[...], wb2_ref[...])
    o_ref[...] = (g1 * y1 + g2 * y2).astype(o_ref.dtype)


def _merge(x, g, o_rwkv, o_conv, wg1, wg2, wb1, wb2):
    T, D = x.shape
    W = o_rwkv.shape[1]
    C = o_conv.shape[1]
    tm = _tile(T, (512, 256, 128, 64, 32, 16, 8))
    tn = _tile(D, (512, 256, 128))
    return pl.pallas_call(
        _merge_kernel,
        grid=(T // tm, D // tn),
        in_specs=[
            pl.BlockSpec((tm, D), lambda i, j: (i, 0)),
            pl.BlockSpec((1, D), lambda i, j: (0, 0)),
            pl.BlockSpec((tm, W), lambda i, j: (i, 0)),
            pl.BlockSpec((tm, C), lambda i, j: (i, 0)),
            pl.BlockSpec((D, tn), lambda i, j: (0, j)),
            pl.BlockSpec((D, tn), lambda i, j: (0, j)),
            pl.BlockSpec((W, tn), lambda i, j: (0, j)),
            pl.BlockSpec((C, tn), lambda i, j: (0, j)),
        ],
        out_specs=pl.BlockSpec((tm, tn), lambda i, j: (i, j)),
        out_shape=jax.ShapeDtypeStruct((T, D), BF16),
        scratch_shapes=[pltpu.VMEM((tm, D), BF16)],
        compiler_params=_cparams(("parallel", "arbitrary")),
        name="merge_gates",
    )(x, g, o_rwkv, o_conv, wg1, wg2, wb1, wb2)


def _mm_res_kernel(a_ref, w_ref, res_ref, o_ref):
    o_ref[...] = res_ref[...] + _dot(a_ref[...], w_ref[...])


def _mm_res(a, w, res):
    T, K = a.shape
    N = w.shape[1]
    tm = _tile(T, (1024, 512, 256, 128, 64, 32, 16, 8))
    tn = _tile(N, (1024, 512, 256, 128))
    return pl.pallas_call(
        _mm_res_kernel,
        grid=(T // tm, N // tn),
        in_specs=[
            pl.BlockSpec((tm, K), lambda i, j: (i, 0)),
            pl.BlockSpec((K, tn), lambda i, j: (0, j)),
            pl.BlockSpec((tm, tn), lambda i, j: (i, j)),
        ],
        out_specs=pl.BlockSpec((tm, tn), lambda i, j: (i, j)),
        out_shape=jax.ShapeDtypeStruct((T, N), F32),
        compiler_params=_cparams(("parallel", "parallel")),
        name="proj_residual",
    )(a, w, res)


def _mm_norm_kernel(x_ref, g_ref, w_ref, o_ref, h_scr):
    @pl.when(pl.program_id(1) == 0)
    def _():
        h_scr[...] = _rms(x_ref[...], g_ref[...]).astype(BF16)

    o_ref[...] = _dot(h_scr[...], w_ref[...]).astype(o_ref.dtype)


def _mm_norm(x, g, w):
    T, D = x.shape
    N = w.shape[1]
    tm = _tile(T, (1024, 512, 256, 128, 64, 32, 16, 8))
    tn = _tile(N, (1024, 512, 256, 128))
    return pl.pallas_call(
        _mm_norm_kernel,
        grid=(T // tm, N // tn),
        in_specs=[
            pl.BlockSpec((tm, D), lambda i, j: (i, 0)),
            pl.BlockSpec((1, D), lambda i, j: (0, 0)),
            pl.BlockSpec((D, tn), lambda i, j: (0, j)),
        ],
        out_specs=pl.BlockSpec((tm, tn), lambda i, j: (i, j)),
        out_shape=jax.ShapeDtypeStruct((T, N), BF16),
        scratch_shapes=[pltpu.VMEM((tm, D), BF16)],
        compiler_params=_cparams(("parallel", "arbitrary")),
        name="norm_proj",
    )(x, g, w)


def _xattn_kernel(q_ref, k_ref, v_ref, o_ref):
    D = q_ref.shape[1]
    hd = D // XATTN_HEADS
    scale = hd ** -0.5
    for h in range(XATTN_HEADS):
        sl = slice(h * hd, (h + 1) * hd)
        s = _dot_nt(q_ref[:, sl], k_ref[:, sl]) * scale
        s = s - jnp.max(s, axis=-1, keepdims=True)
        e = jnp.exp(s)
        p = e / jnp.sum(e, axis=-1, keepdims=True)
        o_ref[:, sl] = _dot(p.astype(BF16), v_ref[:, sl]).astype(o_ref.dtype)


def _xattn(q, kv, n_seq, seq_len, n_mem):
    T, D = q.shape
    tm = _tile(seq_len, (512, 256, 128, 64, 32, 16, 8))
    nT = seq_len // tm
    return pl.pallas_call(
        _xattn_kernel,
        grid=(n_seq, nT),
        in_specs=[
            pl.BlockSpec((tm, D), lambda b, i: (b * nT + i, 0)),
            pl.BlockSpec((n_mem, D), lambda b, i: (b, 0)),
            pl.BlockSpec((n_mem, D), lambda b, i: (b, 1)),
        ],
        out_specs=pl.BlockSpec((tm, D), lambda b, i: (b * nT + i, 0)),
        out_shape=jax.ShapeDtypeStruct((T, D), BF16),
        compiler_params=_cparams(("parallel", "parallel")),
        name="cross_attention",
    )(q, kv, kv)


def _router_kernel(x_ref, g_ref, wh_ref, wl_ref, b_ref, hf_ref, e_ref, gate_ref, rank_ref, count_ref, carry_ref):
    @pl.when(pl.program_id(0) == 0)
    def _():
        carry_ref[...] = jnp.zeros_like(carry_ref)

    hf = _rms(x_ref[...], g_ref[...])
    hf_ref[...] = hf.astype(hf_ref.dtype)
    logits = _dot3(hf, wh_ref[...], wl_ref[...]) + b_ref[...]
    E = logits.shape[1]
    eid = lax.broadcasted_iota(jnp.int32, logits.shape, 1).astype(F32)
    work = logits
    vals = []
    idxs = []
    for _ in range(TOP_K):
        m = jnp.max(work, axis=-1, keepdims=True)
        idx = jnp.min(jnp.where(work == m, eid, float(E)), axis=-1, keepdims=True)
        vals.append(m)
        idxs.append(idx.astype(jnp.int32))
        work = jnp.where(eid == idx, -jnp.inf, work)
    ex = [jnp.exp(vv - vals[0]) for vv in vals]
    den = ex[0]
    for t in ex[1:]:
        den = den + t
    tm = logits.shape[0]
    hits = [eid == idx.astype(F32) for idx in idxs]
    per_tok = hits[0].astype(F32)
    for h in hits[1:]:
        per_tok = per_tok + h.astype(F32)
    trow = lax.broadcasted_iota(jnp.int32, (tm, tm), 0)
    tcol = lax.broadcasted_iota(jnp.int32, (tm, tm), 1)
    earlier = jnp.where(tcol < trow, 1.0, 0.0).astype(BF16)
    base = carry_ref[...] + _dot(earlier, per_tok.astype(BF16))
    carry_ref[...] += jnp.sum(per_tok, axis=0, keepdims=True)
    count_ref[...] = carry_ref[...].astype(jnp.int32)

    kid = lax.broadcasted_iota(jnp.int32, (tm, TOP_K), 1)
    e_out = jnp.zeros((tm, TOP_K), jnp.int32)
    g_out = jnp.zeros((tm, TOP_K), F32)
    r_out = jnp.zeros((tm, TOP_K), jnp.int32)
    for t in range(TOP_K):
        rank_t = jnp.sum(jnp.where(hits[t], base, 0.0), axis=-1, keepdims=True).astype(jnp.int32)
        e_out = jnp.where(kid == t, idxs[t], e_out)
        g_out = jnp.where(kid == t, ex[t] / den, g_out)
        r_out = jnp.where(kid == t, rank_t, r_out)
    e_ref[...] = e_out
    gate_ref[...] = g_out
    rank_ref[...] = r_out


def _router(x, g, w_router, b_router):
    T, D = x.shape
    E = w_router.shape[1]
    wh, wl = _split2(w_router)
    tm = _tile(T, (512, 256, 128, 64, 32, 16, 8))
    return pl.pallas_call(
        _router_kernel,
        grid=(T // tm,),
        in_specs=[
            pl.BlockSpec((tm, D), lambda i: (i, 0)),
            pl.BlockSpec((1, D), lambda i: (0, 0)),
            pl.BlockSpec((D, E), lambda i: (0, 0)),
            pl.BlockSpec((D, E), lambda i: (0, 0)),
            pl.BlockSpec((1, E), lambda i: (0, 0)),
        ],
        out_specs=[
            pl.BlockSpec((tm, D), lambda i: (i, 0)),
            pl.BlockSpec((tm, TOP_K), lambda i: (i, 0)),
            pl.BlockSpec((tm, TOP_K), lambda i: (i, 0)),
            pl.BlockSpec((tm, TOP_K), lambda i: (i, 0)),
            pl.BlockSpec((1, E), lambda i: (0, 0)),
        ],
        out_shape=[jax.ShapeDtypeStruct((T, D), BF16),
                   jax.ShapeDtypeStruct((T, TOP_K), jnp.int32),
                   jax.ShapeDtypeStruct((T, TOP_K), F32),
                   jax.ShapeDtypeStruct((T, TOP_K), jnp.int32),
                   jax.ShapeDtypeStruct((1, E), jnp.int32)],
        scratch_shapes=[pltpu.VMEM((1, E), F32)],
        compiler_params=_cparams(("arbitrary",)),
        name="router",
    )(x, g, wh, wl, b_router)


def _expert_kernel(te_ref, tv_ref, x_ref, wg_ref, wu_ref, bg_ref, bu_ref, wd_ref, bd_ref, o_ref, act_ref, *, nf):
    i = pl.program_id(0)
    s = pl.program_id(1)
    tf = wg_ref.shape[1]
    valid = tv_ref[i] > 0

    @pl.when(valid & (s < nf))
    def _():
        x = x_ref[...]
        g = _dot(x, wg_ref[...].astype(BF16)) + bg_ref[...]
        u = _dot(x, wu_ref[...].astype(BF16)) + bu_ref[...]
        g = jnp.minimum(g, SWIGLU_LIMIT)
        u = jnp.clip(u, -SWIGLU_LIMIT, SWIGLU_LIMIT)
        act = ((u + 1.0) * (g * _sigmoid(SWIGLU_ALPHA * g))).astype(BF16)
        for f in range(nf):
            @pl.when(s == f)
            def _(f=f):
                act_ref[:, f * tf:(f + 1) * tf] = act

    @pl.when(valid & (s >= nf))
    def _():
        y = _dot(act_ref[...], wd_ref[...].astype(BF16)) + bd_ref[...]
        o_ref[...] = y.astype(o_ref.dtype)

    @pl.when(jnp.logical_not(valid) & (s >= nf))
    def _():
        o_ref[...] = jnp.zeros_like(o_ref)


def _experts(xb, tile_e, tile_valid, w_gu, b_gu, w_dn, b_dn, tm):
    R, D = xb.shape
    E, _, F2 = w_gu.shape
    F = F2 // 2
    tf = _tile(F, (512, 256, 128))
    tn = _tile(D, (512, 256, 128))
    nf = F // tf
    nn = D // tn
    n_tiles = R // tm
    b_gu3 = b_gu.reshape(E, 1, F2)
    b_dn3 = b_dn.reshape(E, 1, D)

    def fa(i, s, tv):
        return jnp.where(tv[i] > 0, jnp.minimum(s, nf - 1), nf - 1)

    def nb(i, s, tv):
        return jnp.where(tv[i] > 0, jnp.clip(s - nf, 0, nn - 1), nn - 1)

    grid_spec = pltpu.PrefetchScalarGridSpec(
        num_scalar_prefetch=2,
        grid=(n_tiles, nf + nn),
        in_specs=[
            pl.BlockSpec((tm, D), lambda i, s, te, tv: (i, 0)),
            pl.BlockSpec((None, D, tf), lambda i, s, te, tv: (te[i], 0, fa(i, s, tv))),
            pl.BlockSpec((None, D, tf), lambda i, s, te, tv: (te[i], 0, nf + fa(i, s, tv))),
            pl.BlockSpec((None, 1, tf), lambda i, s, te, tv: (te[i], 0, fa(i, s, tv))),
            pl.BlockSpec((None, 1, tf), lambda i, s, te, tv: (te[i], 0, nf + fa(i, s, tv))),
            pl.BlockSpec((None, F, tn), lambda i, s, te, tv: (te[i], 0, nb(i, s, tv))),
            pl.BlockSpec((None, 1, tn), lambda i, s, te, tv: (te[i], 0, nb(i, s, tv))),
        ],
        out_specs=pl.BlockSpec((tm, tn), lambda i, s, te, tv: (i, jnp.clip(s - nf, 0, nn - 1))),
        scratch_shapes=[pltpu.VMEM((tm, F), BF16)],
    )
    return pl.pallas_call(
        functools.partial(_expert_kernel, nf=nf),
        grid_spec=grid_spec,
        out_shape=jax.ShapeDtypeStruct((R, D), BF16),
        compiler_params=_cparams(("arbitrary", "arbitrary")),
        name="moe_experts",
    )(tile_e, tile_valid, xb, w_gu, w_gu, b_gu3, b_gu3, w_dn, b_dn3)


def _combine_kernel(x_ref, y_ref, gate_ref, g_ref, o_head_ref, o_tail_ref, *, n_head_tiles):
    acc = x_ref[...]
    gate = gate_ref[...]
    for t in range(TOP_K):
        acc = acc + gate[:, t:t + 1] * y_ref[t].astype(F32)
    y = _rms(acc, g_ref[...])
    i = pl.program_id(0)

    @pl.when(i < n_head_tiles)
    def _():
        o_head_ref[...] = y

    @pl.when(i >= n_head_tiles)
    def _():
        o_tail_ref[...] = y


def _combine(x, y4, gate, final_g, t_head):
    T, D = x.shape
    tm = _tile(math.gcd(t_head, T - t_head), (512, 256, 128, 64, 32, 16, 8))
    nh = t_head // tm
    return pl.pallas_call(
        functools.partial(_combine_kernel, n_head_tiles=nh),
        grid=(T // tm,),
        in_specs=[
            pl.BlockSpec((tm, D), lambda i: (i, 0)),
            pl.BlockSpec((TOP_K, tm, D), lambda i: (0, i, 0)),
            pl.BlockSpec((tm, TOP_K), lambda i: (i, 0)),
            pl.BlockSpec((1, D), lambda i: (0, 0)),
        ],
        out_specs=[pl.BlockSpec((tm, D), lambda i: (jnp.minimum(i, nh - 1), 0)),
                   pl.BlockSpec((tm, D), lambda i: (jnp.maximum(i - nh, 0), 0))],
        out_shape=[jax.ShapeDtypeStruct((t_head, D), F32), jax.ShapeDtypeStruct((T - t_head, D), F32)],
        compiler_params=_cparams(("arbitrary",)),
        name="moe_combine_norm",
    )(x, y4, gate, final_g)


def _pad_rows(w, n):
    return jnp.pad(w, ((0, 0),) * (w.ndim - 2) + ((0, n - w.shape[-2]), (0, 0)))


def _pad_cols(w, n):
    return jnp.pad(w, ((0, 0),) * (w.ndim - 1) + ((0, n - w.shape[-1]),))


def _moe_tile_rows(n_assign, n_experts):
    for tm in (1024, 512, 256, 128, 64, 32, 16, 8):
        if n_assign >= 4 * n_experts * tm or tm == 8:
            return tm


def _layer(x, mem, n_seq, seq_len, t_head, norm_mix_g, w_in, shift_mu, decay_w0, decay_w2, iclr_a0, iclr_a2,
           gate_g2, k_k, k_a, r_k, lnx_g, lnx_b, conv_w, conv_b, conv_ln_g, conv_ln_b, w_branch, w_o,
           norm_x_g, norm_mem_g, w_xq, w_xkv, w_xo, norm_ffn_g, w_router, b_router, w_gu, b_gu,
           w_dn, b_dn, final_g):
    T, D = x.shape
    W = k_k.shape[0]
    DL = decay_w2.shape[1]
    AL = iclr_a2.shape[1]
    GL = gate_g2.shape[0]
    C = conv_w.shape[1]
    E = w_router.shape[1]
    n_mem = mem.shape[0] // n_seq
    row = lambda v: v.reshape(1, -1)

    o3 = 3 * W
    o4 = o3 + 2 * DL
    o5 = o4 + 2 * AL
    o6 = o5 + GL
    seg = lambda m, a, b, n: _pad_cols(m[..., a:b], n)
    lora_cols = lambda m: jnp.concatenate(
        [seg(m, o3, o3 + DL, LORA_PAD), seg(m, o3 + DL, o4, LORA_PAD), seg(m, o4, o4 + AL, LORA_PAD),
         seg(m, o4 + AL, o5, LORA_PAD), m[..., o5:o6]], axis=-1)
    w_rkv = w_in[:, :o3].astype(BF16)
    w_lora = lora_cols(w_in).astype(BF16)
    mu_rkv = row(shift_mu[:o3])
    mu_lora = row(lora_cols(shift_mu))
    w_ca = w_in[:, o6:o6 + C].astype(BF16)
    w_cb = w_in[:, o6 + C:o6 + 2 * C].astype(BF16)
    w_g1 = w_in[:, o6 + 2 * C:o6 + 2 * C + D].astype(BF16)
    w_g2 = w_in[:, o6 + 2 * C + D:].astype(BF16)
    g_mix = row(norm_mix_g)

    z_rkv = _inproj_shift(x, g_mix, w_rkv, mu_rkv, seq_len, _tile(o3, (1024, 512, 256, 128)), BF16)
    z_lora = _inproj_shift(x, g_mix, w_lora, mu_lora, seq_len, w_lora.shape[1], F32)
    u = _inproj_glu(x, g_mix, w_ca, w_cb)

    w2h, w2l = _split2(_pad_rows(decay_w2, LORA_PAD))
    a2_bf16 = _pad_rows(iclr_a2, LORA_PAD).astype(BF16)
    wkv_args = (z_rkv, z_lora, decay_w0.reshape(2, 1, W), w2h, w2l, iclr_a0.reshape(2, 1, W), a2_bf16,
                row(k_k), row(k_a), row(r_k), n_seq, seq_len)
    o_f, bonus_f, o_b, bonus_b = _wkv_bidir(*wkv_args)
    o_rwkv = _rwkv_post(o_f, o_b, bonus_f, bonus_b, z_lora, gate_g2.astype(BF16), row(lnx_g), row(lnx_b))
    o_conv = _conv(u, conv_w, row(conv_b), row(conv_ln_g), row(conv_ln_b), seq_len)

    merged = _merge(x, g_mix, o_rwkv, o_conv, w_g1, w_g2, w_branch[:W].astype(BF16),
                    w_branch[W:].astype(BF16))
    x1 = _mm_res(merged, w_o.astype(BF16), x)

    q = _mm_norm(x1, row(norm_x_g), w_xq.astype(BF16))
    kv = _mm_norm(mem, row(norm_mem_g), w_xkv.astype(BF16))
    att = _xattn(q, kv, n_seq, seq_len, n_mem)
    x2 = _mm_res(att, w_xo.astype(BF16), x1)

    hf, top_e, gate, rank, counts = _router(x2, row(norm_ffn_g), w_router, row(b_router))

    A = T * TOP_K
    tm_e = _moe_tile_rows(A, E)
    n_tiles = (A + E * (tm_e - 1) + tm_e - 1) // tm_e
    n_rows = n_tiles * tm_e
    flat_e = top_e.reshape(A)
    counts = counts.reshape(E)
    start = jnp.cumsum(counts) - counts
    padded = (counts + tm_e - 1) // tm_e * tm_e
    pad_end = jnp.cumsum(padded)
    pad_start = pad_end - padded
    dest_of = pad_start[flat_e] + rank.reshape(A)
    _, order = lax.sort((flat_e, jnp.arange(A, dtype=jnp.int32)), num_keys=1)
    tile_start = jnp.arange(n_tiles, dtype=jnp.int32) * tm_e
    tile_e = jnp.minimum(jnp.searchsorted(pad_end, tile_start, side='right'), E - 1).astype(jnp.int32)
    tile_valid = jnp.clip(pad_start[tile_e] + counts[tile_e] - tile_start, 0, tm_e).astype(jnp.int32)
    rows = jnp.arange(n_rows, dtype=jnp.int32)
    row_e = jnp.repeat(tile_e, tm_e)
    row_off = rows - pad_start[row_e]
    src = order[jnp.minimum(start[row_e] + row_off, A - 1)] // TOP_K
    row_tok = jnp.where(row_off < counts[row_e], src, rows % T)

    xb = hf.at[row_tok].get(mode='promise_in_bounds')
    yb = _experts(xb, tile_e, tile_valid, w_gu, b_gu, w_dn, b_dn, tm_e)
    dest_slot_major = dest_of.reshape(T, TOP_K).T.reshape(A)
    y4 = yb.at[dest_slot_major].get(mode='promise_in_bounds').reshape(TOP_K, T, D)
    return _combine(x2, y4, gate, row(final_g), t_head)


def kernel(x_prompt, x_sample, mem_prompt, mem_sample, norm_mix_g, w_in, shift_mu, decay_w0, decay_w2,
           iclr_a0, iclr_a2, gate_g2, k_k, k_a, r_k, lnx_g, lnx_b, conv_w, conv_b, conv_ln_g, conv_ln_b,
           w_branch, w_o, norm_x_g, norm_mem_g, w_xq, w_xkv, w_xo, norm_ffn_g, w_router, b_router,
           w_gu, b_gu, w_dn, b_dn, final_g):
    layer_params = (norm_mix_g, w_in, shift_mu, decay_w0, decay_w2, iclr_a0, iclr_a2, gate_g2, k_k, k_a,
                    r_k, lnx_g, lnx_b, conv_w, conv_b, conv_ln_g, conv_ln_b, w_branch, w_o, norm_x_g,
                    norm_mem_g, w_xq, w_xkv, w_xo, norm_ffn_g, w_router, b_router, w_gu, b_gu, w_dn, b_dn)
    assert all(p.shape[0] == 1 for p in layer_params), "single-layer stack expected"
    bp, seq_len, D = x_prompt.shape
    bs = x_sample.shape[0]
    assert x_sample.shape[1] == seq_len
    n_seq = bp + bs
    x = jnp.concatenate([x_prompt, x_sample], axis=0).reshape(n_seq * seq_len, D)
    mem = jnp.concatenate([mem_prompt, mem_sample], axis=0).reshape(-1, D)
    y_p, y_s = _layer(x, mem, n_seq, seq_len, bp * seq_len, *[p[0] for p in layer_params], final_g)
    return y_p.reshape(bp, seq_len, D), y_s.reshape(bs, seq_len, D)
```

```python
import functools
import math

import jax
import jax.numpy as jnp
from jax import lax
from jax.experimental import pallas as pl
from jax.experimental.pallas import tpu as pltpu

F32 = jnp.float32
BF16 = jnp.bfloat16

RWKV_HEAD = 64
DECAY_SCALE = math.exp(-0.5)
LNX_EPS = RWKV_HEAD * 1e-5
RMS_EPS = 1e-5
LN_EPS = 1e-5
XATTN_HEADS = 4
TOP_K = 4
SWIGLU_ALPHA = 1.702
SWIGLU_LIMIT = 7.0

LANES = 128
SUBLANES = 8
WKV_CHUNK = 64
LORA_PAD = 128
VMEM_LIMIT = 56 * 1024 * 1024


def _cparams(sem):
    return pltpu.CompilerParams(dimension_semantics=sem, vmem_limit_bytes=VMEM_LIMIT)


def _tile(n, prefs):
    for p in prefs:
        if n % p == 0:
            return p
    return n


def _dot(a, b):
    return jnp.dot(a, b, preferred_element_type=F32)


def _dot_nt(a, b):
    return lax.dot_general(a, b, (((1,), (1,)), ((), ())), preferred_element_type=F32)


def _dot_tn(a, b):
    return lax.dot_general(a, b, (((0,), (0,)), ((), ())), preferred_element_type=F32)


def _split2(x):
    hi = x.astype(BF16)
    lo = (x - hi.astype(F32)).astype(BF16)
    return hi, lo


def _dot3(x, w_hi, w_lo):
    xh, xl = _split2(x)
    return _dot(xh, w_hi) + _dot(xl, w_hi) + _dot(xh, w_lo)


def _rms(x, g):
    return x * lax.rsqrt(jnp.mean(x * x, axis=-1, keepdims=True) + RMS_EPS) * g


def _sigmoid(x):
    return 0.5 * jnp.tanh(0.5 * x) + 0.5


def _inproj_shift_kernel(x_ref, xp_ref, xn_ref, g_ref, w_ref, mu_ref, o_ref, h_scr, hp_scr, hn_scr,
                         *, seq_len):
    i = pl.program_id(0)
    j = pl.program_id(1)
    tm = x_ref.shape[0]

    @pl.when(j == 0)
    def _():
        g = g_ref[...]
        h_scr[...] = _rms(x_ref[...], g).astype(BF16)
        hp_scr[...] = _rms(xp_ref[...], g).astype(BF16)
        hn_scr[...] = _rms(xn_ref[...], g).astype(BF16)

    w = w_ref[...]
    p = _dot(h_scr[...], w)
    pp = _dot(hp_scr[...], w)[SUBLANES - 1:SUBLANES, :]
    pn = _dot(hn_scr[...], w)[0:1, :]
    first = (i * tm) % seq_len == 0
    last = ((i + 1) * tm) % seq_len == 0
    pp = jnp.where(first, 0.0, pp)
    pn = jnp.where(last, 0.0, pn)
    rid = lax.broadcasted_iota(jnp.int32, p.shape, 0)
    up = jnp.where(rid == 0, pp, pltpu.roll(p, 1, 0))
    dn = jnp.where(rid == tm - 1, pn, pltpu.roll(p, tm - 1, 0))
    o_ref[...] = (p + mu_ref[...] * (0.5 * (up + dn) - p)).astype(o_ref.dtype)


def _inproj_shift(x, g, w_bf16, mu, seq_len, tn, out_dtype):
    T, D = x.shape
    N = w_bf16.shape[1]
    tm = _tile(seq_len, (1024, 512, 256, 128, 64, 32, 16, 8))
    nb8 = T // SUBLANES
    r8 = tm // SUBLANES
    return pl.pallas_call(
        functools.partial(_inproj_shift_kernel, seq_len=seq_len),
        grid=(T // tm, N // tn),
        in_specs=[
            pl.BlockSpec((tm, D), lambda i, j: (i, 0)),
            pl.BlockSpec((SUBLANES, D), lambda i, j: (jnp.maximum(i * r8 - 1, 0), 0)),
            pl.BlockSpec((SUBLANES, D), lambda i, j: (jnp.minimum((i + 1) * r8, nb8 - 1), 0)),
            pl.BlockSpec((1, D), lambda i, j: (0, 0)),
            pl.BlockSpec((D, tn), lambda i, j: (0, j)),
            pl.BlockSpec((1, tn), lambda i, j: (0, j)),
        ],
        out_specs=pl.BlockSpec((tm, tn), lambda i, j: (i, j)),
        out_shape=jax.ShapeDtypeStruct((T, N), out_dtype),
        scratch_shapes=[pltpu.VMEM((tm, D), BF16), pltpu.VMEM((SUBLANES, D), BF16),
                        pltpu.VMEM((SUBLANES, D), BF16)],
        compiler_params=_cparams(("parallel", "arbitrary")),
        name="inproj_shift",
    )(x, x, x, g, w_bf16, mu)


def _inproj_glu_kernel(x_ref, g_ref, wa_ref, wb_ref, o_ref, h_scr):
    @pl.when(pl.program_id(1) == 0)
    def _():
        h_scr[...] = _rms(x_ref[...], g_ref[...]).astype(BF16)

    h = h_scr[...]
    a = _dot(h, wa_ref[...])
    b = _dot(h, wb_ref[...])
    o_ref[...] = (a * _sigmoid(b)).astype(o_ref.dtype)


def _inproj_glu(x, g, wa, wb):
    T, D = x.shape
    N = wa.shape[1]
    tm = _tile(T, (1024, 512, 256, 128, 64, 32, 16, 8))
    tn = _tile(N, (512, 256, 128))
    return pl.pallas_call(
        _inproj_glu_kernel,
        grid=(T // tm, N // tn),
        in_specs=[
            pl.BlockSpec((tm, D), lambda i, j: (i, 0)),
            pl.BlockSpec((1, D), lambda i, j: (0, 0)),
            pl.BlockSpec((D, tn), lambda i, j: (0, j)),
            pl.BlockSpec((D, tn), lambda i, j: (0, j)),
        ],
        out_specs=pl.BlockSpec((tm, tn), lambda i, j: (i, j)),
        out_shape=jax.ShapeDtypeStruct((T, N), BF16),
        scratch_shapes=[pltpu.VMEM((tm, D), BF16)],
        compiler_params=_cparams(("parallel", "arbitrary")),
        name="inproj_glu",
    )(x, g, wa, wb)


def _chunk_cumsum(x, bwd):
    L = WKV_CHUNK
    n = x.shape[0]
    rin = lax.broadcasted_iota(jnp.int32, (n, 1), 0) % L
    s = 1
    while s < L:
        if bwd:
            x = x + jnp.where(rin < L - s, pltpu.roll(x, n - s, 0), 0.0)
        else:
            x = x + jnp.where(rin >= s, pltpu.roll(x, s, 0), 0.0)
        s *= 2
    return x


WKV_ISSUE_ORDER = "fffffbffbffffff" + "bf" * 4


def _wkv_bidir_kernel(zr_f, zk_f, zv_f, dd_f, ad_f, zr_b, zk_b, zv_b, dd_b, ad_b, w0_ref, w2h_ref, w2l_ref,
                      a0_ref, a2_ref, kk_ref, ka_ref, rk_ref, of_ref, bf_ref, ob_ref, bb_ref, state_ref,
                      *, n_pairs):
    L = WKV_CHUNK
    PW = 2 * RWKV_HEAD
    TT = zr_f.shape[0]
    n_chunks = TT // L

    @pl.when(pl.program_id(2) == 0)
    def _():
        state_ref[...] = jnp.zeros_like(state_ref)

    row = lax.broadcasted_iota(jnp.int32, (L, 2 * L), 0)
    col = lax.broadcasted_iota(jnp.int32, (L, 2 * L), 1) % L
    ipk = jnp.where(col == row, 1.0, 0.0)
    lane = lax.broadcasted_iota(jnp.int32, (1, PW), 1)
    m0 = lane < RWKV_HEAD
    lane2 = lax.broadcasted_iota(jnp.int32, (1, 2 * PW), 1) % PW
    m0w = lane2 < RWKV_HEAD
    srow = lax.broadcasted_iota(jnp.int32, (PW, PW), 0)
    scol = lax.broadcasted_iota(jnp.int32, (PW, PW), 1)
    same_head = (srow // RWKV_HEAD) == (scol // RWKV_HEAD)
    eye = srow == scol
    HSW = min(zr_f.shape[1], 2 * PW)
    hrow = lax.broadcasted_iota(jnp.int32, (HSW, HSW), 0)
    hcol = lax.broadcasted_iota(jnp.int32, (HSW, HSW), 1)
    head_ones_w = jnp.where((hrow // RWKV_HEAD) == (hcol // RWKV_HEAD), 1.0, 0.0).astype(BF16)

    def bd(x):
        return jnp.concatenate([jnp.where(m0, x, 0.0), jnp.where(m0, 0.0, x)], axis=0).astype(BF16)

    def bd2(x):
        return jnp.concatenate([jnp.where(m0w, x, 0.0), jnp.where(m0w, 0.0, x)], axis=0).astype(BF16)

    def head_sum(x, two_pass):
        n = x.shape[0]
        outs = []
        for c in range(x.shape[1] // HSW):
            xc = x[:, c * HSW:(c + 1) * HSW]
            if two_pass:
                hi, lo = _split2(xc)
                res = _dot(jnp.concatenate([hi, lo], axis=0), head_ones_w)
                outs.append(res[:n] + res[n:])
            else:
                outs.append(_dot(xc.astype(BF16), head_ones_w))
        return jnp.concatenate(outs, axis=1) if len(outs) > 1 else outs[0]

    def direction(d, zr_ref, zk_ref, zv_ref, dd_ref, ad_ref, o_ref, bonus_ref):
        bwd = d == 1
        strict = (col > row) if bwd else (col < row)
        incl = (col >= row) if bwd else (col <= row)
        r = zr_ref[...].astype(F32)
        k = zk_ref[...].astype(F32)
        v = zv_ref[...].astype(F32)
        dlin = w0_ref[d] + _dot3(jnp.tanh(dd_ref[...]), w2h_ref[d], w2l_ref[d])
        iclr_lin = a0_ref[d] + _dot(ad_ref[...].astype(BF16), a2_ref[d])
        yield
        lw = -DECAY_SCALE * _sigmoid(dlin)
        iclr = _sigmoid(iclr_lin)
        kkr = k * kk_ref[...]
        kmod = k * (1.0 + (iclr - 1.0) * ka_ref[...])
        ss = head_sum(kkr * kkr, False)
        bonus_ref[...] = (head_sum(r * kmod * rk_ref[...], True) * v).astype(bonus_ref.dtype)
        yield
        cum = _chunk_cumsum(lw, bwd)
        e_ng = jnp.exp(-cum)
        kk = kkr * lax.rsqrt(jnp.maximum(ss, 1e-24))
        b_all = kk * iclr
        ah_all = -kk * jnp.exp(cum - lw)
        rh_all = r * jnp.exp(cum)
        bh_all = b_all * e_ng
        kh_all = kmod * e_ng

        probs = []
        for ci in (range(n_chunks - 1, -1, -1) if bwd else range(n_chunks)):
            for p in range(n_pairs):
                sl = slice(p * PW, (p + 1) * PW)
                rs = slice(ci * L, (ci + 1) * L)
                q = dict(p=p, ci=ci, sl=sl, rs=rs, ah=ah_all[rs, sl], rh=rh_all[rs, sl], v=v[rs, sl])
                lhs = jnp.concatenate([q["ah"], q["rh"]], axis=0).astype(BF16)
                g = _dot_nt(lhs, jnp.concatenate([bd(bh_all[rs, sl]), bd(kh_all[rs, sl])], axis=0))
                a32 = jnp.where(strict, g[:L, :2 * L], 0.0)
                q["a"] = a32.astype(BF16)
                q["t"] = (ipk + a32).astype(BF16)
                q["rb"] = jnp.where(incl, g[L:, :2 * L], 0.0).astype(BF16)
                q["akrk"] = jnp.concatenate([jnp.where(strict, g[:L, 2 * L:], 0.0),
                                             jnp.where(incl, g[L:, 2 * L:], 0.0)], axis=0).astype(BF16)
                probs.append(q)
        yield
        for q in probs:
            q["kv"] = _dot(q["akrk"], bd(q["v"]))
        yield
        for q in probs:
            q["ai"] = _dot(q["a"], bd(q["a"])).astype(BF16)
        yield
        for lvl in range(1, 6):
            more = lvl < 5
            for q in probs:
                lhs = jnp.concatenate([q["t"], q["ai"]], axis=0) if more else q["t"]
                res = _dot(lhs, bd(q["ai"]))
                q["t"] = (q["t"].astype(F32) + res[:L]).astype(BF16)
                if more:
                    q["ai"] = res[L:].astype(BF16)
            yield
        for q in probs:
            q["rbt"] = _dot(q["rb"], bd(q["t"])).astype(BF16)
        yield
        for q in probs:
            y0 = jnp.concatenate([q["ah"], q["kv"][:L]], axis=1)
            q["ry"] = _dot(jnp.concatenate([q["t"], q["rbt"]], axis=0), bd2(y0))
        yield
        for q in probs:
            rs, sl, ry = q["rs"], q["sl"], q["ry"]
            end = q["ci"] * L if bwd else q["ci"] * L + L - 1
            tot = cum[end:end + 1, sl]
            e_rm = jnp.exp(tot - cum[rs, sl])
            q["rt"] = q["rh"] + ry[L:, :PW]
            q["ob"] = ry[L:, PW:] + q["kv"][L:]
            lhs_t = jnp.concatenate([b_all[rs, sl] * e_rm, kmod[rs, sl] * e_rm], axis=0).astype(BF16)
            rhs_t = jnp.concatenate(
                [ry[:L], jnp.concatenate([jnp.zeros((L, PW), F32), q["v"]], axis=1)], axis=0).astype(BF16)
            mn = _dot_tn(lhs_t, rhs_t)
            q["mm"] = jnp.where(eye, jnp.exp(tot), 0.0) + jnp.where(same_head, mn[:, :PW], 0.0)
            q["nn"] = jnp.where(same_head, mn[:, PW:], 0.0)
        yield
        states = [state_ref[d, p] for p in range(n_pairs)]
        for n, q in enumerate(probs):
            p = q["p"]
            res = _dot(jnp.concatenate([q["rt"], q["mm"]], axis=0).astype(BF16), states[p].astype(BF16))
            o_ref[q["rs"], q["sl"]] = (res[:L] + q["ob"]).astype(o_ref.dtype)
            states[p] = res[L:] + q["nn"]
            if p == n_pairs - 1 and n < len(probs) - 1:
                yield
        for p in range(n_pairs):
            state_ref[d, p] = states[p]

    gens = {"f": direction(0, zr_f, zk_f, zv_f, dd_f, ad_f, of_ref, bf_ref),
            "b": direction(1, zr_b, zk_b, zv_b, dd_b, ad_b, ob_ref, bb_ref)}
    order = list(WKV_ISSUE_ORDER)
    while gens:
        key = order.pop(0) if order else next(iter(gens))
        if key in gens and next(gens[key], "done") == "done":
            del gens[key]


def _wkv_bidir(z_rkv, z_lora, decay_w0, w2h, w2l, iclr_a0, a2_bf16, k_k, k_a, r_k, n_seq, seq_len):
    T = z_rkv.shape[0]
    W = z_rkv.shape[1] // 3
    PW = 2 * RWKV_HEAD
    n_pairs = _tile(W // PW, (4, 2, 1))
    GW = n_pairs * PW
    n_groups = W // GW
    TT = _tile(seq_len, (512, 256, 128, 64))
    nT = seq_len // TT
    fwd_t = lambda b, c: b * nT + c
    bwd_t = lambda b, c: b * nT + nT - 1 - c
    in_specs = []
    for tile, d in ((fwd_t, 0), (bwd_t, 1)):
        in_specs += [pl.BlockSpec((TT, GW), lambda b, g, c, tile=tile, off=off: (tile(b, c), off * n_groups + g))
                     for off in range(3)]
        in_specs += [pl.BlockSpec((TT, LORA_PAD), lambda b, g, c, tile=tile, d=d: (tile(b, c), d)),
                     pl.BlockSpec((TT, LORA_PAD), lambda b, g, c, tile=tile, d=d: (tile(b, c), 2 + d))]
    pspec = pl.BlockSpec((2, 1, GW), lambda b, g, c: (0, 0, g))
    lspec = pl.BlockSpec((2, LORA_PAD, GW), lambda b, g, c: (0, 0, g))
    cspec = pl.BlockSpec((1, GW), lambda b, g, c: (0, g))
    in_specs += [pspec, lspec, lspec, pspec, lspec, cspec, cspec, cspec]
    ospec_f = pl.BlockSpec((TT, GW), lambda b, g, c: (fwd_t(b, c), g))
    ospec_b = pl.BlockSpec((TT, GW), lambda b, g, c: (bwd_t(b, c), g))
    out_t = jax.ShapeDtypeStruct((T, W), BF16)
    return pl.pallas_call(
        functools.partial(_wkv_bidir_kernel, n_pairs=n_pairs),
        grid=(n_seq, n_groups, nT),
        in_specs=in_specs,
        out_specs=[ospec_f, ospec_f, ospec_b, ospec_b],
        out_shape=[out_t, out_t, out_t, out_t],
        scratch_shapes=[pltpu.VMEM((2, n_pairs, PW, PW), F32)],
        compiler_params=_cparams(("parallel", "parallel", "arbitrary")),
        name="wkv_scan_bidir",
    )(*([z_rkv] * 3 + [z_lora] * 2) * 2, decay_w0, w2h, w2l, iclr_a0, a2_bf16, k_k, k_a, r_k)


def _rwkv_post_kernel(of_ref, ob_ref, bf_ref, bb_ref, gd_ref, g2_ref, lg_ref, lb_ref, out_ref):
    W = out_ref.shape[1]
    PW = min(W, 4 * RWKV_HEAD)
    srow = lax.broadcasted_iota(jnp.int32, (PW, PW), 0)
    scol = lax.broadcasted_iota(jnp.int32, (PW, PW), 1)
    head_ones = jnp.where((srow // RWKV_HEAD) == (scol // RWKV_HEAD), 1.0, 0.0).astype(BF16)
    inv = 1.0 / RWKV_HEAD
    n = out_ref.shape[0]

    def head_mean(x):
        hi, lo = _split2(x)
        res = _dot(jnp.concatenate([hi, lo], axis=0), head_ones)
        return (res[:n] + res[n:]) * inv

    gate = _dot(_sigmoid(gd_ref[...]).astype(BF16), g2_ref[...])
    for p in range(W // PW):
        sl = slice(p * PW, (p + 1) * PW)
        o = of_ref[:, sl].astype(F32) + ob_ref[:, sl].astype(F32)
        oc = o - head_mean(o)
        var = head_mean(oc * oc)
        y = oc * lax.rsqrt(var + LNX_EPS) * lg_ref[:, sl] + lb_ref[:, sl]
        y = y + bf_ref[:, sl].astype(F32) + bb_ref[:, sl].astype(F32)
        out_ref[:, sl] = (y * gate[:, sl]).astype(out_ref.dtype)


def _rwkv_post(o_f, o_b, bonus_f, bonus_b, z_lora, g2, lnx_g, lnx_b):
    T, W = o_f.shape
    tm = _tile(T, (512, 256, 128, 64, 32, 16, 8))
    GL = g2.shape[0]
    gd_blk = (4 * LORA_PAD) // GL
    tspec = pl.BlockSpec((tm, W), lambda i: (i, 0))
    return pl.pallas_call(
        _rwkv_post_kernel,
        grid=(T // tm,),
        in_specs=[
            tspec, tspec, tspec, tspec,
            pl.BlockSpec((tm, GL), lambda i: (i, gd_blk)),
            pl.BlockSpec((GL, W), lambda i: (0, 0)),
            pl.BlockSpec((1, W), lambda i: (0, 0)),
            pl.BlockSpec((1, W), lambda i: (0, 0)),
        ],
        out_specs=pl.BlockSpec((tm, W), lambda i: (i, 0)),
        out_shape=jax.ShapeDtypeStruct((T, W), BF16),
        compiler_params=_cparams(("parallel",)),
        name="rwkv_post",
    )(o_f, o_b, bonus_f, bonus_b, z_lora, g2, lnx_g, lnx_b)


def _conv_kernel(u_ref, up_ref, un_ref, w_ref, b_ref, lg_ref, lb_ref, o_ref, ext_scr, *, seq_len, halo):
    i = pl.program_id(0)
    tm = u_ref.shape[0]
    K = w_ref.shape[0]
    first = (i * tm) % seq_len == 0
    last = ((i + 1) * tm) % seq_len == 0
    n_ext = tm + 2 * halo
    ext = jnp.concatenate([jnp.where(first, 0.0, up_ref[...].astype(F32)), u_ref[...].astype(F32),
                           jnp.where(last, 0.0, un_ref[...].astype(F32))], axis=0)
    ext_scr[0] = ext
    for r in range(1, SUBLANES):
        ext_scr[r] = pltpu.roll(ext, n_ext - r, 0)
    sub = min(tm, 32)
    base = halo - K // 2
    for s in range(tm // sub):
        acc = jnp.zeros((sub, u_ref.shape[1]), F32)
        for j in range(K):
            off = base + j
            row0 = s * sub + (off // SUBLANES) * SUBLANES
            acc = acc + w_ref[j:j + 1, :] * ext_scr[off % SUBLANES, row0:row0 + sub, :]
        acc = acc + b_ref[...]
        mean = jnp.mean(acc, axis=-1, keepdims=True)
        xc = acc - mean
        var = jnp.mean(xc * xc, axis=-1, keepdims=True)
        y = xc * lax.rsqrt(var + LN_EPS) * lg_ref[...] + lb_ref[...]
        o_ref[s * sub:(s + 1) * sub, :] = (y * _sigmoid(y)).astype(o_ref.dtype)


def _conv(u, conv_w, conv_b, ln_g, ln_b, seq_len):
    T, C = u.shape
    K = conv_w.shape[0]
    halo = 16
    assert K // 2 <= halo
    tm = _tile(seq_len, (256, 128, 64, 32, 16))
    rh = tm // halo
    nbh = T // halo
    return pl.pallas_call(
        functools.partial(_conv_kernel, seq_len=seq_len, halo=halo),
        grid=(T // tm,),
        in_specs=[
            pl.BlockSpec((tm, C), lambda i: (i, 0)),
            pl.BlockSpec((halo, C), lambda i: (jnp.maximum(i * rh - 1, 0), 0)),
            pl.BlockSpec((halo, C), lambda i: (jnp.minimum((i + 1) * rh, nbh - 1), 0)),
            pl.BlockSpec((K, C), lambda i: (0, 0)),
            pl.BlockSpec((1, C), lambda i: (0, 0)),
            pl.BlockSpec((1, C), lambda i: (0, 0)),
            pl.BlockSpec((1, C), lambda i: (0, 0)),
        ],
        out_specs=pl.BlockSpec((tm, C), lambda i: (i, 0)),
        out_shape=jax.ShapeDtypeStruct((T, C), BF16),
        scratch_shapes=[pltpu.VMEM((SUBLANES, tm + 2 * halo, C), F32)],
        compiler_params=_cparams(("parallel",)),
        name="conformer_conv",
    )(u, u, u, conv_w, conv_b, ln_g, ln_b)


def _merge_kernel(x_ref, g_ref, orw_ref, ocv_ref, wg1_ref, wg2_ref, wb1_ref, wb2_ref, o_ref, h_scr):
    @pl.when(pl.program_id(1) == 0)
    def _():
        h_scr[...] = _rms(x_ref[...], g_ref[...]).astype(BF16)

    h = h_scr[...]
    g1 = _sigmoid(_dot(h, wg1_ref[...]))
    g2 = _sigmoid(_dot(h, wg2_ref[...]))
    y1 = _dot(orw_ref[...], wb1_ref[...])
    y2 = _dot(ocv_ref[...], wb2_ref[...])
    o_ref[...] = (g1 * y1 + g2 * y2).astype(o_ref.dtype)


def _merge(x, g, o_rwkv, o_conv, wg1, wg2, wb1, wb2):
    T, D = x.shape
    W = o_rwkv.shape[1]
    C = o_conv.shape[1]
    tm = _tile(T, (512, 256, 128, 64, 32, 16, 8))
    tn = _tile(D, (512, 256, 128))
    return pl.pallas_call(
        _merge_kernel,
        grid=(T // tm, D // tn),
        in_specs=[
            pl.BlockSpec((tm, D), lambda i, j: (i, 0)),
            pl.BlockSpec((1, D), lambda i, j: (0, 0)),
            pl.BlockSpec((tm, W), lambda i, j: (i, 0)),
            pl.BlockSpec((tm, C), lambda i, j: (i, 0)),
            pl.BlockSpec((D, tn), lambda i, j: (0, j)),
            pl.BlockSpec((D, tn), lambda i, j: (0, j)),
            pl.BlockSpec((W, tn), lambda i, j: (0, j)),
            pl.BlockSpec((C, tn), lambda i, j: (0, j)),
        ],
        out_specs=pl.BlockSpec((tm, tn), lambda i, j: (i, j)),
        out_shape=jax.ShapeDtypeStruct((T, D), BF16),
        scratch_shapes=[pltpu.VMEM((tm, D), BF16)],
        compiler_params=_cparams(("parallel", "arbitrary")),
        name="merge_gates",
    )(x, g, o_rwkv, o_conv, wg1, wg2, wb1, wb2)


def _mm_res_kernel(a_ref, w_ref, res_ref, o_ref):
    o_ref[...] = res_ref[...] + _dot(a_ref[...], w_ref[...])


def _mm_res(a, w, res):
    T, K = a.shape
    N = w.shape[1]
    tm = _tile(T, (1024, 512, 256, 128, 64, 32, 16, 8))
    tn = _tile(N, (1024, 512, 256, 128))
    return pl.pallas_call(
        _mm_res_kernel,
        grid=(T // tm, N // tn),
        in_specs=[
            pl.BlockSpec((tm, K), lambda i, j: (i, 0)),
            pl.BlockSpec((K, tn), lambda i, j: (0, j)),
            pl.BlockSpec((tm, tn), lambda i, j: (i, j)),
        ],
        out_specs=pl.BlockSpec((tm, tn), lambda i, j: (i, j)),
        out_shape=jax.ShapeDtypeStruct((T, N), F32),
        compiler_params=_cparams(("parallel", "parallel")),
        name="proj_residual",
    )(a, w, res)


def _mm_norm_kernel(x_ref, g_ref, w_ref, o_ref, h_scr):
    @pl.when(pl.program_id(1) == 0)
    def _():
        h_scr[...] = _rms(x_ref[...], g_ref[...]).astype(BF16)

    o_ref[...] = _dot(h_scr[...], w_ref[...]).astype(o_ref.dtype)


def _mm_norm(x, g, w):
    T, D = x.shape
    N = w.shape[1]
    tm = _tile(T, (1024, 512, 256, 128, 64, 32, 16, 8))
    tn = _tile(N, (1024, 512, 256, 128))
    return pl.pallas_call(
        _mm_norm_kernel,
        grid=(T // tm, N // tn),
        in_specs=[
            pl.BlockSpec((tm, D), lambda i, j: (i, 0)),
            pl.BlockSpec((1, D), lambda i, j: (0, 0)),
            pl.BlockSpec((D, tn), lambda i, j: (0, j)),
        ],
        out_specs=pl.BlockSpec((tm, tn), lambda i, j: (i, j)),
        out_shape=jax.ShapeDtypeStruct((T, N), BF16),
        scratch_shapes=[pltpu.VMEM((tm, D), BF16)],
        compiler_params=_cparams(("parallel", "arbitrary")),
        name="norm_proj",
    )(x, g, w)


def _xattn_kernel(q_ref, k_ref, v_ref, o_ref):
    D = q_ref.shape[1]
    hd = D // XATTN_HEADS
    scale = hd ** -0.5
    for h in range(XATTN_HEADS):
        sl = slice(h * hd, (h + 1) * hd)
        s = _dot_nt(q_ref[:, sl], k_ref[:, sl]) * scale
        s = s - jnp.max(s, axis=-1, keepdims=True)
        e = jnp.exp(s)
        p = e / jnp.sum(e, axis=-1, keepdims=True)
        o_ref[:, sl] = _dot(p.astype(BF16), v_ref[:, sl]).astype(o_ref.dtype)


def _xattn(q, kv, n_seq, seq_len, n_mem):
    T, D = q.shape
    tm = _tile(seq_len, (512, 256, 128, 64, 32, 16, 8))
    nT = seq_len // tm
    return pl.pallas_call(
        _xattn_kernel,
        grid=(n_seq, nT),
        in_specs=[
            pl.BlockSpec((tm, D), lambda b, i: (b * nT + i, 0)),
            pl.BlockSpec((n_mem, D), lambda b, i: (b, 0)),
            pl.BlockSpec((n_mem, D), lambda b, i: (b, 1)),
        ],
        out_specs=pl.BlockSpec((tm, D), lambda b, i: (b * nT + i, 0)),
        out_shape=jax.ShapeDtypeStruct((T, D), BF16),
        compiler_params=_cparams(("parallel", "parallel")),
        name="cross_attention",
    )(q, kv, kv)


def _router_kernel(x_ref, g_ref, wh_ref, wl_ref, b_ref, hf_ref, e_ref, gate_ref, rank_ref, count_ref, carry_ref):
    @pl.when(pl.program_id(0) == 0)
    def _():
        carry_ref[...] = jnp.zeros_like(carry_ref)

    hf = _rms(x_ref[...], g_ref[...])
    hf_ref[...] = hf.astype(hf_ref.dtype)
    logits = _dot3(hf, wh_ref[...], wl_ref[...]) + b_ref[...]
    E = logits.shape[1]
    eid = lax.broadcasted_iota(jnp.int32, logits.shape, 1).astype(F32)
    work = logits
    vals = []
    idxs = []
    for _ in range(TOP_K):
        m = jnp.max(work, axis=-1, keepdims=True)
        idx = jnp.min(jnp.where(work == m, eid, float(E)), axis=-1, keepdims=True)
        vals.append(m)
        idxs.append(idx.astype(jnp.int32))
        work = jnp.where(eid == idx, -jnp.inf, work)
    ex = [jnp.exp(vv - vals[0]) for vv in vals]
    den = ex[0]
    for t in ex[1:]:
        den = den + t
    tm = logits.shape[0]
    hits = [eid == idx.astype(F32) for idx in idxs]
    per_tok = hits[0].astype(F32)
    for h in hits[1:]:
        per_tok = per_tok + h.astype(F32)
    trow = lax.broadcasted_iota(jnp.int32, (tm, tm), 0)
    tcol = lax.broadcasted_iota(jnp.int32, (tm, tm), 1)
    earlier = jnp.where(tcol < trow, 1.0, 0.0).astype(BF16)
    base = carry_ref[...] + _dot(earlier, per_tok.astype(BF16))
    carry_ref[...] += jnp.sum(per_tok, axis=0, keepdims=True)
    count_ref[...] = carry_ref[...].astype(jnp.int32)

    kid = lax.broadcasted_iota(jnp.int32, (tm, TOP_K), 1)
    e_out = jnp.zeros((tm, TOP_K), jnp.int32)
    g_out = jnp.zeros((tm, TOP_K), F32)
    r_out = jnp.zeros((tm, TOP_K), jnp.int32)
    for t in range(TOP_K):
        rank_t = jnp.sum(jnp.where(hits[t], base, 0.0), axis=-1, keepdims=True).astype(jnp.int32)
        e_out = jnp.where(kid == t, idxs[t], e_out)
        g_out = jnp.where(kid == t, ex[t] / den, g_out)
        r_out = jnp.where(kid == t, rank_t, r_out)
    e_ref[...] = e_out
    gate_ref[...] = g_out
    rank_ref[...] = r_out


def _router(x, g, w_router, b_router):
    T, D = x.shape
    E = w_router.shape[1]
    wh, wl = _split2(w_router)
    tm = _tile(T, (512, 256, 128, 64, 32, 16, 8))
    return pl.pallas_call(
        _router_kernel,
        grid=(T // tm,),
        in_specs=[
            pl.BlockSpec((tm, D), lambda i: (i, 0)),
            pl.BlockSpec((1, D), lambda i: (0, 0)),
            pl.BlockSpec((D, E), lambda i: (0, 0)),
            pl.BlockSpec((D, E), lambda i: (0, 0)),
            pl.BlockSpec((1, E), lambda i: (0, 0)),
        ],
        out_specs=[
            pl.BlockSpec((tm, D), lambda i: (i, 0)),
            pl.BlockSpec((tm, TOP_K), lambda i: (i, 0)),
            pl.BlockSpec((tm, TOP_K), lambda i: (i, 0)),
            pl.BlockSpec((tm, TOP_K), lambda i: (i, 0)),
            pl.BlockSpec((1, E), lambda i: (0, 0)),
        ],
        out_shape=[jax.ShapeDtypeStruct((T, D), BF16),
                   jax.ShapeDtypeStruct((T, TOP_K), jnp.int32),
                   jax.ShapeDtypeStruct((T, TOP_K), F32),
                   jax.ShapeDtypeStruct((T, TOP_K), jnp.int32),
                   jax.ShapeDtypeStruct((1, E), jnp.int32)],
        scratch_shapes=[pltpu.VMEM((1, E), F32)],
        compiler_params=_cparams(("arbitrary",)),
        name="router",
    )(x, g, wh, wl, b_router)


def _expert_kernel(te_ref, tv_ref, x_ref, wg_ref, wu_ref, bg_ref, bu_ref, wd_ref, bd_ref, o_ref, act_ref, *, nf):
    i = pl.program_id(0)
    s = pl.program_id(1)
    tf = wg_ref.shape[1]
    valid = tv_ref[i] > 0

    @pl.when(valid & (s < nf))
    def _():
        x = x_ref[...]
        g = _dot(x, wg_ref[...].astype(BF16)) + bg_ref[...]
        u = _dot(x, wu_ref[...].astype(BF16)) + bu_ref[...]
        g = jnp.minimum(g, SWIGLU_LIMIT)
        u = jnp.clip(u, -SWIGLU_LIMIT, SWIGLU_LIMIT)
        act = ((u + 1.0) * (g * _sigmoid(SWIGLU_ALPHA * g))).astype(BF16)
        for f in range(nf):
            @pl.when(s == f)
            def _(f=f):
                act_ref[:, f * tf:(f + 1) * tf] = act

    @pl.when(valid & (s >= nf))
    def _():
        y = _dot(act_ref[...], wd_ref[...].astype(BF16)) + bd_ref[...]
        o_ref[...] = y.astype(o_ref.dtype)

    @pl.when(jnp.logical_not(valid) & (s >= nf))
    def _():
        o_ref[...] = jnp.zeros_like(o_ref)


def _experts(xb, tile_e, tile_valid, w_gu, b_gu, w_dn, b_dn, tm):
    R, D = xb.shape
    E, _, F2 = w_gu.shape
    F = F2 // 2
    tf = _tile(F, (512, 256, 128))
    tn = _tile(D, (512, 256, 128))
    nf = F // tf
    nn = D // tn
    n_tiles = R // tm
    b_gu3 = b_gu.reshape(E, 1, F2)
    b_dn3 = b_dn.reshape(E, 1, D)

    def fa(i, s, tv):
        return jnp.where(tv[i] > 0, jnp.minimum(s, nf - 1), nf - 1)

    def nb(i, s, tv):
        return jnp.where(tv[i] > 0, jnp.clip(s - nf, 0, nn - 1), nn - 1)

    grid_spec = pltpu.PrefetchScalarGridSpec(
        num_scalar_prefetch=2,
        grid=(n_tiles, nf + nn),
        in_specs=[
            pl.BlockSpec((tm, D), lambda i, s, te, tv: (i, 0)),
            pl.BlockSpec((None, D, tf), lambda i, s, te, tv: (te[i], 0, fa(i, s, tv))),
            pl.BlockSpec((None, D, tf), lambda i, s, te, tv: (te[i], 0, nf + fa(i, s, tv))),
            pl.BlockSpec((None, 1, tf), lambda i, s, te, tv: (te[i], 0, fa(i, s, tv))),
            pl.BlockSpec((None, 1, tf), lambda i, s, te, tv: (te[i], 0, nf + fa(i, s, tv))),
            pl.BlockSpec((None, F, tn), lambda i, s, te, tv: (te[i], 0, nb(i, s, tv))),
            pl.BlockSpec((None, 1, tn), lambda i, s, te, tv: (te[i], 0, nb(i, s, tv))),
        ],
        out_specs=pl.BlockSpec((tm, tn), lambda i, s, te, tv: (i, jnp.clip(s - nf, 0, nn - 1))),
        scratch_shapes=[pltpu.VMEM((tm, F), BF16)],
    )
    return pl.pallas_call(
        functools.partial(_expert_kernel, nf=nf),
        grid_spec=grid_spec,
        out_shape=jax.ShapeDtypeStruct((R, D), BF16),
        compiler_params=_cparams(("arbitrary", "arbitrary")),
        name="moe_experts",
    )(tile_e, tile_valid, xb, w_gu, w_gu, b_gu3, b_gu3, w_dn, b_dn3)


def _combine_kernel(x_ref, y_ref, gate_ref, g_ref, o_head_ref, o_tail_ref, *, n_head_tiles):
    acc = x_ref[...]
    gate = gate_ref[...]
    for t in range(TOP_K):
        acc = acc + gate[:, t:t + 1] * y_ref[t].astype(F32)
    y = _rms(acc, g_ref[...])
    i = pl.program_id(0)

    @pl.when(i < n_head_tiles)
    def _():
        o_head_ref[...] = y

    @pl.when(i >= n_head_tiles)
    def _():
        o_tail_ref[...] = y


def _combine(x, y4, gate, final_g, t_head):
    T, D = x.shape
    tm = _tile(math.gcd(t_head, T - t_head), (512, 256, 128, 64, 32, 16, 8))
    nh = t_head // tm
    return pl.pallas_call(
        functools.partial(_combine_kernel, n_head_tiles=nh),
        grid=(T // tm,),
        in_specs=[
            pl.BlockSpec((tm, D), lambda i: (i, 0)),
            pl.BlockSpec((TOP_K, tm, D), lambda i: (0, i, 0)),
            pl.BlockSpec((tm, TOP_K), lambda i: (i, 0)),
            pl.BlockSpec((1, D), lambda i: (0, 0)),
        ],
        out_specs=[pl.BlockSpec((tm, D), lambda i: (jnp.minimum(i, nh - 1), 0)),
                   pl.BlockSpec((tm, D), lambda i: (jnp.maximum(i - nh, 0), 0))],
        out_shape=[jax.ShapeDtypeStruct((t_head, D), F32), jax.ShapeDtypeStruct((T - t_head, D), F32)],
        compiler_params=_cparams(("arbitrary",)),
        name="moe_combine_norm",
    )(x, y4, gate, final_g)


def _pad_rows(w, n):
    return jnp.pad(w, ((0, 0),) * (w.ndim - 2) + ((0, n - w.shape[-2]), (0, 0)))


def _pad_cols(w, n):
    return jnp.pad(w, ((0, 0),) * (w.ndim - 1) + ((0, n - w.shape[-1]),))


def _moe_tile_rows(n_assign, n_experts):
    for tm in (1024, 512, 256, 128, 64, 32, 16, 8):
        if n_assign >= 4 * n_experts * tm or tm == 8:
            return tm


def _layer(x, mem, n_seq, seq_len, t_head, norm_mix_g, w_in, shift_mu, decay_w0, decay_w2, iclr_a0, iclr_a2,
           gate_g2, k_k, k_a, r_k, lnx_g, lnx_b, conv_w, conv_b, conv_ln_g, conv_ln_b, w_branch, w_o,
           norm_x_g, norm_mem_g, w_xq, w_xkv, w_xo, norm_ffn_g, w_router, b_router, w_gu, b_gu,
           w_dn, b_dn, final_g):
    T, D = x.shape
    W = k_k.shape[0]
    DL = decay_w2.shape[1]
    AL = iclr_a2.shape[1]
    GL = gate_g2.shape[0]
    C = conv_w.shape[1]
    E = w_router.shape[1]
    n_mem = mem.shape[0] // n_seq
    row = lambda v: v.reshape(1, -1)

    o3 = 3 * W
    o4 = o3 + 2 * DL
    o5 = o4 + 2 * AL
    o6 = o5 + GL
    seg = lambda m, a, b, n: _pad_cols(m[..., a:b], n)
    lora_cols = lambda m: jnp.concatenate(
        [seg(m, o3, o3 + DL, LORA_PAD), seg(m, o3 + DL, o4, LORA_PAD), seg(m, o4, o4 + AL, LORA_PAD),
         seg(m, o4 + AL, o5, LORA_PAD), m[..., o5:o6]], axis=-1)
    w_rkv = w_in[:, :o3].astype(BF16)
    w_lora = lora_cols(w_in).astype(BF16)
    mu_rkv = row(shift_mu[:o3])
    mu_lora = row(lora_cols(shift_mu))
    w_ca = w_in[:, o6:o6 + C].astype(BF16)
    w_cb = w_in[:, o6 + C:o6 + 2 * C].astype(BF16)
    w_g1 = w_in[:, o6 + 2 * C:o6 + 2 * C + D].astype(BF16)
    w_g2 = w_in[:, o6 + 2 * C + D:].astype(BF16)
    g_mix = row(norm_mix_g)

    z_rkv = _inproj_shift(x, g_mix, w_rkv, mu_rkv, seq_len, _tile(o3, (1024, 512, 256, 128)), BF16)
    z_lora = _inproj_shift(x, g_mix, w_lora, mu_lora, seq_len, w_lora.shape[1], F32)
    u = _inproj_glu(x, g_mix, w_ca, w_cb)

    w2h, w2l = _split2(_pad_rows(decay_w2, LORA_PAD))
    a2_bf16 = _pad_rows(iclr_a2, LORA_PAD).astype(BF16)
    wkv_args = (z_rkv, z_lora, decay_w0.reshape(2, 1, W), w2h, w2l, iclr_a0.reshape(2, 1, W), a2_bf16,
                row(k_k), row(k_a), row(r_k), n_seq, seq_len)
    o_f, bonus_f, o_b, bonus_b = _wkv_bidir(*wkv_args)
    o_rwkv = _rwkv_post(o_f, o_b, bonus_f, bonus_b, z_lora, gate_g2.astype(BF16), row(lnx_g), row(lnx_b))
    o_conv = _conv(u, conv_w, row(conv_b), row(conv_ln_g), row(conv_ln_b), seq_len)

    merged = _merge(x, g_mix, o_rwkv, o_conv, w_g1, w_g2, w_branch[:W].astype(BF16),
                    w_branch[W:].astype(BF16))
    x1 = _mm_res(merged, w_o.astype(BF16), x)

    q = _mm_norm(x1, row(norm_x_g), w_xq.astype(BF16))
    kv = _mm_norm(mem, row(norm_mem_g), w_xkv.astype(BF16))
    att = _xattn(q, kv, n_seq, seq_len, n_mem)
    x2 = _mm_res(att, w_xo.astype(BF16), x1)

    hf, top_e, gate, rank, counts = _router(x2, row(norm_ffn_g), w_router, row(b_router))

    A = T * TOP_K
    tm_e = _moe_tile_rows(A, E)
    n_tiles = (A + E * (tm_e - 1) + tm_e - 1) // tm_e
    n_rows = n_tiles * tm_e
    flat_e = top_e.reshape(A)
    counts = counts.reshape(E)
    start = jnp.cumsum(counts) - counts
    padded = (counts + tm_e - 1) // tm_e * tm_e
    pad_end = jnp.cumsum(padded)
    pad_start = pad_end - padded
    dest_of = pad_start[flat_e] + rank.reshape(A)
    _, order = lax.sort((flat_e, jnp.arange(A, dtype=jnp.int32)), num_keys=1)
    tile_start = jnp.arange(n_tiles, dtype=jnp.int32) * tm_e
    tile_e = jnp.minimum(jnp.sum(tile_start[:, None] >= pad_end[None, :], axis=1), E - 1).astype(jnp.int32)
    tile_off = tile_start - pad_start[tile_e]
    tile_valid = jnp.clip(counts[tile_e] - tile_off, 0, tm_e).astype(jnp.int32)
    per_row = lambda v: jnp.broadcast_to(v[:, None], (n_tiles, tm_e)).reshape(n_rows)
    rows = jnp.arange(n_rows, dtype=jnp.int32)
    row_off = per_row(tile_off) + rows % tm_e
    src = order[jnp.minimum(per_row(start[tile_e]) + row_off, A - 1)] // TOP_K
    row_tok = jnp.where(row_off < per_row(counts[tile_e]), src, rows % T)

    xb = hf.at[row_tok].get(mode='promise_in_bounds')
    yb = _experts(xb, tile_e, tile_valid, w_gu, b_gu, w_dn, b_dn, tm_e)
    dest_slot_major = dest_of.reshape(T, TOP_K).T.reshape(A)
    y4 = yb.at[dest_slot_major].get(mode='promise_in_bounds').reshape(TOP_K, T, D)
    return _combine(x2, y4, gate, row(final_g), t_head)


def kernel(x_prompt, x_sample, mem_prompt, mem_sample, norm_mix_g, w_in, shift_mu, decay_w0, decay_w2,
           iclr_a0, iclr_a2, gate_g2, k_k, k_a, r_k, lnx_g, lnx_b, conv_w, conv_b, conv_ln_g, conv_ln_b,
           w_branch, w_o, norm_x_g, norm_mem_g, w_xq, w_xkv, w_xo, norm_ffn_g, w_router, b_router,
           w_gu, b_gu, w_dn, b_dn, final_g):
    layer_params = (norm_mix_g, w_in, shift_mu, decay_w0, decay_w2, iclr_a0, iclr_a2, gate_g2, k_k, k_a,
                    r_k, lnx_g, lnx_b, conv_w, conv_b, conv_ln_g, conv_ln_b, w_branch, w_o, norm_x_g,
                    norm_mem_g, w_xq, w_xkv, w_xo, norm_ffn_g, w_router, b_router, w_gu, b_gu, w_dn, b_dn)
    assert all(p.shape[0] == 1 for p in layer_params), "single-layer stack expected"
    bp, seq_len, D = x_prompt.shape
    bs = x_sample.shape[0]
    assert x_sample.shape[1] == seq_len
    n_seq = bp + bs
    x = jnp.concatenate([x_prompt, x_sample], axis=0).reshape(n_seq * seq_len, D)
    mem = jnp.concatenate([mem_prompt, mem_sample], axis=0).reshape(-1, D)
    y_p, y_s = _layer(x, mem, n_seq, seq_len, bp * seq_len, *[p[0] for p in layer_params], final_g)
    return y_p.reshape(bp, seq_len, D), y_s.reshape(bs, seq_len, D)
```

```python
import functools
import math

import jax
import jax.numpy as jnp
from jax import lax
from jax.experimental import pallas as pl
from jax.experimental.pallas import tpu as pltpu

F32 = jnp.float32
BF16 = jnp.bfloat16

RWKV_HEAD = 64
DECAY_SCALE = math.exp(-0.5)
LNX_EPS = RWKV_HEAD * 1e-5
RMS_EPS = 1e-5
LN_EPS = 1e-5
XATTN_HEADS = 4
TOP_K = 4
SWIGLU_ALPHA = 1.702
SWIGLU_LIMIT = 7.0

LANES = 128
SUBLANES = 8
WKV_CHUNK = 64
LORA_PAD = 128
VMEM_LIMIT = 56 * 1024 * 1024


def _cparams(sem):
    return pltpu.CompilerParams(dimension_semantics=sem, vmem_limit_bytes=VMEM_LIMIT)


def _tile(n, prefs):
    for p in prefs:
        if n % p == 0:
            return p
    return n


def _dot(a, b):
    return jnp.dot(a, b, preferred_element_type=F32)


def _dot_nt(a, b):
    return lax.dot_general(a, b, (((1,), (1,)), ((), ())), preferred_element_type=F32)


def _dot_tn(a, b):
    return lax.dot_general(a, b, (((0,), (0,)), ((), ())), preferred_element_type=F32)


def _split2(x):
    hi = x.astype(BF16)
    lo = (x - hi.astype(F32)).astype(BF16)
    return hi, lo


def _dot3(x, w_hi, w_lo):
    xh, xl = _split2(x)
    return _dot(xh, w_hi) + _dot(xl, w_hi) + _dot(xh, w_lo)


def _rms(x, g):
    return x * lax.rsqrt(jnp.mean(x * x, axis=-1, keepdims=True) + RMS_EPS) * g


def _sigmoid(x):
    return 0.5 * jnp.tanh(0.5 * x) + 0.5


def _inproj_shift_kernel(x_ref, xp_ref, xn_ref, g_ref, w_ref, mu_ref, o_ref, h_scr, hp_scr, hn_scr,
                         *, seq_len):
    i = pl.program_id(0)
    j = pl.program_id(1)
    tm = x_ref.shape[0]

    @pl.when(j == 0)
    def _():
        g = g_ref[...]
        h_scr[...] = _rms(x_ref[...], g).astype(BF16)
        hp_scr[...] = _rms(xp_ref[...], g).astype(BF16)
        hn_scr[...] = _rms(xn_ref[...], g).astype(BF16)

    w = w_ref[...]
    p = _dot(h_scr[...], w)
    pp = _dot(hp_scr[...], w)[SUBLANES - 1:SUBLANES, :]
    pn = _dot(hn_scr[...], w)[0:1, :]
    first = (i * tm) % seq_len == 0
    last = ((i + 1) * tm) % seq_len == 0
    pp = jnp.where(first, 0.0, pp)
    pn = jnp.where(last, 0.0, pn)
    rid = lax.broadcasted_iota(jnp.int32, p.shape, 0)
    up = jnp.where(rid == 0, pp, pltpu.roll(p, 1, 0))
    dn = jnp.where(rid == tm - 1, pn, pltpu.roll(p, tm - 1, 0))
    o_ref[...] = (p + mu_ref[...] * (0.5 * (up + dn) - p)).astype(o_ref.dtype)


def _inproj_shift(x, g, w_bf16, mu, seq_len, tn, out_dtype):
    T, D = x.shape
    N = w_bf16.shape[1]
    tm = _tile(seq_len, (1024, 512, 256, 128, 64, 32, 16, 8))
    nb8 = T // SUBLANES
    r8 = tm // SUBLANES
    return pl.pallas_call(
        functools.partial(_inproj_shift_kernel, seq_len=seq_len),
        grid=(T // tm, N // tn),
        in_specs=[
            pl.BlockSpec((tm, D), lambda i, j: (i, 0)),
            pl.BlockSpec((SUBLANES, D), lambda i, j: (jnp.maximum(i * r8 - 1, 0), 0)),
            pl.BlockSpec((SUBLANES, D), lambda i, j: (jnp.minimum((i + 1) * r8, nb8 - 1), 0)),
            pl.BlockSpec((1, D), lambda i, j: (0, 0)),
            pl.BlockSpec((D, tn), lambda i, j: (0, j)),
            pl.BlockSpec((1, tn), lambda i, j: (0, j)),
        ],
        out_specs=pl.BlockSpec((tm, tn), lambda i, j: (i, j)),
        out_shape=jax.ShapeDtypeStruct((T, N), out_dtype),
        scratch_shapes=[pltpu.VMEM((tm, D), BF16), pltpu.VMEM((SUBLANES, D), BF16),
                        pltpu.VMEM((SUBLANES, D), BF16)],
        compiler_params=_cparams(("parallel", "arbitrary")),
        name="inproj_shift",
    )(x, x, x, g, w_bf16, mu)


def _inproj_glu_kernel(x_ref, g_ref, wa_ref, wb_ref, o_ref, h_scr):
    @pl.when(pl.program_id(1) == 0)
    def _():
        h_scr[...] = _rms(x_ref[...], g_ref[...]).astype(BF16)

    h = h_scr[...]
    a = _dot(h, wa_ref[...])
    b = _dot(h, wb_ref[...])
    o_ref[...] = (a * _sigmoid(b)).astype(o_ref.dtype)


def _inproj_glu(x, g, wa, wb):
    T, D = x.shape
    N = wa.shape[1]
    tm = _tile(T, (1024, 512, 256, 128, 64, 32, 16, 8))
    tn = _tile(N, (512, 256, 128))
    return pl.pallas_call(
        _inproj_glu_kernel,
        grid=(T // tm, N // tn),
        in_specs=[
            pl.BlockSpec((tm, D), lambda i, j: (i, 0)),
            pl.BlockSpec((1, D), lambda i, j: (0, 0)),
            pl.BlockSpec((D, tn), lambda i, j: (0, j)),
            pl.BlockSpec((D, tn), lambda i, j: (0, j)),
        ],
        out_specs=pl.BlockSpec((tm, tn), lambda i, j: (i, j)),
        out_shape=jax.ShapeDtypeStruct((T, N), BF16),
        scratch_shapes=[pltpu.VMEM((tm, D), BF16)],
        compiler_params=_cparams(("parallel", "arbitrary")),
        name="inproj_glu",
    )(x, g, wa, wb)


def _chunk_cumsum(x, bwd):
    L = WKV_CHUNK
    n = x.shape[0]
    rin = lax.broadcasted_iota(jnp.int32, (n, 1), 0) % L
    s = 1
    while s < L:
        if bwd:
            x = x + jnp.where(rin < L - s, pltpu.roll(x, n - s, 0), 0.0)
        else:
            x = x + jnp.where(rin >= s, pltpu.roll(x, s, 0), 0.0)
        s *= 2
    return x


WKV_ISSUE_ORDER = "fffffbffbffffff" + "bf" * 4


def _wkv_bidir_kernel(zr_f, zk_f, zv_f, dd_f, ad_f, zr_b, zk_b, zv_b, dd_b, ad_b, w0_ref, w2h_ref, w2l_ref,
                      a0_ref, a2_ref, kk_ref, ka_ref, rk_ref, of_ref, bf_ref, ob_ref, bb_ref, state_ref,
                      *, n_pairs):
    L = WKV_CHUNK
    PW = 2 * RWKV_HEAD
    TT = zr_f.shape[0]
    n_chunks = TT // L

    @pl.when(pl.program_id(2) == 0)
    def _():
        state_ref[...] = jnp.zeros_like(state_ref)

    row = lax.broadcasted_iota(jnp.int32, (L, 2 * L), 0)
    col = lax.broadcasted_iota(jnp.int32, (L, 2 * L), 1) % L
    ipk = jnp.where(col == row, 1.0, 0.0)
    lane = lax.broadcasted_iota(jnp.int32, (1, PW), 1)
    m0 = lane < RWKV_HEAD
    lane2 = lax.broadcasted_iota(jnp.int32, (1, 2 * PW), 1) % PW
    m0w = lane2 < RWKV_HEAD
    srow = lax.broadcasted_iota(jnp.int32, (PW, PW), 0)
    scol = lax.broadcasted_iota(jnp.int32, (PW, PW), 1)
    same_head = (srow // RWKV_HEAD) == (scol // RWKV_HEAD)
    eye = srow == scol
    HSW = min(zr_f.shape[1], 2 * PW)
    hrow = lax.broadcasted_iota(jnp.int32, (HSW, HSW), 0)
    hcol = lax.broadcasted_iota(jnp.int32, (HSW, HSW), 1)
    head_ones_w = jnp.where((hrow // RWKV_HEAD) == (hcol // RWKV_HEAD), 1.0, 0.0).astype(BF16)

    def bd(x):
        return jnp.concatenate([jnp.where(m0, x, 0.0), jnp.where(m0, 0.0, x)], axis=0).astype(BF16)

    def bd2(x):
        return jnp.concatenate([jnp.where(m0w, x, 0.0), jnp.where(m0w, 0.0, x)], axis=0).astype(BF16)

    def head_sum(x, two_pass):
        n = x.shape[0]
        outs = []
        for c in range(x.shape[1] // HSW):
            xc = x[:, c * HSW:(c + 1) * HSW]
            if two_pass:
                hi, lo = _split2(xc)
                res = _dot(jnp.concatenate([hi, lo], axis=0), head_ones_w)
                outs.append(res[:n] + res[n:])
            else:
                outs.append(_dot(xc.astype(BF16), head_ones_w))
        return jnp.concatenate(outs, axis=1) if len(outs) > 1 else outs[0]

    def direction(d, zr_ref, zk_ref, zv_ref, dd_ref, ad_ref, o_ref, bonus_ref):
        bwd = d == 1
        strict = (col > row) if bwd else (col < row)
        incl = (col >= row) if bwd else (col <= row)
        r = zr_ref[...].astype(F32)
        k = zk_ref[...].astype(F32)
        v = zv_ref[...].astype(F32)
        dlin = w0_ref[d] + _dot3(jnp.tanh(dd_ref[...]), w2h_ref[d], w2l_ref[d])
        iclr_lin = a0_ref[d] + _dot(ad_ref[...].astype(BF16), a2_ref[d])
        yield
        lw = -DECAY_SCALE * _sigmoid(dlin)
        iclr = _sigmoid(iclr_lin)
        kkr = k * kk_ref[...]
        kmod = k * (1.0 + (iclr - 1.0) * ka_ref[...])
        ss = head_sum(kkr * kkr, False)
        bonus_ref[...] = (head_sum(r * kmod * rk_ref[...], True) * v).astype(bonus_ref.dtype)
        yield
        cum = _chunk_cumsum(lw, bwd)
        e_ng = jnp.exp(-cum)
        kk = kkr * lax.rsqrt(jnp.maximum(ss, 1e-24))
        b_all = kk * iclr
        ah_all = -kk * jnp.exp(cum - lw)
        rh_all = r * jnp.exp(cum)
        bh_all = b_all * e_ng
        kh_all = kmod * e_ng

        probs = []
        for ci in (range(n_chunks - 1, -1, -1) if bwd else range(n_chunks)):
            for p in range(n_pairs):
                sl = slice(p * PW, (p + 1) * PW)
                rs = slice(ci * L, (ci + 1) * L)
                q = dict(p=p, ci=ci, sl=sl, rs=rs, ah=ah_all[rs, sl], rh=rh_all[rs, sl], v=v[rs, sl])
                lhs = jnp.concatenate([q["ah"], q["rh"]], axis=0).astype(BF16)
                g = _dot_nt(lhs, jnp.concatenate([bd(bh_all[rs, sl]), bd(kh_all[rs, sl])], axis=0))
                a32 = jnp.where(strict, g[:L, :2 * L], 0.0)
                q["a"] = a32.astype(BF16)
                q["t"] = (ipk + a32).astype(BF16)
                q["rb"] = jnp.where(incl, g[L:, :2 * L], 0.0).astype(BF16)
                q["akrk"] = jnp.concatenate([jnp.where(strict, g[:L, 2 * L:], 0.0),
                                             jnp.where(incl, g[L:, 2 * L:], 0.0)], axis=0).astype(BF16)
                probs.append(q)
        yield
        for q in probs:
            q["kv"] = _dot(q["akrk"], bd(q["v"]))
        yield
        for q in probs:
            q["ai"] = _dot(q["a"], bd(q["a"])).astype(BF16)
        yield
        for lvl in range(1, 6):
            more = lvl < 5
            for q in probs:
                lhs = jnp.concatenate([q["t"], q["ai"]], axis=0) if more else q["t"]
                res = _dot(lhs, bd(q["ai"]))
                q["t"] = (q["t"].astype(F32) + res[:L]).astype(BF16)
                if more:
                    q["ai"] = res[L:].astype(BF16)
            yield
        for q in probs:
            q["rbt"] = _dot(q["rb"], bd(q["t"])).astype(BF16)
        yield
        for q in probs:
            y0 = jnp.concatenate([q["ah"], q["kv"][:L]], axis=1)
            q["ry"] = _dot(jnp.concatenate([q["t"], q["rbt"]], axis=0), bd2(y0))
        yield
        for q in probs:
            rs, sl, ry = q["rs"], q["sl"], q["ry"]
            end = q["ci"] * L if bwd else q["ci"] * L + L - 1
            tot = cum[end:end + 1, sl]
            e_rm = jnp.exp(tot - cum[rs, sl])
            q["rt"] = q["rh"] + ry[L:, :PW]
            q["ob"] = ry[L:, PW:] + q["kv"][L:]
            lhs_t = jnp.concatenate([b_all[rs, sl] * e_rm, kmod[rs, sl] * e_rm], axis=0).astype(BF16)
            rhs_t = jnp.concatenate(
                [ry[:L], jnp.concatenate([jnp.zeros((L, PW), F32), q["v"]], axis=1)], axis=0).astype(BF16)
            mn = _dot_tn(lhs_t, rhs_t)
            q["mm"] = jnp.where(eye, jnp.exp(tot), 0.0) + jnp.where(same_head, mn[:, :PW], 0.0)
            q["nn"] = jnp.where(same_head, mn[:, PW:], 0.0)
        yield
        states = [state_ref[d, p] for p in range(n_pairs)]
        for n, q in enumerate(probs):
            p = q["p"]
            res = _dot(jnp.concatenate([q["rt"], q["mm"]], axis=0).astype(BF16), states[p].astype(BF16))
            o_ref[q["rs"], q["sl"]] = (res[:L] + q["ob"]).astype(o_ref.dtype)
            states[p] = res[L:] + q["nn"]
            if p == n_pairs - 1 and n < len(probs) - 1:
                yield
        for p in range(n_pairs):
            state_ref[d, p] = states[p]

    gens = {"f": direction(0, zr_f, zk_f, zv_f, dd_f, ad_f, of_ref, bf_ref),
            "b": direction(1, zr_b, zk_b, zv_b, dd_b, ad_b, ob_ref, bb_ref)}
    order = list(WKV_ISSUE_ORDER)
    while gens:
        key = order.pop(0) if order else next(iter(gens))
        if key in gens and next(gens[key], "done") == "done":
            del gens[key]


def _wkv_bidir(z_rkv, z_lora, decay_w0, w2h, w2l, iclr_a0, a2_bf16, k_k, k_a, r_k, n_seq, seq_len):
    T = z_rkv.shape[0]
    W = z_rkv.shape[1] // 3
    PW = 2 * RWKV_HEAD
    n_pairs = _tile(W // PW, (4, 2, 1))
    GW = n_pairs * PW
    n_groups = W // GW
    TT = _tile(seq_len, (512, 256, 128, 64))
    nT = seq_len // TT
    fwd_t = lambda b, c: b * nT + c
    bwd_t = lambda b, c: b * nT + nT - 1 - c
    in_specs = []
    for tile, d in ((fwd_t, 0), (bwd_t, 1)):
        in_specs += [pl.BlockSpec((TT, GW), lambda b, g, c, tile=tile, off=off: (tile(b, c), off * n_groups + g))
                     for off in range(3)]
        in_specs += [pl.BlockSpec((TT, LORA_PAD), lambda b, g, c, tile=tile, d=d: (tile(b, c), d)),
                     pl.BlockSpec((TT, LORA_PAD), lambda b, g, c, tile=tile, d=d: (tile(b, c), 2 + d))]
    pspec = pl.BlockSpec((2, 1, GW), lambda b, g, c: (0, 0, g))
    lspec = pl.BlockSpec((2, LORA_PAD, GW), lambda b, g, c: (0, 0, g))
    cspec = pl.BlockSpec((1, GW), lambda b, g, c: (0, g))
    in_specs += [pspec, lspec, lspec, pspec, lspec, cspec, cspec, cspec]
    ospec_f = pl.BlockSpec((TT, GW), lambda b, g, c: (fwd_t(b, c), g))
    ospec_b = pl.BlockSpec((TT, GW), lambda b, g, c: (bwd_t(b, c), g))
    out_t = jax.ShapeDtypeStruct((T, W), BF16)
    return pl.pallas_call(
        functools.partial(_wkv_bidir_kernel, n_pairs=n_pairs),
        grid=(n_seq, n_groups, nT),
        in_specs=in_specs,
        out_specs=[ospec_f, ospec_f, ospec_b, ospec_b],
        out_shape=[out_t, out_t, out_t, out_t],
        scratch_shapes=[pltpu.VMEM((2, n_pairs, PW, PW), F32)],
        compiler_params=_cparams(("parallel", "parallel", "arbitrary")),
        name="wkv_scan_bidir",
    )(*([z_rkv] * 3 + [z_lora] * 2) * 2, decay_w0, w2h, w2l, iclr_a0, a2_bf16, k_k, k_a, r_k)


def _rwkv_post_kernel(of_ref, ob_ref, bf_ref, bb_ref, gd_ref, g2_ref, lg_ref, lb_ref, out_ref):
    W = out_ref.shape[1]
    PW = min(W, 4 * RWKV_HEAD)
    srow = lax.broadcasted_iota(jnp.int32, (PW, PW), 0)
    scol = lax.broadcasted_iota(jnp.int32, (PW, PW), 1)
    head_ones = jnp.where((srow // RWKV_HEAD) == (scol // RWKV_HEAD), 1.0, 0.0).astype(BF16)
    inv = 1.0 / RWKV_HEAD
    n = out_ref.shape[0]

    def head_mean(x):
        hi, lo = _split2(x)
        res = _dot(jnp.concatenate([hi, lo], axis=0), head_ones)
        return (res[:n] + res[n:]) * inv

    gate = _dot(_sigmoid(gd_ref[...]).astype(BF16), g2_ref[...])
    for p in range(W // PW):
        sl = slice(p * PW, (p + 1) * PW)
        o = of_ref[:, sl].astype(F32) + ob_ref[:, sl].astype(F32)
        oc = o - head_mean(o)
        var = head_mean(oc * oc)
        y = oc * lax.rsqrt(var + LNX_EPS) * lg_ref[:, sl] + lb_ref[:, sl]
        y = y + bf_ref[:, sl].astype(F32) + bb_ref[:, sl].astype(F32)
        out_ref[:, sl] = (y * gate[:, sl]).astype(out_ref.dtype)


def _rwkv_post(o_f, o_b, bonus_f, bonus_b, z_lora, g2, lnx_g, lnx_b):
    T, W = o_f.shape
    tm = _tile(T, (512, 256, 128, 64, 32, 16, 8))
    GL = g2.shape[0]
    gd_blk = (4 * LORA_PAD) // GL
    tspec = pl.BlockSpec((tm, W), lambda i: (i, 0))
    return pl.pallas_call(
        _rwkv_post_kernel,
        grid=(T // tm,),
        in_specs=[
            tspec, tspec, tspec, tspec,
            pl.BlockSpec((tm, GL), lambda i: (i, gd_blk)),
            pl.BlockSpec((GL, W), lambda i: (0, 0)),
            pl.BlockSpec((1, W), lambda i: (0, 0)),
            pl.BlockSpec((1, W), lambda i: (0, 0)),
        ],
        out_specs=pl.BlockSpec((tm, W), lambda i: (i, 0)),
        out_shape=jax.ShapeDtypeStruct((T, W), BF16),
        compiler_params=_cparams(("parallel",)),
        name="rwkv_post",
    )(o_f, o_b, bonus_f, bonus_b, z_lora, g2, lnx_g, lnx_b)


def _conv_kernel(u_ref, up_ref, un_ref, w_ref, b_ref, lg_ref, lb_ref, o_ref, ext_scr, *, seq_len, halo):
    i = pl.program_id(0)
    tm = u_ref.shape[0]
    K = w_ref.shape[0]
    first = (i * tm) % seq_len == 0
    last = ((i + 1) * tm) % seq_len == 0
    n_ext = tm + 2 * halo
    ext = jnp.concatenate([jnp.where(first, 0.0, up_ref[...].astype(F32)), u_ref[...].astype(F32),
                           jnp.where(last, 0.0, un_ref[...].astype(F32))], axis=0)
    ext_scr[0] = ext
    for r in range(1, SUBLANES):
        ext_scr[r] = pltpu.roll(ext, n_ext - r, 0)
    sub = min(tm, 32)
    base = halo - K // 2
    for s in range(tm // sub):
        acc = jnp.zeros((sub, u_ref.shape[1]), F32)
        for j in range(K):
            off = base + j
            row0 = s * sub + (off // SUBLANES) * SUBLANES
            acc = acc + w_ref[j:j + 1, :] * ext_scr[off % SUBLANES, row0:row0 + sub, :]
        acc = acc + b_ref[...]
        mean = jnp.mean(acc, axis=-1, keepdims=True)
        xc = acc - mean
        var = jnp.mean(xc * xc, axis=-1, keepdims=True)
        y = xc * lax.rsqrt(var + LN_EPS) * lg_ref[...] + lb_ref[...]
        o_ref[s * sub:(s + 1) * sub, :] = (y * _sigmoid(y)).astype(o_ref.dtype)


def _conv(u, conv_w, conv_b, ln_g, ln_b, seq_len):
    T, C = u.shape
    K = conv_w.shape[0]
    halo = 16
    assert K // 2 <= halo
    tm = _tile(seq_len, (256, 128, 64, 32, 16))
    rh = tm // halo
    nbh = T // halo
    return pl.pallas_call(
        functools.partial(_conv_kernel, seq_len=seq_len, halo=halo),
        grid=(T // tm,),
        in_specs=[
            pl.BlockSpec((tm, C), lambda i: (i, 0)),
            pl.BlockSpec((halo, C), lambda i: (jnp.maximum(i * rh - 1, 0), 0)),
            pl.BlockSpec((halo, C), lambda i: (jnp.minimum((i + 1) * rh, nbh - 1), 0)),
            pl.BlockSpec((K, C), lambda i: (0, 0)),
            pl.BlockSpec((1, C), lambda i: (0, 0)),
            pl.BlockSpec((1, C), lambda i: (0, 0)),
            pl.BlockSpec((1, C), lambda i: (0, 0)),
        ],
        out_specs=pl.BlockSpec((tm, C), lambda i: (i, 0)),
        out_shape=jax.ShapeDtypeStruct((T, C), BF16),
        scratch_shapes=[pltpu.VMEM((SUBLANES, tm + 2 * halo, C), F32)],
        compiler_params=_cparams(("parallel",)),
        name="conformer_conv",
    )(u, u, u, conv_w, conv_b, ln_g, ln_b)


def _merge_kernel(x_ref, g_ref, orw_ref, ocv_ref, wg1_ref, wg2_ref, wb1_ref, wb2_ref, o_ref, h_scr):
    @pl.when(pl.program_id(1) == 0)
    def _():
        h_scr[...] = _rms(x_ref[...], g_ref[...]).astype(BF16)

    h = h_scr[...]
    g1 = _sigmoid(_dot(h, wg1_ref[...]))
    g2 = _sigmoid(_dot(h, wg2_ref[...]))
    y1 = _dot(orw_ref[...], wb1_ref[...])
    y2 = _dot(ocv_ref[...], wb2_ref[...])
    o_ref[...] = (g1 * y1 + g2 * y2).astype(o_ref.dtype)


def _merge(x, g, o_rwkv, o_conv, wg1, wg2, wb1, wb2):
    T, D = x.shape
    W = o_rwkv.shape[1]
    C = o_conv.shape[1]
    tm = _tile(T, (512, 256, 128, 64, 32, 16, 8))
    tn = _tile(D, (512, 256, 128))
    return pl.pallas_call(
        _merge_kernel,
        grid=(T // tm, D // tn),
        in_specs=[
            pl.BlockSpec((tm, D), lambda i, j: (i, 0)),
            pl.BlockSpec((1, D), lambda i, j: (0, 0)),
            pl.BlockSpec((tm, W), lambda i, j: (i, 0)),
            pl.BlockSpec((tm, C), lambda i, j: (i, 0)),
            pl.BlockSpec((D, tn), lambda i, j: (0, j)),
            pl.BlockSpec((D, tn), lambda i, j: (0, j)),
            pl.BlockSpec((W, tn), lambda i, j: (0, j)),
            pl.BlockSpec((C, tn), lambda i, j: (0, j)),
        ],
        out_specs=pl.BlockSpec((tm, tn), lambda i, j: (i, j)),
        out_shape=jax.ShapeDtypeStruct((T, D), BF16),
        scratch_shapes=[pltpu.VMEM((tm, D), BF16)],
        compiler_params=_cparams(("parallel", "arbitrary")),
        name="merge_gates",
    )(x, g, o_rwkv, o_conv, wg1, wg2, wb1, wb2)


def _mm_res_kernel(a_ref, w_ref, res_ref, o_ref):
    o_ref[...] = res_ref[...] + _dot(a_ref[...], w_ref[...])


def _mm_res(a, w, res):
    T, K = a.shape
    N = w.shape[1]
    tm = _tile(T, (1024, 512, 256, 128, 64, 32, 16, 8))
    tn = _tile(N, (1024, 512, 256, 128))
    return pl.pallas_call(
        _mm_res_kernel,
        grid=(T // tm, N // tn),
        in_specs=[
            pl.BlockSpec((tm, K), lambda i, j: (i, 0)),
            pl.BlockSpec((K, tn), lambda i, j: (0, j)),
            pl.BlockSpec((tm, tn), lambda i, j: (i, j)),
        ],
        out_specs=pl.BlockSpec((tm, tn), lambda i, j: (i, j)),
        out_shape=jax.ShapeDtypeStruct((T, N), F32),
        compiler_params=_cparams(("parallel", "parallel")),
        name="proj_residual",
    )(a, w, res)


def _mm_norm_kernel(x_ref, g_ref, w_ref, o_ref, h_scr):
    @pl.when(pl.program_id(1) == 0)
    def _():
        h_scr[...] = _rms(x_ref[...], g_ref[...]).astype(BF16)

    o_ref[...] = _dot(h_scr[...], w_ref[...]).astype(o_ref.dtype)


def _mm_norm(x, g, w):
    T, D = x.shape
    N = w.shape[1]
    tm = _tile(T, (1024, 512, 256, 128, 64, 32, 16, 8))
    tn = _tile(N, (1024, 512, 256, 128))
    return pl.pallas_call(
        _mm_norm_kernel,
        grid=(T // tm, N // tn),
        in_specs=[
            pl.BlockSpec((tm, D), lambda i, j: (i, 0)),
            pl.BlockSpec((1, D), lambda i, j: (0, 0)),
            pl.BlockSpec((D, tn), lambda i, j: (0, j)),
        ],
        out_specs=pl.BlockSpec((tm, tn), lambda i, j: (i, j)),
        out_shape=jax.ShapeDtypeStruct((T, N), BF16),
        scratch_shapes=[pltpu.VMEM((tm, D), BF16)],
        compiler_params=_cparams(("parallel", "arbitrary")),
        name="norm_proj",
    )(x, g, w)


def _xattn_kernel(q_ref, k_ref, v_ref, o_ref):
    D = q_ref.shape[1]
    hd = D // XATTN_HEADS
    scale = hd ** -0.5
    for h in range(XATTN_HEADS):
        sl = slice(h * hd, (h + 1) * hd)
        s = _dot_nt(q_ref[:, sl], k_ref[:, sl]) * scale
        s = s - jnp.max(s, axis=-1, keepdims=True)
        e = jnp.exp(s)
        p = e / jnp.sum(e, axis=-1, keepdims=True)
        o_ref[:, sl] = _dot(p.astype(BF16), v_ref[:, sl]).astype(o_ref.dtype)


def _xattn(q, kv, n_seq, seq_len, n_mem):
    T, D = q.shape
    tm = _tile(seq_len, (512, 256, 128, 64, 32, 16, 8))
    nT = seq_len // tm
    return pl.pallas_call(
        _xattn_kernel,
        grid=(n_seq, nT),
        in_specs=[
            pl.BlockSpec((tm, D), lambda b, i: (b * nT + i, 0)),
            pl.BlockSpec((n_mem, D), lambda b, i: (b, 0)),
            pl.BlockSpec((n_mem, D), lambda b, i: (b, 1)),
        ],
        out_specs=pl.BlockSpec((tm, D), lambda b, i: (b * nT + i, 0)),
        out_shape=jax.ShapeDtypeStruct((T, D), BF16),
        compiler_params=_cparams(("parallel", "parallel")),
        name="cross_attention",
    )(q, kv, kv)


def _router_kernel(x_ref, g_ref, wh_ref, wl_ref, b_ref, hf_ref, e_ref, gate_ref, rank_ref, count_ref, carry_ref):
    @pl.when(pl.program_id(0) == 0)
    def _():
        carry_ref[...] = jnp.zeros_like(carry_ref)

    hf = _rms(x_ref[...], g_ref[...])
    hf_ref[...] = hf.astype(hf_ref.dtype)
    logits = _dot3(hf, wh_ref[...], wl_ref[...]) + b_ref[...]
    E = logits.shape[1]
    eid = lax.broadcasted_iota(jnp.int32, logits.shape, 1).astype(F32)
    work = logits
    vals = []
    idxs = []
    for _ in range(TOP_K):
        m = jnp.max(work, axis=-1, keepdims=True)
        idx = jnp.min(jnp.where(work == m, eid, float(E)), axis=-1, keepdims=True)
        vals.append(m)
        idxs.append(idx.astype(jnp.int32))
        work = jnp.where(eid == idx, -jnp.inf, work)
    ex = [jnp.exp(vv - vals[0]) for vv in vals]
    den = ex[0]
    for t in ex[1:]:
        den = den + t
    tm = logits.shape[0]
    hits = [eid == idx.astype(F32) for idx in idxs]
    per_tok = hits[0].astype(F32)
    for h in hits[1:]:
        per_tok = per_tok + h.astype(F32)
    trow = lax.broadcasted_iota(jnp.int32, (tm, tm), 0)
    tcol = lax.broadcasted_iota(jnp.int32, (tm, tm), 1)
    earlier = jnp.where(tcol < trow, 1.0, 0.0).astype(BF16)
    base = carry_ref[...] + _dot(earlier, per_tok.astype(BF16))
    carry_ref[...] += jnp.sum(per_tok, axis=0, keepdims=True)
    count_ref[...] = carry_ref[...].astype(jnp.int32)

    kid = lax.broadcasted_iota(jnp.int32, (tm, TOP_K), 1)
    e_out = jnp.zeros((tm, TOP_K), jnp.int32)
    g_out = jnp.zeros((tm, TOP_K), F32)
    r_out = jnp.zeros((tm, TOP_K), jnp.int32)
    for t in range(TOP_K):
        rank_t = jnp.sum(jnp.where(hits[t], base, 0.0), axis=-1, keepdims=True).astype(jnp.int32)
        e_out = jnp.where(kid == t, idxs[t], e_out)
        g_out = jnp.where(kid == t, ex[t] / den, g_out)
        r_out = jnp.where(kid == t, rank_t, r_out)
    e_ref[...] = e_out
    gate_ref[...] = g_out
    rank_ref[...] = r_out


def _router(x, g, w_router, b_router):
    T, D = x.shape
    E = w_router.shape[1]
    wh, wl = _split2(w_router)
    tm = _tile(T, (512, 256, 128, 64, 32, 16, 8))
    return pl.pallas_call(
        _router_kernel,
        grid=(T // tm,),
        in_specs=[
            pl.BlockSpec((tm, D), lambda i: (i, 0)),
            pl.BlockSpec((1, D), lambda i: (0, 0)),
            pl.BlockSpec((D, E), lambda i: (0, 0)),
            pl.BlockSpec((D, E), lambda i: (0, 0)),
            pl.BlockSpec((1, E), lambda i: (0, 0)),
        ],
        out_specs=[
            pl.BlockSpec((tm, D), lambda i: (i, 0)),
            pl.BlockSpec((tm, TOP_K), lambda i: (i, 0)),
            pl.BlockSpec((tm, TOP_K), lambda i: (i, 0)),
            pl.BlockSpec((tm, TOP_K), lambda i: (i, 0)),
            pl.BlockSpec((1, E), lambda i: (0, 0)),
        ],
        out_shape=[jax.ShapeDtypeStruct((T, D), BF16),
                   jax.ShapeDtypeStruct((T, TOP_K), jnp.int32),
                   jax.ShapeDtypeStruct((T, TOP_K), F32),
                   jax.ShapeDtypeStruct((T, TOP_K), jnp.int32),
                   jax.ShapeDtypeStruct((1, E), jnp.int32)],
        scratch_shapes=[pltpu.VMEM((1, E), F32)],
        compiler_params=_cparams(("arbitrary",)),
        name="router",
    )(x, g, wh, wl, b_router)


def _expert_kernel(te_ref, tv_ref, x_ref, wg_ref, wu_ref, bg_ref, bu_ref, wd_ref, bd_ref, *rest, nf):
    o_ref, act_ref = rest[-2:]
    i = pl.program_id(0)
    s = pl.program_id(1)
    tf = wg_ref.shape[1]
    valid = tv_ref[i] > 0

    @pl.when(valid & (s < nf))
    def _():
        x = x_ref[...]
        g = _dot(x, wg_ref[...].astype(BF16)) + bg_ref[...]
        u = _dot(x, wu_ref[...].astype(BF16)) + bu_ref[...]
        g = jnp.minimum(g, SWIGLU_LIMIT)
        u = jnp.clip(u, -SWIGLU_LIMIT, SWIGLU_LIMIT)
        act = ((u + 1.0) * (g * _sigmoid(SWIGLU_ALPHA * g))).astype(BF16)
        for f in range(nf):
            @pl.when(s == f)
            def _(f=f):
                act_ref[:, f * tf:(f + 1) * tf] = act

    @pl.when(valid & (s >= nf))
    def _():
        y = _dot(act_ref[...], wd_ref[...].astype(BF16)) + bd_ref[...]
        o_ref[...] = y.astype(o_ref.dtype)

    @pl.when(jnp.logical_not(valid) & (s >= nf))
    def _():
        o_ref[...] = jnp.zeros_like(o_ref)


def _experts(xb, tile_e, tile_valid, w_gu, b_gu, w_dn, b_dn, tm, tile0, total_rows, prev):
    R, D = xb.shape
    E, _, F2 = w_gu.shape
    F = F2 // 2
    tf = _tile(F, (512, 256, 128))
    tn = _tile(D, (512, 256, 128))
    nf = F // tf
    nn = D // tn
    n_tiles = R // tm
    b_gu3 = b_gu.reshape(E, 1, F2)
    b_dn3 = b_dn.reshape(E, 1, D)

    def fa(i, s, tv):
        return jnp.where(tv[i] > 0, jnp.minimum(s, nf - 1), nf - 1)

    def nb(i, s, tv):
        return jnp.where(tv[i] > 0, jnp.clip(s - nf, 0, nn - 1), nn - 1)

    grid_spec = pltpu.PrefetchScalarGridSpec(
        num_scalar_prefetch=2,
        grid=(n_tiles, nf + nn),
        in_specs=[
            pl.BlockSpec((tm, D), lambda i, s, te, tv: (i, 0)),
            pl.BlockSpec((None, D, tf), lambda i, s, te, tv: (te[i], 0, fa(i, s, tv))),
            pl.BlockSpec((None, D, tf), lambda i, s, te, tv: (te[i], 0, nf + fa(i, s, tv))),
            pl.BlockSpec((None, 1, tf), lambda i, s, te, tv: (te[i], 0, fa(i, s, tv))),
            pl.BlockSpec((None, 1, tf), lambda i, s, te, tv: (te[i], 0, nf + fa(i, s, tv))),
            pl.BlockSpec((None, F, tn), lambda i, s, te, tv: (te[i], 0, nb(i, s, tv))),
            pl.BlockSpec((None, 1, tn), lambda i, s, te, tv: (te[i], 0, nb(i, s, tv))),
        ] + ([] if prev is None else [pl.BlockSpec(memory_space=pl.ANY)]),
        out_specs=pl.BlockSpec((tm, tn), lambda i, s, te, tv: (tile0 + i, jnp.clip(s - nf, 0, nn - 1))),
        scratch_shapes=[pltpu.VMEM((tm, F), BF16)],
    )
    args = (tile_e, tile_valid, xb, w_gu, w_gu, b_gu3, b_gu3, w_dn, b_dn3)
    return pl.pallas_call(
        functools.partial(_expert_kernel, nf=nf),
        grid_spec=grid_spec,
        out_shape=jax.ShapeDtypeStruct((total_rows, D), BF16),
        input_output_aliases={} if prev is None else {len(args): 0},
        compiler_params=_cparams(("arbitrary", "arbitrary")),
        name="moe_experts",
    )(*args, *(() if prev is None else (prev,)))


def _combine_kernel(x_ref, y_ref, gate_ref, g_ref, o_head_ref, o_tail_ref, *, n_head_tiles):
    acc = x_ref[...]
    gate = gate_ref[...]
    for t in range(TOP_K):
        acc = acc + gate[:, t:t + 1] * y_ref[t].astype(F32)
    y = _rms(acc, g_ref[...])
    i = pl.program_id(0)

    @pl.when(i < n_head_tiles)
    def _():
        o_head_ref[...] = y

    @pl.when(i >= n_head_tiles)
    def _():
        o_tail_ref[...] = y


def _combine(x, y4, gate, final_g, t_head):
    T, D = x.shape
    tm = _tile(math.gcd(t_head, T - t_head), (512, 256, 128, 64, 32, 16, 8))
    nh = t_head // tm
    return pl.pallas_call(
        functools.partial(_combine_kernel, n_head_tiles=nh),
        grid=(T // tm,),
        in_specs=[
            pl.BlockSpec((tm, D), lambda i: (i, 0)),
            pl.BlockSpec((TOP_K, tm, D), lambda i: (0, i, 0)),
            pl.BlockSpec((tm, TOP_K), lambda i: (i, 0)),
            pl.BlockSpec((1, D), lambda i: (0, 0)),
        ],
        out_specs=[pl.BlockSpec((tm, D), lambda i: (jnp.minimum(i, nh - 1), 0)),
                   pl.BlockSpec((tm, D), lambda i: (jnp.maximum(i - nh, 0), 0))],
        out_shape=[jax.ShapeDtypeStruct((t_head, D), F32), jax.ShapeDtypeStruct((T - t_head, D), F32)],
        compiler_params=_cparams(("arbitrary",)),
        name="moe_combine_norm",
    )(x, y4, gate, final_g)


def _pad_rows(w, n):
    return jnp.pad(w, ((0, 0),) * (w.ndim - 2) + ((0, n - w.shape[-2]), (0, 0)))


def _pad_cols(w, n):
    return jnp.pad(w, ((0, 0),) * (w.ndim - 1) + ((0, n - w.shape[-1]),))


def _moe_tile_rows(n_assign, n_experts):
    for tm in (1024, 512, 256, 128, 64, 32, 16, 8):
        if n_assign >= 4 * n_experts * tm or tm == 8:
            return tm


def _layer(x, mem, n_seq, seq_len, t_head, norm_mix_g, w_in, shift_mu, decay_w0, decay_w2, iclr_a0, iclr_a2,
           gate_g2, k_k, k_a, r_k, lnx_g, lnx_b, conv_w, conv_b, conv_ln_g, conv_ln_b, w_branch, w_o,
           norm_x_g, norm_mem_g, w_xq, w_xkv, w_xo, norm_ffn_g, w_router, b_router, w_gu, b_gu,
           w_dn, b_dn, final_g):
    T, D = x.shape
    W = k_k.shape[0]
    DL = decay_w2.shape[1]
    AL = iclr_a2.shape[1]
    GL = gate_g2.shape[0]
    C = conv_w.shape[1]
    E = w_router.shape[1]
    n_mem = mem.shape[0] // n_seq
    row = lambda v: v.reshape(1, -1)

    o3 = 3 * W
    o4 = o3 + 2 * DL
    o5 = o4 + 2 * AL
    o6 = o5 + GL
    seg = lambda m, a, b, n: _pad_cols(m[..., a:b], n)
    lora_cols = lambda m: jnp.concatenate(
        [seg(m, o3, o3 + DL, LORA_PAD), seg(m, o3 + DL, o4, LORA_PAD), seg(m, o4, o4 + AL, LORA_PAD),
         seg(m, o4 + AL, o5, LORA_PAD), m[..., o5:o6]], axis=-1)
    w_rkv = w_in[:, :o3].astype(BF16)
    w_lora = lora_cols(w_in).astype(BF16)
    mu_rkv = row(shift_mu[:o3])
    mu_lora = row(lora_cols(shift_mu))
    w_ca = w_in[:, o6:o6 + C].astype(BF16)
    w_cb = w_in[:, o6 + C:o6 + 2 * C].astype(BF16)
    w_g1 = w_in[:, o6 + 2 * C:o6 + 2 * C + D].astype(BF16)
    w_g2 = w_in[:, o6 + 2 * C + D:].astype(BF16)
    g_mix = row(norm_mix_g)

    z_rkv = _inproj_shift(x, g_mix, w_rkv, mu_rkv, seq_len, _tile(o3, (1024, 512, 256, 128)), BF16)
    z_lora = _inproj_shift(x, g_mix, w_lora, mu_lora, seq_len, w_lora.shape[1], F32)
    u = _inproj_glu(x, g_mix, w_ca, w_cb)

    w2h, w2l = _split2(_pad_rows(decay_w2, LORA_PAD))
    a2_bf16 = _pad_rows(iclr_a2, LORA_PAD).astype(BF16)
    wkv_args = (z_rkv, z_lora, decay_w0.reshape(2, 1, W), w2h, w2l, iclr_a0.reshape(2, 1, W), a2_bf16,
                row(k_k), row(k_a), row(r_k), n_seq, seq_len)
    o_f, bonus_f, o_b, bonus_b = _wkv_bidir(*wkv_args)
    o_rwkv = _rwkv_post(o_f, o_b, bonus_f, bonus_b, z_lora, gate_g2.astype(BF16), row(lnx_g), row(lnx_b))
    o_conv = _conv(u, conv_w, row(conv_b), row(conv_ln_g), row(conv_ln_b), seq_len)

    merged = _merge(x, g_mix, o_rwkv, o_conv, w_g1, w_g2, w_branch[:W].astype(BF16),
                    w_branch[W:].astype(BF16))
    x1 = _mm_res(merged, w_o.astype(BF16), x)

    q = _mm_norm(x1, row(norm_x_g), w_xq.astype(BF16))
    kv = _mm_norm(mem, row(norm_mem_g), w_xkv.astype(BF16))
    att = _xattn(q, kv, n_seq, seq_len, n_mem)
    x2 = _mm_res(att, w_xo.astype(BF16), x1)

    hf, top_e, gate, rank, counts = _router(x2, row(norm_ffn_g), w_router, row(b_router))

    A = T * TOP_K
    tm_e = _moe_tile_rows(A, E)
    n_tiles = (A + E * (tm_e - 1) + tm_e - 1) // tm_e
    n_rows = n_tiles * tm_e
    flat_e = top_e.reshape(A)
    counts = counts.reshape(E)
    start = jnp.cumsum(counts) - counts
    padded = (counts + tm_e - 1) // tm_e * tm_e
    pad_end = jnp.cumsum(padded)
    pad_start = pad_end - padded
    dest_of = pad_start[flat_e] + rank.reshape(A)
    _, order = lax.sort((flat_e, jnp.arange(A, dtype=jnp.int32)), num_keys=1)
    tile_start = jnp.arange(n_tiles, dtype=jnp.int32) * tm_e
    tile_e = jnp.minimum(jnp.sum(tile_start[:, None] >= pad_end[None, :], axis=1), E - 1).astype(jnp.int32)
    tile_off = tile_start - pad_start[tile_e]
    tile_valid = jnp.clip(counts[tile_e] - tile_off, 0, tm_e).astype(jnp.int32)
    per_row = lambda v: jnp.broadcast_to(v[:, None], (n_tiles, tm_e)).reshape(n_rows)
    rows = jnp.arange(n_rows, dtype=jnp.int32)
    row_off = per_row(tile_off) + rows % tm_e
    src = order[jnp.minimum(per_row(start[tile_e]) + row_off, A - 1)] // TOP_K
    row_tok = jnp.where(row_off < per_row(counts[tile_e]), src, rows % T)

    yb = None
    half = (n_tiles + 1) // 2
    for t0, t1 in ((0, half), (half, n_tiles)):
        xb = hf.at[row_tok[t0 * tm_e:t1 * tm_e]].get(mode='promise_in_bounds')
        yb = _experts(xb, tile_e[t0:t1], tile_valid[t0:t1], w_gu, b_gu, w_dn, b_dn, tm_e, t0, n_rows, yb)
    dest_slot_major = dest_of.reshape(T, TOP_K).T.reshape(A)
    y4 = yb.at[dest_slot_major].get(mode='promise_in_bounds').reshape(TOP_K, T, D)
    return _combine(x2, y4, gate, row(final_g), t_head)


def kernel(x_prompt, x_sample, mem_prompt, mem_sample, norm_mix_g, w_in, shift_mu, decay_w0, decay_w2,
           iclr_a0, iclr_a2, gate_g2, k_k, k_a, r_k, lnx_g, lnx_b, conv_w, conv_b, conv_ln_g, conv_ln_b,
           w_branch, w_o, norm_x_g, norm_mem_g, w_xq, w_xkv, w_xo, norm_ffn_g, w_router, b_router,
           w_gu, b_gu, w_dn, b_dn, final_g):
    layer_params = (norm_mix_g, w_in, shift_mu, decay_w0, decay_w2, iclr_a0, iclr_a2, gate_g2, k_k, k_a,
                    r_k, lnx_g, lnx_b, conv_w, conv_b, conv_ln_g, conv_ln_b, w_branch, w_o, norm_x_g,
                    norm_mem_g, w_xq, w_xkv, w_xo, norm_ffn_g, w_router, b_router, w_gu, b_gu, w_dn, b_dn)
    assert all(p.shape[0] == 1 for p in layer_params), "single-layer stack expected"
    bp, seq_len, D = x_prompt.shape
    bs = x_sample.shape[0]
    assert x_sample.shape[1] == seq_len
    n_seq = bp + bs
    x = jnp.concatenate([x_prompt, x_sample], axis=0).reshape(n_seq * seq_len, D)
    mem = jnp.concatenate([mem_prompt, mem_sample], axis=0).reshape(-1, D)
    y_p, y_s = _layer(x, mem, n_seq, seq_len, bp * seq_len, *[p[0] for p in layer_params], final_g)
    return y_p.reshape(bp, seq_len, D), y_s.reshape(bs, seq_len, D)
```

```python
import functools
import math

import jax
import jax.numpy as jnp
from jax import lax
from jax.experimental import pallas as pl
from jax.experimental.pallas import tpu as pltpu

F32 = jnp.float32
BF16 = jnp.bfloat16

RWKV_HEAD = 64
DECAY_SCALE = math.exp(-0.5)
LNX_EPS = RWKV_HEAD * 1e-5
RMS_EPS = 1e-5
LN_EPS = 1e-5
XATTN_HEADS = 4
TOP_K = 4
SWIGLU_ALPHA = 1.702
SWIGLU_LIMIT = 7.0

LANES = 128
SUBLANES = 8
WKV_CHUNK = 64
LORA_PAD = 128
VMEM_LIMIT = 56 * 1024 * 1024


def _cparams(sem):
    return pltpu.CompilerParams(dimension_semantics=sem, vmem_limit_bytes=VMEM_LIMIT)


def _tile(n, prefs):
    for p in prefs:
        if n % p == 0:
            return p
    return n


def _dot(a, b):
    return jnp.dot(a, b, preferred_element_type=F32)


def _dot_nt(a, b):
    return lax.dot_general(a, b, (((1,), (1,)), ((), ())), preferred_element_type=F32)


def _dot_tn(a, b):
    return lax.dot_general(a, b, (((0,), (0,)), ((), ())), preferred_element_type=F32)


def _split2(x):
    hi = x.astype(BF16)
    lo = (x - hi.astype(F32)).astype(BF16)
    return hi, lo


def _dot3(x, w_hi, w_lo):
    xh, xl = _split2(x)
    return _dot(xh, w_hi) + _dot(xl, w_hi) + _dot(xh, w_lo)


def _rms(x, g):
    return x * lax.rsqrt(jnp.mean(x * x, axis=-1, keepdims=True) + RMS_EPS) * g


def _sigmoid(x):
    return 0.5 * jnp.tanh(0.5 * x) + 0.5


def _inproj_shift_kernel(x_ref, xp_ref, xn_ref, g_ref, w_ref, mu_ref, o_ref, h_scr, hp_scr, hn_scr,
                         *, seq_len):
    i = pl.program_id(0)
    j = pl.program_id(1)
    tm = x_ref.shape[0]

    @pl.when(j == 0)
    def _():
        g = g_ref[...]
        h_scr[...] = _rms(x_ref[...], g).astype(BF16)
        hp_scr[...] = _rms(xp_ref[...], g).astype(BF16)
        hn_scr[...] = _rms(xn_ref[...], g).astype(BF16)

    w = w_ref[...]
    p = _dot(h_scr[...], w)
    pp = _dot(hp_scr[...], w)[SUBLANES - 1:SUBLANES, :]
    pn = _dot(hn_scr[...], w)[0:1, :]
    first = (i * tm) % seq_len == 0
    last = ((i + 1) * tm) % seq_len == 0
    pp = jnp.where(first, 0.0, pp)
    pn = jnp.where(last, 0.0, pn)
    rid = lax.broadcasted_iota(jnp.int32, p.shape, 0)
    up = jnp.where(rid == 0, pp, pltpu.roll(p, 1, 0))
    dn = jnp.where(rid == tm - 1, pn, pltpu.roll(p, tm - 1, 0))
    o_ref[...] = (p + mu_ref[...] * (0.5 * (up + dn) - p)).astype(o_ref.dtype)


def _inproj_shift(x, g, w_bf16, mu, seq_len, tn, out_dtype):
    T, D = x.shape
    N = w_bf16.shape[1]
    tm = _tile(seq_len, (1024, 512, 256, 128, 64, 32, 16, 8))
    nb8 = T // SUBLANES
    r8 = tm // SUBLANES
    return pl.pallas_call(
        functools.partial(_inproj_shift_kernel, seq_len=seq_len),
        grid=(T // tm, N // tn),
        in_specs=[
            pl.BlockSpec((tm, D), lambda i, j: (i, 0)),
            pl.BlockSpec((SUBLANES, D), lambda i, j: (jnp.maximum(i * r8 - 1, 0), 0)),
            pl.BlockSpec((SUBLANES, D), lambda i, j: (jnp.minimum((i + 1) * r8, nb8 - 1), 0)),
            pl.BlockSpec((1, D), lambda i, j: (0, 0)),
            pl.BlockSpec((D, tn), lambda i, j: (0, j)),
            pl.BlockSpec((1, tn), lambda i, j: (0, j)),
        ],
        out_specs=pl.BlockSpec((tm, tn), lambda i, j: (i, j)),
        out_shape=jax.ShapeDtypeStruct((T, N), out_dtype),
        scratch_shapes=[pltpu.VMEM((tm, D), BF16), pltpu.VMEM((SUBLANES, D), BF16),
                        pltpu.VMEM((SUBLANES, D), BF16)],
        compiler_params=_cparams(("parallel", "arbitrary")),
        name="inproj_shift",
    )(x, x, x, g, w_bf16, mu)


def _inproj_glu_kernel(x_ref, g_ref, wa_ref, wb_ref, o_ref, h_scr):
    @pl.when(pl.program_id(1) == 0)
    def _():
        h_scr[...] = _rms(x_ref[...], g_ref[...]).astype(BF16)

    h = h_scr[...]
    a = _dot(h, wa_ref[...])
    b = _dot(h, wb_ref[...])
    o_ref[...] = (a * _sigmoid(b)).astype(o_ref.dtype)


def _inproj_glu(x, g, wa, wb):
    T, D = x.shape
    N = wa.shape[1]
    tm = _tile(T, (1024, 512, 256, 128, 64, 32, 16, 8))
    tn = _tile(N, (512, 256, 128))
    return pl.pallas_call(
        _inproj_glu_kernel,
        grid=(T // tm, N // tn),
        in_specs=[
            pl.BlockSpec((tm, D), lambda i, j: (i, 0)),
            pl.BlockSpec((1, D), lambda i, j: (0, 0)),
            pl.BlockSpec((D, tn), lambda i, j: (0, j)),
            pl.BlockSpec((D, tn), lambda i, j: (0, j)),
        ],
        out_specs=pl.BlockSpec((tm, tn), lambda i, j: (i, j)),
        out_shape=jax.ShapeDtypeStruct((T, N), BF16),
        scratch_shapes=[pltpu.VMEM((tm, D), BF16)],
        compiler_params=_cparams(("parallel", "arbitrary")),
        name="inproj_glu",
    )(x, g, wa, wb)


def _chunk_cumsum(x, bwd):
    L = WKV_CHUNK
    n = x.shape[0]
    rin = lax.broadcasted_iota(jnp.int32, (n, 1), 0) % L
    s = 1
    while s < L:
        if bwd:
            x = x + jnp.where(rin < L - s, pltpu.roll(x, n - s, 0), 0.0)
        else:
            x = x + jnp.where(rin >= s, pltpu.roll(x, s, 0), 0.0)
        s *= 2
    return x


WKV_ISSUE_ORDER = "fffffbffbffffff" + "bf" * 4


def _wkv_bidir_kernel(zr_f, zk_f, zv_f, dd_f, ad_f, zr_b, zk_b, zv_b, dd_b, ad_b, w0_ref, w2h_ref, w2l_ref,
                      a0_ref, a2_ref, kk_ref, ka_ref, rk_ref, of_ref, bf_ref, ob_ref, bb_ref, state_ref,
                      *, n_pairs):
    L = WKV_CHUNK
    PW = 2 * RWKV_HEAD
    TT = zr_f.shape[0]
    n_chunks = TT // L

    @pl.when(pl.program_id(2) == 0)
    def _():
        state_ref[...] = jnp.zeros_like(state_ref)

    row = lax.broadcasted_iota(jnp.int32, (L, 2 * L), 0)
    col = lax.broadcasted_iota(jnp.int32, (L, 2 * L), 1) % L
    ipk = jnp.where(col == row, 1.0, 0.0)
    lane = lax.broadcasted_iota(jnp.int32, (1, PW), 1)
    m0 = lane < RWKV_HEAD
    lane2 = lax.broadcasted_iota(jnp.int32, (1, 2 * PW), 1) % PW
    m0w = lane2 < RWKV_HEAD
    srow = lax.broadcasted_iota(jnp.int32, (PW, PW), 0)
    scol = lax.broadcasted_iota(jnp.int32, (PW, PW), 1)
    same_head = (srow // RWKV_HEAD) == (scol // RWKV_HEAD)
    eye = srow == scol
    HSW = min(zr_f.shape[1], 2 * PW)
    hrow = lax.broadcasted_iota(jnp.int32, (HSW, HSW), 0)
    hcol = lax.broadcasted_iota(jnp.int32, (HSW, HSW), 1)
    head_ones_w = jnp.where((hrow // RWKV_HEAD) == (hcol // RWKV_HEAD), 1.0, 0.0).astype(BF16)

    def bd(x):
        return jnp.concatenate([jnp.where(m0, x, 0.0), jnp.where(m0, 0.0, x)], axis=0).astype(BF16)

    def bd2(x):
        return jnp.concatenate([jnp.where(m0w, x, 0.0), jnp.where(m0w, 0.0, x)], axis=0).astype(BF16)

    def head_sum(x, two_pass):
        n = x.shape[0]
        outs = []
        for c in range(x.shape[1] // HSW):
            xc = x[:, c * HSW:(c + 1) * HSW]
            if two_pass:
                hi, lo = _split2(xc)
                res = _dot(jnp.concatenate([hi, lo], axis=0), head_ones_w)
                outs.append(res[:n] + res[n:])
            else:
                outs.append(_dot(xc.astype(BF16), head_ones_w))
        return jnp.concatenate(outs, axis=1) if len(outs) > 1 else outs[0]

    def direction(d, zr_ref, zk_ref, zv_ref, dd_ref, ad_ref, o_ref, bonus_ref):
        bwd = d == 1
        strict = (col > row) if bwd else (col < row)
        incl = (col >= row) if bwd else (col <= row)
        r = zr_ref[...].astype(F32)
        k = zk_ref[...].astype(F32)
        v = zv_ref[...].astype(F32)
        dlin = w0_ref[d] + _dot3(jnp.tanh(dd_ref[...]), w2h_ref[d], w2l_ref[d])
        iclr_lin = a0_ref[d] + _dot(ad_ref[...].astype(BF16), a2_ref[d])
        yield
        lw = -DECAY_SCALE * _sigmoid(dlin)
        iclr = _sigmoid(iclr_lin)
        kkr = k * kk_ref[...]
        kmod = k * (1.0 + (iclr - 1.0) * ka_ref[...])
        ss = head_sum(kkr * kkr, False)
        bonus_ref[...] = (head_sum(r * kmod * rk_ref[...], True) * v).astype(bonus_ref.dtype)
        yield
        cum = _chunk_cumsum(lw, bwd)
        e_ng = jnp.exp(-cum)
        kk = kkr * lax.rsqrt(jnp.maximum(ss, 1e-24))
        b_all = kk * iclr
        ah_all = -kk * jnp.exp(cum - lw)
        rh_all = r * jnp.exp(cum)
        bh_all = b_all * e_ng
        kh_all = kmod * e_ng

        probs = []
        for ci in (range(n_chunks - 1, -1, -1) if bwd else range(n_chunks)):
            for p in range(n_pairs):
                sl = slice(p * PW, (p + 1) * PW)
                rs = slice(ci * L, (ci + 1) * L)
                q = dict(p=p, ci=ci, sl=sl, rs=rs, ah=ah_all[rs, sl], rh=rh_all[rs, sl], v=v[rs, sl])
                lhs = jnp.concatenate([q["ah"], q["rh"]], axis=0).astype(BF16)
                g = _dot_nt(lhs, jnp.concatenate([bd(bh_all[rs, sl]), bd(kh_all[rs, sl])], axis=0))
                a32 = jnp.where(strict, g[:L, :2 * L], 0.0)
                q["a"] = a32.astype(BF16)
                q["t"] = (ipk + a32).astype(BF16)
                q["rb"] = jnp.where(incl, g[L:, :2 * L], 0.0).astype(BF16)
                q["akrk"] = jnp.concatenate([jnp.where(strict, g[:L, 2 * L:], 0.0),
                                             jnp.where(incl, g[L:, 2 * L:], 0.0)], axis=0).astype(BF16)
                probs.append(q)
        yield
        for q in probs:
            q["kv"] = _dot(q["akrk"], bd(q["v"]))
        yield
        for q in probs:
            q["ai"] = _dot(q["a"], bd(q["a"])).astype(BF16)
        yield
        for lvl in range(1, 6):
            more = lvl < 5
            for q in probs:
                lhs = jnp.concatenate([q["t"], q["ai"]], axis=0) if more else q["t"]
                res = _dot(lhs, bd(q["ai"]))
                q["t"] = (q["t"].astype(F32) + res[:L]).astype(BF16)
                if more:
                    q["ai"] = res[L:].astype(BF16)
            yield
        for q in probs:
            q["rbt"] = _dot(q["rb"], bd(q["t"])).astype(BF16)
        yield
        for q in probs:
            y0 = jnp.concatenate([q["ah"], q["kv"][:L]], axis=1)
            q["ry"] = _dot(jnp.concatenate([q["t"], q["rbt"]], axis=0), bd2(y0))
        yield
        for q in probs:
            rs, sl, ry = q["rs"], q["sl"], q["ry"]
            end = q["ci"] * L if bwd else q["ci"] * L + L - 1
            tot = cum[end:end + 1, sl]
            e_rm = jnp.exp(tot - cum[rs, sl])
            q["rt"] = q["rh"] + ry[L:, :PW]
            q["ob"] = ry[L:, PW:] + q["kv"][L:]
            lhs_t = jnp.concatenate([b_all[rs, sl] * e_rm, kmod[rs, sl] * e_rm], axis=0).astype(BF16)
            rhs_t = jnp.concatenate(
                [ry[:L], jnp.concatenate([jnp.zeros((L, PW), F32), q["v"]], axis=1)], axis=0).astype(BF16)
            mn = _dot_tn(lhs_t, rhs_t)
            q["mm"] = jnp.where(eye, jnp.exp(tot), 0.0) + jnp.where(same_head, mn[:, :PW], 0.0)
            q["nn"] = jnp.where(same_head, mn[:, PW:], 0.0)
        yield
        states = [state_ref[d, p] for p in range(n_pairs)]
        for n, q in enumerate(probs):
            p = q["p"]
            res = _dot(jnp.concatenate([q["rt"], q["mm"]], axis=0).astype(BF16), states[p].astype(BF16))
            o_ref[q["rs"], q["sl"]] = (res[:L] + q["ob"]).astype(o_ref.dtype)
            states[p] = res[L:] + q["nn"]
            if p == n_pairs - 1 and n < len(probs) - 1:
                yield
        for p in range(n_pairs):
            state_ref[d, p] = states[p]

    gens = {"f": direction(0, zr_f, zk_f, zv_f, dd_f, ad_f, of_ref, bf_ref),
            "b": direction(1, zr_b, zk_b, zv_b, dd_b, ad_b, ob_ref, bb_ref)}
    order = list(WKV_ISSUE_ORDER)
    while gens:
        key = order.pop(0) if order else next(iter(gens))
        if key in gens and next(gens[key], "done") == "done":
            del gens[key]


def _wkv_bidir(z_rkv, z_lora, decay_w0, w2h, w2l, iclr_a0, a2_bf16, k_k, k_a, r_k, n_seq, seq_len):
    T = z_rkv.shape[0]
    W = z_rkv.shape[1] // 3
    PW = 2 * RWKV_HEAD
    n_pairs = _tile(W // PW, (4, 2, 1))
    GW = n_pairs * PW
    n_groups = W // GW
    TT = _tile(seq_len, (512, 256, 128, 64))
    nT = seq_len // TT
    fwd_t = lambda b, c: b * nT + c
    bwd_t = lambda b, c: b * nT + nT - 1 - c
    in_specs = []
    for tile, d in ((fwd_t, 0), (bwd_t, 1)):
        in_specs += [pl.BlockSpec((TT, GW), lambda b, g, c, tile=tile, off=off: (tile(b, c), off * n_groups + g))
                     for off in range(3)]
        in_specs += [pl.BlockSpec((TT, LORA_PAD), lambda b, g, c, tile=tile, d=d: (tile(b, c), d)),
                     pl.BlockSpec((TT, LORA_PAD), lambda b, g, c, tile=tile, d=d: (tile(b, c), 2 + d))]
    pspec = pl.BlockSpec((2, 1, GW), lambda b, g, c: (0, 0, g))
    lspec = pl.BlockSpec((2, LORA_PAD, GW), lambda b, g, c: (0, 0, g))
    cspec = pl.BlockSpec((1, GW), lambda b, g, c: (0, g))
    in_specs += [pspec, lspec, lspec, pspec, lspec, cspec, cspec, cspec]
    ospec_f = pl.BlockSpec((TT, GW), lambda b, g, c: (fwd_t(b, c), g))
    ospec_b = pl.BlockSpec((TT, GW), lambda b, g, c: (bwd_t(b, c), g))
    out_t = jax.ShapeDtypeStruct((T, W), BF16)
    return pl.pallas_call(
        functools.partial(_wkv_bidir_kernel, n_pairs=n_pairs),
        grid=(n_seq, n_groups, nT),
        in_specs=in_specs,
        out_specs=[ospec_f, ospec_f, ospec_b, ospec_b],
        out_shape=[out_t, out_t, out_t, out_t],
        scratch_shapes=[pltpu.VMEM((2, n_pairs, PW, PW), F32)],
        compiler_params=_cparams(("parallel", "parallel", "arbitrary")),
        name="wkv_scan_bidir",
    )(*([z_rkv] * 3 + [z_lora] * 2) * 2, decay_w0, w2h, w2l, iclr_a0, a2_bf16, k_k, k_a, r_k)


def _rwkv_post_kernel(of_ref, ob_ref, bf_ref, bb_ref, gd_ref, g2_ref, lg_ref, lb_ref, out_ref):
    W = out_ref.shape[1]
    PW = min(W, 4 * RWKV_HEAD)
    srow = lax.broadcasted_iota(jnp.int32, (PW, PW), 0)
    scol = lax.broadcasted_iota(jnp.int32, (PW, PW), 1)
    head_ones = jnp.where((srow // RWKV_HEAD) == (scol // RWKV_HEAD), 1.0, 0.0).astype(BF16)
    inv = 1.0 / RWKV_HEAD
    n = out_ref.shape[0]

    def head_mean(x):
        hi, lo = _split2(x)
        res = _dot(jnp.concatenate([hi, lo], axis=0), head_ones)
        return (res[:n] + res[n:]) * inv

    gate = _dot(_sigmoid(gd_ref[...]).astype(BF16), g2_ref[...])
    for p in range(W // PW):
        sl = slice(p * PW, (p + 1) * PW)
        o = of_ref[:, sl].astype(F32) + ob_ref[:, sl].astype(F32)
        oc = o - head_mean(o)
        var = head_mean(oc * oc)
        y = oc * lax.rsqrt(var + LNX_EPS) * lg_ref[:, sl] + lb_ref[:, sl]
        y = y + bf_ref[:, sl].astype(F32) + bb_ref[:, sl].astype(F32)
        out_ref[:, sl] = (y * gate[:, sl]).astype(out_ref.dtype)


def _rwkv_post(o_f, o_b, bonus_f, bonus_b, z_lora, g2, lnx_g, lnx_b):
    T, W = o_f.shape
    tm = _tile(T, (512, 256, 128, 64, 32, 16, 8))
    GL = g2.shape[0]
    gd_blk = (4 * LORA_PAD) // GL
    tspec = pl.BlockSpec((tm, W), lambda i: (i, 0))
    return pl.pallas_call(
        _rwkv_post_kernel,
        grid=(T // tm,),
        in_specs=[
            tspec, tspec, tspec, tspec,
            pl.BlockSpec((tm, GL), lambda i: (i, gd_blk)),
            pl.BlockSpec((GL, W), lambda i: (0, 0)),
            pl.BlockSpec((1, W), lambda i: (0, 0)),
            pl.BlockSpec((1, W), lambda i: (0, 0)),
        ],
        out_specs=pl.BlockSpec((tm, W), lambda i: (i, 0)),
        out_shape=jax.ShapeDtypeStruct((T, W), BF16),
        compiler_params=_cparams(("parallel",)),
        name="rwkv_post",
    )(o_f, o_b, bonus_f, bonus_b, z_lora, g2, lnx_g, lnx_b)


def _conv_kernel(u_ref, up_ref, un_ref, w_ref, b_ref, lg_ref, lb_ref, o_ref, ext_scr, *, seq_len, halo):
    i = pl.program_id(0)
    tm = u_ref.shape[0]
    K = w_ref.shape[0]
    first = (i * tm) % seq_len == 0
    last = ((i + 1) * tm) % seq_len == 0
    n_ext = tm + 2 * halo
    ext = jnp.concatenate([jnp.where(first, 0.0, up_ref[...].astype(F32)), u_ref[...].astype(F32),
                           jnp.where(last, 0.0, un_ref[...].astype(F32))], axis=0)
    ext_scr[0] = ext
    for r in range(1, SUBLANES):
        ext_scr[r] = pltpu.roll(ext, n_ext - r, 0)
    sub = min(tm, 32)
    base = halo - K // 2
    for s in range(tm // sub):
        acc = jnp.zeros((sub, u_ref.shape[1]), F32)
        for j in range(K):
            off = base + j
            row0 = s * sub + (off // SUBLANES) * SUBLANES
            acc = acc + w_ref[j:j + 1, :] * ext_scr[off % SUBLANES, row0:row0 + sub, :]
        acc = acc + b_ref[...]
        mean = jnp.mean(acc, axis=-1, keepdims=True)
        xc = acc - mean
        var = jnp.mean(xc * xc, axis=-1, keepdims=True)
        y = xc * lax.rsqrt(var + LN_EPS) * lg_ref[...] + lb_ref[...]
        o_ref[s * sub:(s + 1) * sub, :] = (y * _sigmoid(y)).astype(o_ref.dtype)


def _conv(u, conv_w, conv_b, ln_g, ln_b, seq_len):
    T, C = u.shape
    K = conv_w.shape[0]
    halo = 16
    assert K // 2 <= halo
    tm = _tile(seq_len, (256, 128, 64, 32, 16))
    rh = tm // halo
    nbh = T // halo
    return pl.pallas_call(
        functools.partial(_conv_kernel, seq_len=seq_len, halo=halo),
        grid=(T // tm,),
        in_specs=[
            pl.BlockSpec((tm, C), lambda i: (i, 0)),
            pl.BlockSpec((halo, C), lambda i: (jnp.maximum(i * rh - 1, 0), 0)),
            pl.BlockSpec((halo, C), lambda i: (jnp.minimum((i + 1) * rh, nbh - 1), 0)),
            pl.BlockSpec((K, C), lambda i: (0, 0)),
            pl.BlockSpec((1, C), lambda i: (0, 0)),
            pl.BlockSpec((1, C), lambda i: (0, 0)),
            pl.BlockSpec((1, C), lambda i: (0, 0)),
        ],
        out_specs=pl.BlockSpec((tm, C), lambda i: (i, 0)),
        out_shape=jax.ShapeDtypeStruct((T, C), BF16),
        scratch_shapes=[pltpu.VMEM((SUBLANES, tm + 2 * halo, C), F32)],
        compiler_params=_cparams(("parallel",)),
        name="conformer_conv",
    )(u, u, u, conv_w, conv_b, ln_g, ln_b)


def _merge_kernel(x_ref, g_ref, orw_ref, ocv_ref, wg1_ref, wg2_ref, wb1_ref, wb2_ref, o_ref, h_scr):
    @pl.when(pl.program_id(1) == 0)
    def _():
        h_scr[...] = _rms(x_ref[...], g_ref[...]).astype(BF16)

    h = h_scr[...]
    g1 = _sigmoid(_dot(h, wg1_ref[...]))
    g2 = _sigmoid(_dot(h, wg2_ref[...]))
    y1 = _dot(orw_ref[...], wb1_ref[...])
    y2 = _dot(ocv_ref[...], wb2_ref[...])
    o_ref[...] = (g1 * y1 + g2 * y2).astype(o_ref.dtype)


def _merge(x, g, o_rwkv, o_conv, wg1, wg2, wb1, wb2):
    T, D = x.shape
    W = o_rwkv.shape[1]
    C = o_conv.shape[1]
    tm = _tile(T, (512, 256, 128, 64, 32, 16, 8))
    tn = _tile(D, (512, 256, 128))
    return pl.pallas_call(
        _merge_kernel,
        grid=(T // tm, D // tn),
        in_specs=[
            pl.BlockSpec((tm, D), lambda i, j: (i, 0)),
            pl.BlockSpec((1, D), lambda i, j: (0, 0)),
            pl.BlockSpec((tm, W), lambda i, j: (i, 0)),
            pl.BlockSpec((tm, C), lambda i, j: (i, 0)),
            pl.BlockSpec((D, tn), lambda i, j: (0, j)),
            pl.BlockSpec((D, tn), lambda i, j: (0, j)),
            pl.BlockSpec((W, tn), lambda i, j: (0, j)),
            pl.BlockSpec((C, tn), lambda i, j: (0, j)),
        ],
        out_specs=pl.BlockSpec((tm, tn), lambda i, j: (i, j)),
        out_shape=jax.ShapeDtypeStruct((T, D), BF16),
        scratch_shapes=[pltpu.VMEM((tm, D), BF16)],
        compiler_params=_cparams(("parallel", "arbitrary")),
        name="merge_gates",
    )(x, g, o_rwkv, o_conv, wg1, wg2, wb1, wb2)


def _mm_res_kernel(a_ref, w_ref, res_ref, o_ref):
    o_ref[...] = res_ref[...] + _dot(a_ref[...], w_ref[...])


def _mm_res(a, w, res):
    T, K = a.shape
    N = w.shape[1]
    tm = _tile(T, (1024, 512, 256, 128, 64, 32, 16, 8))
    tn = _tile(N, (1024, 512, 256, 128))
    return pl.pallas_call(
        _mm_res_kernel,
        grid=(T // tm, N // tn),
        in_specs=[
            pl.BlockSpec((tm, K), lambda i, j: (i, 0)),
            pl.BlockSpec((K, tn), lambda i, j: (0, j)),
            pl.BlockSpec((tm, tn), lambda i, j: (i, j)),
        ],
        out_specs=pl.BlockSpec((tm, tn), lambda i, j: (i, j)),
        out_shape=jax.ShapeDtypeStruct((T, N), F32),
        compiler_params=_cparams(("parallel", "parallel")),
        name="proj_residual",
    )(a, w, res)


def _mm_norm_kernel(x_ref, g_ref, w_ref, o_ref, h_scr):
    @pl.when(pl.program_id(1) == 0)
    def _():
        h_scr[...] = _rms(x_ref[...], g_ref[...]).astype(BF16)

    o_ref[...] = _dot(h_scr[...], w_ref[...]).astype(o_ref.dtype)


def _mm_norm(x, g, w):
    T, D = x.shape
    N = w.shape[1]
    tm = _tile(T, (1024, 512, 256, 128, 64, 32, 16, 8))
    tn = _tile(N, (1024, 512, 256, 128))
    return pl.pallas_call(
        _mm_norm_kernel,
        grid=(T // tm, N // tn),
        in_specs=[
            pl.BlockSpec((tm, D), lambda i, j: (i, 0)),
            pl.BlockSpec((1, D), lambda i, j: (0, 0)),
            pl.BlockSpec((D, tn), lambda i, j: (0, j)),
        ],
        out_specs=pl.BlockSpec((tm, tn), lambda i, j: (i, j)),
        out_shape=jax.ShapeDtypeStruct((T, N), BF16),
        scratch_shapes=[pltpu.VMEM((tm, D), BF16)],
        compiler_params=_cparams(("parallel", "arbitrary")),
        name="norm_proj",
    )(x, g, w)


def _xattn_kernel(q_ref, k_ref, v_ref, o_ref):
    D = q_ref.shape[1]
    hd = D // XATTN_HEADS
    scale = hd ** -0.5
    for h in range(XATTN_HEADS):
        sl = slice(h * hd, (h + 1) * hd)
        s = _dot_nt(q_ref[:, sl], k_ref[:, sl]) * scale
        s = s - jnp.max(s, axis=-1, keepdims=True)
        e = jnp.exp(s)
        p = e / jnp.sum(e, axis=-1, keepdims=True)
        o_ref[:, sl] = _dot(p.astype(BF16), v_ref[:, sl]).astype(o_ref.dtype)


def _xattn(q, kv, n_seq, seq_len, n_mem):
    T, D = q.shape
    tm = _tile(seq_len, (512, 256, 128, 64, 32, 16, 8))
    nT = seq_len // tm
    return pl.pallas_call(
        _xattn_kernel,
        grid=(n_seq, nT),
        in_specs=[
            pl.BlockSpec((tm, D), lambda b, i: (b * nT + i, 0)),
            pl.BlockSpec((n_mem, D), lambda b, i: (b, 0)),
            pl.BlockSpec((n_mem, D), lambda b, i: (b, 1)),
        ],
        out_specs=pl.BlockSpec((tm, D), lambda b, i: (b * nT + i, 0)),
        out_shape=jax.ShapeDtypeStruct((T, D), BF16),
        compiler_params=_cparams(("parallel", "parallel")),
        name="cross_attention",
    )(q, kv, kv)


def _router_kernel(x_ref, g_ref, wh_ref, wl_ref, b_ref, hf_ref, e_ref, gate_ref, rank_ref, count_ref, carry_ref):
    @pl.when(pl.program_id(0) == 0)
    def _():
        carry_ref[...] = jnp.zeros_like(carry_ref)

    hf = _rms(x_ref[...], g_ref[...])
    hf_ref[...] = hf.astype(hf_ref.dtype)
    logits = _dot3(hf, wh_ref[...], wl_ref[...]) + b_ref[...]
    E = logits.shape[1]
    eid = lax.broadcasted_iota(jnp.int32, logits.shape, 1).astype(F32)
    work = logits
    vals = []
    idxs = []
    for _ in range(TOP_K):
        m = jnp.max(work, axis=-1, keepdims=True)
        idx = jnp.min(jnp.where(work == m, eid, float(E)), axis=-1, keepdims=True)
        vals.append(m)
        idxs.append(idx.astype(jnp.int32))
        work = jnp.where(eid == idx, -jnp.inf, work)
    ex = [jnp.exp(vv - vals[0]) for vv in vals]
    den = ex[0]
    for t in ex[1:]:
        den = den + t
    tm = logits.shape[0]
    hits = [eid == idx.astype(F32) for idx in idxs]
    per_tok = hits[0].astype(F32)
    for h in hits[1:]:
        per_tok = per_tok + h.astype(F32)
    trow = lax.broadcasted_iota(jnp.int32, (tm, tm), 0)
    tcol = lax.broadcasted_iota(jnp.int32, (tm, tm), 1)
    earlier = jnp.where(tcol < trow, 1.0, 0.0).astype(BF16)
    base = carry_ref[...] + _dot(earlier, per_tok.astype(BF16))
    carry_ref[...] += jnp.sum(per_tok, axis=0, keepdims=True)
    count_ref[...] = carry_ref[...].astype(jnp.int32)

    kid = lax.broadcasted_iota(jnp.int32, (tm, TOP_K), 1)
    e_out = jnp.zeros((tm, TOP_K), jnp.int32)
    g_out = jnp.zeros((tm, TOP_K), F32)
    r_out = jnp.zeros((tm, TOP_K), jnp.int32)
    for t in range(TOP_K):
        rank_t = jnp.sum(jnp.where(hits[t], base, 0.0), axis=-1, keepdims=True).astype(jnp.int32)
        e_out = jnp.where(kid == t, idxs[t], e_out)
        g_out = jnp.where(kid == t, ex[t] / den, g_out)
        r_out = jnp.where(kid == t, rank_t, r_out)
    e_ref[...] = e_out
    gate_ref[...] = g_out
    rank_ref[...] = r_out


def _router(x, g, w_router, b_router):
    T, D = x.shape
    E = w_router.shape[1]
    wh, wl = _split2(w_router)
    tm = _tile(T, (512, 256, 128, 64, 32, 16, 8))
    return pl.pallas_call(
        _router_kernel,
        grid=(T // tm,),
        in_specs=[
            pl.BlockSpec((tm, D), lambda i: (i, 0)),
            pl.BlockSpec((1, D), lambda i: (0, 0)),
            pl.BlockSpec((D, E), lambda i: (0, 0)),
            pl.BlockSpec((D, E), lambda i: (0, 0)),
            pl.BlockSpec((1, E), lambda i: (0, 0)),
        ],
        out_specs=[
            pl.BlockSpec((tm, D), lambda i: (i, 0)),
            pl.BlockSpec((tm, TOP_K), lambda i: (i, 0)),
            pl.BlockSpec((tm, TOP_K), lambda i: (i, 0)),
            pl.BlockSpec((tm, TOP_K), lambda i: (i, 0)),
            pl.BlockSpec((1, E), lambda i: (0, 0)),
        ],
        out_shape=[jax.ShapeDtypeStruct((T, D), BF16),
                   jax.ShapeDtypeStruct((T, TOP_K), jnp.int32),
                   jax.ShapeDtypeStruct((T, TOP_K), F32),
                   jax.ShapeDtypeStruct((T, TOP_K), jnp.int32),
                   jax.ShapeDtypeStruct((1, E), jnp.int32)],
        scratch_shapes=[pltpu.VMEM((1, E), F32)],
        compiler_params=_cparams(("arbitrary",)),
        name="router",
    )(x, g, wh, wl, b_router)


def _expert_kernel(te_ref, tv_ref, x_ref, wg_ref, wu_ref, bg_ref, bu_ref, wd_ref, bd_ref, *rest, nf):
    o_ref, act_ref = rest[-2:]
    i = pl.program_id(0)
    s = pl.program_id(1)
    tf = wg_ref.shape[1]
    valid = tv_ref[i] > 0

    @pl.when(valid & (s < nf))
    def _():
        x = x_ref[...]
        g = _dot(x, wg_ref[...].astype(BF16)) + bg_ref[...]
        u = _dot(x, wu_ref[...].astype(BF16)) + bu_ref[...]
        g = jnp.minimum(g, SWIGLU_LIMIT)
        u = jnp.clip(u, -SWIGLU_LIMIT, SWIGLU_LIMIT)
        act = ((u + 1.0) * (g * _sigmoid(SWIGLU_ALPHA * g))).astype(BF16)
        for f in range(nf):
            @pl.when(s == f)
            def _(f=f):
                act_ref[:, f * tf:(f + 1) * tf] = act

    @pl.when(valid & (s >= nf))
    def _():
        y = _dot(act_ref[...], wd_ref[...].astype(BF16)) + bd_ref[...]
        o_ref[...] = y.astype(o_ref.dtype)

    @pl.when(jnp.logical_not(valid) & (s >= nf))
    def _():
        o_ref[...] = jnp.zeros_like(o_ref)


def _experts(xb, tile_e, tile_valid, w_gu, b_gu, w_dn, b_dn, tm, tile0, total_rows, prev):
    R, D = xb.shape
    E, _, F2 = w_gu.shape
    F = F2 // 2
    tf = _tile(F, (512, 256, 128))
    tn = _tile(D, (512, 256, 128))
    nf = F // tf
    nn = D // tn
    n_tiles = R // tm
    b_gu3 = b_gu.reshape(E, 1, F2)
    b_dn3 = b_dn.reshape(E, 1, D)

    def fa(i, s, tv):
        return jnp.where(tv[i] > 0, jnp.minimum(s, nf - 1), nf - 1)

    def nb(i, s, tv):
        return jnp.where(tv[i] > 0, jnp.clip(s - nf, 0, nn - 1), nn - 1)

    grid_spec = pltpu.PrefetchScalarGridSpec(
        num_scalar_prefetch=2,
        grid=(n_tiles, nf + nn),
        in_specs=[
            pl.BlockSpec((tm, D), lambda i, s, te, tv: (i, 0)),
            pl.BlockSpec((None, D, tf), lambda i, s, te, tv: (te[i], 0, fa(i, s, tv))),
            pl.BlockSpec((None, D, tf), lambda i, s, te, tv: (te[i], 0, nf + fa(i, s, tv))),
            pl.BlockSpec((None, 1, tf), lambda i, s, te, tv: (te[i], 0, fa(i, s, tv))),
            pl.BlockSpec((None, 1, tf), lambda i, s, te, tv: (te[i], 0, nf + fa(i, s, tv))),
            pl.BlockSpec((None, F, tn), lambda i, s, te, tv: (te[i], 0, nb(i, s, tv))),
            pl.BlockSpec((None, 1, tn), lambda i, s, te, tv: (te[i], 0, nb(i, s, tv))),
        ] + ([] if prev is None else [pl.BlockSpec(memory_space=pl.ANY)]),
        out_specs=pl.BlockSpec((tm, tn), lambda i, s, te, tv: (tile0 + i, jnp.clip(s - nf, 0, nn - 1))),
        scratch_shapes=[pltpu.VMEM((tm, F), BF16)],
    )
    args = (tile_e, tile_valid, xb, w_gu, w_gu, b_gu3, b_gu3, w_dn, b_dn3)
    return pl.pallas_call(
        functools.partial(_expert_kernel, nf=nf),
        grid_spec=grid_spec,
        out_shape=jax.ShapeDtypeStruct((total_rows, D), BF16),
        input_output_aliases={} if prev is None else {len(args): 0},
        compiler_params=_cparams(("arbitrary", "arbitrary")),
        name="moe_experts",
    )(*args, *(() if prev is None else (prev,)))


def _combine_kernel(x_ref, y_ref, gate_ref, g_ref, o_head_ref, o_tail_ref, *, n_head_tiles):
    acc = x_ref[...]
    gate = gate_ref[...]
    for t in range(TOP_K):
        acc = acc + gate[:, t:t + 1] * y_ref[t].astype(F32)
    y = _rms(acc, g_ref[...])
    i = pl.program_id(0)

    @pl.when(i < n_head_tiles)
    def _():
        o_head_ref[...] = y

    @pl.when(i >= n_head_tiles)
    def _():
        o_tail_ref[...] = y


def _combine(x, y4, gate, final_g, t_head):
    T, D = x.shape
    tm = _tile(math.gcd(t_head, T - t_head), (512, 256, 128, 64, 32, 16, 8))
    nh = t_head // tm
    return pl.pallas_call(
        functools.partial(_combine_kernel, n_head_tiles=nh),
        grid=(T // tm,),
        in_specs=[
            pl.BlockSpec((tm, D), lambda i: (i, 0)),
            pl.BlockSpec((TOP_K, tm, D), lambda i: (0, i, 0)),
            pl.BlockSpec((tm, TOP_K), lambda i: (i, 0)),
            pl.BlockSpec((1, D), lambda i: (0, 0)),
        ],
        out_specs=[pl.BlockSpec((tm, D), lambda i: (jnp.minimum(i, nh - 1), 0)),
                   pl.BlockSpec((tm, D), lambda i: (jnp.maximum(i - nh, 0), 0))],
        out_shape=[jax.ShapeDtypeStruct((t_head, D), F32), jax.ShapeDtypeStruct((T - t_head, D), F32)],
        compiler_params=_cparams(("arbitrary",)),
        name="moe_combine_norm",
    )(x, y4, gate, final_g)


def _pad_rows(w, n):
    return jnp.pad(w, ((0, 0),) * (w.ndim - 2) + ((0, n - w.shape[-2]), (0, 0)))


def _pad_cols(w, n):
    return jnp.pad(w, ((0, 0),) * (w.ndim - 1) + ((0, n - w.shape[-1]),))


def _moe_tile_rows(n_assign, n_experts):
    for tm in (1024, 512, 256, 128, 64, 32, 16, 8):
        if n_assign >= 4 * n_experts * tm or tm == 8:
            return tm


def _layer(x, mem, n_seq, seq_len, t_head, norm_mix_g, w_in, shift_mu, decay_w0, decay_w2, iclr_a0, iclr_a2,
           gate_g2, k_k, k_a, r_k, lnx_g, lnx_b, conv_w, conv_b, conv_ln_g, conv_ln_b, w_branch, w_o,
           norm_x_g, norm_mem_g, w_xq, w_xkv, w_xo, norm_ffn_g, w_router, b_router, w_gu, b_gu,
           w_dn, b_dn, final_g):
    T, D = x.shape
    W = k_k.shape[0]
    DL = decay_w2.shape[1]
    AL = iclr_a2.shape[1]
    GL = gate_g2.shape[0]
    C = conv_w.shape[1]
    E = w_router.shape[1]
    n_mem = mem.shape[0] // n_seq
    row = lambda v: v.reshape(1, -1)

    o3 = 3 * W
    o4 = o3 + 2 * DL
    o5 = o4 + 2 * AL
    o6 = o5 + GL
    seg = lambda m, a, b, n: _pad_cols(m[..., a:b], n)
    lora_cols = lambda m: jnp.concatenate(
        [seg(m, o3, o3 + DL, LORA_PAD), seg(m, o3 + DL, o4, LORA_PAD), seg(m, o4, o4 + AL, LORA_PAD),
         seg(m, o4 + AL, o5, LORA_PAD), m[..., o5:o6]], axis=-1)
    w_rkv = w_in[:, :o3].astype(BF16)
    w_lora = lora_cols(w_in).astype(BF16)
    mu_rkv = row(shift_mu[:o3])
    mu_lora = row(lora_cols(shift_mu))
    w_ca = w_in[:, o6:o6 + C].astype(BF16)
    w_cb = w_in[:, o6 + C:o6 + 2 * C].astype(BF16)
    w_g1 = w_in[:, o6 + 2 * C:o6 + 2 * C + D].astype(BF16)
    w_g2 = w_in[:, o6 + 2 * C + D:].astype(BF16)
    g_mix = row(norm_mix_g)

    z_rkv = _inproj_shift(x, g_mix, w_rkv, mu_rkv, seq_len, _tile(o3, (1024, 512, 256, 128)), BF16)
    z_lora = _inproj_shift(x, g_mix, w_lora, mu_lora, seq_len, w_lora.shape[1], F32)
    u = _inproj_glu(x, g_mix, w_ca, w_cb)

    w2h, w2l = _split2(_pad_rows(decay_w2, LORA_PAD))
    a2_bf16 = _pad_rows(iclr_a2, LORA_PAD).astype(BF16)
    wkv_args = (z_rkv, z_lora, decay_w0.reshape(2, 1, W), w2h, w2l, iclr_a0.reshape(2, 1, W), a2_bf16,
                row(k_k), row(k_a), row(r_k), n_seq, seq_len)
    o_f, bonus_f, o_b, bonus_b = _wkv_bidir(*wkv_args)
    o_rwkv = _rwkv_post(o_f, o_b, bonus_f, bonus_b, z_lora, gate_g2.astype(BF16), row(lnx_g), row(lnx_b))
    o_conv = _conv(u, conv_w, row(conv_b), row(conv_ln_g), row(conv_ln_b), seq_len)

    merged = _merge(x, g_mix, o_rwkv, o_conv, w_g1, w_g2, w_branch[:W].astype(BF16),
                    w_branch[W:].astype(BF16))
    x1 = _mm_res(merged, w_o.astype(BF16), x)

    q = _mm_norm(x1, row(norm_x_g), w_xq.astype(BF16))
    kv = _mm_norm(mem, row(norm_mem_g), w_xkv.astype(BF16))
    att = _xattn(q, kv, n_seq, seq_len, n_mem)
    x2 = _mm_res(att, w_xo.astype(BF16), x1)

    hf, top_e, gate, rank, counts = _router(x2, row(norm_ffn_g), w_router, row(b_router))

    A = T * TOP_K
    tm_e = _moe_tile_rows(A, E)
    n_tiles = (A + E * (tm_e - 1) + tm_e - 1) // tm_e
    n_rows = n_tiles * tm_e
    flat_e = top_e.reshape(A)
    counts = counts.reshape(E)
    start = jnp.cumsum(counts) - counts
    padded = (counts + tm_e - 1) // tm_e * tm_e
    pad_end = jnp.cumsum(padded)
    pad_start = pad_end - padded
    dest_of = pad_start[flat_e] + rank.reshape(A)
    _, order = lax.sort((flat_e, jnp.arange(A, dtype=jnp.int32)), num_keys=1)
    tile_start = jnp.arange(n_tiles, dtype=jnp.int32) * tm_e
    tile_e = jnp.minimum(jnp.sum(tile_start[:, None] >= pad_end[None, :], axis=1), E - 1).astype(jnp.int32)
    tile_off = tile_start - pad_start[tile_e]
    tile_valid = jnp.clip(counts[tile_e] - tile_off, 0, tm_e).astype(jnp.int32)
    per_row = lambda v: jnp.broadcast_to(v[:, None], (n_tiles, tm_e)).reshape(n_rows)
    rows = jnp.arange(n_rows, dtype=jnp.int32)
    row_off = per_row(tile_off) + rows % tm_e
    src = order[jnp.minimum(per_row(start[tile_e]) + row_off, A - 1)] // TOP_K
    row_tok = jnp.where(row_off < per_row(counts[tile_e]), src, rows % T)

    yb = None
    half = (n_tiles + 3) // 4
    for t0, t1 in ((0, half), (half, n_tiles)):
        xb = hf.at[row_tok[t0 * tm_e:t1 * tm_e]].get(mode='promise_in_bounds')
        yb = _experts(xb, tile_e[t0:t1], tile_valid[t0:t1], w_gu, b_gu, w_dn, b_dn, tm_e, t0, n_rows, yb)
    dest_slot_major = dest_of.reshape(T, TOP_K).T.reshape(A)
    y4 = yb.at[dest_slot_major].get(mode='promise_in_bounds').reshape(TOP_K, T, D)
    return _combine(x2, y4, gate, row(final_g), t_head)


def kernel(x_prompt, x_sample, mem_prompt, mem_sample, norm_mix_g, w_in, shift_mu, decay_w0, decay_w2,
           iclr_a0, iclr_a2, gate_g2, k_k, k_a, r_k, lnx_g, lnx_b, conv_w, conv_b, conv_ln_g, conv_ln_b,
           w_branch, w_o, norm_x_g, norm_mem_g, w_xq, w_xkv, w_xo, norm_ffn_g, w_router, b_router,
           w_gu, b_gu, w_dn, b_dn, final_g):
    layer_params = (norm_mix_g, w_in, shift_mu, decay_w0, decay_w2, iclr_a0, iclr_a2, gate_g2, k_k, k_a,
                    r_k, lnx_g, lnx_b, conv_w, conv_b, conv_ln_g, conv_ln_b, w_branch, w_o, norm_x_g,
                    norm_mem_g, w_xq, w_xkv, w_xo, norm_ffn_g, w_router, b_router, w_gu, b_gu, w_dn, b_dn)
    assert all(p.shape[0] == 1 for p in layer_params), "single-layer stack expected"
    bp, seq_len, D = x_prompt.shape
    bs = x_sample.shape[0]
    assert x_sample.shape[1] == seq_len
    n_seq = bp + bs
    x = jnp.concatenate([x_prompt, x_sample], axis=0).reshape(n_seq * seq_len, D)
    mem = jnp.concatenate([mem_prompt, mem_sample], axis=0).reshape(-1, D)
    y_p, y_s = _layer(x, mem, n_seq, seq_len, bp * seq_len, *[p[0] for p in layer_params], final_g)
    return y_p.reshape(bp, seq_len, D), y_s.reshape(bs, seq_len, D)
```
